```python
import jax, jax.numpy as jnp
from jax import lax
import numpy as np

D_MODEL = 1024
BATCH = 8
SEQ = 8192
DEPTH = 1
DEC_BATCH = 2
DEC_SEQ = 8192
PAST_LEN = 128

D_CONV = D_MODEL // 2
D_FNET = D_MODEL // 2
D_MIX = D_CONV + D_FNET
GROUP_DIM = 64
N_CONV_GROUPS = D_CONV // GROUP_DIM
N_FNET_GROUPS = D_FNET // GROUP_DIM
D_IN = 2 * D_CONV + D_FNET
CONV_WIDTH = 31
N_EXPERTS = 256
TOP_K = 8
N_EXPERT_GROUPS = 8
TOPK_GROUPS = 4
D_EXPERT = D_MODEL // 4
D_SHARED = D_EXPERT
ROUTED_SCALE = 2.5
EXPERT_BLOCK = 128
EPS = 1e-6

kernel_name = "hymba_conformer_fnet_moe_encoder"


def rmsnorm(x, g):
    xf = x.astype(jnp.float32)
    y = xf * lax.rsqrt(jnp.mean(xf * xf, axis=-1, keepdims=True) + EPS)
    return (y * g.astype(jnp.float32)).astype(x.dtype)


def layernorm(x, g, b):
    xf = x.astype(jnp.float32)
    mu = jnp.mean(xf, axis=-1, keepdims=True)
    xc = xf - mu
    var = jnp.mean(xc * xc, axis=-1, keepdims=True)
    y = xc * lax.rsqrt(var + EPS) * g.astype(jnp.float32) + b.astype(jnp.float32)
    return y.astype(x.dtype)


def conv_group(u, conv_w, conv_b, ln_g, ln_b):
    a, gt = jnp.split(u, 2, axis=-1)
    v = a * jax.nn.sigmoid(gt)
    pad = CONV_WIDTH // 2
    v = lax.conv_general_dilated(
        v, conv_w[:, None, :], window_strides=(1,), padding=[(pad, pad)],
        dimension_numbers=("NWC", "WIO", "NWC"), feature_group_count=D_CONV) + conv_b
    v = layernorm(v, ln_g, ln_b)
    return jax.nn.silu(v)


def fourier_group(z):
    b, s, _ = z.shape
    zf = z.astype(jnp.float32).reshape(b, s, N_FNET_GROUPS, GROUP_DIM)
    f = jnp.fft.fft2(zf, axes=(1, 3), norm="ortho")
    return jnp.real(f).reshape(b, s, D_FNET).astype(z.dtype)


def route(h2d, w_router, b_router):
    t = h2d.shape[0]
    s = jax.nn.sigmoid(jnp.dot(h2d.astype(jnp.float32), w_router.astype(jnp.float32)))
    sb = s + b_router.astype(jnp.float32)
    grp = sb.reshape(t, N_EXPERT_GROUPS, N_EXPERTS // N_EXPERT_GROUPS)
    grp_score = jnp.sum(lax.top_k(grp, 2)[0], axis=-1)
    _, gidx = lax.top_k(grp_score, TOPK_GROUPS)
    gmask = jnp.any(gidx[..., None] == jnp.arange(N_EXPERT_GROUPS), axis=-2)
    emask = jnp.repeat(gmask, N_EXPERTS // N_EXPERT_GROUPS, axis=-1)
    _, idx = lax.top_k(jnp.where(emask, sb, -jnp.inf), TOP_K)
    w = jnp.take_along_axis(s, idx, axis=-1)
    w = w / jnp.sum(w, axis=-1, keepdims=True) * ROUTED_SCALE
    return idx, w


def routed_experts(h2d, idx, w, w_gate, w_up, w_down):
    t, d = h2d.shape
    n = t * TOP_K
    m = EXPERT_BLOCK
    nb = (n + N_EXPERTS * (m - 1) + m - 1) // m
    p = nb * m
    flat_e = idx.reshape(n).astype(jnp.int32)
    flat_tok = jnp.arange(n, dtype=jnp.int32) // TOP_K
    flat_w = w.reshape(n)
    sorted_e, order = lax.sort((flat_e, jnp.arange(n, dtype=jnp.int32)), num_keys=1, is_stable=True)
    counts = jnp.bincount(flat_e, length=N_EXPERTS).astype(jnp.int32)
    starts = jnp.cumsum(counts) - counts
    pcounts = (counts + m - 1) // m * m
    pends = jnp.cumsum(pcounts)
    pstarts = pends - pcounts
    rank = jnp.arange(n, dtype=jnp.int32) - starts[sorted_e]
    dest = pstarts[sorted_e] + rank
    tok_buf = jnp.full((p,), t, jnp.int32).at[dest].set(flat_tok[order])
    w_buf = jnp.zeros((p,), h2d.dtype).at[dest].set(flat_w[order].astype(h2d.dtype))
    block_e = jnp.minimum(
        jnp.searchsorted(pends, jnp.arange(nb, dtype=jnp.int32) * m, side="right"),
        N_EXPERTS - 1).astype(jnp.int32)
    h_pad = jnp.concatenate([h2d, jnp.zeros((1, d), h2d.dtype)], axis=0)

    def block_fn(args):
        e, tok, wt = args
        xb = h_pad[tok]
        hid = jax.nn.silu(xb @ w_gate[e]) * (xb @ w_up[e])
        return (hid @ w_down[e]) * wt[:, None]

    out = lax.map(block_fn, (block_e, tok_buf.reshape(nb, m), w_buf.reshape(nb, m)))
    y = jax.ops.segment_sum(out.reshape(p, d), tok_buf, num_segments=t + 1)
    return y[:t]


def encoder_layer(x, c, p):
    b, s, d = x.shape
    mod = jax.nn.silu(c) @ p["w_ada"] + p["b_ada"]
    sh1, sc1, gt1, sh2, sc2, gt2 = [mm[:, None, :] for mm in jnp.split(mod, 6, axis=-1)]
    h = rmsnorm(x, p["g_mix_pre"]) * (1 + sc1) + sh1
    u = h @ p["w_in"]
    conv_out = conv_group(u[..., :2 * D_CONV], p["conv_w"], p["conv_b"], p["conv_ln_g"], p["conv_ln_b"])
    fnet_out = fourier_group(u[..., 2 * D_CONV:])
    merged = jnp.concatenate([rmsnorm(conv_out, p["g_conv_out"]), rmsnorm(fnet_out, p["g_fnet_out"])], axis=-1)
    mixed = merged @ p["w_out"]
    x = x + gt1 * rmsnorm(mixed, p["g_mix_post"])
    h = rmsnorm(x, p["g_ffn_pre"]) * (1 + sc2) + sh2
    h2 = h.reshape(b * s, d)
    idx, w = route(h2, p["w_router"], p["b_router"])
    y = routed_experts(h2, idx, w, p["w_gate"], p["w_up"], p["w_down"])
    y = y + (jax.nn.silu(h2 @ p["ws_gate"]) * (h2 @ p["ws_up"])) @ p["ws_down"]
    x = x + gt2 * rmsnorm(y.reshape(b, s, d), p["g_ffn_post"])
    return x


def setup_inputs(seed: int = 0) -> dict:
    key = jax.random.key(seed)
    ks = jax.random.split(key, 32)
    f32 = jnp.float32
    nrm = lambda k, shape, sc: jax.random.normal(k, shape, f32) * sc
    gain = lambda k, shape: 1.0 + 0.02 * jax.random.normal(k, shape, f32)
    L = DEPTH
    return {
        "x_prompt": nrm(ks[0], (BATCH, SEQ, D_MODEL), 1.0),
        "x_sample": nrm(ks[1], (DEC_BATCH, DEC_SEQ, D_MODEL), 1.0),
        "c_prompt": nrm(ks[2], (BATCH, D_MODEL), 1.0),
        "c_sample": nrm(ks[3], (DEC_BATCH, D_MODEL), 1.0),
        "w_ada": nrm(ks[4], (L, D_MODEL, 6 * D_MODEL), 0.5 * D_MODEL ** -0.5),
        "b_ada": nrm(ks[5], (L, 6 * D_MODEL), 0.02),
        "g_mix_pre": gain(ks[6], (L, D_MODEL)),
        "w_in": nrm(ks[7], (L, D_MODEL, D_IN), D_MODEL ** -0.5),
        "conv_w": nrm(ks[8], (L, CONV_WIDTH, D_CONV), CONV_WIDTH ** -0.5),
        "conv_b": nrm(ks[9], (L, D_CONV), 0.02),
        "conv_ln_g": gain(ks[10], (L, D_CONV)),
        "conv_ln_b": nrm(ks[11], (L, D_CONV), 0.02),
        "g_conv_out": gain(ks[12], (L, D_CONV)),
        "g_fnet_out": gain(ks[13], (L, D_FNET)),
        "w_out": nrm(ks[14], (L, D_MIX, D_MODEL), D_MIX ** -0.5),
        "g_mix_post": gain(ks[15], (L, D_MODEL)),
        "g_ffn_pre": gain(ks[16], (L, D_MODEL)),
        "w_router": nrm(ks[17], (L, D_MODEL, N_EXPERTS), D_MODEL ** -0.5),
        "b_router": nrm(ks[18], (L, N_EXPERTS), 0.01),
        "w_gate": nrm(ks[19], (L, N_EXPERTS, D_MODEL, D_EXPERT), D_MODEL ** -0.5),
        "w_up": nrm(ks[20], (L, N_EXPERTS, D_MODEL, D_EXPERT), D_MODEL ** -0.5),
        "w_down": nrm(ks[21], (L, N_EXPERTS, D_EXPERT, D_MODEL), D_EXPERT ** -0.5),
        "ws_gate": nrm(ks[22], (L, D_MODEL, D_SHARED), D_MODEL ** -0.5),
        "ws_up": nrm(ks[23], (L, D_MODEL, D_SHARED), D_MODEL ** -0.5),
        "ws_down": nrm(ks[24], (L, D_SHARED, D_MODEL), D_SHARED ** -0.5),
        "g_ffn_post": gain(ks[25], (L, D_MODEL)),
    }


def reference(x_prompt, x_sample, c_prompt, c_sample, w_ada, b_ada, g_mix_pre, w_in,
              conv_w, conv_b, conv_ln_g, conv_ln_b, g_conv_out, g_fnet_out, w_out,
              g_mix_post, g_ffn_pre, w_router, b_router, w_gate, w_up, w_down,
              ws_gate, ws_up, ws_down, g_ffn_post):
    y_prompt = x_prompt
    y_sample = x_sample
    for l in range(DEPTH):
        p = {
            "w_ada": w_ada[l], "b_ada": b_ada[l], "g_mix_pre": g_mix_pre[l], "w_in": w_in[l],
            "conv_w": conv_w[l], "conv_b": conv_b[l], "conv_ln_g": conv_ln_g[l], "conv_ln_b": conv_ln_b[l],
            "g_conv_out": g_conv_out[l], "g_fnet_out": g_fnet_out[l], "w_out": w_out[l],
            "g_mix_post": g_mix_post[l], "g_ffn_pre": g_ffn_pre[l], "w_router": w_router[l],
            "b_router": b_router[l], "w_gate": w_gate[l], "w_up": w_up[l], "w_down": w_down[l],
            "ws_gate": ws_gate[l], "ws_up": ws_up[l], "ws_down": ws_down[l], "g_ffn_post": g_ffn_post[l],
        }
        y_prompt = encoder_layer(y_prompt, c_prompt, p)
        y_sample = encoder_layer(y_sample, c_sample, p)
    return (y_prompt, y_sample)
```

```python
import functools

import numpy as np
import jax
import jax.numpy as jnp
from jax import lax
from jax.experimental import pallas as pl
from jax.experimental.pallas import tpu as pltpu

F32 = jnp.float32
BF16 = jnp.bfloat16

D_MODEL = 1024
D_CONV = 512
D_FNET = 512
GROUP_DIM = 64
CONV_WIDTH = 31
N_EXPERTS = 256
TOP_K = 8
N_EXPERT_GROUPS = 8
GROUP_SIZE = N_EXPERTS // N_EXPERT_GROUPS
TOPK_GROUPS = 4
ROUTED_SCALE = 2.5
EPS = 1e-6

DFT_S1 = 128
DFT_CH = 128
HALO = 16
EXPERT_BM = 256
VMEM_LIMIT = 56 * 1024 * 1024


def _cparams(sem, vmem=None):
    return pltpu.CompilerParams(dimension_semantics=sem, vmem_limit_bytes=vmem or VMEM_LIMIT)


def _rms(x):
    return x * lax.rsqrt(jnp.mean(x * x, axis=-1, keepdims=True) + EPS)


def _silu(x):
    return x * jax.nn.sigmoid(x)


def _ada_kernel(c_ref, w_ref, b_ref, o_ref):
    o_ref[...] = jnp.dot(_silu(c_ref[...]), w_ref[...], preferred_element_type=F32) + b_ref[...]


def _ada(c, w_ada, b_ada):
    bp, d = c.shape
    n = w_ada.shape[1]
    return pl.pallas_call(
        _ada_kernel,
        grid=(n // d,),
        in_specs=[pl.BlockSpec((bp, d), lambda j: (0, 0)),
                  pl.BlockSpec((d, d), lambda j: (0, j)),
                  pl.BlockSpec((1, d), lambda j: (0, j))],
        out_specs=pl.BlockSpec((bp, d), lambda j: (0, j)),
        out_shape=jax.ShapeDtypeStruct((bp, n), F32),
        compiler_params=_cparams(("parallel",)),
        name="ada",
    )(c, w_ada, b_ada)


def _inproj_kernel(x_ref, mod_ref, g_ref, win_ref, cs_ref, v_ref, z_ref):
    x = x_ref[0]
    h = _rms(x) * g_ref[...]
    h = h * (1.0 + mod_ref[0, 1:2, :]) + mod_ref[0, 0:1, :]
    u = jnp.dot(h.astype(BF16), win_ref[...], preferred_element_type=F32)
    a = u[:, :D_CONV]
    gt = u[:, D_CONV:2 * D_CONV]
    z = u[:, 2 * D_CONV:]
    v_ref[0] = a * jax.nn.sigmoid(gt)
    z_ref[0] = jnp.dot(z.astype(BF16), cs_ref[...], preferred_element_type=F32)


def _inproj(x, mod3, g_mix_pre, w_in_b, cs_b, ts):
    b, s, d = x.shape
    return pl.pallas_call(
        _inproj_kernel,
        grid=(b, s // ts),
        in_specs=[pl.BlockSpec((1, ts, d), lambda i, t: (i, t, 0)),
                  pl.BlockSpec((1, 6, d), lambda i, t: (i, 0, 0)),
                  pl.BlockSpec((1, d), lambda i, t: (0, 0)),
                  pl.BlockSpec(w_in_b.shape, lambda i, t: (0, 0)),
                  pl.BlockSpec(cs_b.shape, lambda i, t: (0, 0))],
        out_specs=[pl.BlockSpec((1, ts, D_CONV), lambda i, t: (i, t, 0)),
                   pl.BlockSpec((1, ts, 2 * D_FNET), lambda i, t: (i, t, 0))],
        out_shape=[jax.ShapeDtypeStruct((b, s, D_CONV), F32),
                   jax.ShapeDtypeStruct((b, s, 2 * D_FNET), F32)],
        compiler_params=_cparams(("parallel", "parallel")),
        name="inproj",
    )(x, mod3, g_mix_pre, w_in_b, cs_b)


def _conv_kernel(vp_ref, v_ref, vn_ref, w_ref, b_ref, lg_ref, lb_ref, go_ref, o_ref, pad_ref, *, ts, rc):
    t = pl.program_id(1)
    nt = pl.num_programs(1)
    pad_ref[0:HALO, :] = jnp.where(t > 0, vp_ref[0], 0.0)
    pad_ref[HALO:HALO + ts, :] = v_ref[0]
    pad_ref[HALO + ts:HALO + ts + HALO, :] = jnp.where(t < nt - 1, vn_ref[0], 0.0)
    off = HALO - CONV_WIDTH // 2
    for c in range(ts // rc):
        r0 = c * rc
        acc = jnp.zeros((rc, D_CONV), F32) + b_ref[...]
        for j in range(CONV_WIDTH):
            acc = acc + pad_ref[r0 + off + j:r0 + off + j + rc, :] * w_ref[j:j + 1, :]
        mu = jnp.mean(acc, axis=-1, keepdims=True)
        xc = acc - mu
        var = jnp.mean(xc * xc, axis=-1, keepdims=True)
        y = xc * lax.rsqrt(var + EPS) * lg_ref[...] + lb_ref[...]
        y = _silu(y)
        y = _rms(y) * go_ref[...]
        o_ref[0, r0:r0 + rc, :] = y.astype(o_ref.dtype)


def _conv(v, conv_w, conv_b, ln_g, ln_b, g_out, ts, rc=32):
    b, s, c = v.shape
    hb = ts // HALO
    nh = s // HALO
    vec = pl.BlockSpec((1, c), lambda i, t: (0, 0))
    return pl.pallas_call(
        functools.partial(_conv_kernel, ts=ts, rc=rc),
        grid=(b, s // ts),
        in_specs=[pl.BlockSpec((1, HALO, c), lambda i, t: (i, jnp.maximum(t * hb - 1, 0), 0)),
                  pl.BlockSpec((1, ts, c), lambda i, t: (i, t, 0)),
                  pl.BlockSpec((1, HALO, c), lambda i, t: (i, jnp.minimum((t + 1) * hb, nh - 1), 0)),
                  pl.BlockSpec((CONV_WIDTH, c), lambda i, t: (0, 0)),
                  vec, vec, vec, vec],
        out_specs=pl.BlockSpec((1, ts, c), lambda i, t: (i, t, 0)),
        out_shape=jax.ShapeDtypeStruct((b, s, c), BF16),
        scratch_shapes=[pltpu.VMEM((ts + 2 * HALO, c), F32)],
        compiler_params=_cparams(("parallel", "parallel")),
        name="conv",
    )(v, v, v, conv_w, conv_b, ln_g, ln_b, g_out)


def _dft_tables(s):
    s1 = DFT_S1
    s2 = s // s1
    k1 = np.arange(s1)[None, :, None]
    p1 = np.arange(s1)[None, None, :]
    p2 = np.arange(s2)[:, None, None]
    ang = 2.0 * np.pi * ((k1 * (s2 * p1 + p2)) % s) / s
    g = np.concatenate([np.cos(ang), np.sin(ang)], axis=1)
    k2 = np.arange(s2)[:, None]
    q2 = np.arange(s2)[None, :]
    ang2 = 2.0 * np.pi * ((k2 * q2) % s2) / s2
    h = np.concatenate([np.cos(ang2), np.sin(ang2)], axis=1) / np.sqrt(s)
    return jnp.asarray(g, BF16), jnp.asarray(h, BF16)


def _seqdft_kernel(zc_ref, zs_ref, g_ref, h_ref, o_ref, scr_ref, *, s1, s2):
    ch = DFT_CH
    for p2 in range(s2):
        xc = zc_ref[0, pl.ds(p2, s1, stride=s2), :]
        xs = zs_ref[0, pl.ds(p2, s1, stride=s2), :]
        xx = jnp.concatenate([xc, xs], axis=-1).astype(BF16)
        r = jnp.dot(g_ref[p2], xx, preferred_element_type=F32)
        a_re = r[:s1, :ch] - r[s1:, ch:]
        a_im = -(r[:s1, ch:] + r[s1:, :ch])
        scr_ref[:, p2, :] = a_re
        scr_ref[:, s2 + p2, :] = a_im
    hmat = h_ref[...]
    for k1 in range(s1):
        y = jnp.dot(hmat, scr_ref[k1].astype(BF16), preferred_element_type=F32)
        o_ref[0, pl.ds(k1, s2, stride=s1), :] = y


def _seqdft(zcs, g_tab, h_tab):
    b, s, _ = zcs.shape
    s1 = DFT_S1
    s2 = s // s1
    nch = D_FNET // DFT_CH
    return pl.pallas_call(
        functools.partial(_seqdft_kernel, s1=s1, s2=s2),
        grid=(b, nch),
        in_specs=[pl.BlockSpec((1, s, DFT_CH), lambda i, c: (i, 0, c)),
                  pl.BlockSpec((1, s, DFT_CH), lambda i, c: (i, 0, nch + c)),
                  pl.BlockSpec(g_tab.shape, lambda i, c: (0, 0, 0)),
                  pl.BlockSpec(h_tab.shape, lambda i, c: (0, 0))],
        out_specs=pl.BlockSpec((1, s, DFT_CH), lambda i, c: (i, 0, c)),
        out_shape=jax.ShapeDtypeStruct((b, s, D_FNET), F32),
        scratch_shapes=[pltpu.VMEM((s1, 2 * s2, DFT_CH), F32)],
        compiler_params=_cparams(("parallel", "parallel")),
        name="seqdft",
    )(zcs, zcs, g_tab, h_tab)


def _mix_kernel(x_ref, cn_ref, fy_ref, mod_ref, gf_ref, wout_ref, gpost_ref, gpre_ref,
                wsg_ref, wsu_ref, wsd_ref, x1_ref, h2_ref, ysh_ref):
    fn = _rms(fy_ref[0]) * gf_ref[...]
    mixed = jnp.dot(cn_ref[0], wout_ref[:D_CONV, :], preferred_element_type=F32)
    mixed = mixed + jnp.dot(fn.astype(BF16), wout_ref[D_CONV:, :], preferred_element_type=F32)
    x1 = x_ref[0] + mod_ref[0, 2:3, :] * (_rms(mixed) * gpost_ref[...])
    x1_ref[0] = x1
    h2 = _rms(x1) * gpre_ref[...]
    h2 = h2 * (1.0 + mod_ref[0, 4:5, :]) + mod_ref[0, 3:4, :]
    h2_ref[0] = h2
    hb = h2.astype(BF16)
    hid = _silu(jnp.dot(hb, wsg_ref[...], preferred_element_type=F32))
    hid = hid * jnp.dot(hb, wsu_ref[...], preferred_element_type=F32)
    ysh_ref[0] = jnp.dot(hid.astype(BF16), wsd_ref[...], preferred_element_type=F32)


def _mix(x, cn, fy, mod3, g_fnet, w_out_b, g_post, g_pre, wsg_b, wsu_b, wsd_b, ts):
    b, s, d = x.shape
    tok = lambda c: pl.BlockSpec((1, ts, c), lambda i, t: (i, t, 0))
    full = lambda a: pl.BlockSpec(a.shape, lambda i, t: (0,) * a.ndim)
    return pl.pallas_call(
        _mix_kernel,
        grid=(b, s // ts),
        in_specs=[tok(d), tok(D_CONV), tok(D_FNET),
                  pl.BlockSpec((1, 6, d), lambda i, t: (i, 0, 0)),
                  full(g_fnet), full(w_out_b), full(g_post), full(g_pre),
                  full(wsg_b), full(wsu_b), full(wsd_b)],
        out_specs=[tok(d), tok(d), tok(d)],
        out_shape=[jax.ShapeDtypeStruct((b, s, d), F32)] * 3,
        compiler_params=_cparams(("parallel", "parallel")),
        name="mix",
    )(x, cn, fy, mod3, g_fnet, w_out_b, g_post, g_pre, wsg_b, wsu_b, wsd_b)


def _router_kernel(h_ref, wr_ref, br_ref, u_ref, idx_ref, rank_ref, wgt_ref, cnt_ref, carry_ref, *, tr):
    e = N_EXPERTS

    @pl.when((pl.program_id(0) == 0) & (pl.program_id(1) == 0))
    def _():
        carry_ref[...] = jnp.zeros_like(carry_ref)

    logits = lax.dot_general(wr_ref[...], h_ref[0].astype(BF16), (((1,), (1,)), ((), ())),
                             preferred_element_type=F32)
    sc = jax.nn.sigmoid(logits)
    sb = sc + br_ref[...]
    ninf = jnp.float32(-jnp.inf)

    io_g = lax.broadcasted_iota(jnp.int32, (GROUP_SIZE, tr), 0).astype(F32)
    gs = []
    for g in range(N_EXPERT_GROUPS):
        blk = sb[g * GROUP_SIZE:(g + 1) * GROUP_SIZE]
        m1 = jnp.max(blk, axis=0, keepdims=True)
        i1 = jnp.min(jnp.where(blk == m1, io_g, float(GROUP_SIZE)), axis=0, keepdims=True)
        m2 = jnp.max(jnp.where(io_g == i1, ninf, blk), axis=0, keepdims=True)
        gs.append(m1 + m2)
    masked = []
    for g in range(N_EXPERT_GROUPS):
        beat = jnp.zeros((1, tr), F32)
        for o in range(N_EXPERT_GROUPS):
            if o == g:
                continue
            wins = (gs[o] > gs[g]) | ((gs[o] == gs[g]) & (o < g))
            beat = beat + wins.astype(F32)
        keep = beat < float(TOPK_GROUPS)
        masked.append(jnp.where(keep, sb[g * GROUP_SIZE:(g + 1) * GROUP_SIZE], ninf))
    v = jnp.concatenate(masked, axis=0)

    io_e = lax.broadcasted_iota(jnp.int32, (e, tr), 0).astype(F32)
    ids, ws = [], []
    sel = jnp.zeros((e, tr), F32)
    for _ in range(TOP_K):
        m = jnp.max(v, axis=0, keepdims=True)
        i = jnp.min(jnp.where(v == m, io_e, float(e)), axis=0, keepdims=True)
        oh = io_e == i
        ids.append(i)
        ws.append(jnp.sum(jnp.where(oh, sc, 0.0), axis=0, keepdims=True))
        v = jnp.where(oh, ninf, v)
        sel = sel + oh.astype(F32)

    wsum = ws[0]
    for k in range(1, TOP_K):
        wsum = wsum + ws[k]
    wgt_ref[...] = jnp.concatenate([w / wsum * ROUTED_SCALE for w in ws], axis=0)
    idx_ref[...] = jnp.concatenate(ids, axis=0).astype(jnp.int32)

    excl = jnp.dot(sel.astype(BF16), u_ref[...], preferred_element_type=F32)
    base = carry_ref[:, 0:1]
    rank_full = base + excl
    ranks = [jnp.sum(jnp.where(io_e == ids[k], rank_full, 0.0), axis=0, keepdims=True)
             for k in range(TOP_K)]
    rank_ref[...] = jnp.concatenate(ranks, axis=0).astype(jnp.int32)
    new = base + jnp.sum(sel, axis=1, keepdims=True)
    carry_ref[...] = jnp.broadcast_to(new, carry_ref.shape)
    cnt_ref[...] = jnp.broadcast_to(new, cnt_ref.shape).astype(jnp.int32)


def _router(h2, wr_t_b, b_router_col, tr):
    b, s, d = h2.shape
    t = b * s
    nt = s // tr
    u = jnp.asarray(np.triu(np.ones((tr, tr), np.float32), k=1), BF16)
    col = lambda i, j: (0, i * nt + j)
    return pl.pallas_call(
        functools.partial(_router_kernel, tr=tr),
        grid=(b, nt),
        in_specs=[pl.BlockSpec((1, tr, d), lambda i, j: (i, j, 0)),
                  pl.BlockSpec(wr_t_b.shape, lambda i, j: (0, 0)),
                  pl.BlockSpec((N_EXPERTS, 1), lambda i, j: (0, 0)),
                  pl.BlockSpec((tr, tr), lambda i, j: (0, 0))],
        out_specs=[pl.BlockSpec((TOP_K, tr), col), pl.BlockSpec((TOP_K, tr), col),
                   pl.BlockSpec((TOP_K, tr), col),
                   pl.BlockSpec((N_EXPERTS, 128), lambda i, j: (0, 0))],
        out_shape=[jax.ShapeDtypeStruct((TOP_K, t), jnp.int32),
                   jax.ShapeDtypeStruct((TOP_K, t), jnp.int32),
                   jax.ShapeDtypeStruct((TOP_K, t), F32),
                   jax.ShapeDtypeStruct((N_EXPERTS, 128), jnp.int32)],
        scratch_shapes=[pltpu.VMEM((N_EXPERTS, 128), F32)],
        compiler_params=_cparams(("arbitrary", "arbitrary")),
        name="router",
    )(h2, wr_t_b, b_router_col, u)


def _dispatch_kernel(fill_start_ref, fill_n_ref, h_ref, dest_ref, x_hbm, zero_ref, sem, zsem, *, td):
    d = h_ref.shape[1]

    @pl.when(pl.program_id(0) == 0)
    def _():
        zero_ref[...] = jnp.zeros_like(zero_ref)

        def start_fill(ex, c):
            st = fill_start_ref[ex]

            def one(r, c2):
                row = pl.multiple_of(st + r * 8, 8)
                pltpu.make_async_copy(zero_ref, x_hbm.at[pl.ds(row, 8)], zsem).start()
                return c2
            return lax.fori_loop(0, fill_n_ref[ex], one, c)

        def wait_fill(ex, c):
            def one(r, c2):
                pltpu.make_async_copy(zero_ref, x_hbm.at[pl.ds(0, 8)], zsem).wait()
                return c2
            return lax.fori_loop(0, fill_n_ref[ex], one, c)

        lax.fori_loop(0, N_EXPERTS, start_fill, 0)
        lax.fori_loop(0, N_EXPERTS, wait_fill, 0)

    def body(i, c):
        for k in range(TOP_K):
            dst = dest_ref[k, i]
            pltpu.make_async_copy(h_ref.at[pl.ds(i, 1)], x_hbm.at[pl.ds(dst, 1)], sem).start()
        return c

    lax.fori_loop(0, td, body, 0)
    for _ in range(TOP_K):
        pltpu.make_async_copy(h_ref, x_hbm.at[pl.ds(0, td)], sem).wait()
    del d


def _dispatch(h2_flat, dest, fill_start, fill_n, p_rows, td):
    t, d = h2_flat.shape
    grid_spec = pltpu.PrefetchScalarGridSpec(
        num_scalar_prefetch=2,
        grid=(t // td,),
        in_specs=[pl.BlockSpec((td, d), lambda i, fs, fn: (i, 0)),
                  pl.BlockSpec((TOP_K, td), lambda i, fs, fn: (0, i), memory_space=pltpu.SMEM)],
        out_specs=pl.BlockSpec(memory_space=pl.ANY),
        scratch_shapes=[pltpu.VMEM((8, d), F32), pltpu.SemaphoreType.DMA, pltpu.SemaphoreType.DMA],
    )
    return pl.pallas_call(
        functools.partial(_dispatch_kernel, td=td),
        grid_spec=grid_spec,
        out_shape=jax.ShapeDtypeStruct((p_rows, d), F32),
        compiler_params=_cparams(("arbitrary",)),
        name="dispatch",
    )(fill_start, fill_n, h2_flat, dest)


def _experts_kernel(be_ref, nused_ref, x_ref, wg_ref, wu_ref, wd_ref, o_ref, wg_s, wu_s, wd_s):
    b = pl.program_id(0)

    @pl.when(b < nused_ref[0])
    def _():
        prev = be_ref[jnp.maximum(b - 1, 0)]

        @pl.when((b == 0) | (be_ref[b] != prev))
        def _():
            wg_s[...] = wg_ref[0].astype(BF16)
            wu_s[...] = wu_ref[0].astype(BF16)
            wd_s[...] = wd_ref[0].astype(BF16)

        x = x_ref[...].astype(BF16)
        g = jnp.dot(x, wg_s[...], preferred_element_type=F32)
        u = jnp.dot(x, wu_s[...], preferred_element_type=F32)
        hid = (_silu(g) * u).astype(BF16)
        o_ref[...] = jnp.dot(hid, wd_s[...], preferred_element_type=F32)


def _experts(xbuf, block_e, nused, w_gate, w_up, w_down):
    p, d = xbuf.shape
    nb = p // EXPERT_BM
    de = w_gate.shape[2]
    row = lambda b, be, nu: (jnp.minimum(b, nu[0] - 1), 0)
    wsel = lambda b, be, nu: (be[b], 0, 0)
    grid_spec = pltpu.PrefetchScalarGridSpec(
        num_scalar_prefetch=2,
        grid=(nb,),
        in_specs=[pl.BlockSpec((EXPERT_BM, d), row),
                  pl.BlockSpec((1, d, de), wsel),
                  pl.BlockSpec((1, d, de), wsel),
                  pl.BlockSpec((1, de, d), wsel)],
        out_specs=pl.BlockSpec((EXPERT_BM, d), row),
        scratch_shapes=[pltpu.VMEM((d, de), BF16), pltpu.VMEM((d, de), BF16), pltpu.VMEM((de, d), BF16)],
    )
    return pl.pallas_call(
        _experts_kernel,
        grid_spec=grid_spec,
        out_shape=jax.ShapeDtypeStruct((p, d), F32),
        compiler_params=_cparams(("arbitrary",)),
        name="experts",
    )(block_e, nused, xbuf, w_gate, w_up, w_down)


def _combine_kernel(dest_ref, w_ref, x1_ref, ysh_ref, mod_ref, g_ref, o_hbm, out_ref, buf_ref, sem, *, tc):
    def body(i, c):
        for k in range(TOP_K):
            src = dest_ref[k, i]
            pltpu.make_async_copy(o_hbm.at[pl.ds(src, 1)], buf_ref.at[k, pl.ds(i, 1)], sem).start()
        return c

    lax.fori_loop(0, tc, body, 0)
    for k in range(TOP_K):
        pltpu.make_async_copy(o_hbm.at[pl.ds(0, tc)], buf_ref.at[k], sem).wait()
    y = ysh_ref[...]
    for k in range(TOP_K):
        y = y + buf_ref[k] * w_ref[:, k:k + 1]
    out_ref[...] = x1_ref[...] + mod_ref[0, 5:6, :] * (_rms(y) * g_ref[...])


def _combine(dest, w_rows, x1_flat, ysh_flat, mod3, g_post, obuf, s, tc):
    t, d = x1_flat.shape
    per_seq = s // tc
    tok = pl.BlockSpec((tc, d), lambda i: (i, 0))
    return pl.pallas_call(
        functools.partial(_combine_kernel, tc=tc),
        grid=(t // tc,),
        in_specs=[pl.BlockSpec((TOP_K, tc), lambda i: (0, i), memory_space=pltpu.SMEM),
                  pl.BlockSpec((tc, TOP_K), lambda i: (i, 0)),
                  tok, tok,
                  pl.BlockSpec((1, 6, d), lambda i: (i // per_seq, 0, 0)),
                  pl.BlockSpec((1, d), lambda i: (0, 0)),
                  pl.BlockSpec(memory_space=pl.ANY)],
        out_specs=tok,
        out_shape=jax.ShapeDtypeStruct((t, d), F32),
        scratch_shapes=[pltpu.VMEM((TOP_K, tc, d), F32), pltpu.SemaphoreType.DMA],
        compiler_params=_cparams(("arbitrary",)),
        name="combine",
    )(dest, w_rows, x1_flat, ysh_flat, mod3, g_post, obuf)


def _channel_dft_table():
    c = np.arange(GROUP_DIM)
    ang = 2.0 * np.pi * ((c[:, None] * c[None, :]) % GROUP_DIM) / GROUP_DIM
    eye = np.eye(D_FNET // GROUP_DIM)
    scale = 1.0 / np.sqrt(GROUP_DIM)
    cs = np.concatenate([np.kron(eye, np.cos(ang)), np.kron(eye, np.sin(ang))], axis=1) * scale
    return jnp.asarray(cs, BF16)


def _layer(x, mod, p):
    b, s, d = x.shape
    t = b * s
    mod3 = mod.reshape(b, 6, d)
    ts = min(512, s)
    g_tab, h_tab = _dft_tables(s)

    v, zcs = _inproj(x, mod3, p["g_mix_pre"], p["w_in_b"], _channel_dft_table(), ts)
    cn = _conv(v, p["conv_w"], p["conv_b"], p["conv_ln_g"], p["conv_ln_b"], p["g_conv_out"], min(256, s))
    fy = _seqdft(zcs, g_tab, h_tab)
    x1, h2, ysh = _mix(x, cn, fy, mod3, p["g_fnet_out"], p["w_out_b"], p["g_mix_post"], p["g_ffn_pre"],
                       p["wsg_b"], p["wsu_b"], p["wsd_b"], min(256, s))
    idx, rank, wgt, cnt = _router(h2, p["wr_t_b"], p["b_router_col"], min(256, s))

    bm = EXPERT_BM
    counts = cnt[:, 0]
    pcounts = (counts + bm - 1) // bm * bm
    pends = jnp.cumsum(pcounts)
    pstarts = pends - pcounts
    dest = pstarts[idx] + rank
    n = t * TOP_K
    nb = (n + N_EXPERTS * (bm - 1) + bm - 1) // bm
    nused = (pends[-1] // bm).astype(jnp.int32)
    blk = jnp.arange(nb, dtype=jnp.int32)
    block_e = jnp.searchsorted(pends, jnp.minimum(blk, nused - 1) * bm, side="right").astype(jnp.int32)
    block_e = jnp.minimum(block_e, N_EXPERTS - 1)
    fill_from = counts // 8 * 8
    fill_start = (pstarts + fill_from).astype(jnp.int32)
    fill_n = ((pcounts - fill_from) // 8).astype(jnp.int32)

    xbuf = _dispatch(h2.reshape(t, d), dest, fill_start, fill_n, nb * bm, min(256, s))
    obuf = _experts(xbuf, block_e, nused.reshape(1), p["w_gate"], p["w_up"], p["w_down"])
    out = _combine(dest, wgt.T, x1.reshape(t, d), ysh.reshape(t, d), mod3, p["g_ffn_post"], obuf, s, 128)
    return out.reshape(b, s, d)


def kernel(x_prompt, x_sample, c_prompt, c_sample, w_ada, b_ada, g_mix_pre, w_in, conv_w, conv_b, conv_ln_g, conv_ln_b, g_conv_out, g_fnet_out, w_out, g_mix_post, g_ffn_pre, w_router, b_router, w_gate, w_up, w_down, ws_gate, ws_up, ws_down, g_ffn_post):
    assert w_ada.shape[0] == 1, "single-layer kernel"
    bp, bs = c_prompt.shape[0], c_sample.shape[0]
    rows = -(-(bp + bs) // 8) * 8
    c_all = jnp.zeros((rows, D_MODEL), F32).at[:bp].set(c_prompt).at[bp:bp + bs].set(c_sample)
    mod = _ada(c_all, w_ada[0], b_ada)
    p = {
        "g_mix_pre": g_mix_pre, "w_in_b": w_in[0].astype(BF16),
        "conv_w": conv_w[0], "conv_b": conv_b, "conv_ln_g": conv_ln_g, "conv_ln_b": conv_ln_b,
        "g_conv_out": g_conv_out, "g_fnet_out": g_fnet_out, "w_out_b": w_out[0].astype(BF16),
        "g_mix_post": g_mix_post, "g_ffn_pre": g_ffn_pre,
        "wr_t_b": w_router[0].T.astype(BF16), "b_router_col": b_router[0][:, None],
        "w_gate": w_gate[0], "w_up": w_up[0], "w_down": w_down[0],
        "wsg_b": ws_gate[0].astype(BF16), "wsu_b": ws_up[0].astype(BF16), "wsd_b": ws_down[0].astype(BF16),
        "g_ffn_post": g_ffn_post,
    }
    y_prompt = _layer(x_prompt, mod[:bp], p)
    y_sample = _layer(x_sample, mod[bp:bp + bs], p)
    return (y_prompt, y_sample)
```

```python
import functools

import numpy as np
import jax
import jax.numpy as jnp
from jax import lax
from jax.experimental import pallas as pl
from jax.experimental.pallas import tpu as pltpu

F32 = jnp.float32
BF16 = jnp.bfloat16

D_MODEL = 1024
D_CONV = 512
D_FNET = 512
GROUP_DIM = 64
CONV_WIDTH = 31
N_EXPERTS = 256
TOP_K = 8
N_EXPERT_GROUPS = 8
GROUP_SIZE = N_EXPERTS // N_EXPERT_GROUPS
TOPK_GROUPS = 4
ROUTED_SCALE = 2.5
EPS = 1e-6

DFT_S1 = 128
DFT_CH = 128
HALO = 16
EXPERT_BM = 256
VMEM_LIMIT = 56 * 1024 * 1024


def _cparams(sem, vmem=None):
    return pltpu.CompilerParams(dimension_semantics=sem, vmem_limit_bytes=vmem or VMEM_LIMIT)


def _rms(x):
    return x * lax.rsqrt(jnp.mean(x * x, axis=-1, keepdims=True) + EPS)


def _silu(x):
    return x * jax.nn.sigmoid(x)


def _pack_bf16_pairs(x):
    c = x.shape[-1] // 2
    bits = lax.bitcast_convert_type(x.astype(BF16).astype(F32), jnp.uint32)
    return (bits[:, :c] >> 16) | bits[:, c:]


def _unpack_bf16_pairs(p):
    lo = lax.bitcast_convert_type(p << 16, F32)
    hi = lax.bitcast_convert_type(p & jnp.uint32(0xFFFF0000), F32)
    return jnp.concatenate([lo, hi], axis=-1)


def _ada_kernel(c_ref, w_ref, b_ref, o_ref):
    o_ref[...] = jnp.dot(_silu(c_ref[...]), w_ref[...], preferred_element_type=F32) + b_ref[...]


def _ada(c, w_ada, b_ada):
    bp, d = c.shape
    n = w_ada.shape[1]
    return pl.pallas_call(
        _ada_kernel,
        grid=(n // d,),
        in_specs=[pl.BlockSpec((bp, d), lambda j: (0, 0)),
                  pl.BlockSpec((d, d), lambda j: (0, j)),
                  pl.BlockSpec((1, d), lambda j: (0, j))],
        out_specs=pl.BlockSpec((bp, d), lambda j: (0, j)),
        out_shape=jax.ShapeDtypeStruct((bp, n), F32),
        compiler_params=_cparams(("parallel",)),
        name="ada",
    )(c, w_ada, b_ada)


def _inproj_kernel(x_ref, mod_ref, g_ref, win_ref, cs_ref, v_ref, z_ref):
    x = x_ref[0]
    h = _rms(x) * g_ref[...]
    h = h * (1.0 + mod_ref[0, 1:2, :]) + mod_ref[0, 0:1, :]
    u = jnp.dot(h.astype(BF16), win_ref[...], preferred_element_type=F32)
    a = u[:, :D_CONV]
    gt = u[:, D_CONV:2 * D_CONV]
    z = u[:, 2 * D_CONV:]
    v_ref[0] = a * jax.nn.sigmoid(gt)
    z_ref[0] = jnp.dot(z.astype(BF16), cs_ref[...], preferred_element_type=F32)


def _inproj(x, mod3, g_mix_pre, w_in_b, cs_b, ts):
    b, s, d = x.shape
    return pl.pallas_call(
        _inproj_kernel,
        grid=(b, s // ts),
        in_specs=[pl.BlockSpec((1, ts, d), lambda i, t: (i, t, 0)),
                  pl.BlockSpec((1, 6, d), lambda i, t: (i, 0, 0)),
                  pl.BlockSpec((1, d), lambda i, t: (0, 0)),
                  pl.BlockSpec(w_in_b.shape, lambda i, t: (0, 0)),
                  pl.BlockSpec(cs_b.shape, lambda i, t: (0, 0))],
        out_specs=[pl.BlockSpec((1, ts, D_CONV), lambda i, t: (i, t, 0)),
                   pl.BlockSpec((1, ts, 2 * D_FNET), lambda i, t: (i, t, 0))],
        out_shape=[jax.ShapeDtypeStruct((b, s, D_CONV), F32),
                   jax.ShapeDtypeStruct((b, s, 2 * D_FNET), F32)],
        compiler_params=_cparams(("parallel", "parallel")),
        name="inproj",
    )(x, mod3, g_mix_pre, w_in_b, cs_b)


def _conv_kernel(vp_ref, v_ref, vn_ref, w_ref, b_ref, lg_ref, lb_ref, go_ref, o_ref, pad_ref, *, ts, rc):
    t = pl.program_id(1)
    nt = pl.num_programs(1)
    pad_ref[0:HALO, :] = jnp.where(t > 0, vp_ref[0], 0.0)
    pad_ref[HALO:HALO + ts, :] = v_ref[0]
    pad_ref[HALO + ts:HALO + ts + HALO, :] = jnp.where(t < nt - 1, vn_ref[0], 0.0)
    off = HALO - CONV_WIDTH // 2
    for c in range(ts // rc):
        r0 = c * rc
        acc = jnp.zeros((rc, D_CONV), F32) + b_ref[...]
        for j in range(CONV_WIDTH):
            acc = acc + pad_ref[r0 + off + j:r0 + off + j + rc, :] * w_ref[j:j + 1, :]
        mu = jnp.mean(acc, axis=-1, keepdims=True)
        xc = acc - mu
        var = jnp.mean(xc * xc, axis=-1, keepdims=True)
        y = xc * lax.rsqrt(var + EPS) * lg_ref[...] + lb_ref[...]
        y = _silu(y)
        y = _rms(y) * go_ref[...]
        o_ref[0, r0:r0 + rc, :] = y.astype(o_ref.dtype)


def _conv(v, conv_w, conv_b, ln_g, ln_b, g_out, ts, rc=32):
    b, s, c = v.shape
    hb = ts // HALO
    nh = s // HALO
    vec = pl.BlockSpec((1, c), lambda i, t: (0, 0))
    return pl.pallas_call(
        functools.partial(_conv_kernel, ts=ts, rc=rc),
        grid=(b, s // ts),
        in_specs=[pl.BlockSpec((1, HALO, c), lambda i, t: (i, jnp.maximum(t * hb - 1, 0), 0)),
                  pl.BlockSpec((1, ts, c), lambda i, t: (i, t, 0)),
                  pl.BlockSpec((1, HALO, c), lambda i, t: (i, jnp.minimum((t + 1) * hb, nh - 1), 0)),
                  pl.BlockSpec((CONV_WIDTH, c), lambda i, t: (0, 0)),
                  vec, vec, vec, vec],
        out_specs=pl.BlockSpec((1, ts, c), lambda i, t: (i, t, 0)),
        out_shape=jax.ShapeDtypeStruct((b, s, c), BF16),
        scratch_shapes=[pltpu.VMEM((ts + 2 * HALO, c), F32)],
        compiler_params=_cparams(("parallel", "parallel")),
        name="conv",
    )(v, v, v, conv_w, conv_b, ln_g, ln_b, g_out)


def _dft_tables(s):
    s1 = DFT_S1
    s2 = s // s1
    k1 = np.arange(s1)[None, :, None]
    p1 = np.arange(s1)[None, None, :]
    p2 = np.arange(s2)[:, None, None]
    ang = 2.0 * np.pi * ((k1 * (s2 * p1 + p2)) % s) / s
    g = np.concatenate([np.cos(ang), np.sin(ang)], axis=1)
    k2 = np.arange(s2)[:, None]
    q2 = np.arange(s2)[None, :]
    ang2 = 2.0 * np.pi * ((k2 * q2) % s2) / s2
    h = np.concatenate([np.cos(ang2), np.sin(ang2)], axis=1) / np.sqrt(s)
    return jnp.asarray(g, BF16), jnp.asarray(h, BF16)


def _seqdft_kernel(zc_ref, zs_ref, g_ref, h_ref, o_ref, scr_ref, *, s1, s2):
    ch = DFT_CH
    for p2 in range(s2):
        xc = zc_ref[0, pl.ds(p2, s1, stride=s2), :]
        xs = zs_ref[0, pl.ds(p2, s1, stride=s2), :]
        xx = jnp.concatenate([xc, xs], axis=-1).astype(BF16)
        r = jnp.dot(g_ref[p2], xx, preferred_element_type=F32)
        a_re = r[:s1, :ch] - r[s1:, ch:]
        a_im = -(r[:s1, ch:] + r[s1:, :ch])
        scr_ref[:, p2, :] = a_re
        scr_ref[:, s2 + p2, :] = a_im
    hmat = h_ref[...]
    for k1 in range(s1):
        y = jnp.dot(hmat, scr_ref[k1].astype(BF16), preferred_element_type=F32)
        o_ref[0, pl.ds(k1, s2, stride=s1), :] = y


def _seqdft(zcs, g_tab, h_tab):
    b, s, _ = zcs.shape
    s1 = DFT_S1
    s2 = s // s1
    nch = D_FNET // DFT_CH
    return pl.pallas_call(
        functools.partial(_seqdft_kernel, s1=s1, s2=s2),
        grid=(b, nch),
        in_specs=[pl.BlockSpec((1, s, DFT_CH), lambda i, c: (i, 0, c)),
                  pl.BlockSpec((1, s, DFT_CH), lambda i, c: (i, 0, nch + c)),
                  pl.BlockSpec(g_tab.shape, lambda i, c: (0, 0, 0)),
                  pl.BlockSpec(h_tab.shape, lambda i, c: (0, 0))],
        out_specs=pl.BlockSpec((1, s, DFT_CH), lambda i, c: (i, 0, c)),
        out_shape=jax.ShapeDtypeStruct((b, s, D_FNET), F32),
        scratch_shapes=[pltpu.VMEM((s1, 2 * s2, DFT_CH), F32)],
        compiler_params=_cparams(("parallel", "parallel")),
        name="seqdft",
    )(zcs, zcs, g_tab, h_tab)


def _mix_kernel(x_ref, cn_ref, fy_ref, mod_ref, gf_ref, wout_ref, gpost_ref, gpre_ref,
                wsg_ref, wsu_ref, wsd_ref, x1_ref, h2_ref, ysh_ref):
    fn = _rms(fy_ref[0]) * gf_ref[...]
    mixed = jnp.dot(cn_ref[0], wout_ref[:D_CONV, :], preferred_element_type=F32)
    mixed = mixed + jnp.dot(fn.astype(BF16), wout_ref[D_CONV:, :], preferred_element_type=F32)
    x1 = x_ref[0] + mod_ref[0, 2:3, :] * (_rms(mixed) * gpost_ref[...])
    x1_ref[0] = x1
    h2 = _rms(x1) * gpre_ref[...]
    h2 = h2 * (1.0 + mod_ref[0, 4:5, :]) + mod_ref[0, 3:4, :]
    h2_ref[0] = _pack_bf16_pairs(h2)
    hb = h2.astype(BF16)
    hid = _silu(jnp.dot(hb, wsg_ref[...], preferred_element_type=F32))
    hid = hid * jnp.dot(hb, wsu_ref[...], preferred_element_type=F32)
    ysh_ref[0] = jnp.dot(hid.astype(BF16), wsd_ref[...], preferred_element_type=F32)


def _mix(x, cn, fy, mod3, g_fnet, w_out_b, g_post, g_pre, wsg_b, wsu_b, wsd_b, ts):
    b, s, d = x.shape
    tok = lambda c: pl.BlockSpec((1, ts, c), lambda i, t: (i, t, 0))
    full = lambda a: pl.BlockSpec(a.shape, lambda i, t: (0,) * a.ndim)
    return pl.pallas_call(
        _mix_kernel,
        grid=(b, s // ts),
        in_specs=[tok(d), tok(D_CONV), tok(D_FNET),
                  pl.BlockSpec((1, 6, d), lambda i, t: (i, 0, 0)),
                  full(g_fnet), full(w_out_b), full(g_post), full(g_pre),
                  full(wsg_b), full(wsu_b), full(wsd_b)],
        out_specs=[tok(d), tok(d // 2), tok(d)],
        out_shape=[jax.ShapeDtypeStruct((b, s, d), F32),
                   jax.ShapeDtypeStruct((b, s, d // 2), jnp.uint32),
                   jax.ShapeDtypeStruct((b, s, d), F32)],
        compiler_params=_cparams(("parallel", "parallel")),
        name="mix",
    )(x, cn, fy, mod3, g_fnet, w_out_b, g_post, g_pre, wsg_b, wsu_b, wsd_b)


def _router_kernel(h_ref, wr_ref, br_ref, u_ref, idx_ref, rank_ref, wgt_ref, cnt_ref, carry_ref, *, tr):
    e = N_EXPERTS

    @pl.when((pl.program_id(0) == 0) & (pl.program_id(1) == 0))
    def _():
        carry_ref[...] = jnp.zeros_like(carry_ref)

    logits = lax.dot_general(wr_ref[...], _unpack_bf16_pairs(h_ref[0]).astype(BF16), (((1,), (1,)), ((), ())),
                             preferred_element_type=F32)
    sc = jax.nn.sigmoid(logits)
    sb = sc + br_ref[...]
    ninf = jnp.float32(-jnp.inf)

    io_g = lax.broadcasted_iota(jnp.int32, (GROUP_SIZE, tr), 0).astype(F32)
    gs = []
    for g in range(N_EXPERT_GROUPS):
        blk = sb[g * GROUP_SIZE:(g + 1) * GROUP_SIZE]
        m1 = jnp.max(blk, axis=0, keepdims=True)
        i1 = jnp.min(jnp.where(blk == m1, io_g, float(GROUP_SIZE)), axis=0, keepdims=True)
        m2 = jnp.max(jnp.where(io_g == i1, ninf, blk), axis=0, keepdims=True)
        gs.append(m1 + m2)
    masked = []
    for g in range(N_EXPERT_GROUPS):
        beat = jnp.zeros((1, tr), F32)
        for o in range(N_EXPERT_GROUPS):
            if o == g:
                continue
            wins = (gs[o] > gs[g]) | ((gs[o] == gs[g]) & (o < g))
            beat = beat + wins.astype(F32)
        keep = beat < float(TOPK_GROUPS)
        masked.append(jnp.where(keep, sb[g * GROUP_SIZE:(g + 1) * GROUP_SIZE], ninf))
    v = jnp.concatenate(masked, axis=0)

    io_e = lax.broadcasted_iota(jnp.int32, (e, tr), 0).astype(F32)
    ids, ws = [], []
    sel = jnp.zeros((e, tr), F32)
    for _ in range(TOP_K):
        m = jnp.max(v, axis=0, keepdims=True)
        i = jnp.min(jnp.where(v == m, io_e, float(e)), axis=0, keepdims=True)
        oh = io_e == i
        ids.append(i)
        ws.append(jnp.sum(jnp.where(oh, sc, 0.0), axis=0, keepdims=True))
        v = jnp.where(oh, ninf, v)
        sel = sel + oh.astype(F32)

    wsum = ws[0]
    for k in range(1, TOP_K):
        wsum = wsum + ws[k]
    wgt_ref[...] = jnp.concatenate([w / wsum * ROUTED_SCALE for w in ws], axis=0)
    idx_ref[...] = jnp.concatenate(ids, axis=0).astype(jnp.int32)

    excl = jnp.dot(sel.astype(BF16), u_ref[...], preferred_element_type=F32)
    base = carry_ref[:, 0:1]
    rank_full = base + excl
    ranks = [jnp.sum(jnp.where(io_e == ids[k], rank_full, 0.0), axis=0, keepdims=True)
             for k in range(TOP_K)]
    rank_ref[...] = jnp.concatenate(ranks, axis=0).astype(jnp.int32)
    new = base + jnp.sum(sel, axis=1, keepdims=True)
    carry_ref[...] = jnp.broadcast_to(new, carry_ref.shape)
    cnt_ref[...] = jnp.broadcast_to(new, cnt_ref.shape).astype(jnp.int32)


def _router(h2p, wr_t_b, b_router_col, tr):
    b, s, d = h2p.shape
    t = b * s
    nt = s // tr
    u = jnp.asarray(np.triu(np.ones((tr, tr), np.float32), k=1), BF16)
    col = lambda i, j: (0, i * nt + j)
    return pl.pallas_call(
        functools.partial(_router_kernel, tr=tr),
        grid=(b, nt),
        in_specs=[pl.BlockSpec((1, tr, d), lambda i, j: (i, j, 0)),
                  pl.BlockSpec(wr_t_b.shape, lambda i, j: (0, 0)),
                  pl.BlockSpec((N_EXPERTS, 1), lambda i, j: (0, 0)),
                  pl.BlockSpec((tr, tr), lambda i, j: (0, 0))],
        out_specs=[pl.BlockSpec((TOP_K, tr), col), pl.BlockSpec((TOP_K, tr), col),
                   pl.BlockSpec((TOP_K, tr), col),
                   pl.BlockSpec((N_EXPERTS, 128), lambda i, j: (0, 0))],
        out_shape=[jax.ShapeDtypeStruct((TOP_K, t), jnp.int32),
                   jax.ShapeDtypeStruct((TOP_K, t), jnp.int32),
                   jax.ShapeDtypeStruct((TOP_K, t), F32),
                   jax.ShapeDtypeStruct((N_EXPERTS, 128), jnp.int32)],
        scratch_shapes=[pltpu.VMEM((N_EXPERTS, 128), F32)],
        compiler_params=_cparams(("arbitrary", "arbitrary")),
        name="router",
    )(h2p, wr_t_b, b_router_col, u)


def _dest_kernel(pstart_ref, idx_ref, rank_ref, dest_ref):
    idx = idx_ref[...]

    def body(g, acc):
        for j in range(8):
            e = g * 8 + j
            acc = jnp.where(idx == e, pstart_ref[e], acc)
        return acc

    dest_ref[...] = lax.fori_loop(0, N_EXPERTS // 8, body, jnp.zeros_like(idx)) + rank_ref[...]


def _dest(pstarts, idx, rank, tl):
    k, t = idx.shape
    grid_spec = pltpu.PrefetchScalarGridSpec(
        num_scalar_prefetch=1,
        grid=(t // tl,),
        in_specs=[pl.BlockSpec((k, tl), lambda i, ps: (0, i)), pl.BlockSpec((k, tl), lambda i, ps: (0, i))],
        out_specs=pl.BlockSpec((k, tl), lambda i, ps: (0, i)),
    )
    return pl.pallas_call(
        _dest_kernel,
        grid_spec=grid_spec,
        out_shape=jax.ShapeDtypeStruct((k, t), jnp.int32),
        compiler_params=_cparams(("parallel",)),
        name="dest",
    )(pstarts, idx, rank)


def _dispatch_kernel(fill_start_ref, fill_n_ref, h_ref, dest_ref, x_hbm, zero_ref, sem, zsem, *, td):
    @pl.when(pl.program_id(0) == 0)
    def _():
        zero_ref[...] = jnp.zeros_like(zero_ref)

        def start_fill(ex, c):
            st = fill_start_ref[ex]

            def one(r, c2):
                row = pl.multiple_of(st + r * 8, 8)
                pltpu.make_async_copy(zero_ref, x_hbm.at[pl.ds(row, 8)], zsem).start()
                return c2
            return lax.fori_loop(0, fill_n_ref[ex], one, c)

        def wait_fill(ex, c):
            def one(r, c2):
                pltpu.make_async_copy(zero_ref, x_hbm.at[pl.ds(0, 8)], zsem).wait()
                return c2
            return lax.fori_loop(0, fill_n_ref[ex], one, c)

        lax.fori_loop(0, N_EXPERTS, start_fill, 0)
        lax.fori_loop(0, N_EXPERTS, wait_fill, 0)

    def body(i, c):
        for k in range(TOP_K):
            dst = dest_ref[k, i]
            pltpu.make_async_copy(h_ref.at[pl.ds(i, 1)], x_hbm.at[pl.ds(dst, 1)], sem).start()
        return c

    lax.fori_loop(0, td, body, 0)
    for _ in range(TOP_K):
        pltpu.make_async_copy(h_ref, x_hbm.at[pl.ds(0, td)], sem).wait()


def _dispatch(h2_flat, dest, fill_start, fill_n, p_rows, td):
    t, d = h2_flat.shape
    grid_spec = pltpu.PrefetchScalarGridSpec(
        num_scalar_prefetch=2,
        grid=(t // td,),
        in_specs=[pl.BlockSpec((td, d), lambda i, fs, fn: (i, 0)),
                  pl.BlockSpec((TOP_K, td), lambda i, fs, fn: (0, i), memory_space=pltpu.SMEM)],
        out_specs=pl.BlockSpec(memory_space=pl.ANY),
        scratch_shapes=[pltpu.VMEM((8, d), h2_flat.dtype), pltpu.SemaphoreType.DMA, pltpu.SemaphoreType.DMA],
    )
    return pl.pallas_call(
        functools.partial(_dispatch_kernel, td=td),
        grid_spec=grid_spec,
        out_shape=jax.ShapeDtypeStruct((p_rows, d), h2_flat.dtype),
        compiler_params=_cparams(("arbitrary",)),
        name="dispatch",
    )(fill_start, fill_n, h2_flat, dest)


def _experts_kernel(be_ref, nused_ref, x_ref, wg_ref, wu_ref, wd_ref, o_ref, wg_s, wu_s, wd_s):
    b = pl.program_id(0)

    @pl.when(b < nused_ref[0])
    def _():
        prev = be_ref[jnp.maximum(b - 1, 0)]

        @pl.when((b == 0) | (be_ref[b] != prev))
        def _():
            wg_s[...] = wg_ref[0].astype(BF16)
            wu_s[...] = wu_ref[0].astype(BF16)
            wd_s[...] = wd_ref[0].astype(BF16)

        x = _unpack_bf16_pairs(x_ref[...]).astype(BF16)
        g = jnp.dot(x, wg_s[...], preferred_element_type=F32)
        u = jnp.dot(x, wu_s[...], preferred_element_type=F32)
        hid = (_silu(g) * u).astype(BF16)
        o_ref[...] = _pack_bf16_pairs(jnp.dot(hid, wd_s[...], preferred_element_type=F32))

    @pl.when(b >= nused_ref[0])
    def _():
        o_ref[...] = jnp.zeros_like(o_ref)


def _experts(xbuf, block_e, nused, w_gate, w_up, w_down):
    p, dh = xbuf.shape
    nb = p // EXPERT_BM
    d, de = w_gate.shape[1:]
    row = lambda b, be, nu: (jnp.minimum(b, nu[0] - 1), 0)
    wsel = lambda b, be, nu: (be[b], 0, 0)
    grid_spec = pltpu.PrefetchScalarGridSpec(
        num_scalar_prefetch=2,
        grid=(nb,),
        in_specs=[pl.BlockSpec((EXPERT_BM, dh), row),
                  pl.BlockSpec((1, d, de), wsel),
                  pl.BlockSpec((1, d, de), wsel),
                  pl.BlockSpec((1, de, d), wsel)],
        out_specs=pl.BlockSpec((EXPERT_BM, dh), lambda b, be, nu: (b, 0)),
        scratch_shapes=[pltpu.VMEM((d, de), BF16), pltpu.VMEM((d, de), BF16), pltpu.VMEM((de, d), BF16)],
    )
    return pl.pallas_call(
        _experts_kernel,
        grid_spec=grid_spec,
        out_shape=jax.ShapeDtypeStruct((p, dh), jnp.uint32),
        compiler_params=_cparams(("arbitrary",)),
        name="experts",
    )(block_e, nused, xbuf, w_gate, w_up, w_down)


def _combine_kernel(dest_ref, w_ref, x1_ref, ysh_ref, mod_ref, g_ref, o_hbm, out_ref, buf_ref, sem, *, tc):
    def body(i, c):
        for k in range(TOP_K):
            src = dest_ref[k, i]
            pltpu.make_async_copy(o_hbm.at[pl.ds(src, 1)], buf_ref.at[k, pl.ds(i, 1)], sem).start()
        return c

    lax.fori_loop(0, tc, body, 0)
    for k in range(TOP_K):
        pltpu.make_async_copy(o_hbm.at[pl.ds(0, tc)], buf_ref.at[k], sem).wait()
    y = ysh_ref[...]
    for k in range(TOP_K):
        y = y + _unpack_bf16_pairs(buf_ref[k]) * w_ref[:, k:k + 1]
    out_ref[...] = x1_ref[...] + mod_ref[0, 5:6, :] * (_rms(y) * g_ref[...])


def _combine(dest, w_rows, x1_flat, ysh_flat, mod3, g_post, obuf, s, tc):
    t, d = x1_flat.shape
    per_seq = s // tc
    tok = pl.BlockSpec((tc, d), lambda i: (i, 0))
    return pl.pallas_call(
        functools.partial(_combine_kernel, tc=tc),
        grid=(t // tc,),
        in_specs=[pl.BlockSpec((TOP_K, tc), lambda i: (0, i), memory_space=pltpu.SMEM),
                  pl.BlockSpec((tc, TOP_K), lambda i: (i, 0)),
                  tok, tok,
                  pl.BlockSpec((1, 6, d), lambda i: (i // per_seq, 0, 0)),
                  pl.BlockSpec((1, d), lambda i: (0, 0)),
                  pl.BlockSpec(memory_space=pl.ANY)],
        out_specs=tok,
        out_shape=jax.ShapeDtypeStruct((t, d), F32),
        scratch_shapes=[pltpu.VMEM((TOP_K, tc, d // 2), jnp.uint32), pltpu.SemaphoreType.DMA],
        compiler_params=_cparams(("arbitrary",)),
        name="combine",
    )(dest, w_rows, x1_flat, ysh_flat, mod3, g_post, obuf)


def _channel_dft_table():
    c = np.arange(GROUP_DIM)
    ang = 2.0 * np.pi * ((c[:, None] * c[None, :]) % GROUP_DIM) / GROUP_DIM
    eye = np.eye(D_FNET // GROUP_DIM)
    scale = 1.0 / np.sqrt(GROUP_DIM)
    cs = np.concatenate([np.kron(eye, np.cos(ang)), np.kron(eye, np.sin(ang))], axis=1) * scale
    return jnp.asarray(cs, BF16)


def _layer(x, mod, p):
    b, s, d = x.shape
    t = b * s
    mod3 = mod.reshape(b, 6, d)
    ts = min(512, s)
    g_tab, h_tab = _dft_tables(s)

    v, zcs = _inproj(x, mod3, p["g_mix_pre"], p["w_in_b"], _channel_dft_table(), ts)
    cn = _conv(v, p["conv_w"], p["conv_b"], p["conv_ln_g"], p["conv_ln_b"], p["g_conv_out"], min(256, s))
    fy = _seqdft(zcs, g_tab, h_tab)
    x1, h2, ysh = _mix(x, cn, fy, mod3, p["g_fnet_out"], p["w_out_b"], p["g_mix_post"], p["g_ffn_pre"],
                       p["wsg_b"], p["wsu_b"], p["wsd_b"], min(256, s))
    idx, rank, wgt, cnt = _router(h2, p["wr_t_b"], p["b_router_col"], min(256, s))

    bm = EXPERT_BM
    counts = cnt[:, 0]
    pcounts = (counts + bm - 1) // bm * bm
    pends = jnp.cumsum(pcounts)
    pstarts = pends - pcounts
    dest = _dest(pstarts.astype(jnp.int32), idx, rank, min(2048, t))
    n = t * TOP_K
    nb = (n + N_EXPERTS * (bm - 1) + bm - 1) // bm
    nused = (pends[-1] // bm).astype(jnp.int32)
    blk = jnp.minimum(jnp.arange(nb, dtype=jnp.int32), nused - 1) * bm
    block_e = jnp.sum((pends[None, :] <= blk[:, None]).astype(jnp.int32), axis=1)
    block_e = jnp.minimum(block_e, N_EXPERTS - 1)
    fill_from = counts // 8 * 8
    fill_start = (pstarts + fill_from).astype(jnp.int32)
    fill_n = ((pcounts - fill_from) // 8).astype(jnp.int32)
    fill_n = fill_n.at[N_EXPERTS - 1].add((nb * bm - pends[-1]) // 8)

    xbuf = _dispatch(h2.reshape(t, d // 2), dest, fill_start, fill_n, nb * bm, min(256, s))
    obuf = _experts(xbuf, block_e, nused.reshape(1), p["w_gate"], p["w_up"], p["w_down"])
    out = _combine(dest, wgt.T, x1.reshape(t, d), ysh.reshape(t, d), mod3, p["g_ffn_post"], obuf, s, 128)
    return out.reshape(b, s, d)


def kernel(x_prompt, x_sample, c_prompt, c_sample, w_ada, b_ada, g_mix_pre, w_in, conv_w, conv_b, conv_ln_g, conv_ln_b, g_conv_out, g_fnet_out, w_out, g_mix_post, g_ffn_pre, w_router, b_router, w_gate, w_up, w_down, ws_gate, ws_up, ws_down, g_ffn_post):
    assert w_ada.shape[0] == 1, "single-layer kernel"
    bp, bs = c_prompt.shape[0], c_sample.shape[0]
    rows = -(-(bp + bs) // 8) * 8
    c_all = jnp.zeros((rows, D_MODEL), F32).at[:bp].set(c_prompt).at[bp:bp + bs].set(c_sample)
    mod = _ada(c_all, w_ada[0], b_ada)
    p = {
        "g_mix_pre": g_mix_pre, "w_in_b": w_in[0].astype(BF16),
        "conv_w": conv_w[0], "conv_b": conv_b, "conv_ln_g": conv_ln_g, "conv_ln_b": conv_ln_b,
        "g_conv_out": g_conv_out, "g_fnet_out": g_fnet_out, "w_out_b": w_out[0].astype(BF16),
        "g_mix_post": g_mix_post, "g_ffn_pre": g_ffn_pre,
        "wr_t_b": w_router[0].T.astype(BF16), "b_router_col": b_router[0][:, None],
        "w_gate": w_gate[0], "w_up": w_up[0], "w_down": w_down[0],
        "wsg_b": ws_gate[0].astype(BF16), "wsu_b": ws_up[0].astype(BF16), "wsd_b": ws_down[0].astype(BF16),
        "g_ffn_post": g_ffn_post,
    }
    y_prompt = _layer(x_prompt, mod[:bp], p)
    y_sample = _layer(x_sample, mod[bp:bp + bs], p)
    return (y_prompt, y_sample)
```

```python
import functools

import numpy as np
import jax
import jax.numpy as jnp
from jax import lax
from jax.experimental import pallas as pl
from jax.experimental.pallas import tpu as pltpu
from jax.experimental.pallas import tpu_sc as plsc

F32 = jnp.float32
BF16 = jnp.bfloat16

D_MODEL = 1024
D_CONV = 512
D_FNET = 512
GROUP_DIM = 64
CONV_WIDTH = 31
N_EXPERTS = 256
TOP_K = 8
N_EXPERT_GROUPS = 8
GROUP_SIZE = N_EXPERTS // N_EXPERT_GROUPS
TOPK_GROUPS = 4
ROUTED_SCALE = 2.5
EPS = 1e-6

DFT_S1 = 128
DFT_CH = 128
HALO = 16
VMEM_LIMIT = 56 * 1024 * 1024
SC_CORES = 2
SC_SUBCORES = 16
SC_WORKERS = SC_CORES * SC_SUBCORES
SC_ROWS = 64


def _cparams(sem, vmem=None):
    return pltpu.CompilerParams(dimension_semantics=sem, vmem_limit_bytes=vmem or VMEM_LIMIT)


def _rms(x):
    return x * lax.rsqrt(jnp.mean(x * x, axis=-1, keepdims=True) + EPS)


def _silu(x):
    return x * jax.nn.sigmoid(x)


def _pack_bf16_pairs(x):
    c = x.shape[-1] // 2
    bits = lax.bitcast_convert_type(x.astype(BF16).astype(F32), jnp.uint32)
    return (bits[:, :c] >> 16) | bits[:, c:]


def _unpack_bf16_pairs(p):
    lo = lax.bitcast_convert_type(p << 16, F32)
    hi = lax.bitcast_convert_type(p & jnp.uint32(0xFFFF0000), F32)
    return jnp.concatenate([lo, hi], axis=-1)


def _ada_kernel(c_ref, w_ref, b_ref, o_ref):
    o_ref[...] = jnp.dot(_silu(c_ref[...]), w_ref[...], preferred_element_type=F32) + b_ref[...]


def _ada(c, w_ada, b_ada):
    bp, d = c.shape
    n = w_ada.shape[1]
    return pl.pallas_call(
        _ada_kernel,
        grid=(n // d,),
        in_specs=[pl.BlockSpec((bp, d), lambda j: (0, 0)),
                  pl.BlockSpec((d, d), lambda j: (0, j)),
                  pl.BlockSpec((1, d), lambda j: (0, j))],
        out_specs=pl.BlockSpec((bp, d), lambda j: (0, j)),
        out_shape=jax.ShapeDtypeStruct((bp, n), F32),
        compiler_params=_cparams(("parallel",)),
        name="ada",
    )(c, w_ada, b_ada)


def _inproj_kernel(x_ref, mod_ref, g_ref, win_ref, cs_ref, v_ref, z_ref):
    x = x_ref[0]
    h = _rms(x) * g_ref[...]
    h = h * (1.0 + mod_ref[0, 1:2, :]) + mod_ref[0, 0:1, :]
    u = jnp.dot(h.astype(BF16), win_ref[...], preferred_element_type=F32)
    a = u[:, :D_CONV]
    gt = u[:, D_CONV:2 * D_CONV]
    z = u[:, 2 * D_CONV:]
    v_ref[0] = a * jax.nn.sigmoid(gt)
    z_ref[0] = jnp.dot(z.astype(BF16), cs_ref[...], preferred_element_type=F32)


def _inproj(x, mod3, g_mix_pre, w_in_b, cs_b, ts):
    b, s, d = x.shape
    return pl.pallas_call(
        _inproj_kernel,
        grid=(b, s // ts),
        in_specs=[pl.BlockSpec((1, ts, d), lambda i, t: (i, t, 0)),
                  pl.BlockSpec((1, 6, d), lambda i, t: (i, 0, 0)),
                  pl.BlockSpec((1, d), lambda i, t: (0, 0)),
                  pl.BlockSpec(w_in_b.shape, lambda i, t: (0, 0)),
                  pl.BlockSpec(cs_b.shape, lambda i, t: (0, 0))],
        out_specs=[pl.BlockSpec((1, ts, D_CONV), lambda i, t: (i, t, 0)),
                   pl.BlockSpec((1, ts, 2 * D_FNET), lambda i, t: (i, t, 0))],
        out_shape=[jax.ShapeDtypeStruct((b, s, D_CONV), F32),
                   jax.ShapeDtypeStruct((b, s, 2 * D_FNET), F32)],
        compiler_params=_cparams(("parallel", "parallel")),
        name="inproj",
    )(x, mod3, g_mix_pre, w_in_b, cs_b)


def _conv_kernel(vp_ref, v_ref, vn_ref, w_ref, b_ref, lg_ref, lb_ref, go_ref, o_ref, pad_ref, *, ts, rc):
    t = pl.program_id(1)
    nt = pl.num_programs(1)
    pad_ref[0:HALO, :] = jnp.where(t > 0, vp_ref[0], 0.0)
    pad_ref[HALO:HALO + ts, :] = v_ref[0]
    pad_ref[HALO + ts:HALO + ts + HALO, :] = jnp.where(t < nt - 1, vn_ref[0], 0.0)
    off = HALO - CONV_WIDTH // 2
    for c in range(ts // rc):
        r0 = c * rc
        acc = jnp.zeros((rc, D_CONV), F32) + b_ref[...]
        for j in range(CONV_WIDTH):
            acc = acc + pad_ref[r0 + off + j:r0 + off + j + rc, :] * w_ref[j:j + 1, :]
        mu = jnp.mean(acc, axis=-1, keepdims=True)
        xc = acc - mu
        var = jnp.mean(xc * xc, axis=-1, keepdims=True)
        y = xc * lax.rsqrt(var + EPS) * lg_ref[...] + lb_ref[...]
        y = _silu(y)
        y = _rms(y) * go_ref[...]
        o_ref[0, r0:r0 + rc, :] = y.astype(o_ref.dtype)


def _conv(v, conv_w, conv_b, ln_g, ln_b, g_out, ts, rc=32):
    b, s, c = v.shape
    hb = ts // HALO
    nh = s // HALO
    vec = pl.BlockSpec((1, c), lambda i, t: (0, 0))
    return pl.pallas_call(
        functools.partial(_conv_kernel, ts=ts, rc=rc),
        grid=(b, s // ts),
        in_specs=[pl.BlockSpec((1, HALO, c), lambda i, t: (i, jnp.maximum(t * hb - 1, 0), 0)),
                  pl.BlockSpec((1, ts, c), lambda i, t: (i, t, 0)),
                  pl.BlockSpec((1, HALO, c), lambda i, t: (i, jnp.minimum((t + 1) * hb, nh - 1), 0)),
                  pl.BlockSpec((CONV_WIDTH, c), lambda i, t: (0, 0)),
                  vec, vec, vec, vec],
        out_specs=pl.BlockSpec((1, ts, c), lambda i, t: (i, t, 0)),
        out_shape=jax.ShapeDtypeStruct((b, s, c), BF16),
        scratch_shapes=[pltpu.VMEM((ts + 2 * HALO, c), F32)],
        compiler_params=_cparams(("parallel", "parallel")),
        name="conv",
    )(v, v, v, conv_w, conv_b, ln_g, ln_b, g_out)


def _dft_tables(s):
    s1 = DFT_S1
    s2 = s // s1
    k1 = np.arange(s1)[None, :, None]
    p1 = np.arange(s1)[None, None, :]
    p2 = np.arange(s2)[:, None, None]
    ang = 2.0 * np.pi * ((k1 * (s2 * p1 + p2)) % s) / s
    g = np.concatenate([np.cos(ang), np.sin(ang)], axis=1)
    k2 = np.arange(s2)[:, None]
    q2 = np.arange(s2)[None, :]
    ang2 = 2.0 * np.pi * ((k2 * q2) % s2) / s2
    h = np.concatenate([np.cos(ang2), np.sin(ang2)], axis=1) / np.sqrt(s)
    return jnp.asarray(g, BF16), jnp.asarray(h, BF16)


def _seqdft_kernel(zc_ref, zs_ref, g_ref, h_ref, o_ref, scr_ref, *, s1, s2):
    ch = DFT_CH
    for p2 in range(s2):
        xc = zc_ref[0, pl.ds(p2, s1, stride=s2), :]
        xs = zs_ref[0, pl.ds(p2, s1, stride=s2), :]
        xx = jnp.concatenate([xc, xs], axis=-1).astype(BF16)
        r = jnp.dot(g_ref[p2], xx, preferred_element_type=F32)
        a_re = r[:s1, :ch] - r[s1:, ch:]
        a_im = -(r[:s1, ch:] + r[s1:, :ch])
        scr_ref[:, p2, :] = a_re
        scr_ref[:, s2 + p2, :] = a_im
    hmat = h_ref[...]
    for k1 in range(s1):
        y = jnp.dot(hmat, scr_ref[k1].astype(BF16), preferred_element_type=F32)
        o_ref[0, pl.ds(k1, s2, stride=s1), :] = y


def _seqdft(zcs, g_tab, h_tab):
    b, s, _ = zcs.shape
    s1 = DFT_S1
    s2 = s // s1
    nch = D_FNET // DFT_CH
    return pl.pallas_call(
        functools.partial(_seqdft_kernel, s1=s1, s2=s2),
        grid=(b, nch),
        in_specs=[pl.BlockSpec((1, s, DFT_CH), lambda i, c: (i, 0, c)),
                  pl.BlockSpec((1, s, DFT_CH), lambda i, c: (i, 0, nch + c)),
                  pl.BlockSpec(g_tab.shape, lambda i, c: (0, 0, 0)),
                  pl.BlockSpec(h_tab.shape, lambda i, c: (0, 0))],
        out_specs=pl.BlockSpec((1, s, DFT_CH), lambda i, c: (i, 0, c)),
        out_shape=jax.ShapeDtypeStruct((b, s, D_FNET), F32),
        scratch_shapes=[pltpu.VMEM((s1, 2 * s2, DFT_CH), F32)],
        compiler_params=_cparams(("parallel", "parallel")),
        name="seqdft",
    )(zcs, zcs, g_tab, h_tab)


def _mix_kernel(x_ref, cn_ref, fy_ref, mod_ref, gf_ref, wout_ref, gpost_ref, gpre_ref,
                wsg_ref, wsu_ref, wsd_ref, x1_ref, h2_ref, ysh_ref):
    fn = _rms(fy_ref[0]) * gf_ref[...]
    mixed = jnp.dot(cn_ref[0], wout_ref[:D_CONV, :], preferred_element_type=F32)
    mixed = mixed + jnp.dot(fn.astype(BF16), wout_ref[D_CONV:, :], preferred_element_type=F32)
    x1 = x_ref[0] + mod_ref[0, 2:3, :] * (_rms(mixed) * gpost_ref[...])
    x1_ref[0] = x1
    h2 = _rms(x1) * gpre_ref[...]
    h2 = h2 * (1.0 + mod_ref[0, 4:5, :]) + mod_ref[0, 3:4, :]
    h2_ref[0] = _pack_bf16_pairs(h2)
    hb = h2.astype(BF16)
    hid = _silu(jnp.dot(hb, wsg_ref[...], preferred_element_type=F32))
    hid = hid * jnp.dot(hb, wsu_ref[...], preferred_element_type=F32)
    ysh_ref[0] = jnp.dot(hid.astype(BF16), wsd_ref[...], preferred_element_type=F32)


def _mix(x, cn, fy, mod3, g_fnet, w_out_b, g_post, g_pre, wsg_b, wsu_b, wsd_b, ts):
    b, s, d = x.shape
    tok = lambda c: pl.BlockSpec((1, ts, c), lambda i, t: (i, t, 0))
    full = lambda a: pl.BlockSpec(a.shape, lambda i, t: (0,) * a.ndim)
    return pl.pallas_call(
        _mix_kernel,
        grid=(b, s // ts),
        in_specs=[tok(d), tok(D_CONV), tok(D_FNET),
                  pl.BlockSpec((1, 6, d), lambda i, t: (i, 0, 0)),
                  full(g_fnet), full(w_out_b), full(g_post), full(g_pre),
                  full(wsg_b), full(wsu_b), full(wsd_b)],
        out_specs=[tok(d), tok(d // 2), tok(d)],
        out_shape=[jax.ShapeDtypeStruct((b, s, d), F32),
                   jax.ShapeDtypeStruct((b, s, d // 2), jnp.uint32),
                   jax.ShapeDtypeStruct((b, s, d), F32)],
        compiler_params=_cparams(("parallel", "parallel")),
        name="mix",
    )(x, cn, fy, mod3, g_fnet, w_out_b, g_post, g_pre, wsg_b, wsu_b, wsd_b)


def _router_kernel(h_ref, wr_ref, br_ref, u_ref, idx_ref, rank_ref, wgt_ref, cnt_ref, carry_ref, *, tr):
    e = N_EXPERTS

    @pl.when((pl.program_id(0) == 0) & (pl.program_id(1) == 0))
    def _():
        carry_ref[...] = jnp.zeros_like(carry_ref)

    logits = lax.dot_general(wr_ref[...], _unpack_bf16_pairs(h_ref[0]).astype(BF16), (((1,), (1,)), ((), ())),
                             preferred_element_type=F32)
    sc = jax.nn.sigmoid(logits)
    sb = sc + br_ref[...]
    ninf = jnp.float32(-jnp.inf)

    io_g = lax.broadcasted_iota(jnp.int32, (GROUP_SIZE, tr), 0).astype(F32)
    gs = []
    for g in range(N_EXPERT_GROUPS):
        blk = sb[g * GROUP_SIZE:(g + 1) * GROUP_SIZE]
        m1 = jnp.max(blk, axis=0, keepdims=True)
        i1 = jnp.min(jnp.where(blk == m1, io_g, float(GROUP_SIZE)), axis=0, keepdims=True)
        m2 = jnp.max(jnp.where(io_g == i1, ninf, blk), axis=0, keepdims=True)
        gs.append(m1 + m2)
    masked = []
    for g in range(N_EXPERT_GROUPS):
        beat = jnp.zeros((1, tr), F32)
        for o in range(N_EXPERT_GROUPS):
            if o == g:
                continue
            wins = (gs[o] > gs[g]) | ((gs[o] == gs[g]) & (o < g))
            beat = beat + wins.astype(F32)
        keep = beat < float(TOPK_GROUPS)
        masked.append(jnp.where(keep, sb[g * GROUP_SIZE:(g + 1) * GROUP_SIZE], ninf))
    v = jnp.concatenate(masked, axis=0)

    io_e = lax.broadcasted_iota(jnp.int32, (e, tr), 0).astype(F32)
    ids, ws = [], []
    sel = jnp.zeros((e, tr), F32)
    for _ in range(TOP_K):
        m = jnp.max(v, axis=0, keepdims=True)
        i = jnp.min(jnp.where(v == m, io_e, float(e)), axis=0, keepdims=True)
        oh = io_e == i
        ids.append(i)
        ws.append(jnp.sum(jnp.where(oh, sc, 0.0), axis=0, keepdims=True))
        v = jnp.where(oh, ninf, v)
        sel = sel + oh.astype(F32)

    wsum = ws[0]
    for k in range(1, TOP_K):
        wsum = wsum + ws[k]
    wgt_ref[...] = jnp.concatenate([w / wsum * ROUTED_SCALE for w in ws], axis=0)
    idx_ref[...] = jnp.concatenate(ids, axis=0).astype(jnp.int32)

    excl = jnp.dot(sel.astype(BF16), u_ref[...], preferred_element_type=F32)
    base = carry_ref[:, 0:1]
    rank_full = base + excl
    ranks = [jnp.sum(jnp.where(io_e == ids[k], rank_full, 0.0), axis=0, keepdims=True)
             for k in range(TOP_K)]
    rank_ref[...] = jnp.concatenate(ranks, axis=0).astype(jnp.int32)
    new = base + jnp.sum(sel, axis=1, keepdims=True)
    carry_ref[...] = jnp.broadcast_to(new, carry_ref.shape)
    cnt_ref[...] = jnp.broadcast_to(new, cnt_ref.shape).astype(jnp.int32)


def _router(h2p, wr_t_b, b_router_col, tr):
    b, s, d = h2p.shape
    t = b * s
    nt = s // tr
    u = jnp.asarray(np.triu(np.ones((tr, tr), np.float32), k=1), BF16)
    col = lambda i, j: (0, i * nt + j)
    return pl.pallas_call(
        functools.partial(_router_kernel, tr=tr),
        grid=(b, nt),
        in_specs=[pl.BlockSpec((1, tr, d), lambda i, j: (i, j, 0)),
                  pl.BlockSpec(wr_t_b.shape, lambda i, j: (0, 0)),
                  pl.BlockSpec((N_EXPERTS, 1), lambda i, j: (0, 0)),
                  pl.BlockSpec((tr, tr), lambda i, j: (0, 0))],
        out_specs=[pl.BlockSpec((TOP_K, tr), col), pl.BlockSpec((TOP_K, tr), col),
                   pl.BlockSpec((TOP_K, tr), col),
                   pl.BlockSpec((N_EXPERTS, 128), lambda i, j: (0, 0))],
        out_shape=[jax.ShapeDtypeStruct((TOP_K, t), jnp.int32),
                   jax.ShapeDtypeStruct((TOP_K, t), jnp.int32),
                   jax.ShapeDtypeStruct((TOP_K, t), F32),
                   jax.ShapeDtypeStruct((N_EXPERTS, 128), jnp.int32)],
        scratch_shapes=[pltpu.VMEM((N_EXPERTS, 128), F32)],
        compiler_params=_cparams(("arbitrary", "arbitrary")),
        name="router",
    )(h2p, wr_t_b, b_router_col, u)


def _dest_kernel(pstart_ref, idx_ref, rank_ref, dest_ref):
    idx = idx_ref[...]

    def body(g, acc):
        for j in range(8):
            e = g * 8 + j
            acc = jnp.where(idx == e, pstart_ref[e], acc)
        return acc

    dest_ref[...] = lax.fori_loop(0, N_EXPERTS // 8, body, jnp.zeros_like(idx)) + rank_ref[...]


def _dest(pstarts, idx, rank, tl):
    k, t = idx.shape
    grid_spec = pltpu.PrefetchScalarGridSpec(
        num_scalar_prefetch=1,
        grid=(t // tl,),
        in_specs=[pl.BlockSpec((k, tl), lambda i, ps: (0, i)), pl.BlockSpec((k, tl), lambda i, ps: (0, i))],
        out_specs=pl.BlockSpec((k, tl), lambda i, ps: (0, i)),
    )
    return pl.pallas_call(
        _dest_kernel,
        grid_spec=grid_spec,
        out_shape=jax.ShapeDtypeStruct((k, t), jnp.int32),
        compiler_params=_cparams(("parallel",)),
        name="dest",
    )(pstarts, idx, rank)


def _sc_mesh():
    return plsc.VectorSubcoreMesh(core_axis_name="c", subcore_axis_name="s",
                                  num_cores=SC_CORES, num_subcores=SC_SUBCORES)


def _sc_worker_base(per_worker):
    return (lax.axis_index("s") * SC_CORES + lax.axis_index("c")) * per_worker


def _dispatch(h2_flat, dest, p_rows):
    t, dh = h2_flat.shape
    r = SC_ROWS
    per_w = t // SC_WORKERS
    nchunk = per_w // r
    assert per_w % (2 * r) == 0

    @functools.partial(
        pl.kernel, mesh=_sc_mesh(),
        out_type=jax.ShapeDtypeStruct((p_rows, dh), h2_flat.dtype),
        scratch_types=[pltpu.VMEM((2, TOP_K, r), jnp.int32), pltpu.VMEM((2, r, dh), h2_flat.dtype),
                       pltpu.SemaphoreType.DMA((2,)), pltpu.SemaphoreType.DMA((2,))],
        name="sc_dispatch",
    )
    def k(rows_hbm, dest_hbm, out_hbm, idx_v, rows_v, lsem, ssem):
        base = _sc_worker_base(per_w)

        def load(ci, slot):
            t0 = base + ci * r
            for kk in range(TOP_K):
                pltpu.sync_copy(dest_hbm.at[kk, pl.ds(t0, r)], idx_v.at[slot, kk])
            pltpu.async_copy(rows_hbm.at[pl.ds(t0, r)], rows_v.at[slot], lsem.at[slot])

        def scatter(ci, slot):
            t0 = base + ci * r
            pltpu.make_async_copy(rows_hbm.at[pl.ds(t0, r)], rows_v.at[slot], lsem.at[slot]).wait()
            for kk in range(TOP_K):
                pltpu.async_copy(rows_v.at[slot], out_hbm.at[idx_v.at[slot, kk]], ssem.at[slot])

        def drain(slot):
            for kk in range(TOP_K):
                pltpu.make_async_copy(rows_v.at[slot], out_hbm.at[idx_v.at[slot, kk]], ssem.at[slot]).wait()

        load(0, 0)

        @pl.loop(0, nchunk, step=2)
        def _(c0):
            for s in range(2):
                ci = c0 + s

                @pl.when(ci + 1 < nchunk)
                def _():
                    @pl.when(ci >= 1)
                    def _():
                        drain(1 - s)
                    load(ci + 1, 1 - s)

                scatter(ci, s)

        drain(0)
        drain(1)

    return k(h2_flat, dest)


def _gather(obuf, dest):
    _, dh = obuf.shape
    kk_n, t = dest.shape
    r = SC_ROWS
    per_w = t // SC_WORKERS
    nitem = (per_w // r) * kk_n
    assert per_w % r == 0 and nitem % 2 == 0

    @functools.partial(
        pl.kernel, mesh=_sc_mesh(),
        out_type=jax.ShapeDtypeStruct((kk_n, t, dh), obuf.dtype),
        scratch_types=[pltpu.VMEM((2, r), jnp.int32), pltpu.VMEM((2, r, dh), obuf.dtype),
                       pltpu.SemaphoreType.DMA((2,)), pltpu.SemaphoreType.DMA((2,))],
        name="sc_gather",
    )
    def k(table_hbm, idx_hbm, out_hbm, idx_v, rows_v, gsem, wsem):
        base = _sc_worker_base(per_w)

        def item(i):
            return i % kk_n, base + (i // kk_n) * r

        def start_gather(i, slot):
            kk, t0 = item(i)
            pltpu.sync_copy(idx_hbm.at[kk, pl.ds(t0, r)], idx_v.at[slot])
            pltpu.async_copy(table_hbm.at[idx_v.at[slot]], rows_v.at[slot], gsem.at[slot])

        def finish(i, slot):
            kk, t0 = item(i)
            pltpu.make_async_copy(table_hbm.at[idx_v.at[slot]], rows_v.at[slot], gsem.at[slot]).wait()
            pltpu.async_copy(rows_v.at[slot], out_hbm.at[kk, pl.ds(t0, r)], wsem.at[slot])

        def wait_write(i, slot):
            kk, t0 = item(i)
            pltpu.make_async_copy(rows_v.at[slot], out_hbm.at[kk, pl.ds(t0, r)], wsem.at[slot]).wait()

        start_gather(0, 0)

        @pl.loop(0, nitem, step=2)
        def _(i0):
            for s in range(2):
                i = i0 + s

                @pl.when(i + 1 < nitem)
                def _():
                    @pl.when(i >= 1)
                    def _():
                        wait_write(i - 1, 1 - s)
                    start_gather(i + 1, 1 - s)

                finish(i, s)

        wait_write(nitem - 2, 0)
        wait_write(nitem - 1, 1)

    return k(obuf, dest)


def _experts_kernel(be_ref, nvalid_ref, nused_ref, x_ref, wg_ref, wu_ref, wd_ref, o_ref, wg_s, wu_s, wd_s):
    b = pl.program_id(0)

    @pl.when(b < nused_ref[0])
    def _():
        prev = be_ref[jnp.maximum(b - 1, 0)]

        @pl.when((b == 0) | (be_ref[b] != prev))
        def _():
            wg_s[...] = wg_ref[0].astype(BF16)
            wu_s[...] = wu_ref[0].astype(BF16)
            wd_s[...] = wd_ref[0].astype(BF16)

        rows = lax.broadcasted_iota(jnp.int32, x_ref.shape, 0)
        xp = jnp.where(rows < nvalid_ref[b], x_ref[...], jnp.uint32(0))
        x = _unpack_bf16_pairs(xp).astype(BF16)
        g = jnp.dot(x, wg_s[...], preferred_element_type=F32)
        u = jnp.dot(x, wu_s[...], preferred_element_type=F32)
        hid = (_silu(g) * u).astype(BF16)
        o_ref[...] = _pack_bf16_pairs(jnp.dot(hid, wd_s[...], preferred_element_type=F32))

    @pl.when(b >= nused_ref[0])
    def _():
        o_ref[...] = jnp.zeros_like(o_ref)


def _experts(xbuf, block_e, nvalid, nused, w_gate, w_up, w_down, bm):
    p, dh = xbuf.shape
    nb = p // bm
    d, de = w_gate.shape[1:]
    row = lambda b, be, nv, nu: (jnp.minimum(b, nu[0] - 1), 0)
    wsel = lambda b, be, nv, nu: (be[b], 0, 0)
    grid_spec = pltpu.PrefetchScalarGridSpec(
        num_scalar_prefetch=3,
        grid=(nb,),
        in_specs=[pl.BlockSpec((bm, dh), row),
                  pl.BlockSpec((1, d, de), wsel),
                  pl.BlockSpec((1, d, de), wsel),
                  pl.BlockSpec((1, de, d), wsel)],
        out_specs=pl.BlockSpec((bm, dh), lambda b, be, nv, nu: (b, 0)),
        scratch_shapes=[pltpu.VMEM((d, de), BF16), pltpu.VMEM((d, de), BF16), pltpu.VMEM((de, d), BF16)],
    )
    return pl.pallas_call(
        _experts_kernel,
        grid_spec=grid_spec,
        out_shape=jax.ShapeDtypeStruct((p, dh), jnp.uint32),
        compiler_params=_cparams(("arbitrary",)),
        name="experts",
    )(block_e, nvalid, nused, xbuf, w_gate, w_up, w_down)


def _combine_kernel(gb_ref, w_ref, x1_ref, ysh_ref, mod_ref, g_ref, out_ref):
    y = ysh_ref[...]
    for k in range(TOP_K):
        y = y + _unpack_bf16_pairs(gb_ref[k]) * w_ref[:, k:k + 1]
    out_ref[...] = x1_ref[...] + mod_ref[0, 5:6, :] * (_rms(y) * g_ref[...])


def _combine(gbuf, w_rows, x1_flat, ysh_flat, mod3, g_post, s, tc):
    t, d = x1_flat.shape
    per_seq = s // tc
    tok = pl.BlockSpec((tc, d), lambda i: (i, 0))
    return pl.pallas_call(
        _combine_kernel,
        grid=(t // tc,),
        in_specs=[pl.BlockSpec((TOP_K, tc, d // 2), lambda i: (0, i, 0)),
                  pl.BlockSpec((tc, TOP_K), lambda i: (i, 0)),
                  tok, tok,
                  pl.BlockSpec((1, 6, d), lambda i: (i // per_seq, 0, 0)),
                  pl.BlockSpec((1, d), lambda i: (0, 0))],
        out_specs=tok,
        out_shape=jax.ShapeDtypeStruct((t, d), F32),
        compiler_params=_cparams(("parallel",)),
        name="combine",
    )(gbuf, w_rows, x1_flat, ysh_flat, mod3, g_post)


def _channel_dft_table():
    c = np.arange(GROUP_DIM)
    ang = 2.0 * np.pi * ((c[:, None] * c[None, :]) % GROUP_DIM) / GROUP_DIM
    eye = np.eye(D_FNET // GROUP_DIM)
    scale = 1.0 / np.sqrt(GROUP_DIM)
    cs = np.concatenate([np.kron(eye, np.cos(ang)), np.kron(eye, np.sin(ang))], axis=1) * scale
    return jnp.asarray(cs, BF16)


def _expert_block_rows(n_assign):
    return 512 if n_assign // N_EXPERTS >= 1024 else 256


def _layer(x, mod, p):
    b, s, d = x.shape
    t = b * s
    mod3 = mod.reshape(b, 6, d)
    ts = min(512, s)
    g_tab, h_tab = _dft_tables(s)

    v, zcs = _inproj(x, mod3, p["g_mix_pre"], p["w_in_b"], _channel_dft_table(), ts)
    cn = _conv(v, p["conv_w"], p["conv_b"], p["conv_ln_g"], p["conv_ln_b"], p["g_conv_out"], min(256, s))
    fy = _seqdft(zcs, g_tab, h_tab)
    x1, h2, ysh = _mix(x, cn, fy, mod3, p["g_fnet_out"], p["w_out_b"], p["g_mix_post"], p["g_ffn_pre"],
                       p["wsg_b"], p["wsu_b"], p["wsd_b"], min(256, s))
    idx, rank, wgt, cnt = _router(h2, p["wr_t_b"], p["b_router_col"], min(256, s))

    n = t * TOP_K
    bm = _expert_block_rows(n)
    counts = cnt[:, 0]
    pcounts = (counts + bm - 1) // bm * bm
    pends = jnp.cumsum(pcounts)
    pstarts = pends - pcounts
    dest = _dest(pstarts.astype(jnp.int32), idx, rank, min(2048, t))
    nb = (n + N_EXPERTS * (bm - 1) + bm - 1) // bm
    nused = (pends[-1] // bm).astype(jnp.int32)
    blk = jnp.minimum(jnp.arange(nb, dtype=jnp.int32), nused - 1) * bm
    block_e = jnp.sum((pends[None, :] <= blk[:, None]).astype(jnp.int32), axis=1)
    block_e = jnp.minimum(block_e, N_EXPERTS - 1)
    nvalid = jnp.clip(pstarts[block_e] + counts[block_e] - blk, 0, bm).astype(jnp.int32)

    xbuf = _dispatch(h2.reshape(t, d // 2), dest, nb * bm)
    obuf = _experts(xbuf, block_e, nvalid, nused.reshape(1), p["w_gate"], p["w_up"], p["w_down"], bm)
    gbuf = _gather(obuf, dest)
    out = _combine(gbuf, wgt.T, x1.reshape(t, d), ysh.reshape(t, d), mod3, p["g_ffn_post"], s, min(256, s))
    return out.reshape(b, s, d)


def kernel(x_prompt, x_sample, c_prompt, c_sample, w_ada, b_ada, g_mix_pre, w_in, conv_w, conv_b, conv_ln_g, conv_ln_b, g_conv_out, g_fnet_out, w_out, g_mix_post, g_ffn_pre, w_router, b_router, w_gate, w_up, w_down, ws_gate, ws_up, ws_down, g_ffn_post):
    assert w_ada.shape[0] == 1, "single-layer kernel"
    bp, bs = c_prompt.shape[0], c_sample.shape[0]
    rows = -(-(bp + bs) // 8) * 8
    c_all = jnp.zeros((rows, D_MODEL), F32).at[:bp].set(c_prompt).at[bp:bp + bs].set(c_sample)
    mod = _ada(c_all, w_ada[0], b_ada)
    p = {
        "g_mix_pre": g_mix_pre, "w_in_b": w_in[0].astype(BF16),
        "conv_w": conv_w[0], "conv_b": conv_b, "conv_ln_g": conv_ln_g, "conv_ln_b": conv_ln_b,
        "g_conv_out": g_conv_out, "g_fnet_out": g_fnet_out, "w_out_b": w_out[0].astype(BF16),
        "g_mix_post": g_mix_post, "g_ffn_pre": g_ffn_pre,
        "wr_t_b": w_router[0].T.astype(BF16), "b_router_col": b_router[0][:, None],
        "w_gate": w_gate[0], "w_up": w_up[0], "w_down": w_down[0],
        "wsg_b": ws_gate[0].astype(BF16), "wsu_b": ws_up[0].astype(BF16), "wsd_b": ws_down[0].astype(BF16),
        "g_ffn_post": g_ffn_post,
    }
    y_prompt = _layer(x_prompt, mod[:bp], p)
    y_sample = _layer(x_sample, mod[bp:bp + bs], p)
    return (y_prompt, y_sample)
```

```python
import functools

import numpy as np
import jax
import jax.numpy as jnp
from jax import lax
from jax.experimental import pallas as pl
from jax.experimental.pallas import tpu as pltpu
from jax.experimental.pallas import tpu_sc as plsc

F32 = jnp.float32
BF16 = jnp.bfloat16

D_MODEL = 1024
D_CONV = 512
D_FNET = 512
GROUP_DIM = 64
CONV_WIDTH = 31
N_EXPERTS = 256
TOP_K = 8
N_EXPERT_GROUPS = 8
GROUP_SIZE = N_EXPERTS // N_EXPERT_GROUPS
TOPK_GROUPS = 4
ROUTED_SCALE = 2.5
EPS = 1e-6

DFT_S1 = 128
DFT_CH = 128
HALO = 16
VMEM_LIMIT = 56 * 1024 * 1024
SC_CORES = 2
SC_SUBCORES = 16
SC_WORKERS = SC_CORES * SC_SUBCORES
SC_ROWS = 64


def _cparams(sem, vmem=None):
    return pltpu.CompilerParams(dimension_semantics=sem, vmem_limit_bytes=vmem or VMEM_LIMIT)


def _rms(x):
    return x * lax.rsqrt(jnp.mean(x * x, axis=-1, keepdims=True) + EPS)


def _silu(x):
    return x * jax.nn.sigmoid(x)


def _pack_bf16_pairs(x):
    c = x.shape[-1] // 2
    bits = lax.bitcast_convert_type(x.astype(BF16).astype(F32), jnp.uint32)
    return (bits[:, :c] >> 16) | bits[:, c:]


def _unpack_bf16_pairs(p):
    lo = lax.bitcast_convert_type(p << 16, F32)
    hi = lax.bitcast_convert_type(p & jnp.uint32(0xFFFF0000), F32)
    return jnp.concatenate([lo, hi], axis=-1)


def _ada_kernel(c_ref, w_ref, b_ref, o_ref):
    o_ref[...] = jnp.dot(_silu(c_ref[...]), w_ref[...], preferred_element_type=F32) + b_ref[...]


def _ada(c, w_ada, b_ada):
    bp, d = c.shape
    n = w_ada.shape[1]
    return pl.pallas_call(
        _ada_kernel,
        grid=(n // d,),
        in_specs=[pl.BlockSpec((bp, d), lambda j: (0, 0)),
                  pl.BlockSpec((d, d), lambda j: (0, j)),
                  pl.BlockSpec((1, d), lambda j: (0, j))],
        out_specs=pl.BlockSpec((bp, d), lambda j: (0, j)),
        out_shape=jax.ShapeDtypeStruct((bp, n), F32),
        compiler_params=_cparams(("parallel",)),
        name="ada",
    )(c, w_ada, b_ada)


def _inproj_kernel(x_ref, mod_ref, g_ref, win_ref, cs_ref, v_ref, z_ref):
    x = x_ref[0]
    h = _rms(x) * g_ref[...]
    h = h * (1.0 + mod_ref[0, 1:2, :]) + mod_ref[0, 0:1, :]
    u = jnp.dot(h.astype(BF16), win_ref[...], preferred_element_type=F32)
    a = u[:, :D_CONV]
    gt = u[:, D_CONV:2 * D_CONV]
    z = u[:, 2 * D_CONV:]
    v_ref[0] = a * jax.nn.sigmoid(gt)
    z_ref[0] = jnp.dot(z.astype(BF16), cs_ref[...], preferred_element_type=F32)


def _inproj(x, mod3, g_mix_pre, w_in_b, cs_b, ts):
    b, s, d = x.shape
    return pl.pallas_call(
        _inproj_kernel,
        grid=(b, s // ts),
        in_specs=[pl.BlockSpec((1, ts, d), lambda i, t: (i, t, 0)),
                  pl.BlockSpec((1, 6, d), lambda i, t: (i, 0, 0)),
                  pl.BlockSpec((1, d), lambda i, t: (0, 0)),
                  pl.BlockSpec(w_in_b.shape, lambda i, t: (0, 0)),
                  pl.BlockSpec(cs_b.shape, lambda i, t: (0, 0))],
        out_specs=[pl.BlockSpec((1, ts, D_CONV), lambda i, t: (i, t, 0)),
                   pl.BlockSpec((1, ts, 2 * D_FNET), lambda i, t: (i, t, 0))],
        out_shape=[jax.ShapeDtypeStruct((b, s, D_CONV), F32),
                   jax.ShapeDtypeStruct((b, s, 2 * D_FNET), F32)],
        compiler_params=_cparams(("parallel", "parallel")),
        name="inproj",
    )(x, mod3, g_mix_pre, w_in_b, cs_b)


def _conv_kernel(vp_ref, v_ref, vn_ref, w_ref, b_ref, lg_ref, lb_ref, go_ref, o_ref, pad_ref, sh_ref, *, ts, rc):
    t = pl.program_id(1)
    nt = pl.num_programs(1)
    pad_ref[0:HALO, :] = jnp.where(t > 0, vp_ref[0], 0.0)
    pad_ref[HALO:HALO + ts, :] = v_ref[0]
    pad_ref[HALO + ts:HALO + ts + HALO, :] = jnp.where(t < nt - 1, vn_ref[0], 0.0)
    span = ts + 2 * HALO - 8
    for m in range(8):
        sh_ref[m] = pad_ref[m:m + span, :]
    off = HALO - CONV_WIDTH // 2
    for c in range(ts // rc):
        r0 = c * rc
        acc = jnp.zeros((rc // 8, 8, D_CONV), F32)
        for j in range(CONV_WIDTH):
            m, q = (off + j) % 8, (off + j) // 8
            tap = sh_ref[m, r0 + 8 * q:r0 + 8 * q + rc, :].reshape(rc // 8, 8, D_CONV)
            acc = acc + tap * w_ref[j][None]
        acc = acc.reshape(rc, D_CONV) + b_ref[...]
        mu = jnp.mean(acc, axis=-1, keepdims=True)
        xc = acc - mu
        var = jnp.mean(xc * xc, axis=-1, keepdims=True)
        y = xc * lax.rsqrt(var + EPS) * lg_ref[...] + lb_ref[...]
        y = _silu(y)
        y = _rms(y) * go_ref[...]
        o_ref[0, r0:r0 + rc, :] = y.astype(o_ref.dtype)


def _conv(v, conv_w, conv_b, ln_g, ln_b, g_out, ts, rc=32):
    b, s, c = v.shape
    hb = ts // HALO
    nh = s // HALO
    vec = pl.BlockSpec((1, c), lambda i, t: (0, 0))
    return pl.pallas_call(
        functools.partial(_conv_kernel, ts=ts, rc=rc),
        grid=(b, s // ts),
        in_specs=[pl.BlockSpec((1, HALO, c), lambda i, t: (i, jnp.maximum(t * hb - 1, 0), 0)),
                  pl.BlockSpec((1, ts, c), lambda i, t: (i, t, 0)),
                  pl.BlockSpec((1, HALO, c), lambda i, t: (i, jnp.minimum((t + 1) * hb, nh - 1), 0)),
                  pl.BlockSpec((CONV_WIDTH, 8, c), lambda i, t: (0, 0, 0)),
                  vec, vec, vec, vec],
        out_specs=pl.BlockSpec((1, ts, c), lambda i, t: (i, t, 0)),
        out_shape=jax.ShapeDtypeStruct((b, s, c), BF16),
        scratch_shapes=[pltpu.VMEM((ts + 2 * HALO, c), F32), pltpu.VMEM((8, ts + 2 * HALO - 8, c), F32)],
        compiler_params=_cparams(("parallel", "parallel")),
        name="conv",
    )(v, v, v, jnp.broadcast_to(conv_w[:, None, :], (CONV_WIDTH, 8, c)), conv_b, ln_g, ln_b, g_out)


def _dft_tables(s):
    s1 = DFT_S1
    s2 = s // s1
    k1 = np.arange(s1)[None, :, None]
    p1 = np.arange(s1)[None, None, :]
    p2 = np.arange(s2)[:, None, None]
    ang = 2.0 * np.pi * ((k1 * (s2 * p1 + p2)) % s) / s
    g = np.concatenate([np.cos(ang), np.sin(ang)], axis=1)
    k2 = np.arange(s2)[:, None]
    q2 = np.arange(s2)[None, :]
    ang2 = 2.0 * np.pi * ((k2 * q2) % s2) / s2
    h = np.concatenate([np.cos(ang2), np.sin(ang2)], axis=1) / np.sqrt(s)
    return jnp.asarray(g, BF16), jnp.asarray(h, BF16)


def _seqdft_kernel(zc_ref, zs_ref, g_ref, h_ref, o_ref, scr_ref, *, s1, s2):
    ch = DFT_CH
    for p2 in range(s2):
        xc = zc_ref[0, pl.ds(p2, s1, stride=s2), :]
        xs = zs_ref[0, pl.ds(p2, s1, stride=s2), :]
        xx = jnp.concatenate([xc, xs], axis=-1).astype(BF16)
        r = jnp.dot(g_ref[p2], xx, preferred_element_type=F32)
        a_re = r[:s1, :ch] - r[s1:, ch:]
        a_im = -(r[:s1, ch:] + r[s1:, :ch])
        scr_ref[:, p2, :] = a_re
        scr_ref[:, s2 + p2, :] = a_im
    hmat = h_ref[...]
    for k1 in range(s1):
        y = jnp.dot(hmat, scr_ref[k1].astype(BF16), preferred_element_type=F32)
        o_ref[0, pl.ds(k1, s2, stride=s1), :] = y


def _seqdft(zcs, g_tab, h_tab):
    b, s, _ = zcs.shape
    s1 = DFT_S1
    s2 = s // s1
    nch = D_FNET // DFT_CH
    return pl.pallas_call(
        functools.partial(_seqdft_kernel, s1=s1, s2=s2),
        grid=(b, nch),
        in_specs=[pl.BlockSpec((1, s, DFT_CH), lambda i, c: (i, 0, c)),
                  pl.BlockSpec((1, s, DFT_CH), lambda i, c: (i, 0, nch + c)),
                  pl.BlockSpec(g_tab.shape, lambda i, c: (0, 0, 0)),
                  pl.BlockSpec(h_tab.shape, lambda i, c: (0, 0))],
        out_specs=pl.BlockSpec((1, s, DFT_CH), lambda i, c: (i, 0, c)),
        out_shape=jax.ShapeDtypeStruct((b, s, D_FNET), F32),
        scratch_shapes=[pltpu.VMEM((s1, 2 * s2, DFT_CH), F32)],
        compiler_params=_cparams(("parallel", "parallel")),
        name="seqdft",
    )(zcs, zcs, g_tab, h_tab)


def _mix_kernel(x_ref, cn_ref, fy_ref, mod_ref, gf_ref, wout_ref, gpost_ref, gpre_ref,
                wsg_ref, wsu_ref, wsd_ref, x1_ref, h2_ref, ysh_ref):
    fn = _rms(fy_ref[0]) * gf_ref[...]
    mixed = jnp.dot(cn_ref[0], wout_ref[:D_CONV, :], preferred_element_type=F32)
    mixed = mixed + jnp.dot(fn.astype(BF16), wout_ref[D_CONV:, :], preferred_element_type=F32)
    x1 = x_ref[0] + mod_ref[0, 2:3, :] * (_rms(mixed) * gpost_ref[...])
    x1_ref[0] = x1
    h2 = _rms(x1) * gpre_ref[...]
    h2 = h2 * (1.0 + mod_ref[0, 4:5, :]) + mod_ref[0, 3:4, :]
    h2_ref[0] = _pack_bf16_pairs(h2)
    hb = h2.astype(BF16)
    hid = _silu(jnp.dot(hb, wsg_ref[...], preferred_element_type=F32))
    hid = hid * jnp.dot(hb, wsu_ref[...], preferred_element_type=F32)
    ysh_ref[0] = _pack_bf16_pairs(jnp.dot(hid.astype(BF16), wsd_ref[...], preferred_element_type=F32))


def _mix(x, cn, fy, mod3, g_fnet, w_out_b, g_post, g_pre, wsg_b, wsu_b, wsd_b, ts):
    b, s, d = x.shape
    tok = lambda c: pl.BlockSpec((1, ts, c), lambda i, t: (i, t, 0))
    full = lambda a: pl.BlockSpec(a.shape, lambda i, t: (0,) * a.ndim)
    return pl.pallas_call(
        _mix_kernel,
        grid=(b, s // ts),
        in_specs=[tok(d), tok(D_CONV), tok(D_FNET),
                  pl.BlockSpec((1, 6, d), lambda i, t: (i, 0, 0)),
                  full(g_fnet), full(w_out_b), full(g_post), full(g_pre),
                  full(wsg_b), full(wsu_b), full(wsd_b)],
        out_specs=[tok(d), tok(d // 2), tok(d // 2)],
        out_shape=[jax.ShapeDtypeStruct((b, s, d), F32),
                   jax.ShapeDtypeStruct((b, s, d // 2), jnp.uint32),
                   jax.ShapeDtypeStruct((b, s, d // 2), jnp.uint32)],
        compiler_params=_cparams(("parallel", "parallel")),
        name="mix",
    )(x, cn, fy, mod3, g_fnet, w_out_b, g_post, g_pre, wsg_b, wsu_b, wsd_b)


def _router_kernel(h_ref, wr_ref, br_ref, u_ref, idx_ref, rank_ref, wgt_ref, cnt_ref, carry_ref, *, tr):
    e = N_EXPERTS

    @pl.when((pl.program_id(0) == 0) & (pl.program_id(1) == 0))
    def _():
        carry_ref[...] = jnp.zeros_like(carry_ref)

    logits = lax.dot_general(wr_ref[...], _unpack_bf16_pairs(h_ref[0]).astype(BF16), (((1,), (1,)), ((), ())),
                             preferred_element_type=F32)
    sc = jax.nn.sigmoid(logits)
    sb = sc + br_ref[...]
    ninf = jnp.float32(-jnp.inf)

    io_g = lax.broadcasted_iota(jnp.int32, (GROUP_SIZE, tr), 0).astype(F32)
    gs = []
    for g in range(N_EXPERT_GROUPS):
        blk = sb[g * GROUP_SIZE:(g + 1) * GROUP_SIZE]
        m1 = jnp.max(blk, axis=0, keepdims=True)
        i1 = jnp.min(jnp.where(blk == m1, io_g, float(GROUP_SIZE)), axis=0, keepdims=True)
        m2 = jnp.max(jnp.where(io_g == i1, ninf, blk), axis=0, keepdims=True)
        gs.append(m1 + m2)
    masked = []
    for g in range(N_EXPERT_GROUPS):
        beat = jnp.zeros((1, tr), F32)
        for o in range(N_EXPERT_GROUPS):
            if o == g:
                continue
            wins = (gs[o] > gs[g]) | ((gs[o] == gs[g]) & (o < g))
            beat = beat + wins.astype(F32)
        keep = beat < float(TOPK_GROUPS)
        masked.append(jnp.where(keep, sb[g * GROUP_SIZE:(g + 1) * GROUP_SIZE], ninf))
    v = jnp.concatenate(masked, axis=0)

    io_e = lax.broadcasted_iota(jnp.int32, (e, tr), 0).astype(F32)
    ids, ws = [], []
    sel = jnp.zeros((e, tr), F32)
    for _ in range(TOP_K):
        m = jnp.max(v, axis=0, keepdims=True)
        i = jnp.min(jnp.where(v == m, io_e, float(e)), axis=0, keepdims=True)
        oh = io_e == i
        ids.append(i)
        ws.append(jnp.sum(jnp.where(oh, sc, 0.0), axis=0, keepdims=True))
        v = jnp.where(oh, ninf, v)
        sel = sel + oh.astype(F32)

    wsum = ws[0]
    for k in range(1, TOP_K):
        wsum = wsum + ws[k]
    wgt_ref[...] = jnp.concatenate([w / wsum * ROUTED_SCALE for w in ws], axis=0)
    idx_ref[...] = jnp.concatenate(ids, axis=0).astype(jnp.int32)

    excl = jnp.dot(sel.astype(BF16), u_ref[...], preferred_element_type=F32)
    base = carry_ref[:, 0:1]
    rank_full = base + excl
    ranks = [jnp.sum(jnp.where(io_e == ids[k], rank_full, 0.0), axis=0, keepdims=True)
             for k in range(TOP_K)]
    rank_ref[...] = jnp.concatenate(ranks, axis=0).astype(jnp.int32)
    new = base + jnp.sum(sel, axis=1, keepdims=True)
    carry_ref[...] = jnp.broadcast_to(new, carry_ref.shape)
    cnt_ref[...] = jnp.broadcast_to(new, cnt_ref.shape).astype(jnp.int32)


def _router(h2p, wr_t_b, b_router_col, tr):
    b, s, d = h2p.shape
    t = b * s
    nt = s // tr
    u = jnp.asarray(np.triu(np.ones((tr, tr), np.float32), k=1), BF16)
    col = lambda i, j: (0, i * nt + j)
    return pl.pallas_call(
        functools.partial(_router_kernel, tr=tr),
        grid=(b, nt),
        in_specs=[pl.BlockSpec((1, tr, d), lambda i, j: (i, j, 0)),
                  pl.BlockSpec(wr_t_b.shape, lambda i, j: (0, 0)),
                  pl.BlockSpec((N_EXPERTS, 1), lambda i, j: (0, 0)),
                  pl.BlockSpec((tr, tr), lambda i, j: (0, 0))],
        out_specs=[pl.BlockSpec((TOP_K, tr), col), pl.BlockSpec((TOP_K, tr), col),
                   pl.BlockSpec((TOP_K, tr), col),
                   pl.BlockSpec((N_EXPERTS, 128), lambda i, j: (0, 0))],
        out_shape=[jax.ShapeDtypeStruct((TOP_K, t), jnp.int32),
                   jax.ShapeDtypeStruct((TOP_K, t), jnp.int32),
                   jax.ShapeDtypeStruct((TOP_K, t), F32),
                   jax.ShapeDtypeStruct((N_EXPERTS, 128), jnp.int32)],
        scratch_shapes=[pltpu.VMEM((N_EXPERTS, 128), F32)],
        compiler_params=_cparams(("arbitrary", "arbitrary")),
        name="router",
    )(h2p, wr_t_b, b_router_col, u)


def _dest_kernel(pstart_ref, idx_ref, rank_ref, dest_ref):
    idx = idx_ref[...]

    def body(g, acc):
        for j in range(8):
            e = g * 8 + j
            acc = jnp.where(idx == e, pstart_ref[e], acc)
        return acc

    dest_ref[...] = lax.fori_loop(0, N_EXPERTS // 8, body, jnp.zeros_like(idx)) + rank_ref[...]


def _dest(pstarts, idx, rank, tl):
    k, t = idx.shape
    grid_spec = pltpu.PrefetchScalarGridSpec(
        num_scalar_prefetch=1,
        grid=(t // tl,),
        in_specs=[pl.BlockSpec((k, tl), lambda i, ps: (0, i)), pl.BlockSpec((k, tl), lambda i, ps: (0, i))],
        out_specs=pl.BlockSpec((k, tl), lambda i, ps: (0, i)),
    )
    return pl.pallas_call(
        _dest_kernel,
        grid_spec=grid_spec,
        out_shape=jax.ShapeDtypeStruct((k, t), jnp.int32),
        compiler_params=_cparams(("parallel",)),
        name="dest",
    )(pstarts, idx, rank)


def _sc_mesh():
    return plsc.VectorSubcoreMesh(core_axis_name="c", subcore_axis_name="s",
                                  num_cores=SC_CORES, num_subcores=SC_SUBCORES)


def _sc_worker_base(per_worker):
    return (lax.axis_index("s") * SC_CORES + lax.axis_index("c")) * per_worker


def _dispatch(h2_flat, dest, p_rows):
    t, dh = h2_flat.shape
    r = SC_ROWS
    per_w = t // SC_WORKERS
    nchunk = per_w // r
    assert per_w % (2 * r) == 0

    @functools.partial(
        pl.kernel, mesh=_sc_mesh(),
        out_type=jax.ShapeDtypeStruct((p_rows, dh), h2_flat.dtype),
        scratch_types=[pltpu.VMEM((2, TOP_K, r), jnp.int32), pltpu.VMEM((2, r, dh), h2_flat.dtype),
                       pltpu.SemaphoreType.DMA((2,)), pltpu.SemaphoreType.DMA((2,))],
        name="sc_dispatch",
    )
    def k(rows_hbm, dest_hbm, out_hbm, idx_v, rows_v, lsem, ssem):
        base = _sc_worker_base(per_w)

        def load(ci, slot):
            t0 = base + ci * r
            for kk in range(TOP_K):
                pltpu.sync_copy(dest_hbm.at[kk, pl.ds(t0, r)], idx_v.at[slot, kk])
            pltpu.async_copy(rows_hbm.at[pl.ds(t0, r)], rows_v.at[slot], lsem.at[slot])

        def scatter(ci, slot):
            t0 = base + ci * r
            pltpu.make_async_copy(rows_hbm.at[pl.ds(t0, r)], rows_v.at[slot], lsem.at[slot]).wait()
            for kk in range(TOP_K):
                pltpu.async_copy(rows_v.at[slot], out_hbm.at[idx_v.at[slot, kk]], ssem.at[slot])

        def drain(slot):
            for kk in range(TOP_K):
                pltpu.make_async_copy(rows_v.at[slot], out_hbm.at[idx_v.at[slot, kk]], ssem.at[slot]).wait()

        load(0, 0)

        @pl.loop(0, nchunk, step=2)
        def _(c0):
            for s in range(2):
                ci = c0 + s

                @pl.when(ci + 1 < nchunk)
                def _():
                    @pl.when(ci >= 1)
                    def _():
                        drain(1 - s)
                    load(ci + 1, 1 - s)

                scatter(ci, s)

        drain(0)
        drain(1)

    return k(h2_flat, dest)


def _gather(obuf, dest):
    _, dh = obuf.shape
    kk_n, t = dest.shape
    r = SC_ROWS
    per_w = t // SC_WORKERS
    nitem = (per_w // r) * kk_n
    assert per_w % r == 0 and nitem % 2 == 0

    @functools.partial(
        pl.kernel, mesh=_sc_mesh(),
        out_type=jax.ShapeDtypeStruct((kk_n, t, dh), obuf.dtype),
        scratch_types=[pltpu.VMEM((2, r), jnp.int32), pltpu.VMEM((2, r, dh), obuf.dtype),
                       pltpu.SemaphoreType.DMA((2,)), pltpu.SemaphoreType.DMA((2,))],
        name="sc_gather",
    )
    def k(table_hbm, idx_hbm, out_hbm, idx_v, rows_v, gsem, wsem):
        base = _sc_worker_base(per_w)

        def item(i):
            return i % kk_n, base + (i // kk_n) * r

        def start_gather(i, slot):
            kk, t0 = item(i)
            pltpu.sync_copy(idx_hbm.at[kk, pl.ds(t0, r)], idx_v.at[slot])
            pltpu.async_copy(table_hbm.at[idx_v.at[slot]], rows_v.at[slot], gsem.at[slot])

        def finish(i, slot):
            kk, t0 = item(i)
            pltpu.make_async_copy(table_hbm.at[idx_v.at[slot]], rows_v.at[slot], gsem.at[slot]).wait()
            pltpu.async_copy(rows_v.at[slot], out_hbm.at[kk, pl.ds(t0, r)], wsem.at[slot])

        def wait_write(i, slot):
            kk, t0 = item(i)
            pltpu.make_async_copy(rows_v.at[slot], out_hbm.at[kk, pl.ds(t0, r)], wsem.at[slot]).wait()

        start_gather(0, 0)

        @pl.loop(0, nitem, step=2)
        def _(i0):
            for s in range(2):
                i = i0 + s

                @pl.when(i + 1 < nitem)
                def _():
                    @pl.when(i >= 1)
                    def _():
                        wait_write(i - 1, 1 - s)
                    start_gather(i + 1, 1 - s)

                finish(i, s)

        wait_write(nitem - 2, 0)
        wait_write(nitem - 1, 1)

    return k(obuf, dest)


def _experts_kernel(be_ref, nvalid_ref, nused_ref, x_ref, wg_ref, wu_ref, wd_ref, o_ref, wg_s, wu_s, wd_s):
    b = pl.program_id(0)

    @pl.when(b < nused_ref[0])
    def _():
        prev = be_ref[jnp.maximum(b - 1, 0)]

        @pl.when((b == 0) | (be_ref[b] != prev))
        def _():
            wg_s[...] = wg_ref[0].astype(BF16)
            wu_s[...] = wu_ref[0].astype(BF16)
            wd_s[...] = wd_ref[0].astype(BF16)

        rows = lax.broadcasted_iota(jnp.int32, x_ref.shape, 0)
        xp = jnp.where(rows < nvalid_ref[b], x_ref[...], jnp.uint32(0))
        x = _unpack_bf16_pairs(xp).astype(BF16)
        g = jnp.dot(x, wg_s[...], preferred_element_type=F32)
        u = jnp.dot(x, wu_s[...], preferred_element_type=F32)
        hid = (_silu(g) * u).astype(BF16)
        o_ref[...] = _pack_bf16_pairs(jnp.dot(hid, wd_s[...], preferred_element_type=F32))

    @pl.when(b >= nused_ref[0])
    def _():
        o_ref[...] = jnp.zeros_like(o_ref)


def _experts(xbuf, block_e, nvalid, nused, w_gate, w_up, w_down, bm):
    p, dh = xbuf.shape
    nb = p // bm
    d, de = w_gate.shape[1:]
    row = lambda b, be, nv, nu: (jnp.minimum(b, nu[0] - 1), 0)
    wsel = lambda b, be, nv, nu: (be[b], 0, 0)
    grid_spec = pltpu.PrefetchScalarGridSpec(
        num_scalar_prefetch=3,
        grid=(nb,),
        in_specs=[pl.BlockSpec((bm, dh), row),
                  pl.BlockSpec((1, d, de), wsel),
                  pl.BlockSpec((1, d, de), wsel),
                  pl.BlockSpec((1, de, d), wsel)],
        out_specs=pl.BlockSpec((bm, dh), lambda b, be, nv, nu: (b, 0)),
        scratch_shapes=[pltpu.VMEM((d, de), BF16), pltpu.VMEM((d, de), BF16), pltpu.VMEM((de, d), BF16)],
    )
    return pl.pallas_call(
        _experts_kernel,
        grid_spec=grid_spec,
        out_shape=jax.ShapeDtypeStruct((p, dh), jnp.uint32),
        compiler_params=_cparams(("arbitrary",)),
        name="experts",
    )(block_e, nvalid, nused, xbuf, w_gate, w_up, w_down)


def _combine_kernel(gb_ref, w_ref, x1_ref, ysh_ref, mod_ref, g_ref, out_ref):
    y = _unpack_bf16_pairs(ysh_ref[...])
    for k in range(TOP_K):
        y = y + _unpack_bf16_pairs(gb_ref[k]) * w_ref[:, k:k + 1]
    out_ref[...] = x1_ref[...] + mod_ref[0, 5:6, :] * (_rms(y) * g_ref[...])


def _combine(gbuf, w_rows, x1_flat, ysh_flat, mod3, g_post, s, tc):
    t, d = x1_flat.shape
    per_seq = s // tc
    tok = pl.BlockSpec((tc, d), lambda i: (i, 0))
    return pl.pallas_call(
        _combine_kernel,
        grid=(t // tc,),
        in_specs=[pl.BlockSpec((TOP_K, tc, d // 2), lambda i: (0, i, 0)),
                  pl.BlockSpec((tc, TOP_K), lambda i: (i, 0)),
                  tok, pl.BlockSpec((tc, d // 2), lambda i: (i, 0)),
                  pl.BlockSpec((1, 6, d), lambda i: (i // per_seq, 0, 0)),
                  pl.BlockSpec((1, d), lambda i: (0, 0))],
        out_specs=tok,
        out_shape=jax.ShapeDtypeStruct((t, d), F32),
        compiler_params=_cparams(("parallel",)),
        name="combine",
    )(gbuf, w_rows, x1_flat, ysh_flat, mod3, g_post)


def _channel_dft_table():
    c = np.arange(GROUP_DIM)
    ang = 2.0 * np.pi * ((c[:, None] * c[None, :]) % GROUP_DIM) / GROUP_DIM
    eye = np.eye(D_FNET // GROUP_DIM)
    scale = 1.0 / np.sqrt(GROUP_DIM)
    cs = np.concatenate([np.kron(eye, np.cos(ang)), np.kron(eye, np.sin(ang))], axis=1) * scale
    return jnp.asarray(cs, BF16)


def _expert_block_rows(n_assign):
    return 512 if n_assign // N_EXPERTS >= 256 else 256


def _layer(x, mod, p):
    b, s, d = x.shape
    t = b * s
    mod3 = mod.reshape(b, 6, d)
    ts = min(512, s)
    g_tab, h_tab = _dft_tables(s)

    v, zcs = _inproj(x, mod3, p["g_mix_pre"], p["w_in_b"], _channel_dft_table(), ts)
    cn = _conv(v, p["conv_w"], p["conv_b"], p["conv_ln_g"], p["conv_ln_b"], p["g_conv_out"], min(256, s))
    fy = _seqdft(zcs, g_tab, h_tab)
    x1, h2, ysh = _mix(x, cn, fy, mod3, p["g_fnet_out"], p["w_out_b"], p["g_mix_post"], p["g_ffn_pre"],
                       p["wsg_b"], p["wsu_b"], p["wsd_b"], min(256, s))
    idx, rank, wgt, cnt = _router(h2, p["wr_t_b"], p["b_router_col"], min(256, s))

    n = t * TOP_K
    bm = _expert_block_rows(n)
    counts = cnt[:, 0]
    pcounts = (counts + bm - 1) // bm * bm
    pends = jnp.cumsum(pcounts)
    pstarts = pends - pcounts
    dest = _dest(pstarts.astype(jnp.int32), idx, rank, min(2048, t))
    nb = (n + N_EXPERTS * (bm - 1) + bm - 1) // bm
    nused = (pends[-1] // bm).astype(jnp.int32)
    blk = jnp.minimum(jnp.arange(nb, dtype=jnp.int32), nused - 1) * bm
    block_e = jnp.sum((pends[None, :] <= blk[:, None]).astype(jnp.int32), axis=1)
    block_e = jnp.minimum(block_e, N_EXPERTS - 1)
    nvalid = jnp.clip(pstarts[block_e] + counts[block_e] - blk, 0, bm).astype(jnp.int32)

    xbuf = _dispatch(h2.reshape(t, d // 2), dest, nb * bm)
    obuf = _experts(xbuf, block_e, nvalid, nused.reshape(1), p["w_gate"], p["w_up"], p["w_down"], bm)
    gbuf = _gather(obuf, dest)
    out = _combine(gbuf, wgt.T, x1.reshape(t, d), ysh.reshape(t, d // 2), mod3, p["g_ffn_post"], s, min(256, s))
    return out.reshape(b, s, d)


def kernel(x_prompt, x_sample, c_prompt, c_sample, w_ada, b_ada, g_mix_pre, w_in, conv_w, conv_b, conv_ln_g, conv_ln_b, g_conv_out, g_fnet_out, w_out, g_mix_post, g_ffn_pre, w_router, b_router, w_gate, w_up, w_down, ws_gate, ws_up, ws_down, g_ffn_post):
    assert w_ada.shape[0] == 1, "single-layer kernel"
    bp, bs = c_prompt.shape[0], c_sample.shape[0]
    rows = -(-(bp + bs) // 8) * 8
    c_all = jnp.zeros((rows, D_MODEL), F32).at[:bp].set(c_prompt).at[bp:bp + bs].set(c_sample)
    mod = _ada(c_all, w_ada[0], b_ada)
    p = {
        "g_mix_pre": g_mix_pre, "w_in_b": w_in[0].astype(BF16),
        "conv_w": conv_w[0], "conv_b": conv_b, "conv_ln_g": conv_ln_g, "conv_ln_b": conv_ln_b,
        "g_conv_out": g_conv_out, "g_fnet_out": g_fnet_out, "w_out_b": w_out[0].astype(BF16),
        "g_mix_post": g_mix_post, "g_ffn_pre": g_ffn_pre,
        "wr_t_b": w_router[0].T.astype(BF16), "b_router_col": b_router[0][:, None],
        "w_gate": w_gate[0], "w_up": w_up[0], "w_down": w_down[0],
        "wsg_b": ws_gate[0].astype(BF16), "wsu_b": ws_up[0].astype(BF16), "wsd_b": ws_down[0].astype(BF16),
        "g_ffn_post": g_ffn_post,
    }
    y_prompt = _layer(x_prompt, mod[:bp], p)
    y_sample = _layer(x_sample, mod[bp:bp + bs], p)
    return (y_prompt, y_sample)
```

```python
import functools

import numpy as np
import jax
import jax.numpy as jnp
from jax import lax
from jax.experimental import pallas as pl
from jax.experimental.pallas import tpu as pltpu
from jax.experimental.pallas import tpu_sc as plsc

F32 = jnp.float32
BF16 = jnp.bfloat16

D_MODEL = 1024
D_CONV = 512
D_FNET = 512
GROUP_DIM = 64
CONV_WIDTH = 31
N_EXPERTS = 256
TOP_K = 8
N_EXPERT_GROUPS = 8
GROUP_SIZE = N_EXPERTS // N_EXPERT_GROUPS
TOPK_GROUPS = 4
ROUTED_SCALE = 2.5
EPS = 1e-6

DFT_S1 = 128
DFT_CH = 128
HALO = 16
VMEM_LIMIT = 56 * 1024 * 1024
SC_CORES = 2
SC_SUBCORES = 16
SC_WORKERS = SC_CORES * SC_SUBCORES
SC_ROWS = 64


def _cparams(sem, vmem=None):
    return pltpu.CompilerParams(dimension_semantics=sem, vmem_limit_bytes=vmem or VMEM_LIMIT)


def _rms(x):
    return x * lax.rsqrt(jnp.mean(x * x, axis=-1, keepdims=True) + EPS)


def _silu(x):
    return x * jax.nn.sigmoid(x)


def _pack_bf16_pairs(x):
    c = x.shape[-1] // 2
    bits = lax.bitcast_convert_type(x.astype(BF16).astype(F32), jnp.uint32)
    return (bits[:, :c] >> 16) | bits[:, c:]


def _unpack_bf16_pairs(p):
    lo = lax.bitcast_convert_type(p << 16, F32)
    hi = lax.bitcast_convert_type(p & jnp.uint32(0xFFFF0000), F32)
    return jnp.concatenate([lo, hi], axis=-1)


def _ada_kernel(c_ref, w_ref, b_ref, o_ref):
    o_ref[...] = jnp.dot(_silu(c_ref[...]), w_ref[...], preferred_element_type=F32) + b_ref[...]


def _ada(c, w_ada, b_ada):
    bp, d = c.shape
    n = w_ada.shape[1]
    return pl.pallas_call(
        _ada_kernel,
        grid=(n // d,),
        in_specs=[pl.BlockSpec((bp, d), lambda j: (0, 0)),
                  pl.BlockSpec((d, d), lambda j: (0, j)),
                  pl.BlockSpec((1, d), lambda j: (0, j))],
        out_specs=pl.BlockSpec((bp, d), lambda j: (0, j)),
        out_shape=jax.ShapeDtypeStruct((bp, n), F32),
        compiler_params=_cparams(("parallel",)),
        name="ada",
    )(c, w_ada, b_ada)


def _inproj_kernel(x_ref, mod_ref, g_ref, win_ref, cs_ref, v_ref, z_ref):
    x = x_ref[0]
    h = _rms(x) * g_ref[...]
    h = h * (1.0 + mod_ref[0, 1:2, :]) + mod_ref[0, 0:1, :]
    u = jnp.dot(h.astype(BF16), win_ref[...], preferred_element_type=F32)
    a = u[:, :D_CONV]
    gt = u[:, D_CONV:2 * D_CONV]
    z = u[:, 2 * D_CONV:]
    v_ref[0] = a * jax.nn.sigmoid(gt)
    z_ref[0] = jnp.dot(z.astype(BF16), cs_ref[...], preferred_element_type=F32)


def _inproj(x, mod3, g_mix_pre, w_in_b, cs_b, ts):
    b, s, d = x.shape
    return pl.pallas_call(
        _inproj_kernel,
        grid=(b, s // ts),
        in_specs=[pl.BlockSpec((1, ts, d), lambda i, t: (i, t, 0)),
                  pl.BlockSpec((1, 6, d), lambda i, t: (i, 0, 0)),
                  pl.BlockSpec((1, d), lambda i, t: (0, 0)),
                  pl.BlockSpec(w_in_b.shape, lambda i, t: (0, 0)),
                  pl.BlockSpec(cs_b.shape, lambda i, t: (0, 0))],
        out_specs=[pl.BlockSpec((1, ts, D_CONV), lambda i, t: (i, t, 0)),
                   pl.BlockSpec((1, ts, 2 * D_FNET), lambda i, t: (i, t, 0))],
        out_shape=[jax.ShapeDtypeStruct((b, s, D_CONV), F32),
                   jax.ShapeDtypeStruct((b, s, 2 * D_FNET), F32)],
        compiler_params=_cparams(("parallel", "parallel")),
        name="inproj",
    )(x, mod3, g_mix_pre, w_in_b, cs_b)


def _conv_kernel(vp_ref, v_ref, vn_ref, w_ref, b_ref, lg_ref, lb_ref, go_ref, o_ref, pad_ref, sh_ref, *, ts, rc):
    t = pl.program_id(1)
    nt = pl.num_programs(1)
    pad_ref[0:HALO, :] = jnp.where(t > 0, vp_ref[0], 0.0)
    pad_ref[HALO:HALO + ts, :] = v_ref[0]
    pad_ref[HALO + ts:HALO + ts + HALO, :] = jnp.where(t < nt - 1, vn_ref[0], 0.0)
    span = ts + 2 * HALO - 8
    for m in range(8):
        sh_ref[m] = pad_ref[m:m + span, :]
    off = HALO - CONV_WIDTH // 2
    for c in range(ts // rc):
        r0 = c * rc
        acc = jnp.zeros((rc // 8, 8, D_CONV), F32)
        for j in range(CONV_WIDTH):
            m, q = (off + j) % 8, (off + j) // 8
            tap = sh_ref[m, r0 + 8 * q:r0 + 8 * q + rc, :].reshape(rc // 8, 8, D_CONV)
            acc = acc + tap * w_ref[j][None]
        acc = acc.reshape(rc, D_CONV) + b_ref[...]
        mu = jnp.mean(acc, axis=-1, keepdims=True)
        xc = acc - mu
        var = jnp.mean(xc * xc, axis=-1, keepdims=True)
        y = xc * lax.rsqrt(var + EPS) * lg_ref[...] + lb_ref[...]
        y = _silu(y)
        y = _rms(y) * go_ref[...]
        o_ref[0, r0:r0 + rc, :] = y.astype(o_ref.dtype)


def _conv(v, conv_w, conv_b, ln_g, ln_b, g_out, ts, rc=32):
    b, s, c = v.shape
    hb = ts // HALO
    nh = s // HALO
    vec = pl.BlockSpec((1, c), lambda i, t: (0, 0))
    return pl.pallas_call(
        functools.partial(_conv_kernel, ts=ts, rc=rc),
        grid=(b, s // ts),
        in_specs=[pl.BlockSpec((1, HALO, c), lambda i, t: (i, jnp.maximum(t * hb - 1, 0), 0)),
                  pl.BlockSpec((1, ts, c), lambda i, t: (i, t, 0)),
                  pl.BlockSpec((1, HALO, c), lambda i, t: (i, jnp.minimum((t + 1) * hb, nh - 1), 0)),
                  pl.BlockSpec((CONV_WIDTH, 8, c), lambda i, t: (0, 0, 0)),
                  vec, vec, vec, vec],
        out_specs=pl.BlockSpec((1, ts, c), lambda i, t: (i, t, 0)),
        out_shape=jax.ShapeDtypeStruct((b, s, c), BF16),
        scratch_shapes=[pltpu.VMEM((ts + 2 * HALO, c), F32), pltpu.VMEM((8, ts + 2 * HALO - 8, c), F32)],
        compiler_params=_cparams(("parallel", "parallel")),
        name="conv",
    )(v, v, v, jnp.broadcast_to(conv_w[:, None, :], (CONV_WIDTH, 8, c)), conv_b, ln_g, ln_b, g_out)


def _dft_tables(s):
    s1 = DFT_S1
    s2 = s // s1
    k1 = np.arange(s1)[None, :, None]
    p1 = np.arange(s1)[None, None, :]
    p2 = np.arange(s2)[:, None, None]
    ang = 2.0 * np.pi * ((k1 * (s2 * p1 + p2)) % s) / s
    g = np.concatenate([np.cos(ang), np.sin(ang)], axis=1)
    k2 = np.arange(s2)[:, None]
    q2 = np.arange(s2)[None, :]
    ang2 = 2.0 * np.pi * ((k2 * q2) % s2) / s2
    h = np.concatenate([np.cos(ang2), np.sin(ang2)], axis=1) / np.sqrt(s)
    return jnp.asarray(g, BF16), jnp.asarray(h, BF16)


def _seqdft_kernel(zc_ref, zs_ref, g_ref, h_ref, o_ref, scr_ref, *, s1, s2):
    ch = DFT_CH
    for p2 in range(s2):
        xc = zc_ref[0, pl.ds(p2, s1, stride=s2), :]
        xs = zs_ref[0, pl.ds(p2, s1, stride=s2), :]
        xx = jnp.concatenate([xc, xs], axis=-1).astype(BF16)
        r = jnp.dot(g_ref[p2], xx, preferred_element_type=F32)
        a_re = r[:s1, :ch] - r[s1:, ch:]
        a_im = -(r[:s1, ch:] + r[s1:, :ch])
        scr_ref[:, p2, :] = a_re
        scr_ref[:, s2 + p2, :] = a_im
    hmat = h_ref[...]
    for k1 in range(s1):
        y = jnp.dot(hmat, scr_ref[k1].astype(BF16), preferred_element_type=F32)
        o_ref[0, pl.ds(k1, s2, stride=s1), :] = y


def _seqdft(zcs, g_tab, h_tab):
    b, s, _ = zcs.shape
    s1 = DFT_S1
    s2 = s // s1
    nch = D_FNET // DFT_CH
    return pl.pallas_call(
        functools.partial(_seqdft_kernel, s1=s1, s2=s2),
        grid=(b, nch),
        in_specs=[pl.BlockSpec((1, s, DFT_CH), lambda i, c: (i, 0, c)),
                  pl.BlockSpec((1, s, DFT_CH), lambda i, c: (i, 0, nch + c)),
                  pl.BlockSpec(g_tab.shape, lambda i, c: (0, 0, 0)),
                  pl.BlockSpec(h_tab.shape, lambda i, c: (0, 0))],
        out_specs=pl.BlockSpec((1, s, DFT_CH), lambda i, c: (i, 0, c)),
        out_shape=jax.ShapeDtypeStruct((b, s, D_FNET), F32),
        scratch_shapes=[pltpu.VMEM((s1, 2 * s2, DFT_CH), F32)],
        compiler_params=_cparams(("parallel", "parallel")),
        name="seqdft",
    )(zcs, zcs, g_tab, h_tab)


def _mix_kernel(x_ref, cn_ref, fy_ref, mod_ref, gf_ref, wout_ref, gpost_ref, gpre_ref,
                wsg_ref, wsu_ref, wsd_ref, x1_ref, h2_ref, ysh_ref):
    fn = _rms(fy_ref[0]) * gf_ref[...]
    mixed = jnp.dot(cn_ref[0], wout_ref[:D_CONV, :], preferred_element_type=F32)
    mixed = mixed + jnp.dot(fn.astype(BF16), wout_ref[D_CONV:, :], preferred_element_type=F32)
    x1 = x_ref[0] + mod_ref[0, 2:3, :] * (_rms(mixed) * gpost_ref[...])
    x1_ref[0] = x1
    h2 = _rms(x1) * gpre_ref[...]
    h2 = h2 * (1.0 + mod_ref[0, 4:5, :]) + mod_ref[0, 3:4, :]
    h2_ref[0] = _pack_bf16_pairs(h2)
    hb = h2.astype(BF16)
    hid = _silu(jnp.dot(hb, wsg_ref[...], preferred_element_type=F32))
    hid = hid * jnp.dot(hb, wsu_ref[...], preferred_element_type=F32)
    ysh_ref[0] = _pack_bf16_pairs(jnp.dot(hid.astype(BF16), wsd_ref[...], preferred_element_type=F32))


def _mix(x, cn, fy, mod3, g_fnet, w_out_b, g_post, g_pre, wsg_b, wsu_b, wsd_b, ts):
    b, s, d = x.shape
    tok = lambda c: pl.BlockSpec((1, ts, c), lambda i, t: (i, t, 0))
    full = lambda a: pl.BlockSpec(a.shape, lambda i, t: (0,) * a.ndim)
    return pl.pallas_call(
        _mix_kernel,
        grid=(b, s // ts),
        in_specs=[tok(d), tok(D_CONV), tok(D_FNET),
                  pl.BlockSpec((1, 6, d), lambda i, t: (i, 0, 0)),
                  full(g_fnet), full(w_out_b), full(g_post), full(g_pre),
                  full(wsg_b), full(wsu_b), full(wsd_b)],
        out_specs=[tok(d), tok(d // 2), tok(d // 2)],
        out_shape=[jax.ShapeDtypeStruct((b, s, d), F32),
                   jax.ShapeDtypeStruct((b, s, d // 2), jnp.uint32),
                   jax.ShapeDtypeStruct((b, s, d // 2), jnp.uint32)],
        compiler_params=_cparams(("parallel", "parallel")),
        name="mix",
    )(x, cn, fy, mod3, g_fnet, w_out_b, g_post, g_pre, wsg_b, wsu_b, wsd_b)


def _router_kernel(h_ref, wr_ref, br_ref, u_ref, idx_ref, rank_ref, wgt_ref, cnt_ref, carry_ref, *, tr):
    e = N_EXPERTS

    @pl.when((pl.program_id(0) == 0) & (pl.program_id(1) == 0))
    def _():
        carry_ref[...] = jnp.zeros_like(carry_ref)

    logits = lax.dot_general(wr_ref[...], _unpack_bf16_pairs(h_ref[0]).astype(BF16), (((1,), (1,)), ((), ())),
                             preferred_element_type=F32)
    sc = jax.nn.sigmoid(logits)
    sb = sc + br_ref[...]
    ninf = jnp.float32(-jnp.inf)

    io_g = lax.broadcasted_iota(jnp.int32, (GROUP_SIZE, tr), 0).astype(F32)
    gs = []
    for g in range(N_EXPERT_GROUPS):
        blk = sb[g * GROUP_SIZE:(g + 1) * GROUP_SIZE]
        m1 = jnp.max(blk, axis=0, keepdims=True)
        i1 = jnp.min(jnp.where(blk == m1, io_g, float(GROUP_SIZE)), axis=0, keepdims=True)
        m2 = jnp.max(jnp.where(io_g == i1, ninf, blk), axis=0, keepdims=True)
        gs.append(m1 + m2)
    masked = []
    for g in range(N_EXPERT_GROUPS):
        beat = jnp.zeros((1, tr), F32)
        for o in range(N_EXPERT_GROUPS):
            if o == g:
                continue
            wins = (gs[o] > gs[g]) | ((gs[o] == gs[g]) & (o < g))
            beat = beat + wins.astype(F32)
        keep = beat < float(TOPK_GROUPS)
        masked.append(jnp.where(keep, sb[g * GROUP_SIZE:(g + 1) * GROUP_SIZE], ninf))
    v = jnp.concatenate(masked, axis=0)

    io_e = lax.broadcasted_iota(jnp.int32, (e, tr), 0).astype(F32)
    ids, ws = [], []
    sel = jnp.zeros((e, tr), F32)
    for _ in range(TOP_K):
        m = jnp.max(v, axis=0, keepdims=True)
        i = jnp.min(jnp.where(v == m, io_e, float(e)), axis=0, keepdims=True)
        oh = io_e == i
        ids.append(i)
        ws.append(jnp.sum(jnp.where(oh, sc, 0.0), axis=0, keepdims=True))
        v = jnp.where(oh, ninf, v)
        sel = sel + oh.astype(F32)

    wsum = ws[0]
    for k in range(1, TOP_K):
        wsum = wsum + ws[k]
    wgt_ref[...] = jnp.concatenate([w / wsum * ROUTED_SCALE for w in ws], axis=0)
    idx_ref[...] = jnp.concatenate(ids, axis=0).astype(jnp.int32)

    excl = jnp.dot(sel.astype(BF16), u_ref[...], preferred_element_type=F32)
    base = carry_ref[:, 0:1]
    rank_full = base + excl
    ranks = [jnp.sum(jnp.where(io_e == ids[k], rank_full, 0.0), axis=0, keepdims=True)
             for k in range(TOP_K)]
    rank_ref[...] = jnp.concatenate(ranks, axis=0).astype(jnp.int32)
    new = base + jnp.sum(sel, axis=1, keepdims=True)
    carry_ref[...] = jnp.broadcast_to(new, carry_ref.shape)
    cnt_ref[...] = jnp.broadcast_to(new, cnt_ref.shape).astype(jnp.int32)


def _router(h2p, wr_t_b, b_router_col, tr):
    b, s, d = h2p.shape
    t = b * s
    nt = s // tr
    u = jnp.asarray(np.triu(np.ones((tr, tr), np.float32), k=1), BF16)
    col = lambda i, j: (0, i * nt + j)
    return pl.pallas_call(
        functools.partial(_router_kernel, tr=tr),
        grid=(b, nt),
        in_specs=[pl.BlockSpec((1, tr, d), lambda i, j: (i, j, 0)),
                  pl.BlockSpec(wr_t_b.shape, lambda i, j: (0, 0)),
                  pl.BlockSpec((N_EXPERTS, 1), lambda i, j: (0, 0)),
                  pl.BlockSpec((tr, tr), lambda i, j: (0, 0))],
        out_specs=[pl.BlockSpec((TOP_K, tr), col), pl.BlockSpec((TOP_K, tr), col),
                   pl.BlockSpec((TOP_K, tr), col),
                   pl.BlockSpec((N_EXPERTS, 128), lambda i, j: (0, 0))],
        out_shape=[jax.ShapeDtypeStruct((TOP_K, t), jnp.int32),
                   jax.ShapeDtypeStruct((TOP_K, t), jnp.int32),
                   jax.ShapeDtypeStruct((TOP_K, t), F32),
                   jax.ShapeDtypeStruct((N_EXPERTS, 128), jnp.int32)],
        scratch_shapes=[pltpu.VMEM((N_EXPERTS, 128), F32)],
        compiler_params=_cparams(("arbitrary", "arbitrary")),
        name="router",
    )(h2p, wr_t_b, b_router_col, u)


def _dest_kernel(pstart_ref, idx_ref, rank_ref, dest_ref):
    idx = idx_ref[...]

    def body(g, acc):
        for j in range(8):
            e = g * 8 + j
            acc = jnp.where(idx == e, pstart_ref[e], acc)
        return acc

    dest_ref[...] = lax.fori_loop(0, N_EXPERTS // 8, body, jnp.zeros_like(idx)) + rank_ref[...]


def _dest(pstarts, idx, rank, tl):
    k, t = idx.shape
    grid_spec = pltpu.PrefetchScalarGridSpec(
        num_scalar_prefetch=1,
        grid=(t // tl,),
        in_specs=[pl.BlockSpec((k, tl), lambda i, ps: (0, i)), pl.BlockSpec((k, tl), lambda i, ps: (0, i))],
        out_specs=pl.BlockSpec((k, tl), lambda i, ps: (0, i)),
    )
    return pl.pallas_call(
        _dest_kernel,
        grid_spec=grid_spec,
        out_shape=jax.ShapeDtypeStruct((k, t), jnp.int32),
        compiler_params=_cparams(("parallel",)),
        name="dest",
    )(pstarts, idx, rank)


def _sc_mesh():
    return plsc.VectorSubcoreMesh(core_axis_name="c", subcore_axis_name="s",
                                  num_cores=SC_CORES, num_subcores=SC_SUBCORES)


def _sc_worker_base(per_worker):
    return (lax.axis_index("s") * SC_CORES + lax.axis_index("c")) * per_worker


def _dispatch(h2_flat, dest, p_rows):
    t, dh = h2_flat.shape
    r = SC_ROWS
    per_w = t // SC_WORKERS
    nchunk = per_w // r
    assert per_w % (2 * r) == 0

    @functools.partial(
        pl.kernel, mesh=_sc_mesh(),
        out_type=jax.ShapeDtypeStruct((p_rows, dh), h2_flat.dtype),
        scratch_types=[pltpu.VMEM((2, TOP_K, r), jnp.int32), pltpu.VMEM((2, r, dh), h2_flat.dtype),
                       pltpu.SemaphoreType.DMA((2,)), pltpu.SemaphoreType.DMA((2,))],
        name="sc_dispatch",
    )
    def k(rows_hbm, dest_hbm, out_hbm, idx_v, rows_v, lsem, ssem):
        base = _sc_worker_base(per_w)

        def load(ci, slot):
            t0 = base + ci * r
            for kk in range(TOP_K):
                pltpu.sync_copy(dest_hbm.at[kk, pl.ds(t0, r)], idx_v.at[slot, kk])
            pltpu.async_copy(rows_hbm.at[pl.ds(t0, r)], rows_v.at[slot], lsem.at[slot])

        def scatter(ci, slot):
            t0 = base + ci * r
            pltpu.make_async_copy(rows_hbm.at[pl.ds(t0, r)], rows_v.at[slot], lsem.at[slot]).wait()
            for kk in range(TOP_K):
                pltpu.async_copy(rows_v.at[slot], out_hbm.at[idx_v.at[slot, kk]], ssem.at[slot])

        def drain(slot):
            for kk in range(TOP_K):
                pltpu.make_async_copy(rows_v.at[slot], out_hbm.at[idx_v.at[slot, kk]], ssem.at[slot]).wait()

        load(0, 0)

        @pl.loop(0, nchunk, step=2)
        def _(c0):
            for s in range(2):
                ci = c0 + s

                @pl.when(ci + 1 < nchunk)
                def _():
                    @pl.when(ci >= 1)
                    def _():
                        drain(1 - s)
                    load(ci + 1, 1 - s)

                scatter(ci, s)

        drain(0)
        drain(1)

    return k(h2_flat, dest)


def _gather(obuf, dest):
    _, dh = obuf.shape
    kk_n, t = dest.shape
    r = SC_ROWS
    per_w = t // SC_WORKERS
    nitem = (per_w // r) * kk_n
    assert per_w % r == 0 and nitem % 2 == 0

    @functools.partial(
        pl.kernel, mesh=_sc_mesh(),
        out_type=jax.ShapeDtypeStruct((kk_n, t, dh), obuf.dtype),
        scratch_types=[pltpu.VMEM((2, r), jnp.int32), pltpu.VMEM((2, r, dh), obuf.dtype),
                       pltpu.SemaphoreType.DMA((2,)), pltpu.SemaphoreType.DMA((2,))],
        name="sc_gather",
    )
    def k(table_hbm, idx_hbm, out_hbm, idx_v, rows_v, gsem, wsem):
        base = _sc_worker_base(per_w)

        def item(i):
            return i % kk_n, base + (i // kk_n) * r

        def start_gather(i, slot):
            kk, t0 = item(i)
            pltpu.sync_copy(idx_hbm.at[kk, pl.ds(t0, r)], idx_v.at[slot])
            pltpu.async_copy(table_hbm.at[idx_v.at[slot]], rows_v.at[slot], gsem.at[slot])

        def finish(i, slot):
            kk, t0 = item(i)
            pltpu.make_async_copy(table_hbm.at[idx_v.at[slot]], rows_v.at[slot], gsem.at[slot]).wait()
            pltpu.async_copy(rows_v.at[slot], out_hbm.at[kk, pl.ds(t0, r)], wsem.at[slot])

        def wait_write(i, slot):
            kk, t0 = item(i)
            pltpu.make_async_copy(rows_v.at[slot], out_hbm.at[kk, pl.ds(t0, r)], wsem.at[slot]).wait()

        start_gather(0, 0)

        @pl.loop(0, nitem, step=2)
        def _(i0):
            for s in range(2):
                i = i0 + s

                @pl.when(i + 1 < nitem)
                def _():
                    @pl.when(i >= 1)
                    def _():
                        wait_write(i - 1, 1 - s)
                    start_gather(i + 1, 1 - s)

                finish(i, s)

        wait_write(nitem - 2, 0)
        wait_write(nitem - 1, 1)

    return k(obuf, dest)


def _experts_kernel(be_ref, nvalid_ref, run_ref, nxt_ref, nused_ref, x_ref, wg_hbm, wu_hbm, wd_hbm, o_ref,
                    wg_f, wu_f, wd_f, wg_s, wu_s, wd_s, sem):
    b = pl.program_id(0)

    def weight_copies(e, slot):
        return (pltpu.make_async_copy(wg_hbm.at[e], wg_f.at[slot], sem.at[slot]),
                pltpu.make_async_copy(wu_hbm.at[e], wu_f.at[slot], sem.at[slot]),
                pltpu.make_async_copy(wd_hbm.at[e], wd_f.at[slot], sem.at[slot]))

    @pl.when(b < nused_ref[0])
    def _():
        e = be_ref[b]
        slot = run_ref[b] % 2

        @pl.when(b == 0)
        def _():
            for c in weight_copies(e, slot):
                c.start()

        @pl.when((b == 0) | (e != be_ref[jnp.maximum(b - 1, 0)]))
        def _():
            for c in weight_copies(e, slot):
                c.wait()

            @pl.when(nxt_ref[b] >= 0)
            def _():
                for c in weight_copies(nxt_ref[b], 1 - slot):
                    c.start()

            wg_s[...] = wg_f[slot].astype(BF16)
            wu_s[...] = wu_f[slot].astype(BF16)
            wd_s[...] = wd_f[slot].astype(BF16)

        rows = lax.broadcasted_iota(jnp.int32, x_ref.shape, 0)
        xp = jnp.where(rows < nvalid_ref[b], x_ref[...], jnp.uint32(0))
        x = _unpack_bf16_pairs(xp).astype(BF16)
        g = jnp.dot(x, wg_s[...], preferred_element_type=F32)
        u = jnp.dot(x, wu_s[...], preferred_element_type=F32)
        hid = (_silu(g) * u).astype(BF16)
        o_ref[...] = _pack_bf16_pairs(jnp.dot(hid, wd_s[...], preferred_element_type=F32))

    @pl.when(b >= nused_ref[0])
    def _():
        o_ref[...] = jnp.zeros_like(o_ref)


def _experts(xbuf, block_e, nvalid, run, nxt, nused, w_gate, w_up, w_down, bm):
    p, dh = xbuf.shape
    nb = p // bm
    d, de = w_gate.shape[1:]
    hbm = pl.BlockSpec(memory_space=pl.ANY)
    grid_spec = pltpu.PrefetchScalarGridSpec(
        num_scalar_prefetch=5,
        grid=(nb,),
        in_specs=[pl.BlockSpec((bm, dh), lambda b, be, nv, rn, nx, nu: (jnp.minimum(b, nu[0] - 1), 0)),
                  hbm, hbm, hbm],
        out_specs=pl.BlockSpec((bm, dh), lambda b, be, nv, rn, nx, nu: (b, 0)),
        scratch_shapes=[pltpu.VMEM((2, d, de), F32), pltpu.VMEM((2, d, de), F32), pltpu.VMEM((2, de, d), F32),
                        pltpu.VMEM((d, de), BF16), pltpu.VMEM((d, de), BF16), pltpu.VMEM((de, d), BF16),
                        pltpu.SemaphoreType.DMA((2,))],
    )
    return pl.pallas_call(
        _experts_kernel,
        grid_spec=grid_spec,
        out_shape=jax.ShapeDtypeStruct((p, dh), jnp.uint32),
        compiler_params=_cparams(("arbitrary",)),
        name="experts",
    )(block_e, nvalid, run, nxt, nused, xbuf, w_gate, w_up, w_down)


def _combine_kernel(gb_ref, w_ref, x1_ref, ysh_ref, mod_ref, g_ref, out_ref):
    y = _unpack_bf16_pairs(ysh_ref[...])
    for k in range(TOP_K):
        y = y + _unpack_bf16_pairs(gb_ref[k]) * w_ref[:, k:k + 1]
    out_ref[...] = x1_ref[...] + mod_ref[0, 5:6, :] * (_rms(y) * g_ref[...])


def _combine(gbuf, w_rows, x1_flat, ysh_flat, mod3, g_post, s, tc):
    t, d = x1_flat.shape
    per_seq = s // tc
    tok = pl.BlockSpec((tc, d), lambda i: (i, 0))
    return pl.pallas_call(
        _combine_kernel,
        grid=(t // tc,),
        in_specs=[pl.BlockSpec((TOP_K, tc, d // 2), lambda i: (0, i, 0)),
                  pl.BlockSpec((tc, TOP_K), lambda i: (i, 0)),
                  tok, pl.BlockSpec((tc, d // 2), lambda i: (i, 0)),
                  pl.BlockSpec((1, 6, d), lambda i: (i // per_seq, 0, 0)),
                  pl.BlockSpec((1, d), lambda i: (0, 0))],
        out_specs=tok,
        out_shape=jax.ShapeDtypeStruct((t, d), F32),
        compiler_params=_cparams(("parallel",)),
        name="combine",
    )(gbuf, w_rows, x1_flat, ysh_flat, mod3, g_post)


def _channel_dft_table():
    c = np.arange(GROUP_DIM)
    ang = 2.0 * np.pi * ((c[:, None] * c[None, :]) % GROUP_DIM) / GROUP_DIM
    eye = np.eye(D_FNET // GROUP_DIM)
    scale = 1.0 / np.sqrt(GROUP_DIM)
    cs = np.concatenate([np.kron(eye, np.cos(ang)), np.kron(eye, np.sin(ang))], axis=1) * scale
    return jnp.asarray(cs, BF16)


def _expert_block_rows(n_assign):
    return 512 if n_assign // N_EXPERTS >= 256 else 256


def _layer(x, mod, p):
    b, s, d = x.shape
    t = b * s
    mod3 = mod.reshape(b, 6, d)
    ts = min(512, s)
    g_tab, h_tab = _dft_tables(s)

    v, zcs = _inproj(x, mod3, p["g_mix_pre"], p["w_in_b"], _channel_dft_table(), ts)
    cn = _conv(v, p["conv_w"], p["conv_b"], p["conv_ln_g"], p["conv_ln_b"], p["g_conv_out"], min(256, s))
    fy = _seqdft(zcs, g_tab, h_tab)
    x1, h2, ysh = _mix(x, cn, fy, mod3, p["g_fnet_out"], p["w_out_b"], p["g_mix_post"], p["g_ffn_pre"],
                       p["wsg_b"], p["wsu_b"], p["wsd_b"], min(256, s))
    idx, rank, wgt, cnt = _router(h2, p["wr_t_b"], p["b_router_col"], min(256, s))

    n = t * TOP_K
    bm = _expert_block_rows(n)
    counts = cnt[:, 0]
    pcounts = (counts + bm - 1) // bm * bm
    pends = jnp.cumsum(pcounts)
    pstarts = pends - pcounts
    dest = _dest(pstarts.astype(jnp.int32), idx, rank, min(2048, t))
    nb = (n + N_EXPERTS * (bm - 1) + bm - 1) // bm
    nused = (pends[-1] // bm).astype(jnp.int32)
    blk = jnp.minimum(jnp.arange(nb, dtype=jnp.int32), nused - 1) * bm
    block_e = jnp.sum((pends[None, :] <= blk[:, None]).astype(jnp.int32), axis=1)
    block_e = jnp.minimum(block_e, N_EXPERTS - 1)
    nvalid = jnp.clip(pstarts[block_e] + counts[block_e] - blk, 0, bm).astype(jnp.int32)
    first = jnp.concatenate([jnp.ones((1,), jnp.int32), (block_e[1:] != block_e[:-1]).astype(jnp.int32)])
    run = jnp.cumsum(first) - 1
    eid = jnp.arange(N_EXPERTS, dtype=jnp.int32)
    later = lax.cummin(jnp.where(pcounts > 0, eid, N_EXPERTS)[::-1])[::-1]
    nxt_e = jnp.concatenate([later[1:], jnp.full((1,), N_EXPERTS, jnp.int32)])
    nxt = jnp.where(nxt_e < N_EXPERTS, nxt_e, -1)[block_e].astype(jnp.int32)

    xbuf = _dispatch(h2.reshape(t, d // 2), dest, nb * bm)
    obuf = _experts(xbuf, block_e, nvalid, run.astype(jnp.int32), nxt, nused.reshape(1),
                    p["w_gate"], p["w_up"], p["w_down"], bm)
    gbuf = _gather(obuf, dest)
    out = _combine(gbuf, wgt.T, x1.reshape(t, d), ysh.reshape(t, d // 2), mod3, p["g_ffn_post"], s, min(256, s))
    return out.reshape(b, s, d)


def kernel(x_prompt, x_sample, c_prompt, c_sample, w_ada, b_ada, g_mix_pre, w_in, conv_w, conv_b, conv_ln_g, conv_ln_b, g_conv_out, g_fnet_out, w_out, g_mix_post, g_ffn_pre, w_router, b_router, w_gate, w_up, w_down, ws_gate, ws_up, ws_down, g_ffn_post):
    assert w_ada.shape[0] == 1, "single-layer kernel"
    bp, bs = c_prompt.shape[0], c_sample.shape[0]
    rows = -(-(bp + bs) // 8) * 8
    c_all = jnp.zeros((rows, D_MODEL), F32).at[:bp].set(c_prompt).at[bp:bp + bs].set(c_sample)
    mod = _ada(c_all, w_ada[0], b_ada)
    p = {
        "g_mix_pre": g_mix_pre, "w_in_b": w_in[0].astype(BF16),
        "conv_w": conv_w[0], "conv_b": conv_b, "conv_ln_g": conv_ln_g, "conv_ln_b": conv_ln_b,
        "g_conv_out": g_conv_out, "g_fnet_out": g_fnet_out, "w_out_b": w_out[0].astype(BF16),
        "g_mix_post": g_mix_post, "g_ffn_pre": g_ffn_pre,
        "wr_t_b": w_router[0].T.astype(BF16), "b_router_col": b_router[0][:, None],
        "w_gate": w_gate[0], "w_up": w_up[0], "w_down": w_down[0],
        "wsg_b": ws_gate[0].astype(BF16), "wsu_b": ws_up[0].astype(BF16), "wsd_b": ws_down[0].astype(BF16),
        "g_ffn_post": g_ffn_post,
    }
    y_prompt = _layer(x_prompt, mod[:bp], p)
    y_sample = _layer(x_sample, mod[bp:bp + bs], p)
    return (y_prompt, y_sample)
```

```python
import functools

import numpy as np
import jax
import jax.numpy as jnp
from jax import lax
from jax.experimental import pallas as pl
from jax.experimental.pallas import tpu as pltpu
from jax.experimental.pallas import tpu_sc as plsc

F32 = jnp.float32
BF16 = jnp.bfloat16

D_MODEL = 1024
D_CONV = 512
D_FNET = 512
GROUP_DIM = 64
CONV_WIDTH = 31
N_EXPERTS = 256
TOP_K = 8
N_EXPERT_GROUPS = 8
GROUP_SIZE = N_EXPERTS // N_EXPERT_GROUPS
TOPK_GROUPS = 4
ROUTED_SCALE = 2.5
EPS = 1e-6

DFT_S1 = 128
DFT_CH = 128
HALO = 16
VMEM_LIMIT = 56 * 1024 * 1024
SC_CORES = 2
SC_SUBCORES = 16
SC_WORKERS = SC_CORES * SC_SUBCORES
SC_ROWS = 64
SC_LANES = 16
SC_SUM_TOKENS = 8


def _cparams(sem, vmem=None):
    return pltpu.CompilerParams(dimension_semantics=sem, vmem_limit_bytes=vmem or VMEM_LIMIT)


def _rms(x):
    return x * lax.rsqrt(jnp.mean(x * x, axis=-1, keepdims=True) + EPS)


def _silu(x):
    return x * jax.nn.sigmoid(x)


def _pack_bf16_pairs(x):
    c = x.shape[-1] // 2
    bits = lax.bitcast_convert_type(x.astype(BF16).astype(F32), jnp.uint32)
    return (bits[:, :c] >> 16) | bits[:, c:]


def _unpack_bf16_pairs(p):
    lo = lax.bitcast_convert_type(p << 16, F32)
    hi = lax.bitcast_convert_type(p & jnp.uint32(0xFFFF0000), F32)
    return jnp.concatenate([lo, hi], axis=-1)


def _ada_kernel(c_ref, w_ref, b_ref, o_ref):
    o_ref[...] = jnp.dot(_silu(c_ref[...]), w_ref[...], preferred_element_type=F32) + b_ref[...]


def _ada(c, w_ada, b_ada):
    bp, d = c.shape
    n = w_ada.shape[1]
    return pl.pallas_call(
        _ada_kernel,
        grid=(n // d,),
        in_specs=[pl.BlockSpec((bp, d), lambda j: (0, 0)),
                  pl.BlockSpec((d, d), lambda j: (0, j)),
                  pl.BlockSpec((1, d), lambda j: (0, j))],
        out_specs=pl.BlockSpec((bp, d), lambda j: (0, j)),
        out_shape=jax.ShapeDtypeStruct((bp, n), F32),
        compiler_params=_cparams(("parallel",)),
        name="ada",
    )(c, w_ada, b_ada)


def _inproj_kernel(x_ref, mod_ref, g_ref, win_ref, cs_ref, v_ref, z_ref):
    x = x_ref[0]
    h = _rms(x) * g_ref[...]
    h = h * (1.0 + mod_ref[0, 1:2, :]) + mod_ref[0, 0:1, :]
    u = jnp.dot(h.astype(BF16), win_ref[...], preferred_element_type=F32)
    a = u[:, :D_CONV]
    gt = u[:, D_CONV:2 * D_CONV]
    z = u[:, 2 * D_CONV:]
    v_ref[0] = a * jax.nn.sigmoid(gt)
    z_ref[0] = jnp.dot(z.astype(BF16), cs_ref[...], preferred_element_type=F32)


def _inproj(x, mod3, g_mix_pre, w_in_b, cs_b, ts):
    b, s, d = x.shape
    return pl.pallas_call(
        _inproj_kernel,
        grid=(b, s // ts),
        in_specs=[pl.BlockSpec((1, ts, d), lambda i, t: (i, t, 0)),
                  pl.BlockSpec((1, 6, d), lambda i, t: (i, 0, 0)),
                  pl.BlockSpec((1, d), lambda i, t: (0, 0)),
                  pl.BlockSpec(w_in_b.shape, lambda i, t: (0, 0)),
                  pl.BlockSpec(cs_b.shape, lambda i, t: (0, 0))],
        out_specs=[pl.BlockSpec((1, ts, D_CONV), lambda i, t: (i, t, 0)),
                   pl.BlockSpec((1, ts, 2 * D_FNET), lambda i, t: (i, t, 0))],
        out_shape=[jax.ShapeDtypeStruct((b, s, D_CONV), F32),
                   jax.ShapeDtypeStruct((b, s, 2 * D_FNET), F32)],
        compiler_params=_cparams(("parallel", "parallel")),
        name="inproj",
    )(x, mod3, g_mix_pre, w_in_b, cs_b)


def _conv_kernel(vp_ref, v_ref, vn_ref, w_ref, b_ref, lg_ref, lb_ref, go_ref, o_ref, pad_ref, sh_ref, *, ts, rc):
    t = pl.program_id(1)
    nt = pl.num_programs(1)
    pad_ref[0:HALO, :] = jnp.where(t > 0, vp_ref[0], 0.0)
    pad_ref[HALO:HALO + ts, :] = v_ref[0]
    pad_ref[HALO + ts:HALO + ts + HALO, :] = jnp.where(t < nt - 1, vn_ref[0], 0.0)
    span = ts + 2 * HALO - 8
    for m in range(8):
        sh_ref[m] = pad_ref[m:m + span, :]
    off = HALO - CONV_WIDTH // 2
    for c in range(ts // rc):
        r0 = c * rc
        acc = jnp.zeros((rc // 8, 8, D_CONV), F32)
        for j in range(CONV_WIDTH):
            m, q = (off + j) % 8, (off + j) // 8
            tap = sh_ref[m, r0 + 8 * q:r0 + 8 * q + rc, :].reshape(rc // 8, 8, D_CONV)
            acc = acc + tap * w_ref[j][None]
        acc = acc.reshape(rc, D_CONV) + b_ref[...]
        mu = jnp.mean(acc, axis=-1, keepdims=True)
        xc = acc - mu
        var = jnp.mean(xc * xc, axis=-1, keepdims=True)
        y = xc * lax.rsqrt(var + EPS) * lg_ref[...] + lb_ref[...]
        y = _silu(y)
        y = _rms(y) * go_ref[...]
        o_ref[0, r0:r0 + rc, :] = y.astype(o_ref.dtype)


def _conv(v, conv_w, conv_b, ln_g, ln_b, g_out, ts, rc=32):
    b, s, c = v.shape
    hb = ts // HALO
    nh = s // HALO
    vec = pl.BlockSpec((1, c), lambda i, t: (0, 0))
    return pl.pallas_call(
        functools.partial(_conv_kernel, ts=ts, rc=rc),
        grid=(b, s // ts),
        in_specs=[pl.BlockSpec((1, HALO, c), lambda i, t: (i, jnp.maximum(t * hb - 1, 0), 0)),
                  pl.BlockSpec((1, ts, c), lambda i, t: (i, t, 0)),
                  pl.BlockSpec((1, HALO, c), lambda i, t: (i, jnp.minimum((t + 1) * hb, nh - 1), 0)),
                  pl.BlockSpec((CONV_WIDTH, 8, c), lambda i, t: (0, 0, 0)),
                  vec, vec, vec, vec],
        out_specs=pl.BlockSpec((1, ts, c), lambda i, t: (i, t, 0)),
        out_shape=jax.ShapeDtypeStruct((b, s, c), BF16),
        scratch_shapes=[pltpu.VMEM((ts + 2 * HALO, c), F32), pltpu.VMEM((8, ts + 2 * HALO - 8, c), F32)],
        compiler_params=_cparams(("parallel", "parallel")),
        name="conv",
    )(v, v, v, jnp.broadcast_to(conv_w[:, None, :], (CONV_WIDTH, 8, c)), conv_b, ln_g, ln_b, g_out)


def _dft_tables(s):
    s1 = DFT_S1
    s2 = s // s1
    k1 = np.arange(s1)[None, :, None]
    p1 = np.arange(s1)[None, None, :]
    p2 = np.arange(s2)[:, None, None]
    ang = 2.0 * np.pi * ((k1 * (s2 * p1 + p2)) % s) / s
    g = np.concatenate([np.cos(ang), np.sin(ang)], axis=1)
    k2 = np.arange(s2)[:, None]
    q2 = np.arange(s2)[None, :]
    ang2 = 2.0 * np.pi * ((k2 * q2) % s2) / s2
    h = np.concatenate([np.cos(ang2), np.sin(ang2)], axis=1) / np.sqrt(s)
    return jnp.asarray(g, BF16), jnp.asarray(h, BF16)


def _seqdft_kernel(zc_ref, zs_ref, g_ref, h_ref, o_ref, scr_ref, *, s1, s2):
    ch = DFT_CH
    for p2 in range(s2):
        xc = zc_ref[0, pl.ds(p2, s1, stride=s2), :]
        xs = zs_ref[0, pl.ds(p2, s1, stride=s2), :]
        xx = jnp.concatenate([xc, xs], axis=-1).astype(BF16)
        r = jnp.dot(g_ref[p2], xx, preferred_element_type=F32)
        a_re = r[:s1, :ch] - r[s1:, ch:]
        a_im = -(r[:s1, ch:] + r[s1:, :ch])
        scr_ref[:, p2, :] = a_re
        scr_ref[:, s2 + p2, :] = a_im
    hmat = h_ref[...]
    for k1 in range(s1):
        y = jnp.dot(hmat, scr_ref[k1].astype(BF16), preferred_element_type=F32)
        o_ref[0, pl.ds(k1, s2, stride=s1), :] = y


def _seqdft(zcs, g_tab, h_tab):
    b, s, _ = zcs.shape
    s1 = DFT_S1
    s2 = s // s1
    nch = D_FNET // DFT_CH
    return pl.pallas_call(
        functools.partial(_seqdft_kernel, s1=s1, s2=s2),
        grid=(b, nch),
        in_specs=[pl.BlockSpec((1, s, DFT_CH), lambda i, c: (i, 0, c)),
                  pl.BlockSpec((1, s, DFT_CH), lambda i, c: (i, 0, nch + c)),
                  pl.BlockSpec(g_tab.shape, lambda i, c: (0, 0, 0)),
                  pl.BlockSpec(h_tab.shape, lambda i, c: (0, 0))],
        out_specs=pl.BlockSpec((1, s, DFT_CH), lambda i, c: (i, 0, c)),
        out_shape=jax.ShapeDtypeStruct((b, s, D_FNET), F32),
        scratch_shapes=[pltpu.VMEM((s1, 2 * s2, DFT_CH), F32)],
        compiler_params=_cparams(("parallel", "parallel")),
        name="seqdft",
    )(zcs, zcs, g_tab, h_tab)


def _mix_kernel(x_ref, cn_ref, fy_ref, mod_ref, gf_ref, wout_ref, gpost_ref, gpre_ref,
                wsg_ref, wsu_ref, wsd_ref, x1_ref, h2_ref, ysh_ref):
    fn = _rms(fy_ref[0]) * gf_ref[...]
    mixed = jnp.dot(cn_ref[0], wout_ref[:D_CONV, :], preferred_element_type=F32)
    mixed = mixed + jnp.dot(fn.astype(BF16), wout_ref[D_CONV:, :], preferred_element_type=F32)
    x1 = x_ref[0] + mod_ref[0, 2:3, :] * (_rms(mixed) * gpost_ref[...])
    x1_ref[0] = x1
    h2 = _rms(x1) * gpre_ref[...]
    h2 = h2 * (1.0 + mod_ref[0, 4:5, :]) + mod_ref[0, 3:4, :]
    h2_ref[0] = _pack_bf16_pairs(h2)
    hb = h2.astype(BF16)
    hid = _silu(jnp.dot(hb, wsg_ref[...], preferred_element_type=F32))
    hid = hid * jnp.dot(hb, wsu_ref[...], preferred_element_type=F32)
    ysh_ref[0] = _pack_bf16_pairs(jnp.dot(hid.astype(BF16), wsd_ref[...], preferred_element_type=F32))


def _mix(x, cn, fy, mod3, g_fnet, w_out_b, g_post, g_pre, wsg_b, wsu_b, wsd_b, ts):
    b, s, d = x.shape
    tok = lambda c: pl.BlockSpec((1, ts, c), lambda i, t: (i, t, 0))
    full = lambda a: pl.BlockSpec(a.shape, lambda i, t: (0,) * a.ndim)
    return pl.pallas_call(
        _mix_kernel,
        grid=(b, s // ts),
        in_specs=[tok(d), tok(D_CONV), tok(D_FNET),
                  pl.BlockSpec((1, 6, d), lambda i, t: (i, 0, 0)),
                  full(g_fnet), full(w_out_b), full(g_post), full(g_pre),
                  full(wsg_b), full(wsu_b), full(wsd_b)],
        out_specs=[tok(d), tok(d // 2), tok(d // 2)],
        out_shape=[jax.ShapeDtypeStruct((b, s, d), F32),
                   jax.ShapeDtypeStruct((b, s, d // 2), jnp.uint32),
                   jax.ShapeDtypeStruct((b, s, d // 2), jnp.uint32)],
        compiler_params=_cparams(("parallel", "parallel")),
        name="mix",
    )(x, cn, fy, mod3, g_fnet, w_out_b, g_post, g_pre, wsg_b, wsu_b, wsd_b)


def _router_kernel(h_ref, wr_ref, br_ref, u_ref, idx_ref, rank_ref, wgt_ref, cnt_ref, carry_ref, *, tr):
    e = N_EXPERTS

    @pl.when((pl.program_id(0) == 0) & (pl.program_id(1) == 0))
    def _():
        carry_ref[...] = jnp.zeros_like(carry_ref)

    logits = lax.dot_general(wr_ref[...], _unpack_bf16_pairs(h_ref[0]).astype(BF16), (((1,), (1,)), ((), ())),
                             preferred_element_type=F32)
    sc = jax.nn.sigmoid(logits)
    sb = sc + br_ref[...]
    ninf = jnp.float32(-jnp.inf)

    io_g = lax.broadcasted_iota(jnp.int32, (GROUP_SIZE, tr), 0).astype(F32)
    gs = []
    for g in range(N_EXPERT_GROUPS):
        blk = sb[g * GROUP_SIZE:(g + 1) * GROUP_SIZE]
        m1 = jnp.max(blk, axis=0, keepdims=True)
        i1 = jnp.min(jnp.where(blk == m1, io_g, float(GROUP_SIZE)), axis=0, keepdims=True)
        m2 = jnp.max(jnp.where(io_g == i1, ninf, blk), axis=0, keepdims=True)
        gs.append(m1 + m2)
    masked = []
    for g in range(N_EXPERT_GROUPS):
        beat = jnp.zeros((1, tr), F32)
        for o in range(N_EXPERT_GROUPS):
            if o == g:
                continue
            wins = (gs[o] > gs[g]) | ((gs[o] == gs[g]) & (o < g))
            beat = beat + wins.astype(F32)
        keep = beat < float(TOPK_GROUPS)
        masked.append(jnp.where(keep, sb[g * GROUP_SIZE:(g + 1) * GROUP_SIZE], ninf))
    v = jnp.concatenate(masked, axis=0)

    io_e = lax.broadcasted_iota(jnp.int32, (e, tr), 0).astype(F32)
    ids, ws = [], []
    sel = jnp.zeros((e, tr), F32)
    for _ in range(TOP_K):
        m = jnp.max(v, axis=0, keepdims=True)
        i = jnp.min(jnp.where(v == m, io_e, float(e)), axis=0, keepdims=True)
        oh = io_e == i
        ids.append(i)
        ws.append(jnp.sum(jnp.where(oh, sc, 0.0), axis=0, keepdims=True))
        v = jnp.where(oh, ninf, v)
        sel = sel + oh.astype(F32)

    wsum = ws[0]
    for k in range(1, TOP_K):
        wsum = wsum + ws[k]
    wgt_ref[...] = jnp.concatenate([w / wsum * ROUTED_SCALE for w in ws], axis=0)
    idx_ref[...] = jnp.concatenate(ids, axis=0).astype(jnp.int32)

    excl = jnp.dot(sel.astype(BF16), u_ref[...], preferred_element_type=F32)
    base = carry_ref[:, 0:1]
    rank_full = base + excl
    ranks = [jnp.sum(jnp.where(io_e == ids[k], rank_full, 0.0), axis=0, keepdims=True)
             for k in range(TOP_K)]
    rank_ref[...] = jnp.concatenate(ranks, axis=0).astype(jnp.int32)
    new = base + jnp.sum(sel, axis=1, keepdims=True)
    carry_ref[...] = jnp.broadcast_to(new, carry_ref.shape)
    cnt_ref[...] = jnp.broadcast_to(new, cnt_ref.shape).astype(jnp.int32)


def _router(h2p, wr_t_b, b_router_col, tr):
    b, s, d = h2p.shape
    t = b * s
    nt = s // tr
    u = jnp.asarray(np.triu(np.ones((tr, tr), np.float32), k=1), BF16)
    col = lambda i, j: (0, i * nt + j)
    return pl.pallas_call(
        functools.partial(_router_kernel, tr=tr),
        grid=(b, nt),
        in_specs=[pl.BlockSpec((1, tr, d), lambda i, j: (i, j, 0)),
                  pl.BlockSpec(wr_t_b.shape, lambda i, j: (0, 0)),
                  pl.BlockSpec((N_EXPERTS, 1), lambda i, j: (0, 0)),
                  pl.BlockSpec((tr, tr), lambda i, j: (0, 0))],
        out_specs=[pl.BlockSpec((TOP_K, tr), col), pl.BlockSpec((TOP_K, tr), col),
                   pl.BlockSpec((TOP_K, tr), col),
                   pl.BlockSpec((N_EXPERTS, 128), lambda i, j: (0, 0))],
        out_shape=[jax.ShapeDtypeStruct((TOP_K, t), jnp.int32),
                   jax.ShapeDtypeStruct((TOP_K, t), jnp.int32),
                   jax.ShapeDtypeStruct((TOP_K, t), F32),
                   jax.ShapeDtypeStruct((N_EXPERTS, 128), jnp.int32)],
        scratch_shapes=[pltpu.VMEM((N_EXPERTS, 128), F32)],
        compiler_params=_cparams(("arbitrary", "arbitrary")),
        name="router",
    )(h2p, wr_t_b, b_router_col, u)


def _dest_kernel(pstart_ref, idx_ref, rank_ref, dest_ref):
    idx = idx_ref[...]

    def body(g, acc):
        for j in range(8):
            e = g * 8 + j
            acc = jnp.where(idx == e, pstart_ref[e], acc)
        return acc

    dest_ref[...] = lax.fori_loop(0, N_EXPERTS // 8, body, jnp.zeros_like(idx)) + rank_ref[...]


def _dest(pstarts, idx, rank, tl):
    k, t = idx.shape
    grid_spec = pltpu.PrefetchScalarGridSpec(
        num_scalar_prefetch=1,
        grid=(t // tl,),
        in_specs=[pl.BlockSpec((k, tl), lambda i, ps: (0, i)), pl.BlockSpec((k, tl), lambda i, ps: (0, i))],
        out_specs=pl.BlockSpec((k, tl), lambda i, ps: (0, i)),
    )
    return pl.pallas_call(
        _dest_kernel,
        grid_spec=grid_spec,
        out_shape=jax.ShapeDtypeStruct((k, t), jnp.int32),
        compiler_params=_cparams(("parallel",)),
        name="dest",
    )(pstarts, idx, rank)


def _sc_mesh():
    return plsc.VectorSubcoreMesh(core_axis_name="c", subcore_axis_name="s",
                                  num_cores=SC_CORES, num_subcores=SC_SUBCORES)


def _sc_worker_base(per_worker):
    return (lax.axis_index("s") * SC_CORES + lax.axis_index("c")) * per_worker


def _dispatch(h2_flat, dest, p_rows):
    t, dh = h2_flat.shape
    r = SC_ROWS
    per_w = t // SC_WORKERS
    nchunk = per_w // r
    assert per_w % (2 * r) == 0

    @functools.partial(
        pl.kernel, mesh=_sc_mesh(),
        out_type=jax.ShapeDtypeStruct((p_rows, dh), h2_flat.dtype),
        scratch_types=[pltpu.VMEM((2, TOP_K, r), jnp.int32), pltpu.VMEM((2, r, dh), h2_flat.dtype),
                       pltpu.SemaphoreType.DMA((2,)), pltpu.SemaphoreType.DMA((2,))],
        name="sc_dispatch",
    )
    def k(rows_hbm, dest_hbm, out_hbm, idx_v, rows_v, lsem, ssem):
        base = _sc_worker_base(per_w)

        def load(ci, slot):
            t0 = base + ci * r
            for kk in range(TOP_K):
                pltpu.sync_copy(dest_hbm.at[kk, pl.ds(t0, r)], idx_v.at[slot, kk])
            pltpu.async_copy(rows_hbm.at[pl.ds(t0, r)], rows_v.at[slot], lsem.at[slot])

        def scatter(ci, slot):
            t0 = base + ci * r
            pltpu.make_async_copy(rows_hbm.at[pl.ds(t0, r)], rows_v.at[slot], lsem.at[slot]).wait()
            for kk in range(TOP_K):
                pltpu.async_copy(rows_v.at[slot], out_hbm.at[idx_v.at[slot, kk]], ssem.at[slot])

        def drain(slot):
            for kk in range(TOP_K):
                pltpu.make_async_copy(rows_v.at[slot], out_hbm.at[idx_v.at[slot, kk]], ssem.at[slot]).wait()

        load(0, 0)

        @pl.loop(0, nchunk, step=2)
        def _(c0):
            for s in range(2):
                ci = c0 + s

                @pl.when(ci + 1 < nchunk)
                def _():
                    @pl.when(ci >= 1)
                    def _():
                        drain(1 - s)
                    load(ci + 1, 1 - s)

                scatter(ci, s)

        drain(0)
        drain(1)

    return k(h2_flat, dest)


def _gather_sum(obuf, dest, wgt):
    _, dh = obuf.shape
    kk_n, t = dest.shape
    r, lanes = SC_SUM_TOKENS, SC_LANES
    assert r * lanes == 128
    per_w = t // SC_WORKERS
    nchunk = per_w // r
    idx_rows = per_w * kk_n // 128
    assert per_w % (2 * r) == 0 and idx_rows % 8 == 0
    nj = dh // lanes
    dest_c = dest.reshape(kk_n, t // r, r).transpose(1, 0, 2).reshape(t * kk_n // 128, 128)
    w_c = jnp.broadcast_to(wgt.reshape(kk_n, t // r, r).transpose(1, 0, 2)[..., None],
                           (t // r, kk_n, r, lanes)).reshape(t // r, kk_n, r * lanes)

    @functools.partial(
        pl.kernel, mesh=_sc_mesh(),
        out_type=jax.ShapeDtypeStruct((t, 2 * dh), F32),
        scratch_types=[pltpu.VMEM((idx_rows, 128), jnp.int32), pltpu.VMEM((2, kk_n, r, dh), obuf.dtype),
                       pltpu.VMEM((2, kk_n, r * lanes), F32), pltpu.VMEM((2, r, 2 * dh), F32),
                       pltpu.SemaphoreType.DMA((2,)), pltpu.SemaphoreType.DMA((2,))],
        compiler_params=pltpu.CompilerParams(needs_layout_passes=False),
        name="sc_gather_sum",
    )
    def k(table_hbm, idx_hbm, w_hbm, out_hbm, idx_v, rows_v, w_v, out_v, gsem, wsem):
        wid = lax.axis_index("s") * SC_CORES + lax.axis_index("c")
        base = wid * per_w
        cbase = wid * nchunk
        pltpu.sync_copy(idx_hbm.at[pl.ds(wid * idx_rows, idx_rows)], idx_v)

        def idx_list(ci, kk):
            off = (ci * kk_n + kk) * r
            return idx_v.at[off // 128, pl.ds(pl.multiple_of(off % 128, 8), r)]

        def copies(ci, slot):
            yield pltpu.make_async_copy(w_hbm.at[cbase + ci], w_v.at[slot], gsem.at[slot])
            for kk in range(kk_n):
                yield pltpu.make_async_copy(table_hbm.at[idx_list(ci, kk)], rows_v.at[slot, kk], gsem.at[slot])

        def out_copy(ci, slot):
            return pltpu.make_async_copy(out_v.at[slot], out_hbm.at[pl.ds(base + ci * r, r)], wsem.at[slot])

        def compute(slot):
            @pl.loop(0, r)
            def _(i):
                ws = [w_v[slot, kk, pl.ds(pl.multiple_of(i * lanes, lanes), lanes)] for kk in range(kk_n)]

                @plsc.parallel_loop(0, nj, unroll=4)
                def _(j):
                    col = pl.multiple_of(j * lanes, lanes)
                    lo = jnp.zeros((lanes,), F32)
                    hi = jnp.zeros((lanes,), F32)
                    for kk in range(kk_n):
                        v = rows_v[slot, kk, i, pl.ds(col, lanes)]
                        lo = lo + ws[kk] * plsc.bitcast(v << 16, F32)
                        hi = hi + ws[kk] * plsc.bitcast(v & jnp.uint32(0xFFFF0000), F32)
                    out_v[slot, i, pl.ds(col, lanes)] = lo
                    out_v[slot, i, pl.ds(dh + col, lanes)] = hi

        for c in copies(0, 0):
            c.start()

        @pl.loop(0, nchunk, step=2)
        def _(c0):
            for s in range(2):
                ci = c0 + s

                @pl.when(ci + 1 < nchunk)
                def _():
                    for c in copies(ci + 1, 1 - s):
                        c.start()

                for c in copies(ci, s):
                    c.wait()

                @pl.when(ci >= 2)
                def _():
                    out_copy(ci - 2, s).wait()

                compute(s)
                out_copy(ci, s).start()

        out_copy(nchunk - 2, 0).wait()
        out_copy(nchunk - 1, 1).wait()

    return k(obuf, dest_c, w_c)


def _experts_kernel(be_ref, nvalid_ref, run_ref, nxt_ref, nused_ref, x_ref, wg_hbm, wu_hbm, wd_hbm, o_ref,
                    wg_f, wu_f, wd_f, wg_s, wu_s, wd_s, sem):
    b = pl.program_id(0)

    def weight_copies(e, slot):
        return (pltpu.make_async_copy(wg_hbm.at[e], wg_f.at[slot], sem.at[slot]),
                pltpu.make_async_copy(wu_hbm.at[e], wu_f.at[slot], sem.at[slot]),
                pltpu.make_async_copy(wd_hbm.at[e], wd_f.at[slot], sem.at[slot]))

    @pl.when(b < nused_ref[0])
    def _():
        e = be_ref[b]
        slot = run_ref[b] % 2

        @pl.when(b == 0)
        def _():
            for c in weight_copies(e, slot):
                c.start()

        @pl.when((b == 0) | (e != be_ref[jnp.maximum(b - 1, 0)]))
        def _():
            for c in weight_copies(e, slot):
                c.wait()

            @pl.when(nxt_ref[b] >= 0)
            def _():
                for c in weight_copies(nxt_ref[b], 1 - slot):
                    c.start()

            wg_s[...] = wg_f[slot].astype(BF16)
            wu_s[...] = wu_f[slot].astype(BF16)
            wd_s[...] = wd_f[slot].astype(BF16)

        rows = lax.broadcasted_iota(jnp.int32, x_ref.shape, 0)
        xp = jnp.where(rows < nvalid_ref[b], x_ref[...], jnp.uint32(0))
        x = _unpack_bf16_pairs(xp).astype(BF16)
        g = jnp.dot(x, wg_s[...], preferred_element_type=F32)
        u = jnp.dot(x, wu_s[...], preferred_element_type=F32)
        hid = (_silu(g) * u).astype(BF16)
        o_ref[...] = _pack_bf16_pairs(jnp.dot(hid, wd_s[...], preferred_element_type=F32))

    @pl.when(b >= nused_ref[0])
    def _():
        o_ref[...] = jnp.zeros_like(o_ref)


def _experts(xbuf, block_e, nvalid, run, nxt, nused, w_gate, w_up, w_down, bm):
    p, dh = xbuf.shape
    nb = p // bm
    d, de = w_gate.shape[1:]
    hbm = pl.BlockSpec(memory_space=pl.ANY)
    grid_spec = pltpu.PrefetchScalarGridSpec(
        num_scalar_prefetch=5,
        grid=(nb,),
        in_specs=[pl.BlockSpec((bm, dh), lambda b, be, nv, rn, nx, nu: (jnp.minimum(b, nu[0] - 1), 0)),
                  hbm, hbm, hbm],
        out_specs=pl.BlockSpec((bm, dh), lambda b, be, nv, rn, nx, nu: (b, 0)),
        scratch_shapes=[pltpu.VMEM((2, d, de), F32), pltpu.VMEM((2, d, de), F32), pltpu.VMEM((2, de, d), F32),
                        pltpu.VMEM((d, de), BF16), pltpu.VMEM((d, de), BF16), pltpu.VMEM((de, d), BF16),
                        pltpu.SemaphoreType.DMA((2,))],
    )
    return pl.pallas_call(
        _experts_kernel,
        grid_spec=grid_spec,
        out_shape=jax.ShapeDtypeStruct((p, dh), jnp.uint32),
        compiler_params=_cparams(("arbitrary",)),
        name="experts",
    )(block_e, nvalid, run, nxt, nused, xbuf, w_gate, w_up, w_down)


def _combine_kernel(y_ref, x1_ref, ysh_ref, mod_ref, g_ref, out_ref):
    y = y_ref[...] + _unpack_bf16_pairs(ysh_ref[...])
    out_ref[...] = x1_ref[...] + mod_ref[0, 5:6, :] * (_rms(y) * g_ref[...])


def _combine(y_routed, x1_flat, ysh_flat, mod3, g_post, s, tc):
    t, d = x1_flat.shape
    per_seq = s // tc
    tok = pl.BlockSpec((tc, d), lambda i: (i, 0))
    return pl.pallas_call(
        _combine_kernel,
        grid=(t // tc,),
        in_specs=[tok, tok, pl.BlockSpec((tc, d // 2), lambda i: (i, 0)),
                  pl.BlockSpec((1, 6, d), lambda i: (i // per_seq, 0, 0)),
                  pl.BlockSpec((1, d), lambda i: (0, 0))],
        out_specs=tok,
        out_shape=jax.ShapeDtypeStruct((t, d), F32),
        compiler_params=_cparams(("parallel",)),
        name="combine",
    )(y_routed, x1_flat, ysh_flat, mod3, g_post)


def _channel_dft_table():
    c = np.arange(GROUP_DIM)
    ang = 2.0 * np.pi * ((c[:, None] * c[None, :]) % GROUP_DIM) / GROUP_DIM
    eye = np.eye(D_FNET // GROUP_DIM)
    scale = 1.0 / np.sqrt(GROUP_DIM)
    cs = np.concatenate([np.kron(eye, np.cos(ang)), np.kron(eye, np.sin(ang))], axis=1) * scale
    return jnp.asarray(cs, BF16)


def _expert_block_rows(n_assign):
    return 512 if n_assign // N_EXPERTS >= 256 else 256


def _layer(x, mod, p):
    b, s, d = x.shape
    t = b * s
    mod3 = mod.reshape(b, 6, d)
    ts = min(512, s)
    g_tab, h_tab = _dft_tables(s)

    v, zcs = _inproj(x, mod3, p["g_mix_pre"], p["w_in_b"], _channel_dft_table(), ts)
    cn = _conv(v, p["conv_w"], p["conv_b"], p["conv_ln_g"], p["conv_ln_b"], p["g_conv_out"], min(256, s))
    fy = _seqdft(zcs, g_tab, h_tab)
    x1, h2, ysh = _mix(x, cn, fy, mod3, p["g_fnet_out"], p["w_out_b"], p["g_mix_post"], p["g_ffn_pre"],
                       p["wsg_b"], p["wsu_b"], p["wsd_b"], min(256, s))
    idx, rank, wgt, cnt = _router(h2, p["wr_t_b"], p["b_router_col"], min(256, s))

    n = t * TOP_K
    bm = _expert_block_rows(n)
    counts = cnt[:, 0]
    pcounts = (counts + bm - 1) // bm * bm
    pends = jnp.cumsum(pcounts)
    pstarts = pends - pcounts
    dest = _dest(pstarts.astype(jnp.int32), idx, rank, min(2048, t))
    nb = (n + N_EXPERTS * (bm - 1) + bm - 1) // bm
    nused = (pends[-1] // bm).astype(jnp.int32)
    blk = jnp.minimum(jnp.arange(nb, dtype=jnp.int32), nused - 1) * bm
    block_e = jnp.sum((pends[None, :] <= blk[:, None]).astype(jnp.int32), axis=1)
    block_e = jnp.minimum(block_e, N_EXPERTS - 1)
    nvalid = jnp.clip(pstarts[block_e] + counts[block_e] - blk, 0, bm).astype(jnp.int32)
    first = jnp.concatenate([jnp.ones((1,), jnp.int32), (block_e[1:] != block_e[:-1]).astype(jnp.int32)])
    run = jnp.cumsum(first) - 1
    eid = jnp.arange(N_EXPERTS, dtype=jnp.int32)
    later = lax.cummin(jnp.where(pcounts > 0, eid, N_EXPERTS)[::-1])[::-1]
    nxt_e = jnp.concatenate([later[1:], jnp.full((1,), N_EXPERTS, jnp.int32)])
    nxt = jnp.where(nxt_e < N_EXPERTS, nxt_e, -1)[block_e].astype(jnp.int32)

    xbuf = _dispatch(h2.reshape(t, d // 2), dest, nb * bm)
    obuf = _experts(xbuf, block_e, nvalid, run.astype(jnp.int32), nxt, nused.reshape(1),
                    p["w_gate"], p["w_up"], p["w_down"], bm)
    y_routed = _gather_sum(obuf, dest, wgt)
    out = _combine(y_routed, x1.reshape(t, d), ysh.reshape(t, d // 2), mod3, p["g_ffn_post"], s, min(512, s))
    return out.reshape(b, s, d)


def kernel(x_prompt, x_sample, c_prompt, c_sample, w_ada, b_ada, g_mix_pre, w_in, conv_w, conv_b, conv_ln_g, conv_ln_b, g_conv_out, g_fnet_out, w_out, g_mix_post, g_ffn_pre, w_router, b_router, w_gate, w_up, w_down, ws_gate, ws_up, ws_down, g_ffn_post):
    assert w_ada.shape[0] == 1, "single-layer kernel"
    bp, bs = c_prompt.shape[0], c_sample.shape[0]
    rows = -(-(bp + bs) // 8) * 8
    c_all = jnp.zeros((rows, D_MODEL), F32).at[:bp].set(c_prompt).at[bp:bp + bs].set(c_sample)
    mod = _ada(c_all, w_ada[0], b_ada)
    p = {
        "g_mix_pre": g_mix_pre, "w_in_b": w_in[0].astype(BF16),
        "conv_w": conv_w[0], "conv_b": conv_b, "conv_ln_g": conv_ln_g, "conv_ln_b": conv_ln_b,
        "g_conv_out": g_conv_out, "g_fnet_out": g_fnet_out, "w_out_b": w_out[0].astype(BF16),
        "g_mix_post": g_mix_post, "g_ffn_pre": g_ffn_pre,
        "wr_t_b": w_router[0].T.astype(BF16), "b_router_col": b_router[0][:, None],
        "w_gate": w_gate[0], "w_up": w_up[0], "w_down": w_down[0],
        "wsg_b": ws_gate[0].astype(BF16), "wsu_b": ws_up[0].astype(BF16), "wsd_b": ws_down[0].astype(BF16),
        "g_ffn_post": g_ffn_post,
    }
    y_prompt = _layer(x_prompt, mod[:bp], p)
    y_sample = _layer(x_sample, mod[bp:bp + bs], p)
    return (y_prompt, y_sample)
```

```python
import functools

import numpy as np
import jax
import jax.numpy as jnp
from jax import lax
from jax.experimental import pallas as pl
from jax.experimental.pallas import tpu as pltpu
from jax.experimental.pallas import tpu_sc as plsc

F32 = jnp.float32
BF16 = jnp.bfloat16

D_MODEL = 1024
D_CONV = 512
D_FNET = 512
GROUP_DIM = 64
CONV_WIDTH = 31
N_EXPERTS = 256
TOP_K = 8
N_EXPERT_GROUPS = 8
GROUP_SIZE = N_EXPERTS // N_EXPERT_GROUPS
TOPK_GROUPS = 4
ROUTED_SCALE = 2.5
EPS = 1e-6

DFT_S1 = 128
DFT_CH = 128
HALO = 16
VMEM_LIMIT = 56 * 1024 * 1024
SC_CORES = 2
SC_SUBCORES = 16
SC_WORKERS = SC_CORES * SC_SUBCORES
SC_ROWS = 64
SC_LANES = 16
SC_SUM_TOKENS = 8


def _cparams(sem, vmem=None):
    return pltpu.CompilerParams(dimension_semantics=sem, vmem_limit_bytes=vmem or VMEM_LIMIT)


def _rms(x):
    return x * lax.rsqrt(jnp.mean(x * x, axis=-1, keepdims=True) + EPS)


def _silu(x):
    return x * jax.nn.sigmoid(x)


def _pack_bf16_pairs(x):
    c = x.shape[-1] // 2
    bits = lax.bitcast_convert_type(x.astype(BF16).astype(F32), jnp.uint32)
    return (bits[:, :c] >> 16) | bits[:, c:]


def _unpack_bf16_pairs(p):
    lo = lax.bitcast_convert_type(p << 16, F32)
    hi = lax.bitcast_convert_type(p & jnp.uint32(0xFFFF0000), F32)
    return jnp.concatenate([lo, hi], axis=-1)


def _ada_kernel(c_ref, w_ref, b_ref, o_ref):
    o_ref[...] = jnp.dot(_silu(c_ref[...]), w_ref[...], preferred_element_type=F32) + b_ref[...]


def _ada(c, w_ada, b_ada):
    bp, d = c.shape
    n = w_ada.shape[1]
    return pl.pallas_call(
        _ada_kernel,
        grid=(n // d,),
        in_specs=[pl.BlockSpec((bp, d), lambda j: (0, 0)),
                  pl.BlockSpec((d, d), lambda j: (0, j)),
                  pl.BlockSpec((1, d), lambda j: (0, j))],
        out_specs=pl.BlockSpec((bp, d), lambda j: (0, j)),
        out_shape=jax.ShapeDtypeStruct((bp, n), F32),
        compiler_params=_cparams(("parallel",)),
        name="ada",
    )(c, w_ada, b_ada)


def _inproj_kernel(x_ref, mod_ref, g_ref, win_ref, cs_ref, v_ref, z_ref):
    x = x_ref[0]
    h = _rms(x) * g_ref[...]
    h = h * (1.0 + mod_ref[0, 1:2, :]) + mod_ref[0, 0:1, :]
    u = jnp.dot(h.astype(BF16), win_ref[...], preferred_element_type=F32)
    a = u[:, :D_CONV]
    gt = u[:, D_CONV:2 * D_CONV]
    z = u[:, 2 * D_CONV:]
    v_ref[0] = a * jax.nn.sigmoid(gt)
    z_ref[0] = jnp.dot(z.astype(BF16), cs_ref[...], preferred_element_type=F32)


def _inproj(x, mod3, g_mix_pre, w_in_b, cs_b, ts):
    b, s, d = x.shape
    return pl.pallas_call(
        _inproj_kernel,
        grid=(b, s // ts),
        in_specs=[pl.BlockSpec((1, ts, d), lambda i, t: (i, t, 0)),
                  pl.BlockSpec((1, 6, d), lambda i, t: (i, 0, 0)),
                  pl.BlockSpec((1, d), lambda i, t: (0, 0)),
                  pl.BlockSpec(w_in_b.shape, lambda i, t: (0, 0)),
                  pl.BlockSpec(cs_b.shape, lambda i, t: (0, 0))],
        out_specs=[pl.BlockSpec((1, ts, D_CONV), lambda i, t: (i, t, 0)),
                   pl.BlockSpec((1, ts, 2 * D_FNET), lambda i, t: (i, t, 0))],
        out_shape=[jax.ShapeDtypeStruct((b, s, D_CONV), F32),
                   jax.ShapeDtypeStruct((b, s, 2 * D_FNET), F32)],
        compiler_params=_cparams(("parallel", "parallel")),
        name="inproj",
    )(x, mod3, g_mix_pre, w_in_b, cs_b)


def _conv_kernel(vp_ref, v_ref, vn_ref, w_ref, b_ref, lg_ref, lb_ref, go_ref, o_ref, pad_ref, sh_ref, *, ts, rc):
    t = pl.program_id(1)
    nt = pl.num_programs(1)
    pad_ref[0:HALO, :] = jnp.where(t > 0, vp_ref[0], 0.0)
    pad_ref[HALO:HALO + ts, :] = v_ref[0]
    pad_ref[HALO + ts:HALO + ts + HALO, :] = jnp.where(t < nt - 1, vn_ref[0], 0.0)
    span = ts + 2 * HALO - 8
    for m in range(8):
        sh_ref[m] = pad_ref[m:m + span, :]
    off = HALO - CONV_WIDTH // 2
    for c in range(ts // rc):
        r0 = c * rc
        acc = jnp.zeros((rc // 8, 8, D_CONV), F32)
        for j in range(CONV_WIDTH):
            m, q = (off + j) % 8, (off + j) // 8
            tap = sh_ref[m, r0 + 8 * q:r0 + 8 * q + rc, :].reshape(rc // 8, 8, D_CONV)
            acc = acc + tap * w_ref[j][None]
        acc = acc.reshape(rc, D_CONV) + b_ref[...]
        mu = jnp.mean(acc, axis=-1, keepdims=True)
        xc = acc - mu
        var = jnp.mean(xc * xc, axis=-1, keepdims=True)
        y = xc * lax.rsqrt(var + EPS) * lg_ref[...] + lb_ref[...]
        y = _silu(y)
        y = _rms(y) * go_ref[...]
        o_ref[0, r0:r0 + rc, :] = y.astype(o_ref.dtype)


def _conv(v, conv_w, conv_b, ln_g, ln_b, g_out, ts, rc=32):
    b, s, c = v.shape
    hb = ts // HALO
    nh = s // HALO
    vec = pl.BlockSpec((1, c), lambda i, t: (0, 0))
    return pl.pallas_call(
        functools.partial(_conv_kernel, ts=ts, rc=rc),
        grid=(b, s // ts),
        in_specs=[pl.BlockSpec((1, HALO, c), lambda i, t: (i, jnp.maximum(t * hb - 1, 0), 0)),
                  pl.BlockSpec((1, ts, c), lambda i, t: (i, t, 0)),
                  pl.BlockSpec((1, HALO, c), lambda i, t: (i, jnp.minimum((t + 1) * hb, nh - 1), 0)),
                  pl.BlockSpec((CONV_WIDTH, 8, c), lambda i, t: (0, 0, 0)),
                  vec, vec, vec, vec],
        out_specs=pl.BlockSpec((1, ts, c), lambda i, t: (i, t, 0)),
        out_shape=jax.ShapeDtypeStruct((b, s, c), BF16),
        scratch_shapes=[pltpu.VMEM((ts + 2 * HALO, c), F32), pltpu.VMEM((8, ts + 2 * HALO - 8, c), F32)],
        compiler_params=_cparams(("parallel", "parallel")),
        name="conv",
    )(v, v, v, jnp.broadcast_to(conv_w[:, None, :], (CONV_WIDTH, 8, c)), conv_b, ln_g, ln_b, g_out)


def _dft_tables(s):
    s1 = DFT_S1
    s2 = s // s1
    k1 = np.arange(s1)[None, :, None]
    p1 = np.arange(s1)[None, None, :]
    p2 = np.arange(s2)[:, None, None]
    ang = 2.0 * np.pi * ((k1 * (s2 * p1 + p2)) % s) / s
    g = np.concatenate([np.cos(ang), np.sin(ang)], axis=1)
    k2 = np.arange(s2)[:, None]
    q2 = np.arange(s2)[None, :]
    ang2 = 2.0 * np.pi * ((k2 * q2) % s2) / s2
    h = np.concatenate([np.cos(ang2), np.sin(ang2)], axis=1) / np.sqrt(s)
    return jnp.asarray(g, BF16), jnp.asarray(h, BF16)


def _seqdft_kernel(zc_ref, zs_ref, g_ref, h_ref, o_ref, scr_ref, *, s1, s2):
    ch = DFT_CH
    for p2 in range(s2):
        xc = zc_ref[0, pl.ds(p2, s1, stride=s2), :]
        xs = zs_ref[0, pl.ds(p2, s1, stride=s2), :]
        xx = jnp.concatenate([xc, xs], axis=-1).astype(BF16)
        r = jnp.dot(g_ref[p2], xx, preferred_element_type=F32)
        a_re = r[:s1, :ch] - r[s1:, ch:]
        a_im = -(r[:s1, ch:] + r[s1:, :ch])
        scr_ref[:, p2, :] = a_re
        scr_ref[:, s2 + p2, :] = a_im
    hmat = h_ref[...]
    for k1 in range(s1):
        y = jnp.dot(hmat, scr_ref[k1].astype(BF16), preferred_element_type=F32)
        o_ref[0, pl.ds(k1, s2, stride=s1), :] = y


def _seqdft(zcs, g_tab, h_tab):
    b, s, _ = zcs.shape
    s1 = DFT_S1
    s2 = s // s1
    nch = D_FNET // DFT_CH
    return pl.pallas_call(
        functools.partial(_seqdft_kernel, s1=s1, s2=s2),
        grid=(b, nch),
        in_specs=[pl.BlockSpec((1, s, DFT_CH), lambda i, c: (i, 0, c)),
                  pl.BlockSpec((1, s, DFT_CH), lambda i, c: (i, 0, nch + c)),
                  pl.BlockSpec(g_tab.shape, lambda i, c: (0, 0, 0)),
                  pl.BlockSpec(h_tab.shape, lambda i, c: (0, 0))],
        out_specs=pl.BlockSpec((1, s, DFT_CH), lambda i, c: (i, 0, c)),
        out_shape=jax.ShapeDtypeStruct((b, s, D_FNET), F32),
        scratch_shapes=[pltpu.VMEM((s1, 2 * s2, DFT_CH), F32)],
        compiler_params=_cparams(("parallel", "parallel")),
        name="seqdft",
    )(zcs, zcs, g_tab, h_tab)


def _mix_kernel(x_ref, cn_ref, fy_ref, mod_ref, gf_ref, wout_ref, gpost_ref, gpre_ref,
                wsg_ref, wsu_ref, wsd_ref, x1_ref, h2_ref, ysh_ref):
    fn = _rms(fy_ref[0]) * gf_ref[...]
    mixed = jnp.dot(cn_ref[0], wout_ref[:D_CONV, :], preferred_element_type=F32)
    mixed = mixed + jnp.dot(fn.astype(BF16), wout_ref[D_CONV:, :], preferred_element_type=F32)
    x1 = x_ref[0] + mod_ref[0, 2:3, :] * (_rms(mixed) * gpost_ref[...])
    x1_ref[0] = x1
    h2 = _rms(x1) * gpre_ref[...]
    h2 = h2 * (1.0 + mod_ref[0, 4:5, :]) + mod_ref[0, 3:4, :]
    h2_ref[0] = _pack_bf16_pairs(h2)
    hb = h2.astype(BF16)
    hid = _silu(jnp.dot(hb, wsg_ref[...], preferred_element_type=F32))
    hid = hid * jnp.dot(hb, wsu_ref[...], preferred_element_type=F32)
    ysh_ref[0] = _pack_bf16_pairs(jnp.dot(hid.astype(BF16), wsd_ref[...], preferred_element_type=F32))


def _mix(x, cn, fy, mod3, g_fnet, w_out_b, g_post, g_pre, wsg_b, wsu_b, wsd_b, ts):
    b, s, d = x.shape
    tok = lambda c: pl.BlockSpec((1, ts, c), lambda i, t: (i, t, 0))
    full = lambda a: pl.BlockSpec(a.shape, lambda i, t: (0,) * a.ndim)
    return pl.pallas_call(
        _mix_kernel,
        grid=(b, s // ts),
        in_specs=[tok(d), tok(D_CONV), tok(D_FNET),
                  pl.BlockSpec((1, 6, d), lambda i, t: (i, 0, 0)),
                  full(g_fnet), full(w_out_b), full(g_post), full(g_pre),
                  full(wsg_b), full(wsu_b), full(wsd_b)],
        out_specs=[tok(d), tok(d // 2), tok(d // 2)],
        out_shape=[jax.ShapeDtypeStruct((b, s, d), F32),
                   jax.ShapeDtypeStruct((b, s, d // 2), jnp.uint32),
                   jax.ShapeDtypeStruct((b, s, d // 2), jnp.uint32)],
        compiler_params=_cparams(("parallel", "parallel")),
        name="mix",
    )(x, cn, fy, mod3, g_fnet, w_out_b, g_post, g_pre, wsg_b, wsu_b, wsd_b)


def _router_kernel(h_ref, wr_ref, br_ref, u_ref, idx_ref, rank_ref, wgt_ref, cnt_ref, carry_ref, *, tr):
    e = N_EXPERTS

    @pl.when((pl.program_id(0) == 0) & (pl.program_id(1) == 0))
    def _():
        carry_ref[...] = jnp.zeros_like(carry_ref)

    logits = lax.dot_general(wr_ref[...], _unpack_bf16_pairs(h_ref[0]).astype(BF16), (((1,), (1,)), ((), ())),
                             preferred_element_type=F32)
    sc = jax.nn.sigmoid(logits)
    sb = sc + br_ref[...]
    ninf = jnp.float32(-jnp.inf)

    io_g = lax.broadcasted_iota(jnp.int32, (GROUP_SIZE, tr), 0).astype(F32)
    gs = []
    for g in range(N_EXPERT_GROUPS):
        blk = sb[g * GROUP_SIZE:(g + 1) * GROUP_SIZE]
        m1 = jnp.max(blk, axis=0, keepdims=True)
        i1 = jnp.min(jnp.where(blk == m1, io_g, float(GROUP_SIZE)), axis=0, keepdims=True)
        m2 = jnp.max(jnp.where(io_g == i1, ninf, blk), axis=0, keepdims=True)
        gs.append(m1 + m2)
    masked = []
    for g in range(N_EXPERT_GROUPS):
        beat = jnp.zeros((1, tr), F32)
        for o in range(N_EXPERT_GROUPS):
            if o == g:
                continue
            wins = (gs[o] > gs[g]) | ((gs[o] == gs[g]) & (o < g))
            beat = beat + wins.astype(F32)
        keep = beat < float(TOPK_GROUPS)
        masked.append(jnp.where(keep, sb[g * GROUP_SIZE:(g + 1) * GROUP_SIZE], ninf))
    v = jnp.concatenate(masked, axis=0)

    io_e = lax.broadcasted_iota(jnp.int32, (e, tr), 0).astype(F32)
    ids, ws = [], []
    sel = jnp.zeros((e, tr), F32)
    for _ in range(TOP_K):
        m = jnp.max(v, axis=0, keepdims=True)
        i = jnp.min(jnp.where(v == m, io_e, float(e)), axis=0, keepdims=True)
        oh = io_e == i
        ids.append(i)
        ws.append(jnp.sum(jnp.where(oh, sc, 0.0), axis=0, keepdims=True))
        v = jnp.where(oh, ninf, v)
        sel = sel + oh.astype(F32)

    wsum = ws[0]
    for k in range(1, TOP_K):
        wsum = wsum + ws[k]
    wgt_ref[...] = jnp.concatenate([w / wsum * ROUTED_SCALE for w in ws], axis=0)
    idx_ref[...] = jnp.concatenate(ids, axis=0).astype(jnp.int32)

    excl = jnp.dot(sel.astype(BF16), u_ref[...], preferred_element_type=F32)
    base = carry_ref[:, 0:1]
    rank_full = base + excl
    ranks = [jnp.sum(jnp.where(io_e == ids[k], rank_full, 0.0), axis=0, keepdims=True)
             for k in range(TOP_K)]
    rank_ref[...] = jnp.concatenate(ranks, axis=0).astype(jnp.int32)
    new = base + jnp.sum(sel, axis=1, keepdims=True)
    carry_ref[...] = jnp.broadcast_to(new, carry_ref.shape)
    cnt_ref[...] = jnp.broadcast_to(new, cnt_ref.shape).astype(jnp.int32)


def _router(h2p, wr_t_b, b_router_col, tr):
    b, s, d = h2p.shape
    t = b * s
    nt = s // tr
    u = jnp.asarray(np.triu(np.ones((tr, tr), np.float32), k=1), BF16)
    col = lambda i, j: (0, i * nt + j)
    return pl.pallas_call(
        functools.partial(_router_kernel, tr=tr),
        grid=(b, nt),
        in_specs=[pl.BlockSpec((1, tr, d), lambda i, j: (i, j, 0)),
                  pl.BlockSpec(wr_t_b.shape, lambda i, j: (0, 0)),
                  pl.BlockSpec((N_EXPERTS, 1), lambda i, j: (0, 0)),
                  pl.BlockSpec((tr, tr), lambda i, j: (0, 0))],
        out_specs=[pl.BlockSpec((TOP_K, tr), col), pl.BlockSpec((TOP_K, tr), col),
                   pl.BlockSpec((TOP_K, tr), col),
                   pl.BlockSpec((N_EXPERTS, 128), lambda i, j: (0, 0))],
        out_shape=[jax.ShapeDtypeStruct((TOP_K, t), jnp.int32),
                   jax.ShapeDtypeStruct((TOP_K, t), jnp.int32),
                   jax.ShapeDtypeStruct((TOP_K, t), F32),
                   jax.ShapeDtypeStruct((N_EXPERTS, 128), jnp.int32)],
        scratch_shapes=[pltpu.VMEM((N_EXPERTS, 128), F32)],
        compiler_params=_cparams(("arbitrary", "arbitrary")),
        name="router",
    )(h2p, wr_t_b, b_router_col, u)


def _dest_kernel(pstart_ref, idx_ref, rank_ref, dest_ref):
    idx = idx_ref[...]

    def body(g, acc):
        for j in range(8):
            e = g * 8 + j
            acc = jnp.where(idx == e, pstart_ref[e], acc)
        return acc

    dest_ref[...] = lax.fori_loop(0, N_EXPERTS // 8, body, jnp.zeros_like(idx)) + rank_ref[...]


def _dest(pstarts, idx, rank, tl):
    k, t = idx.shape
    grid_spec = pltpu.PrefetchScalarGridSpec(
        num_scalar_prefetch=1,
        grid=(t // tl,),
        in_specs=[pl.BlockSpec((k, tl), lambda i, ps: (0, i)), pl.BlockSpec((k, tl), lambda i, ps: (0, i))],
        out_specs=pl.BlockSpec((k, tl), lambda i, ps: (0, i)),
    )
    return pl.pallas_call(
        _dest_kernel,
        grid_spec=grid_spec,
        out_shape=jax.ShapeDtypeStruct((k, t), jnp.int32),
        compiler_params=_cparams(("parallel",)),
        name="dest",
    )(pstarts, idx, rank)


def _sc_mesh():
    return plsc.VectorSubcoreMesh(core_axis_name="c", subcore_axis_name="s",
                                  num_cores=SC_CORES, num_subcores=SC_SUBCORES)


def _sc_worker_base(per_worker):
    return (lax.axis_index("s") * SC_CORES + lax.axis_index("c")) * per_worker


def _dispatch(h2_flat, dest, p_rows):
    t, dh = h2_flat.shape
    r = SC_ROWS
    per_w = t // SC_WORKERS
    nchunk = per_w // r
    assert per_w % (2 * r) == 0

    @functools.partial(
        pl.kernel, mesh=_sc_mesh(),
        out_type=jax.ShapeDtypeStruct((p_rows, dh), h2_flat.dtype),
        scratch_types=[pltpu.VMEM((2, TOP_K, r), jnp.int32), pltpu.VMEM((2, r, dh), h2_flat.dtype),
                       pltpu.SemaphoreType.DMA((2,)), pltpu.SemaphoreType.DMA((2,))],
        name="sc_dispatch",
    )
    def k(rows_hbm, dest_hbm, out_hbm, idx_v, rows_v, lsem, ssem):
        base = _sc_worker_base(per_w)

        def load(ci, slot):
            t0 = base + ci * r
            for kk in range(TOP_K):
                pltpu.sync_copy(dest_hbm.at[kk, pl.ds(t0, r)], idx_v.at[slot, kk])
            pltpu.async_copy(rows_hbm.at[pl.ds(t0, r)], rows_v.at[slot], lsem.at[slot])

        def scatter(ci, slot):
            t0 = base + ci * r
            pltpu.make_async_copy(rows_hbm.at[pl.ds(t0, r)], rows_v.at[slot], lsem.at[slot]).wait()
            for kk in range(TOP_K):
                pltpu.async_copy(rows_v.at[slot], out_hbm.at[idx_v.at[slot, kk]], ssem.at[slot])

        def drain(slot):
            for kk in range(TOP_K):
                pltpu.make_async_copy(rows_v.at[slot], out_hbm.at[idx_v.at[slot, kk]], ssem.at[slot]).wait()

        load(0, 0)

        @pl.loop(0, nchunk, step=2)
        def _(c0):
            for s in range(2):
                ci = c0 + s

                @pl.when(ci + 1 < nchunk)
                def _():
                    @pl.when(ci >= 1)
                    def _():
                        drain(1 - s)
                    load(ci + 1, 1 - s)

                scatter(ci, s)

        drain(0)
        drain(1)

    return k(h2_flat, dest)


def _gather_sum(obuf, dest, wgt):
    _, dh = obuf.shape
    kk_n, t = dest.shape
    r, lanes = SC_SUM_TOKENS, SC_LANES
    assert r * lanes == 128
    per_w = t // SC_WORKERS
    nchunk = per_w // r
    idx_rows = per_w * kk_n // 128
    assert per_w % (2 * r) == 0 and idx_rows % 8 == 0
    nj = dh // lanes
    dest_c = dest.reshape(kk_n, t // r, r).transpose(1, 0, 2).reshape(t * kk_n // 128, 128)
    w_c = jnp.broadcast_to(wgt.reshape(kk_n, t // r, r).transpose(1, 0, 2)[..., None],
                           (t // r, kk_n, r, lanes)).reshape(t // r, kk_n, r * lanes)

    @functools.partial(
        pl.kernel, mesh=_sc_mesh(),
        out_type=jax.ShapeDtypeStruct((t, 2 * dh), F32),
        scratch_types=[pltpu.VMEM((idx_rows, 128), jnp.int32), pltpu.VMEM((2, kk_n, r, dh), obuf.dtype),
                       pltpu.VMEM((2, kk_n, r * lanes), F32), pltpu.VMEM((2, r, 2 * dh), F32),
                       pltpu.SemaphoreType.DMA((2,)), pltpu.SemaphoreType.DMA((2,))],
        compiler_params=pltpu.CompilerParams(needs_layout_passes=False),
        name="sc_gather_sum",
    )
    def k(table_hbm, idx_hbm, w_hbm, out_hbm, idx_v, rows_v, w_v, out_v, gsem, wsem):
        wid = lax.axis_index("s") * SC_CORES + lax.axis_index("c")
        base = wid * per_w
        cbase = wid * nchunk
        pltpu.sync_copy(idx_hbm.at[pl.ds(wid * idx_rows, idx_rows)], idx_v)

        def idx_list(ci, kk):
            off = (ci * kk_n + kk) * r
            return idx_v.at[off // 128, pl.ds(pl.multiple_of(off % 128, 8), r)]

        def copies(ci, slot):
            yield pltpu.make_async_copy(w_hbm.at[cbase + ci], w_v.at[slot], gsem.at[slot])
            for kk in range(kk_n):
                yield pltpu.make_async_copy(table_hbm.at[idx_list(ci, kk)], rows_v.at[slot, kk], gsem.at[slot])

        def out_copy(ci, slot):
            return pltpu.make_async_copy(out_v.at[slot], out_hbm.at[pl.ds(base + ci * r, r)], wsem.at[slot])

        def compute(slot):
            @pl.loop(0, r)
            def _(i):
                ws = [w_v[slot, kk, pl.ds(pl.multiple_of(i * lanes, lanes), lanes)] for kk in range(kk_n)]

                @plsc.parallel_loop(0, nj, unroll=4)
                def _(j):
                    col = pl.multiple_of(j * lanes, lanes)
                    lo = jnp.zeros((lanes,), F32)
                    hi = jnp.zeros((lanes,), F32)
                    for kk in range(kk_n):
                        v = rows_v[slot, kk, i, pl.ds(col, lanes)]
                        lo = lo + ws[kk] * plsc.bitcast(v << 16, F32)
                        hi = hi + ws[kk] * plsc.bitcast(v & jnp.uint32(0xFFFF0000), F32)
                    out_v[slot, i, pl.ds(col, lanes)] = lo
                    out_v[slot, i, pl.ds(dh + col, lanes)] = hi

        for c in copies(0, 0):
            c.start()

        @pl.loop(0, nchunk, step=2)
        def _(c0):
            for s in range(2):
                ci = c0 + s

                @pl.when(ci + 1 < nchunk)
                def _():
                    for c in copies(ci + 1, 1 - s):
                        c.start()

                for c in copies(ci, s):
                    c.wait()

                @pl.when(ci >= 2)
                def _():
                    out_copy(ci - 2, s).wait()

                compute(s)
                out_copy(ci, s).start()

        out_copy(nchunk - 2, 0).wait()
        out_copy(nchunk - 1, 1).wait()

    return k(obuf, dest_c, w_c)


def _experts_kernel(first_ref, nblk_ref, nvalid_ref, nused_ref, x_hbm, wg_ref, wu_ref, wd_ref, o_hbm,
                    xb, ob, wg_s, wu_s, wd_s, xsem, osem, *, bm):
    e = pl.program_id(0)
    nused = nused_ref[0]

    def x_copy(g, slot):
        return pltpu.make_async_copy(x_hbm.at[pl.ds(pl.multiple_of(g * bm, bm), bm)], xb.at[slot], xsem.at[slot])

    def o_copy(g, slot):
        return pltpu.make_async_copy(ob.at[slot], o_hbm.at[pl.ds(pl.multiple_of(g * bm, bm), bm)], osem.at[slot])

    @pl.when((e == 0) & (nused > 0))
    def _():
        x_copy(0, 0).start()

    wg_s[...] = wg_ref[0].astype(BF16)
    wu_s[...] = wu_ref[0].astype(BF16)
    wd_s[...] = wd_ref[0].astype(BF16)
    first = first_ref[e]

    def block(i, carry):
        g = first + i
        slot = g % 2
        x_copy(g, slot).wait()

        @pl.when(g + 1 < nused)
        def _():
            x_copy(g + 1, 1 - slot).start()

        rows = lax.broadcasted_iota(jnp.int32, (bm, xb.shape[2]), 0)
        xp = jnp.where(rows < nvalid_ref[g], xb[slot], jnp.uint32(0))
        x = _unpack_bf16_pairs(xp).astype(BF16)
        gate = jnp.dot(x, wg_s[...], preferred_element_type=F32)
        up = jnp.dot(x, wu_s[...], preferred_element_type=F32)
        hid = (_silu(gate) * up).astype(BF16)
        out = _pack_bf16_pairs(jnp.dot(hid, wd_s[...], preferred_element_type=F32))

        @pl.when(g >= 2)
        def _():
            o_copy(g - 2, slot).wait()

        ob[slot] = out
        o_copy(g, slot).start()

        @pl.when(g == nused - 1)
        def _():
            o_copy(g, slot).wait()

            @pl.when(g >= 1)
            def _():
                o_copy(g - 1, 1 - slot).wait()

        return carry

    lax.fori_loop(0, nblk_ref[e], block, 0)


def _experts(xbuf, first_blk, nblk, nvalid, nused, w_gate, w_up, w_down, bm):
    p, dh = xbuf.shape
    n_e, d, de = w_gate.shape
    hbm = pl.BlockSpec(memory_space=pl.ANY)
    wsel = lambda e, fb, nbk, nv, nu: (e, 0, 0)
    grid_spec = pltpu.PrefetchScalarGridSpec(
        num_scalar_prefetch=4,
        grid=(n_e,),
        in_specs=[hbm, pl.BlockSpec((1, d, de), wsel), pl.BlockSpec((1, d, de), wsel),
                  pl.BlockSpec((1, de, d), wsel)],
        out_specs=hbm,
        scratch_shapes=[pltpu.VMEM((2, bm, dh), jnp.uint32), pltpu.VMEM((2, bm, dh), jnp.uint32),
                        pltpu.VMEM((d, de), BF16), pltpu.VMEM((d, de), BF16), pltpu.VMEM((de, d), BF16),
                        pltpu.SemaphoreType.DMA((2,)), pltpu.SemaphoreType.DMA((2,))],
    )
    return pl.pallas_call(
        functools.partial(_experts_kernel, bm=bm),
        grid_spec=grid_spec,
        out_shape=jax.ShapeDtypeStruct((p, dh), jnp.uint32),
        compiler_params=_cparams(("arbitrary",)),
        name="experts",
    )(first_blk, nblk, nvalid, nused, xbuf, w_gate, w_up, w_down)


def _combine_kernel(y_ref, x1_ref, ysh_ref, mod_ref, g_ref, out_ref):
    y = y_ref[...] + _unpack_bf16_pairs(ysh_ref[...])
    out_ref[...] = x1_ref[...] + mod_ref[0, 5:6, :] * (_rms(y) * g_ref[...])


def _combine(y_routed, x1_flat, ysh_flat, mod3, g_post, s, tc):
    t, d = x1_flat.shape
    per_seq = s // tc
    tok = pl.BlockSpec((tc, d), lambda i: (i, 0))
    return pl.pallas_call(
        _combine_kernel,
        grid=(t // tc,),
        in_specs=[tok, tok, pl.BlockSpec((tc, d // 2), lambda i: (i, 0)),
                  pl.BlockSpec((1, 6, d), lambda i: (i // per_seq, 0, 0)),
                  pl.BlockSpec((1, d), lambda i: (0, 0))],
        out_specs=tok,
        out_shape=jax.ShapeDtypeStruct((t, d), F32),
        compiler_params=_cparams(("parallel",)),
        name="combine",
    )(y_routed, x1_flat, ysh_flat, mod3, g_post)


def _channel_dft_table():
    c = np.arange(GROUP_DIM)
    ang = 2.0 * np.pi * ((c[:, None] * c[None, :]) % GROUP_DIM) / GROUP_DIM
    eye = np.eye(D_FNET // GROUP_DIM)
    scale = 1.0 / np.sqrt(GROUP_DIM)
    cs = np.concatenate([np.kron(eye, np.cos(ang)), np.kron(eye, np.sin(ang))], axis=1) * scale
    return jnp.asarray(cs, BF16)


def _expert_block_rows(n_assign):
    return 512 if n_assign // N_EXPERTS >= 256 else 256


def _layer(x, mod, p):
    b, s, d = x.shape
    t = b * s
    mod3 = mod.reshape(b, 6, d)
    ts = min(512, s)
    g_tab, h_tab = _dft_tables(s)

    v, zcs = _inproj(x, mod3, p["g_mix_pre"], p["w_in_b"], _channel_dft_table(), ts)
    cn = _conv(v, p["conv_w"], p["conv_b"], p["conv_ln_g"], p["conv_ln_b"], p["g_conv_out"], min(256, s))
    fy = _seqdft(zcs, g_tab, h_tab)
    x1, h2, ysh = _mix(x, cn, fy, mod3, p["g_fnet_out"], p["w_out_b"], p["g_mix_post"], p["g_ffn_pre"],
                       p["wsg_b"], p["wsu_b"], p["wsd_b"], min(256, s))
    idx, rank, wgt, cnt = _router(h2, p["wr_t_b"], p["b_router_col"], min(256, s))

    n = t * TOP_K
    bm = _expert_block_rows(n)
    counts = cnt[:, 0]
    pcounts = (counts + bm - 1) // bm * bm
    pends = jnp.cumsum(pcounts)
    pstarts = pends - pcounts
    dest = _dest(pstarts.astype(jnp.int32), idx, rank, min(2048, t))
    nb = (n + N_EXPERTS * (bm - 1) + bm - 1) // bm
    nused = (pends[-1] // bm).astype(jnp.int32)
    blk = jnp.minimum(jnp.arange(nb, dtype=jnp.int32), nused - 1) * bm
    block_e = jnp.sum((pends[None, :] <= blk[:, None]).astype(jnp.int32), axis=1)
    block_e = jnp.minimum(block_e, N_EXPERTS - 1)
    nvalid = jnp.clip(pstarts[block_e] + counts[block_e] - blk, 0, bm).astype(jnp.int32)

    xbuf = _dispatch(h2.reshape(t, d // 2), dest, nb * bm)
    obuf = _experts(xbuf, (pstarts // bm).astype(jnp.int32), (pcounts // bm).astype(jnp.int32), nvalid,
                    nused.reshape(1), p["w_gate"], p["w_up"], p["w_down"], bm)
    y_routed = _gather_sum(obuf, dest, wgt)
    out = _combine(y_routed, x1.reshape(t, d), ysh.reshape(t, d // 2), mod3, p["g_ffn_post"], s, min(512, s))
    return out.reshape(b, s, d)


def kernel(x_prompt, x_sample, c_prompt, c_sample, w_ada, b_ada, g_mix_pre, w_in, conv_w, conv_b, conv_ln_g, conv_ln_b, g_conv_out, g_fnet_out, w_out, g_mix_post, g_ffn_pre, w_router, b_router, w_gate, w_up, w_down, ws_gate, ws_up, ws_down, g_ffn_post):
    assert w_ada.shape[0] == 1, "single-layer kernel"
    bp, bs = c_prompt.shape[0], c_sample.shape[0]
    rows = -(-(bp + bs) // 8) * 8
    c_all = jnp.zeros((rows, D_MODEL), F32).at[:bp].set(c_prompt).at[bp:bp + bs].set(c_sample)
    mod = _ada(c_all, w_ada[0], b_ada)
    p = {
        "g_mix_pre": g_mix_pre, "w_in_b": w_in[0].astype(BF16),
        "conv_w": conv_w[0], "conv_b": conv_b, "conv_ln_g": conv_ln_g, "conv_ln_b": conv_ln_b,
        "g_conv_out": g_conv_out, "g_fnet_out": g_fnet_out, "w_out_b": w_out[0].astype(BF16),
        "g_mix_post": g_mix_post, "g_ffn_pre": g_ffn_pre,
        "wr_t_b": w_router[0].T.astype(BF16), "b_router_col": b_router[0][:, None],
        "w_gate": w_gate[0], "w_up": w_up[0], "w_down": w_down[0],
        "wsg_b": ws_gate[0].astype(BF16), "wsu_b": ws_up[0].astype(BF16), "wsd_b": ws_down[0].astype(BF16),
        "g_ffn_post": g_ffn_post,
    }
    y_prompt = _layer(x_prompt, mod[:bp], p)
    y_sample = _layer(x_sample, mod[bp:bp + bs], p)
    return (y_prompt, y_sample)
```

```python
import functools

import numpy as np
import jax
import jax.numpy as jnp
from jax import lax
from jax.experimental import pallas as pl
from jax.experimental.pallas import tpu as pltpu
from jax.experimental.pallas import tpu_sc as plsc

F32 = jnp.float32
BF16 = jnp.bfloat16

D_MODEL = 1024
D_CONV = 512
D_FNET = 512
GROUP_DIM = 64
CONV_WIDTH = 31
N_EXPERTS = 256
TOP_K = 8
N_EXPERT_GROUPS = 8
GROUP_SIZE = N_EXPERTS // N_EXPERT_GROUPS
TOPK_GROUPS = 4
ROUTED_SCALE = 2.5
EPS = 1e-6

DFT_S1 = 128
DFT_CH = 128
HALO = 16
VMEM_LIMIT = 56 * 1024 * 1024
SC_CORES = 2
SC_SUBCORES = 16
SC_WORKERS = SC_CORES * SC_SUBCORES
SC_ROWS = 64
SC_LANES = 16
SC_SUM_TOKENS = 8


def _cparams(sem, vmem=None):
    return pltpu.CompilerParams(dimension_semantics=sem, vmem_limit_bytes=vmem or VMEM_LIMIT)


def _rms(x):
    return x * lax.rsqrt(jnp.mean(x * x, axis=-1, keepdims=True) + EPS)


def _silu(x):
    return x * jax.nn.sigmoid(x)


def _pack_bf16_pairs(x):
    c = x.shape[-1] // 2
    bits = lax.bitcast_convert_type(x.astype(BF16).astype(F32), jnp.uint32)
    return (bits[:, :c] >> 16) | bits[:, c:]


def _unpack_bf16_pairs(p):
    lo = lax.bitcast_convert_type(p << 16, F32)
    hi = lax.bitcast_convert_type(p & jnp.uint32(0xFFFF0000), F32)
    return jnp.concatenate([lo, hi], axis=-1)


def _ada_kernel(c_ref, w_ref, b_ref, o_ref):
    o_ref[...] = jnp.dot(_silu(c_ref[...]), w_ref[...], preferred_element_type=F32) + b_ref[...]


def _ada(c, w_ada, b_ada):
    bp, d = c.shape
    n = w_ada.shape[1]
    return pl.pallas_call(
        _ada_kernel,
        grid=(n // d,),
        in_specs=[pl.BlockSpec((bp, d), lambda j: (0, 0)),
                  pl.BlockSpec((d, d), lambda j: (0, j)),
                  pl.BlockSpec((1, d), lambda j: (0, j))],
        out_specs=pl.BlockSpec((bp, d), lambda j: (0, j)),
        out_shape=jax.ShapeDtypeStruct((bp, n), F32),
        compiler_params=_cparams(("parallel",)),
        name="ada",
    )(c, w_ada, b_ada)


def _inproj_kernel(x_ref, mod_ref, g_ref, win_ref, cs_ref, v_ref, z_ref):
    x = x_ref[0]
    h = _rms(x) * g_ref[...]
    h = h * (1.0 + mod_ref[0, 1:2, :]) + mod_ref[0, 0:1, :]
    u = jnp.dot(h.astype(BF16), win_ref[...], preferred_element_type=F32)
    a = u[:, :D_CONV]
    gt = u[:, D_CONV:2 * D_CONV]
    z = u[:, 2 * D_CONV:]
    v_ref[0] = a * jax.nn.sigmoid(gt)
    z_ref[0] = jnp.dot(z.astype(BF16), cs_ref[...], preferred_element_type=F32)


def _inproj(x, mod3, g_mix_pre, w_in_b, cs_b, ts):
    b, s, d = x.shape
    return pl.pallas_call(
        _inproj_kernel,
        grid=(b, s // ts),
        in_specs=[pl.BlockSpec((1, ts, d), lambda i, t: (i, t, 0)),
                  pl.BlockSpec((1, 6, d), lambda i, t: (i, 0, 0)),
                  pl.BlockSpec((1, d), lambda i, t: (0, 0)),
                  pl.BlockSpec(w_in_b.shape, lambda i, t: (0, 0)),
                  pl.BlockSpec(cs_b.shape, lambda i, t: (0, 0))],
        out_specs=[pl.BlockSpec((1, ts, D_CONV), lambda i, t: (i, t, 0)),
                   pl.BlockSpec((1, ts, 2 * D_FNET), lambda i, t: (i, t, 0))],
        out_shape=[jax.ShapeDtypeStruct((b, s, D_CONV), F32),
                   jax.ShapeDtypeStruct((b, s, 2 * D_FNET), F32)],
        compiler_params=_cparams(("parallel", "parallel")),
        name="inproj",
    )(x, mod3, g_mix_pre, w_in_b, cs_b)


def _conv_kernel(vp_ref, v_ref, vn_ref, w_ref, b_ref, lg_ref, lb_ref, go_ref, o_ref, pad_ref, sh_ref, *, ts, rc):
    t = pl.program_id(1)
    nt = pl.num_programs(1)
    pad_ref[0:HALO, :] = jnp.where(t > 0, vp_ref[0], 0.0)
    pad_ref[HALO:HALO + ts, :] = v_ref[0]
    pad_ref[HALO + ts:HALO + ts + HALO, :] = jnp.where(t < nt - 1, vn_ref[0], 0.0)
    span = ts + 2 * HALO - 8
    for m in range(8):
        sh_ref[m] = pad_ref[m:m + span, :]
    off = HALO - CONV_WIDTH // 2
    for c in range(ts // rc):
        r0 = c * rc
        acc = jnp.zeros((rc // 8, 8, D_CONV), F32)
        for j in range(CONV_WIDTH):
            m, q = (off + j) % 8, (off + j) // 8
            tap = sh_ref[m, r0 + 8 * q:r0 + 8 * q + rc, :].reshape(rc // 8, 8, D_CONV)
            acc = acc + tap * w_ref[j][None]
        acc = acc.reshape(rc, D_CONV) + b_ref[...]
        mu = jnp.mean(acc, axis=-1, keepdims=True)
        xc = acc - mu
        var = jnp.mean(xc * xc, axis=-1, keepdims=True)
        y = xc * lax.rsqrt(var + EPS) * lg_ref[...] + lb_ref[...]
        y = _silu(y)
        y = _rms(y) * go_ref[...]
        o_ref[0, r0:r0 + rc, :] = y.astype(o_ref.dtype)


def _conv(v, conv_w, conv_b, ln_g, ln_b, g_out, ts, rc=32):
    b, s, c = v.shape
    hb = ts // HALO
    nh = s // HALO
    vec = pl.BlockSpec((1, c), lambda i, t: (0, 0))
    return pl.pallas_call(
        functools.partial(_conv_kernel, ts=ts, rc=rc),
        grid=(b, s // ts),
        in_specs=[pl.BlockSpec((1, HALO, c), lambda i, t: (i, jnp.maximum(t * hb - 1, 0), 0)),
                  pl.BlockSpec((1, ts, c), lambda i, t: (i, t, 0)),
                  pl.BlockSpec((1, HALO, c), lambda i, t: (i, jnp.minimum((t + 1) * hb, nh - 1), 0)),
                  pl.BlockSpec((CONV_WIDTH, 8, c), lambda i, t: (0, 0, 0)),
                  vec, vec, vec, vec],
        out_specs=pl.BlockSpec((1, ts, c), lambda i, t: (i, t, 0)),
        out_shape=jax.ShapeDtypeStruct((b, s, c), BF16),
        scratch_shapes=[pltpu.VMEM((ts + 2 * HALO, c), F32), pltpu.VMEM((8, ts + 2 * HALO - 8, c), F32)],
        compiler_params=_cparams(("parallel", "parallel")),
        name="conv",
    )(v, v, v, jnp.broadcast_to(conv_w[:, None, :], (CONV_WIDTH, 8, c)), conv_b, ln_g, ln_b, g_out)


def _dft_tables(s):
    s1 = DFT_S1
    s2 = s // s1
    k1 = np.arange(s1)[None, :, None]
    p1 = np.arange(s1)[None, None, :]
    p2 = np.arange(s2)[:, None, None]
    ang = 2.0 * np.pi * ((k1 * (s2 * p1 + p2)) % s) / s
    g = np.concatenate([np.cos(ang), np.sin(ang)], axis=1)
    k2 = np.arange(s2)[:, None]
    q2 = np.arange(s2)[None, :]
    ang2 = 2.0 * np.pi * ((k2 * q2) % s2) / s2
    h = np.concatenate([np.cos(ang2), np.sin(ang2)], axis=1) / np.sqrt(s)
    return jnp.asarray(g, BF16), jnp.asarray(h, BF16)


def _seqdft_kernel(zc_ref, zs_ref, g_ref, h_ref, o_ref, scr_ref, *, s1, s2):
    ch = DFT_CH
    for p2 in range(s2):
        xc = zc_ref[0, pl.ds(p2, s1, stride=s2), :]
        xs = zs_ref[0, pl.ds(p2, s1, stride=s2), :]
        xx = jnp.concatenate([xc, xs], axis=-1).astype(BF16)
        r = jnp.dot(g_ref[p2], xx, preferred_element_type=F32)
        a_re = r[:s1, :ch] - r[s1:, ch:]
        a_im = -(r[:s1, ch:] + r[s1:, :ch])
        scr_ref[:, p2, :] = a_re
        scr_ref[:, s2 + p2, :] = a_im
    hmat = h_ref[...]
    for k1 in range(s1):
        y = jnp.dot(hmat, scr_ref[k1].astype(BF16), preferred_element_type=F32)
        o_ref[0, pl.ds(k1, s2, stride=s1), :] = y


def _seqdft(zcs, g_tab, h_tab):
    b, s, _ = zcs.shape
    s1 = DFT_S1
    s2 = s // s1
    nch = D_FNET // DFT_CH
    return pl.pallas_call(
        functools.partial(_seqdft_kernel, s1=s1, s2=s2),
        grid=(b, nch),
        in_specs=[pl.BlockSpec((1, s, DFT_CH), lambda i, c: (i, 0, c)),
                  pl.BlockSpec((1, s, DFT_CH), lambda i, c: (i, 0, nch + c)),
                  pl.BlockSpec(g_tab.shape, lambda i, c: (0, 0, 0)),
                  pl.BlockSpec(h_tab.shape, lambda i, c: (0, 0))],
        out_specs=pl.BlockSpec((1, s, DFT_CH), lambda i, c: (i, 0, c)),
        out_shape=jax.ShapeDtypeStruct((b, s, D_FNET), F32),
        scratch_shapes=[pltpu.VMEM((s1, 2 * s2, DFT_CH), F32)],
        compiler_params=_cparams(("parallel", "parallel")),
        name="seqdft",
    )(zcs, zcs, g_tab, h_tab)


def _mix_kernel(x_ref, cn_ref, fy_ref, mod_ref, gf_ref, wout_ref, gpost_ref, gpre_ref,
                wsg_ref, wsu_ref, wsd_ref, x1_ref, h2_ref, ysh_ref):
    fn = _rms(fy_ref[0]) * gf_ref[...]
    mixed = jnp.dot(cn_ref[0], wout_ref[:D_CONV, :], preferred_element_type=F32)
    mixed = mixed + jnp.dot(fn.astype(BF16), wout_ref[D_CONV:, :], preferred_element_type=F32)
    x1 = x_ref[0] + mod_ref[0, 2:3, :] * (_rms(mixed) * gpost_ref[...])
    x1_ref[0] = x1
    h2 = _rms(x1) * gpre_ref[...]
    h2 = h2 * (1.0 + mod_ref[0, 4:5, :]) + mod_ref[0, 3:4, :]
    h2_ref[0] = _pack_bf16_pairs(h2)
    hb = h2.astype(BF16)
    hid = _silu(jnp.dot(hb, wsg_ref[...], preferred_element_type=F32))
    hid = hid * jnp.dot(hb, wsu_ref[...], preferred_element_type=F32)
    ysh_ref[0] = _pack_bf16_pairs(jnp.dot(hid.astype(BF16), wsd_ref[...], preferred_element_type=F32))


def _mix(x, cn, fy, mod3, g_fnet, w_out_b, g_post, g_pre, wsg_b, wsu_b, wsd_b, ts):
    b, s, d = x.shape
    tok = lambda c: pl.BlockSpec((1, ts, c), lambda i, t: (i, t, 0))
    full = lambda a: pl.BlockSpec(a.shape, lambda i, t: (0,) * a.ndim)
    return pl.pallas_call(
        _mix_kernel,
        grid=(b, s // ts),
        in_specs=[tok(d), tok(D_CONV), tok(D_FNET),
                  pl.BlockSpec((1, 6, d), lambda i, t: (i, 0, 0)),
                  full(g_fnet), full(w_out_b), full(g_post), full(g_pre),
                  full(wsg_b), full(wsu_b), full(wsd_b)],
        out_specs=[tok(d), tok(d // 2), tok(d // 2)],
        out_shape=[jax.ShapeDtypeStruct((b, s, d), F32),
                   jax.ShapeDtypeStruct((b, s, d // 2), jnp.uint32),
                   jax.ShapeDtypeStruct((b, s, d // 2), jnp.uint32)],
        compiler_params=_cparams(("parallel", "parallel")),
        name="mix",
    )(x, cn, fy, mod3, g_fnet, w_out_b, g_post, g_pre, wsg_b, wsu_b, wsd_b)


def _router_kernel(h_ref, wr_ref, br_ref, u_ref, idx_ref, rank_ref, wgt_ref, cnt_ref, carry_ref, *, tr):
    e = N_EXPERTS

    @pl.when((pl.program_id(0) == 0) & (pl.program_id(1) == 0))
    def _():
        carry_ref[...] = jnp.zeros_like(carry_ref)

    logits = lax.dot_general(wr_ref[...], _unpack_bf16_pairs(h_ref[0]).astype(BF16), (((1,), (1,)), ((), ())),
                             preferred_element_type=F32)
    sc = jax.nn.sigmoid(logits)
    sb = sc + br_ref[...]
    ninf = jnp.float32(-jnp.inf)

    io_g = lax.broadcasted_iota(jnp.int32, (GROUP_SIZE, tr), 0).astype(F32)
    gs = []
    for g in range(N_EXPERT_GROUPS):
        blk = sb[g * GROUP_SIZE:(g + 1) * GROUP_SIZE]
        m1 = jnp.max(blk, axis=0, keepdims=True)
        i1 = jnp.min(jnp.where(blk == m1, io_g, float(GROUP_SIZE)), axis=0, keepdims=True)
        m2 = jnp.max(jnp.where(io_g == i1, ninf, blk), axis=0, keepdims=True)
        gs.append(m1 + m2)
    masked = []
    for g in range(N_EXPERT_GROUPS):
        beat = jnp.zeros((1, tr), F32)
        for o in range(N_EXPERT_GROUPS):
            if o == g:
                continue
            wins = (gs[o] > gs[g]) | ((gs[o] == gs[g]) & (o < g))
            beat = beat + wins.astype(F32)
        keep = beat < float(TOPK_GROUPS)
        masked.append(jnp.where(keep, sb[g * GROUP_SIZE:(g + 1) * GROUP_SIZE], ninf))
    v = jnp.concatenate(masked, axis=0)

    io_e = lax.broadcasted_iota(jnp.int32, (e, tr), 0).astype(F32)
    ids, ws = [], []
    sel = jnp.zeros((e, tr), F32)
    for _ in range(TOP_K):
        m = jnp.max(v, axis=0, keepdims=True)
        i = jnp.min(jnp.where(v == m, io_e, float(e)), axis=0, keepdims=True)
        oh = io_e == i
        ids.append(i)
        ws.append(jnp.sum(jnp.where(oh, sc, 0.0), axis=0, keepdims=True))
        v = jnp.where(oh, ninf, v)
        sel = sel + oh.astype(F32)

    wsum = ws[0]
    for k in range(1, TOP_K):
        wsum = wsum + ws[k]
    wgt_ref[...] = jnp.concatenate([w / wsum * ROUTED_SCALE for w in ws], axis=0)
    idx_ref[...] = jnp.concatenate(ids, axis=0).astype(jnp.int32)

    excl = jnp.dot(sel.astype(BF16), u_ref[...], preferred_element_type=F32)
    base = carry_ref[:, 0:1]
    rank_full = base + excl
    ranks = [jnp.sum(jnp.where(io_e == ids[k], rank_full, 0.0), axis=0, keepdims=True)
             for k in range(TOP_K)]
    rank_ref[...] = jnp.concatenate(ranks, axis=0).astype(jnp.int32)
    new = base + jnp.sum(sel, axis=1, keepdims=True)
    carry_ref[...] = jnp.broadcast_to(new, carry_ref.shape)
    cnt_ref[...] = jnp.broadcast_to(new, cnt_ref.shape).astype(jnp.int32)


def _router(h2p, wr_t_b, b_router_col, tr):
    b, s, d = h2p.shape
    t = b * s
    nt = s // tr
    u = jnp.asarray(np.triu(np.ones((tr, tr), np.float32), k=1), BF16)
    col = lambda i, j: (0, i * nt + j)
    return pl.pallas_call(
        functools.partial(_router_kernel, tr=tr),
        grid=(b, nt),
        in_specs=[pl.BlockSpec((1, tr, d), lambda i, j: (i, j, 0)),
                  pl.BlockSpec(wr_t_b.shape, lambda i, j: (0, 0)),
                  pl.BlockSpec((N_EXPERTS, 1), lambda i, j: (0, 0)),
                  pl.BlockSpec((tr, tr), lambda i, j: (0, 0))],
        out_specs=[pl.BlockSpec((TOP_K, tr), col), pl.BlockSpec((TOP_K, tr), col),
                   pl.BlockSpec((TOP_K, tr), col),
                   pl.BlockSpec((N_EXPERTS, 128), lambda i, j: (0, 0))],
        out_shape=[jax.ShapeDtypeStruct((TOP_K, t), jnp.int32),
                   jax.ShapeDtypeStruct((TOP_K, t), jnp.int32),
                   jax.ShapeDtypeStruct((TOP_K, t), F32),
                   jax.ShapeDtypeStruct((N_EXPERTS, 128), jnp.int32)],
        scratch_shapes=[pltpu.VMEM((N_EXPERTS, 128), F32)],
        compiler_params=_cparams(("arbitrary", "arbitrary")),
        name="router",
    )(h2p, wr_t_b, b_router_col, u)


def _dest_kernel(pstart_ref, idx_ref, rank_ref, dest_ref):
    idx = idx_ref[...]

    def body(g, acc):
        for j in range(8):
            e = g * 8 + j
            acc = jnp.where(idx == e, pstart_ref[e], acc)
        return acc

    dest_ref[...] = lax.fori_loop(0, N_EXPERTS // 8, body, jnp.zeros_like(idx)) + rank_ref[...]


def _dest(pstarts, idx, rank, tl):
    k, t = idx.shape
    grid_spec = pltpu.PrefetchScalarGridSpec(
        num_scalar_prefetch=1,
        grid=(t // tl,),
        in_specs=[pl.BlockSpec((k, tl), lambda i, ps: (0, i)), pl.BlockSpec((k, tl), lambda i, ps: (0, i))],
        out_specs=pl.BlockSpec((k, tl), lambda i, ps: (0, i)),
    )
    return pl.pallas_call(
        _dest_kernel,
        grid_spec=grid_spec,
        out_shape=jax.ShapeDtypeStruct((k, t), jnp.int32),
        compiler_params=_cparams(("parallel",)),
        name="dest",
    )(pstarts, idx, rank)


def _sc_mesh():
    return plsc.VectorSubcoreMesh(core_axis_name="c", subcore_axis_name="s",
                                  num_cores=SC_CORES, num_subcores=SC_SUBCORES)


def _sc_worker_base(per_worker):
    return (lax.axis_index("s") * SC_CORES + lax.axis_index("c")) * per_worker


def _dispatch(h2_flat, dest, p_rows):
    t, dh = h2_flat.shape
    r = SC_ROWS
    per_w = t // SC_WORKERS
    nchunk = per_w // r
    assert per_w % (2 * r) == 0

    @functools.partial(
        pl.kernel, mesh=_sc_mesh(),
        out_type=jax.ShapeDtypeStruct((p_rows, dh), h2_flat.dtype),
        scratch_types=[pltpu.VMEM((2, TOP_K, r), jnp.int32), pltpu.VMEM((2, r, dh), h2_flat.dtype),
                       pltpu.SemaphoreType.DMA((2,)), pltpu.SemaphoreType.DMA((2,))],
        name="sc_dispatch",
    )
    def k(rows_hbm, dest_hbm, out_hbm, idx_v, rows_v, lsem, ssem):
        base = _sc_worker_base(per_w)

        def load(ci, slot):
            t0 = base + ci * r
            for kk in range(TOP_K):
                pltpu.sync_copy(dest_hbm.at[kk, pl.ds(t0, r)], idx_v.at[slot, kk])
            pltpu.async_copy(rows_hbm.at[pl.ds(t0, r)], rows_v.at[slot], lsem.at[slot])

        def scatter(ci, slot):
            t0 = base + ci * r
            pltpu.make_async_copy(rows_hbm.at[pl.ds(t0, r)], rows_v.at[slot], lsem.at[slot]).wait()
            for kk in range(TOP_K):
                pltpu.async_copy(rows_v.at[slot], out_hbm.at[idx_v.at[slot, kk]], ssem.at[slot])

        def drain(slot):
            for kk in range(TOP_K):
                pltpu.make_async_copy(rows_v.at[slot], out_hbm.at[idx_v.at[slot, kk]], ssem.at[slot]).wait()

        load(0, 0)

        @pl.loop(0, nchunk, step=2)
        def _(c0):
            for s in range(2):
                ci = c0 + s

                @pl.when(ci + 1 < nchunk)
                def _():
                    @pl.when(ci >= 1)
                    def _():
                        drain(1 - s)
                    load(ci + 1, 1 - s)

                scatter(ci, s)

        drain(0)
        drain(1)

    return k(h2_flat, dest)


def _gather_sum(obuf, dest, wgt):
    _, dh = obuf.shape
    kk_n, t = dest.shape
    r, lanes = SC_SUM_TOKENS, SC_LANES
    assert r * lanes == 128
    per_w = t // SC_WORKERS
    nchunk = per_w // r
    idx_rows = per_w * kk_n // 128
    assert per_w % (2 * r) == 0 and idx_rows % 8 == 0
    nj = dh // lanes
    dest_c = dest.reshape(kk_n, t // r, r).transpose(1, 0, 2).reshape(t * kk_n // 128, 128)
    w_c = jnp.broadcast_to(wgt.reshape(kk_n, t // r, r).transpose(1, 0, 2)[..., None],
                           (t // r, kk_n, r, lanes)).reshape(t // r, kk_n, r * lanes)

    @functools.partial(
        pl.kernel, mesh=_sc_mesh(),
        out_type=jax.ShapeDtypeStruct((t, 2 * dh), F32),
        scratch_types=[pltpu.VMEM((idx_rows, 128), jnp.int32), pltpu.VMEM((2, kk_n, r, dh), obuf.dtype),
                       pltpu.VMEM((2, kk_n, r * lanes), F32), pltpu.VMEM((2, r, 2 * dh), F32),
                       pltpu.SemaphoreType.DMA((2,)), pltpu.SemaphoreType.DMA((2,))],
        compiler_params=pltpu.CompilerParams(needs_layout_passes=False),
        name="sc_gather_sum",
    )
    def k(table_hbm, idx_hbm, w_hbm, out_hbm, idx_v, rows_v, w_v, out_v, gsem, wsem):
        wid = lax.axis_index("s") * SC_CORES + lax.axis_index("c")
        base = wid * per_w
        cbase = wid * nchunk
        pltpu.sync_copy(idx_hbm.at[pl.ds(wid * idx_rows, idx_rows)], idx_v)

        def idx_list(ci, kk):
            off = (ci * kk_n + kk) * r
            return idx_v.at[off // 128, pl.ds(pl.multiple_of(off % 128, 8), r)]

        def copies(ci, slot):
            yield pltpu.make_async_copy(w_hbm.at[cbase + ci], w_v.at[slot], gsem.at[slot])
            for kk in range(kk_n):
                yield pltpu.make_async_copy(table_hbm.at[idx_list(ci, kk)], rows_v.at[slot, kk], gsem.at[slot])

        def out_copy(ci, slot):
            return pltpu.make_async_copy(out_v.at[slot], out_hbm.at[pl.ds(base + ci * r, r)], wsem.at[slot])

        def compute(slot):
            @pl.loop(0, r)
            def _(i):
                ws = [w_v[slot, kk, pl.ds(pl.multiple_of(i * lanes, lanes), lanes)] for kk in range(kk_n)]

                @plsc.parallel_loop(0, nj, unroll=4)
                def _(j):
                    col = pl.multiple_of(j * lanes, lanes)
                    lo = jnp.zeros((lanes,), F32)
                    hi = jnp.zeros((lanes,), F32)
                    for kk in range(kk_n):
                        v = rows_v[slot, kk, i, pl.ds(col, lanes)]
                        lo = lo + ws[kk] * plsc.bitcast(v << 16, F32)
                        hi = hi + ws[kk] * plsc.bitcast(v & jnp.uint32(0xFFFF0000), F32)
                    out_v[slot, i, pl.ds(col, lanes)] = lo
                    out_v[slot, i, pl.ds(dh + col, lanes)] = hi

        for c in copies(0, 0):
            c.start()

        @pl.loop(0, nchunk, step=2)
        def _(c0):
            for s in range(2):
                ci = c0 + s

                @pl.when(ci + 1 < nchunk)
                def _():
                    for c in copies(ci + 1, 1 - s):
                        c.start()

                for c in copies(ci, s):
                    c.wait()

                @pl.when(ci >= 2)
                def _():
                    out_copy(ci - 2, s).wait()

                compute(s)
                out_copy(ci, s).start()

        out_copy(nchunk - 2, 0).wait()
        out_copy(nchunk - 1, 1).wait()

    return k(obuf, dest_c, w_c)


def _experts_kernel(be_ref, nvalid_ref, run_ref, nxt_ref, nused_ref, x_ref, wg_hbm, wu_hbm, wd_hbm, o_ref,
                    wg_f, wu_f, wd_f, wg_s, wu_s, wd_s, sem):
    b = pl.program_id(0)

    def weight_copies(e, slot):
        return (pltpu.make_async_copy(wg_hbm.at[e], wg_f.at[slot], sem.at[slot]),
                pltpu.make_async_copy(wu_hbm.at[e], wu_f.at[slot], sem.at[slot]),
                pltpu.make_async_copy(wd_hbm.at[e], wd_f.at[slot], sem.at[slot]))

    @pl.when(b < nused_ref[0])
    def _():
        e = be_ref[b]
        slot = run_ref[b] % 2

        @pl.when(b == 0)
        def _():
            for c in weight_copies(e, slot):
                c.start()

        @pl.when((b == 0) | (e != be_ref[jnp.maximum(b - 1, 0)]))
        def _():
            for c in weight_copies(e, slot):
                c.wait()

            @pl.when(nxt_ref[b] >= 0)
            def _():
                for c in weight_copies(nxt_ref[b], 1 - slot):
                    c.start()

            wg_s[...] = wg_f[slot].astype(BF16)
            wu_s[...] = wu_f[slot].astype(BF16)
            wd_s[...] = wd_f[slot].astype(BF16)

        rows = lax.broadcasted_iota(jnp.int32, x_ref.shape, 0)
        xp = jnp.where(rows < nvalid_ref[b], x_ref[...], jnp.uint32(0))
        x = _unpack_bf16_pairs(xp).astype(BF16)
        g = jnp.dot(x, wg_s[...], preferred_element_type=F32)
        u = jnp.dot(x, wu_s[...], preferred_element_type=F32)
        hid = (_silu(g) * u).astype(BF16)
        o_ref[...] = _pack_bf16_pairs(jnp.dot(hid, wd_s[...], preferred_element_type=F32))

    @pl.when(b >= nused_ref[0])
    def _():
        o_ref[...] = jnp.zeros_like(o_ref)


def _experts(xbuf, block_e, nvalid, run, nxt, nused, w_gate, w_up, w_down, bm):
    p, dh = xbuf.shape
    nb = p // bm
    d, de = w_gate.shape[1:]
    hbm = pl.BlockSpec(memory_space=pl.ANY)
    grid_spec = pltpu.PrefetchScalarGridSpec(
        num_scalar_prefetch=5,
        grid=(nb,),
        in_specs=[pl.BlockSpec((bm, dh), lambda b, be, nv, rn, nx, nu: (jnp.minimum(b, nu[0] - 1), 0)),
                  hbm, hbm, hbm],
        out_specs=pl.BlockSpec((bm, dh), lambda b, be, nv, rn, nx, nu: (b, 0)),
        scratch_shapes=[pltpu.VMEM((2, d, de), F32), pltpu.VMEM((2, d, de), F32), pltpu.VMEM((2, de, d), F32),
                        pltpu.VMEM((d, de), BF16), pltpu.VMEM((d, de), BF16), pltpu.VMEM((de, d), BF16),
                        pltpu.SemaphoreType.DMA((2,))],
    )
    return pl.pallas_call(
        _experts_kernel,
        grid_spec=grid_spec,
        out_shape=jax.ShapeDtypeStruct((p, dh), jnp.uint32),
        compiler_params=_cparams(("arbitrary",)),
        name="experts",
    )(block_e, nvalid, run, nxt, nused, xbuf, w_gate, w_up, w_down)


def _combine_kernel(y_ref, x1_ref, ysh_ref, mod_ref, g_ref, out_ref):
    y = y_ref[...] + _unpack_bf16_pairs(ysh_ref[...])
    out_ref[...] = x1_ref[...] + mod_ref[0, 5:6, :] * (_rms(y) * g_ref[...])


def _combine(y_routed, x1_flat, ysh_flat, mod3, g_post, s, tc):
    t, d = x1_flat.shape
    per_seq = s // tc
    tok = pl.BlockSpec((tc, d), lambda i: (i, 0))
    return pl.pallas_call(
        _combine_kernel,
        grid=(t // tc,),
        in_specs=[tok, tok, pl.BlockSpec((tc, d // 2), lambda i: (i, 0)),
                  pl.BlockSpec((1, 6, d), lambda i: (i // per_seq, 0, 0)),
                  pl.BlockSpec((1, d), lambda i: (0, 0))],
        out_specs=tok,
        out_shape=jax.ShapeDtypeStruct((t, d), F32),
        compiler_params=_cparams(("parallel",)),
        name="combine",
    )(y_routed, x1_flat, ysh_flat, mod3, g_post)


def _channel_dft_table():
    c = np.arange(GROUP_DIM)
    ang = 2.0 * np.pi * ((c[:, None] * c[None, :]) % GROUP_DIM) / GROUP_DIM
    eye = np.eye(D_FNET // GROUP_DIM)
    scale = 1.0 / np.sqrt(GROUP_DIM)
    cs = np.concatenate([np.kron(eye, np.cos(ang)), np.kron(eye, np.sin(ang))], axis=1) * scale
    return jnp.asarray(cs, BF16)


EXPERT_BLOCK_MAX = 1280
EXPERT_BLOCK_ALIGN = 64


def _expert_block_rows(n_assign):
    target = max(n_assign // N_EXPERTS * 9 // 8, 2 * EXPERT_BLOCK_ALIGN)
    k = -(-target // EXPERT_BLOCK_MAX)
    return -(-target // (k * EXPERT_BLOCK_ALIGN)) * EXPERT_BLOCK_ALIGN


def _layer(x, mod, p):
    b, s, d = x.shape
    t = b * s
    mod3 = mod.reshape(b, 6, d)
    ts = min(512, s)
    g_tab, h_tab = _dft_tables(s)

    v, zcs = _inproj(x, mod3, p["g_mix_pre"], p["w_in_b"], _channel_dft_table(), ts)
    cn = _conv(v, p["conv_w"], p["conv_b"], p["conv_ln_g"], p["conv_ln_b"], p["g_conv_out"], min(256, s))
    fy = _seqdft(zcs, g_tab, h_tab)
    x1, h2, ysh = _mix(x, cn, fy, mod3, p["g_fnet_out"], p["w_out_b"], p["g_mix_post"], p["g_ffn_pre"],
                       p["wsg_b"], p["wsu_b"], p["wsd_b"], min(256, s))
    idx, rank, wgt, cnt = _router(h2, p["wr_t_b"], p["b_router_col"], min(256, s))

    n = t * TOP_K
    bm = _expert_block_rows(n)
    counts = cnt[:, 0]
    pcounts = (counts + bm - 1) // bm * bm
    pends = jnp.cumsum(pcounts)
    pstarts = pends - pcounts
    dest = _dest(pstarts.astype(jnp.int32), idx, rank, min(2048, t))
    nb = (n + N_EXPERTS * (bm - 1) + bm - 1) // bm
    nused = (pends[-1] // bm).astype(jnp.int32)
    blk = jnp.minimum(jnp.arange(nb, dtype=jnp.int32), nused - 1) * bm
    block_e = jnp.sum((pends[None, :] <= blk[:, None]).astype(jnp.int32), axis=1)
    block_e = jnp.minimum(block_e, N_EXPERTS - 1)
    nvalid = jnp.clip(pstarts[block_e] + counts[block_e] - blk, 0, bm).astype(jnp.int32)
    first = jnp.concatenate([jnp.ones((1,), jnp.int32), (block_e[1:] != block_e[:-1]).astype(jnp.int32)])
    run = jnp.cumsum(first) - 1
    eid = jnp.arange(N_EXPERTS, dtype=jnp.int32)
    later = lax.cummin(jnp.where(pcounts > 0, eid, N_EXPERTS)[::-1])[::-1]
    nxt_e = jnp.concatenate([later[1:], jnp.full((1,), N_EXPERTS, jnp.int32)])
    nxt = jnp.where(nxt_e < N_EXPERTS, nxt_e, -1)[block_e].astype(jnp.int32)

    xbuf = _dispatch(h2.reshape(t, d // 2), dest, nb * bm)
    obuf = _experts(xbuf, block_e, nvalid, run.astype(jnp.int32), nxt, nused.reshape(1),
                    p["w_gate"], p["w_up"], p["w_down"], bm)
    y_routed = _gather_sum(obuf, dest, wgt)
    out = _combine(y_routed, x1.reshape(t, d), ysh.reshape(t, d // 2), mod3, p["g_ffn_post"], s, min(512, s))
    return out.reshape(b, s, d)


def kernel(x_prompt, x_sample, c_prompt, c_sample, w_ada, b_ada, g_mix_pre, w_in, conv_w, conv_b, conv_ln_g, conv_ln_b, g_conv_out, g_fnet_out, w_out, g_mix_post, g_ffn_pre, w_router, b_router, w_gate, w_up, w_down, ws_gate, ws_up, ws_down, g_ffn_post):
    assert w_ada.shape[0] == 1, "single-layer kernel"
    bp, bs = c_prompt.shape[0], c_sample.shape[0]
    rows = -(-(bp + bs) // 8) * 8
    c_all = jnp.zeros((rows, D_MODEL), F32).at[:bp].set(c_prompt).at[bp:bp + bs].set(c_sample)
    mod = _ada(c_all, w_ada[0], b_ada)
    p = {
        "g_mix_pre": g_mix_pre, "w_in_b": w_in[0].astype(BF16),
        "conv_w": conv_w[0], "conv_b": conv_b, "conv_ln_g": conv_ln_g, "conv_ln_b": conv_ln_b,
        "g_conv_out": g_conv_out, "g_fnet_out": g_fnet_out, "w_out_b": w_out[0].astype(BF16),
        "g_mix_post": g_mix_post, "g_ffn_pre": g_ffn_pre,
        "wr_t_b": w_router[0].T.astype(BF16), "b_router_col": b_router[0][:, None],
        "w_gate": w_gate[0], "w_up": w_up[0], "w_down": w_down[0],
        "wsg_b": ws_gate[0].astype(BF16), "wsu_b": ws_up[0].astype(BF16), "wsd_b": ws_down[0].astype(BF16),
        "g_ffn_post": g_ffn_post,
    }
    y_prompt = _layer(x_prompt, mod[:bp], p)
    y_sample = _layer(x_sample, mod[bp:bp + bs], p)
    return (y_prompt, y_sample)
```

```python
import functools

import numpy as np
import jax
import jax.numpy as jnp
from jax import lax
from jax.experimental import pallas as pl
from jax.experimental.pallas import tpu as pltpu
from jax.experimental.pallas import tpu_sc as plsc

F32 = jnp.float32
BF16 = jnp.bfloat16

D_MODEL = 1024
D_CONV = 512
D_FNET = 512
GROUP_DIM = 64
CONV_WIDTH = 31
N_EXPERTS = 256
TOP_K = 8
N_EXPERT_GROUPS = 8
GROUP_SIZE = N_EXPERTS // N_EXPERT_GROUPS
TOPK_GROUPS = 4
ROUTED_SCALE = 2.5
EPS = 1e-6

DFT_S1 = 128
DFT_CH = 128
HALO = 16
VMEM_LIMIT = 56 * 1024 * 1024
SC_CORES = 2
SC_SUBCORES = 16
SC_WORKERS = SC_CORES * SC_SUBCORES
SC_ROWS = 64
SC_LANES = 16
SC_SUM_TOKENS = 8


def _cparams(sem, vmem=None):
    return pltpu.CompilerParams(dimension_semantics=sem, vmem_limit_bytes=vmem or VMEM_LIMIT)


def _rms(x):
    return x * lax.rsqrt(jnp.mean(x * x, axis=-1, keepdims=True) + EPS)


def _silu(x):
    return x * jax.nn.sigmoid(x)


def _pack_bf16_pairs(x):
    c = x.shape[-1] // 2
    bits = lax.bitcast_convert_type(x.astype(BF16).astype(F32), jnp.uint32)
    return (bits[:, :c] >> 16) | bits[:, c:]


def _unpack_bf16_pairs(p):
    lo = lax.bitcast_convert_type(p << 16, F32)
    hi = lax.bitcast_convert_type(p & jnp.uint32(0xFFFF0000), F32)
    return jnp.concatenate([lo, hi], axis=-1)


def _ada_kernel(c_ref, w_ref, b_ref, o_ref):
    o_ref[...] = jnp.dot(_silu(c_ref[...]), w_ref[...], preferred_element_type=F32) + b_ref[...]


def _ada(c, w_ada, b_ada):
    bp, d = c.shape
    n = w_ada.shape[1]
    return pl.pallas_call(
        _ada_kernel,
        grid=(n // d,),
        in_specs=[pl.BlockSpec((bp, d), lambda j: (0, 0)),
                  pl.BlockSpec((d, d), lambda j: (0, j)),
                  pl.BlockSpec((1, d), lambda j: (0, j))],
        out_specs=pl.BlockSpec((bp, d), lambda j: (0, j)),
        out_shape=jax.ShapeDtypeStruct((bp, n), F32),
        compiler_params=_cparams(("parallel",)),
        name="ada",
    )(c, w_ada, b_ada)


def _inproj_kernel(x_ref, mod_ref, g_ref, win_ref, cs_ref, perm_ref, v_ref, z_ref, *, s2, p1l):
    x = x_ref[0]
    h = _rms(x) * g_ref[...]
    h = h * (1.0 + mod_ref[0, 1:2, :]) + mod_ref[0, 0:1, :]
    u = jnp.dot(h.astype(BF16), win_ref[...], preferred_element_type=F32)
    a = u[:, :D_CONV]
    gt = u[:, D_CONV:2 * D_CONV]
    z = u[:, 2 * D_CONV:]
    v_ref[0] = a * jax.nn.sigmoid(gt)
    zcs = jnp.dot(z.astype(BF16), cs_ref[...], preferred_element_type=F32).astype(BF16)
    zp = jnp.dot(perm_ref[...], zcs, preferred_element_type=F32).astype(BF16)
    z_ref[0] = zp.reshape(s2, p1l, zp.shape[-1])


def _inproj(x, mod3, g_mix_pre, w_in_b, cs_b):
    b, s, d = x.shape
    s2 = s // DFT_S1
    p1l = 16
    ts = p1l * s2
    r = np.arange(ts)
    perm = np.zeros((ts, ts), np.float32)
    perm[(r % s2) * p1l + r // s2, r] = 1.0
    perm = jnp.asarray(perm, BF16)
    return pl.pallas_call(
        functools.partial(_inproj_kernel, s2=s2, p1l=p1l),
        grid=(b, s // ts),
        in_specs=[pl.BlockSpec((1, ts, d), lambda i, t: (i, t, 0)),
                  pl.BlockSpec((1, 6, d), lambda i, t: (i, 0, 0)),
                  pl.BlockSpec((1, d), lambda i, t: (0, 0)),
                  pl.BlockSpec(w_in_b.shape, lambda i, t: (0, 0)),
                  pl.BlockSpec(cs_b.shape, lambda i, t: (0, 0)),
                  pl.BlockSpec((ts, ts), lambda i, t: (0, 0))],
        out_specs=[pl.BlockSpec((1, ts, D_CONV), lambda i, t: (i, t, 0)),
                   pl.BlockSpec((1, s2, p1l, 2 * D_FNET), lambda i, t: (i, 0, t, 0))],
        out_shape=[jax.ShapeDtypeStruct((b, s, D_CONV), F32),
                   jax.ShapeDtypeStruct((b, s2, DFT_S1, 2 * D_FNET), BF16)],
        compiler_params=_cparams(("parallel", "parallel")),
        name="inproj",
    )(x, mod3, g_mix_pre, w_in_b, cs_b, perm)


def _conv_kernel(vp_ref, v_ref, vn_ref, w_ref, b_ref, lg_ref, lb_ref, go_ref, o_ref, pad_ref, sh_ref, *, ts, rc):
    t = pl.program_id(1)
    nt = pl.num_programs(1)
    pad_ref[0:HALO, :] = jnp.where(t > 0, vp_ref[0], 0.0)
    pad_ref[HALO:HALO + ts, :] = v_ref[0]
    pad_ref[HALO + ts:HALO + ts + HALO, :] = jnp.where(t < nt - 1, vn_ref[0], 0.0)
    span = ts + 2 * HALO - 8
    for m in range(8):
        sh_ref[m] = pad_ref[m:m + span, :]
    off = HALO - CONV_WIDTH // 2
    for c in range(ts // rc):
        r0 = c * rc
        acc = jnp.zeros((rc // 8, 8, D_CONV), F32)
        for j in range(CONV_WIDTH):
            m, q = (off + j) % 8, (off + j) // 8
            tap = sh_ref[m, r0 + 8 * q:r0 + 8 * q + rc, :].reshape(rc // 8, 8, D_CONV)
            acc = acc + tap * w_ref[j][None]
        acc = acc.reshape(rc, D_CONV) + b_ref[...]
        mu = jnp.mean(acc, axis=-1, keepdims=True)
        xc = acc - mu
        var = jnp.mean(xc * xc, axis=-1, keepdims=True)
        y = xc * lax.rsqrt(var + EPS) * lg_ref[...] + lb_ref[...]
        y = _silu(y)
        y = _rms(y) * go_ref[...]
        o_ref[0, r0:r0 + rc, :] = y.astype(o_ref.dtype)


def _conv(v, conv_w, conv_b, ln_g, ln_b, g_out, ts, rc=32):
    b, s, c = v.shape
    hb = ts // HALO
    nh = s // HALO
    vec = pl.BlockSpec((1, c), lambda i, t: (0, 0))
    return pl.pallas_call(
        functools.partial(_conv_kernel, ts=ts, rc=rc),
        grid=(b, s // ts),
        in_specs=[pl.BlockSpec((1, HALO, c), lambda i, t: (i, jnp.maximum(t * hb - 1, 0), 0)),
                  pl.BlockSpec((1, ts, c), lambda i, t: (i, t, 0)),
                  pl.BlockSpec((1, HALO, c), lambda i, t: (i, jnp.minimum((t + 1) * hb, nh - 1), 0)),
                  pl.BlockSpec((CONV_WIDTH, 8, c), lambda i, t: (0, 0, 0)),
                  vec, vec, vec, vec],
        out_specs=pl.BlockSpec((1, ts, c), lambda i, t: (i, t, 0)),
        out_shape=jax.ShapeDtypeStruct((b, s, c), BF16),
        scratch_shapes=[pltpu.VMEM((ts + 2 * HALO, c), F32), pltpu.VMEM((8, ts + 2 * HALO - 8, c), F32)],
        compiler_params=_cparams(("parallel", "parallel")),
        name="conv",
    )(v, v, v, jnp.broadcast_to(conv_w[:, None, :], (CONV_WIDTH, 8, c)), conv_b, ln_g, ln_b, g_out)


def _dft_tables(s):
    s1 = DFT_S1
    s2 = s // s1
    k1 = np.arange(s1)[None, :, None]
    p1 = np.arange(s1)[None, None, :]
    p2 = np.arange(s2)[:, None, None]
    ang = 2.0 * np.pi * ((k1 * (s2 * p1 + p2)) % s) / s
    g = np.concatenate([np.cos(ang), np.sin(ang)], axis=1)
    k2 = np.arange(s2)[:, None]
    q2 = np.arange(s2)[None, :]
    ang2 = 2.0 * np.pi * ((k2 * q2) % s2) / s2
    h = np.concatenate([np.cos(ang2), np.sin(ang2)], axis=1) / np.sqrt(s)
    return jnp.asarray(g, BF16), jnp.asarray(h, BF16)


def _seqdft_kernel(z_ref, g_ref, h_ref, o_ref, scr_ref, *, s1, s2):
    ch = DFT_CH
    for p2 in range(s2):
        r = jnp.dot(g_ref[p2], z_ref[0, p2], preferred_element_type=F32)
        a_re = r[:s1, :ch] - r[s1:, ch:]
        a_im = -(r[:s1, ch:] + r[s1:, :ch])
        scr_ref[:, p2, :] = a_re
        scr_ref[:, s2 + p2, :] = a_im
    hmat = h_ref[...]
    for k1 in range(s1):
        y = jnp.dot(hmat, scr_ref[k1].astype(BF16), preferred_element_type=F32)
        o_ref[0, pl.ds(k1, s2, stride=s1), :] = y


def _seqdft(zp, g_tab, h_tab):
    b, s2, s1, _ = zp.shape
    s = s1 * s2
    nch = D_FNET // DFT_CH
    return pl.pallas_call(
        functools.partial(_seqdft_kernel, s1=s1, s2=s2),
        grid=(b, nch),
        in_specs=[pl.BlockSpec((1, s2, s1, 2 * DFT_CH), lambda i, c: (i, 0, 0, c)),
                  pl.BlockSpec(g_tab.shape, lambda i, c: (0, 0, 0)),
                  pl.BlockSpec(h_tab.shape, lambda i, c: (0, 0))],
        out_specs=pl.BlockSpec((1, s, DFT_CH), lambda i, c: (i, 0, c)),
        out_shape=jax.ShapeDtypeStruct((b, s, D_FNET), F32),
        scratch_shapes=[pltpu.VMEM((s1, 2 * s2, DFT_CH), F32)],
        compiler_params=_cparams(("parallel", "parallel")),
        name="seqdft",
    )(zp, g_tab, h_tab)


def _mix_kernel(x_ref, cn_ref, fy_ref, mod_ref, gf_ref, wout_ref, gpost_ref, gpre_ref,
                wsg_ref, wsu_ref, wsd_ref, x1_ref, h2_ref, ysh_ref):
    fn = _rms(fy_ref[0]) * gf_ref[...]
    mixed = jnp.dot(cn_ref[0], wout_ref[:D_CONV, :], preferred_element_type=F32)
    mixed = mixed + jnp.dot(fn.astype(BF16), wout_ref[D_CONV:, :], preferred_element_type=F32)
    x1 = x_ref[0] + mod_ref[0, 2:3, :] * (_rms(mixed) * gpost_ref[...])
    x1_ref[0] = x1
    h2 = _rms(x1) * gpre_ref[...]
    h2 = h2 * (1.0 + mod_ref[0, 4:5, :]) + mod_ref[0, 3:4, :]
    h2_ref[0] = _pack_bf16_pairs(h2)
    hb = h2.astype(BF16)
    hid = _silu(jnp.dot(hb, wsg_ref[...], preferred_element_type=F32))
    hid = hid * jnp.dot(hb, wsu_ref[...], preferred_element_type=F32)
    ysh_ref[0] = _pack_bf16_pairs(jnp.dot(hid.astype(BF16), wsd_ref[...], preferred_element_type=F32))


def _mix(x, cn, fy, mod3, g_fnet, w_out_b, g_post, g_pre, wsg_b, wsu_b, wsd_b, ts):
    b, s, d = x.shape
    tok = lambda c: pl.BlockSpec((1, ts, c), lambda i, t: (i, t, 0))
    full = lambda a: pl.BlockSpec(a.shape, lambda i, t: (0,) * a.ndim)
    return pl.pallas_call(
        _mix_kernel,
        grid=(b, s // ts),
        in_specs=[tok(d), tok(D_CONV), tok(D_FNET),
                  pl.BlockSpec((1, 6, d), lambda i, t: (i, 0, 0)),
                  full(g_fnet), full(w_out_b), full(g_post), full(g_pre),
                  full(wsg_b), full(wsu_b), full(wsd_b)],
        out_specs=[tok(d), tok(d // 2), tok(d // 2)],
        out_shape=[jax.ShapeDtypeStruct((b, s, d), F32),
                   jax.ShapeDtypeStruct((b, s, d // 2), jnp.uint32),
                   jax.ShapeDtypeStruct((b, s, d // 2), jnp.uint32)],
        compiler_params=_cparams(("parallel", "parallel")),
        name="mix",
    )(x, cn, fy, mod3, g_fnet, w_out_b, g_post, g_pre, wsg_b, wsu_b, wsd_b)


def _router_kernel(h_ref, wr_ref, br_ref, u_ref, idx_ref, rank_ref, wgt_ref, cnt_ref, carry_ref, *, tr):
    e = N_EXPERTS

    @pl.when((pl.program_id(0) == 0) & (pl.program_id(1) == 0))
    def _():
        carry_ref[...] = jnp.zeros_like(carry_ref)

    logits = lax.dot_general(wr_ref[...], _unpack_bf16_pairs(h_ref[0]).astype(BF16), (((1,), (1,)), ((), ())),
                             preferred_element_type=F32)
    sc = jax.nn.sigmoid(logits)
    sb = sc + br_ref[...]
    ninf = jnp.float32(-jnp.inf)

    io_g = lax.broadcasted_iota(jnp.int32, (GROUP_SIZE, tr), 0).astype(F32)
    gs = []
    for g in range(N_EXPERT_GROUPS):
        blk = sb[g * GROUP_SIZE:(g + 1) * GROUP_SIZE]
        m1 = jnp.max(blk, axis=0, keepdims=True)
        i1 = jnp.min(jnp.where(blk == m1, io_g, float(GROUP_SIZE)), axis=0, keepdims=True)
        m2 = jnp.max(jnp.where(io_g == i1, ninf, blk), axis=0, keepdims=True)
        gs.append(m1 + m2)
    masked = []
    for g in range(N_EXPERT_GROUPS):
        beat = jnp.zeros((1, tr), F32)
        for o in range(N_EXPERT_GROUPS):
            if o == g:
                continue
            wins = (gs[o] > gs[g]) | ((gs[o] == gs[g]) & (o < g))
            beat = beat + wins.astype(F32)
        keep = beat < float(TOPK_GROUPS)
        masked.append(jnp.where(keep, sb[g * GROUP_SIZE:(g + 1) * GROUP_SIZE], ninf))
    v = jnp.concatenate(masked, axis=0)

    io_e = lax.broadcasted_iota(jnp.int32, (e, tr), 0).astype(F32)
    ids, ws = [], []
    sel = jnp.zeros((e, tr), F32)
    for _ in range(TOP_K):
        m = jnp.max(v, axis=0, keepdims=True)
        i = jnp.min(jnp.where(v == m, io_e, float(e)), axis=0, keepdims=True)
        oh = io_e == i
        ids.append(i)
        ws.append(jnp.sum(jnp.where(oh, sc, 0.0), axis=0, keepdims=True))
        v = jnp.where(oh, ninf, v)
        sel = sel + oh.astype(F32)

    wsum = ws[0]
    for k in range(1, TOP_K):
        wsum = wsum + ws[k]
    wgt_ref[...] = jnp.concatenate([w / wsum * ROUTED_SCALE for w in ws], axis=0)
    idx_ref[...] = jnp.concatenate(ids, axis=0).astype(jnp.int32)

    excl = jnp.dot(sel.astype(BF16), u_ref[...], preferred_element_type=F32)
    base = carry_ref[:, 0:1]
    rank_full = base + excl
    ranks = [jnp.sum(jnp.where(io_e == ids[k], rank_full, 0.0), axis=0, keepdims=True)
             for k in range(TOP_K)]
    rank_ref[...] = jnp.concatenate(ranks, axis=0).astype(jnp.int32)
    new = base + jnp.sum(sel, axis=1, keepdims=True)
    carry_ref[...] = jnp.broadcast_to(new, carry_ref.shape)
    cnt_ref[...] = jnp.broadcast_to(new, cnt_ref.shape).astype(jnp.int32)


def _router(h2p, wr_t_b, b_router_col, tr):
    b, s, d = h2p.shape
    t = b * s
    nt = s // tr
    u = jnp.asarray(np.triu(np.ones((tr, tr), np.float32), k=1), BF16)
    col = lambda i, j: (0, i * nt + j)
    return pl.pallas_call(
        functools.partial(_router_kernel, tr=tr),
        grid=(b, nt),
        in_specs=[pl.BlockSpec((1, tr, d), lambda i, j: (i, j, 0)),
                  pl.BlockSpec(wr_t_b.shape, lambda i, j: (0, 0)),
                  pl.BlockSpec((N_EXPERTS, 1), lambda i, j: (0, 0)),
                  pl.BlockSpec((tr, tr), lambda i, j: (0, 0))],
        out_specs=[pl.BlockSpec((TOP_K, tr), col), pl.BlockSpec((TOP_K, tr), col),
                   pl.BlockSpec((TOP_K, tr), col),
                   pl.BlockSpec((N_EXPERTS, 128), lambda i, j: (0, 0))],
        out_shape=[jax.ShapeDtypeStruct((TOP_K, t), jnp.int32),
                   jax.ShapeDtypeStruct((TOP_K, t), jnp.int32),
                   jax.ShapeDtypeStruct((TOP_K, t), F32),
                   jax.ShapeDtypeStruct((N_EXPERTS, 128), jnp.int32)],
        scratch_shapes=[pltpu.VMEM((N_EXPERTS, 128), F32)],
        compiler_params=_cparams(("arbitrary", "arbitrary")),
        name="router",
    )(h2p, wr_t_b, b_router_col, u)


def _dest_kernel(pstart_ref, idx_ref, rank_ref, dest_ref):
    idx = idx_ref[...]

    def body(g, acc):
        for j in range(8):
            e = g * 8 + j
            acc = jnp.where(idx == e, pstart_ref[e], acc)
        return acc

    dest_ref[...] = lax.fori_loop(0, N_EXPERTS // 8, body, jnp.zeros_like(idx)) + rank_ref[...]


def _dest(pstarts, idx, rank, tl):
    k, t = idx.shape
    grid_spec = pltpu.PrefetchScalarGridSpec(
        num_scalar_prefetch=1,
        grid=(t // tl,),
        in_specs=[pl.BlockSpec((k, tl), lambda i, ps: (0, i)), pl.BlockSpec((k, tl), lambda i, ps: (0, i))],
        out_specs=pl.BlockSpec((k, tl), lambda i, ps: (0, i)),
    )
    return pl.pallas_call(
        _dest_kernel,
        grid_spec=grid_spec,
        out_shape=jax.ShapeDtypeStruct((k, t), jnp.int32),
        compiler_params=_cparams(("parallel",)),
        name="dest",
    )(pstarts, idx, rank)


def _sc_mesh():
    return plsc.VectorSubcoreMesh(core_axis_name="c", subcore_axis_name="s",
                                  num_cores=SC_CORES, num_subcores=SC_SUBCORES)


def _sc_worker_base(per_worker):
    return (lax.axis_index("s") * SC_CORES + lax.axis_index("c")) * per_worker


def _dispatch(h2_flat, dest, p_rows):
    t, dh = h2_flat.shape
    r = SC_ROWS
    per_w = t // SC_WORKERS
    nchunk = per_w // r
    assert per_w % (2 * r) == 0

    @functools.partial(
        pl.kernel, mesh=_sc_mesh(),
        out_type=jax.ShapeDtypeStruct((p_rows, dh), h2_flat.dtype),
        scratch_types=[pltpu.VMEM((2, TOP_K, r), jnp.int32), pltpu.VMEM((2, r, dh), h2_flat.dtype),
                       pltpu.SemaphoreType.DMA((2,)), pltpu.SemaphoreType.DMA((2,))],
        name="sc_dispatch",
    )
    def k(rows_hbm, dest_hbm, out_hbm, idx_v, rows_v, lsem, ssem):
        base = _sc_worker_base(per_w)

        def load(ci, slot):
            t0 = base + ci * r
            for kk in range(TOP_K):
                pltpu.sync_copy(dest_hbm.at[kk, pl.ds(t0, r)], idx_v.at[slot, kk])
            pltpu.async_copy(rows_hbm.at[pl.ds(t0, r)], rows_v.at[slot], lsem.at[slot])

        def scatter(ci, slot):
            t0 = base + ci * r
            pltpu.make_async_copy(rows_hbm.at[pl.ds(t0, r)], rows_v.at[slot], lsem.at[slot]).wait()
            for kk in range(TOP_K):
                pltpu.async_copy(rows_v.at[slot], out_hbm.at[idx_v.at[slot, kk]], ssem.at[slot])

        def drain(slot):
            for kk in range(TOP_K):
                pltpu.make_async_copy(rows_v.at[slot], out_hbm.at[idx_v.at[slot, kk]], ssem.at[slot]).wait()

        load(0, 0)

        @pl.loop(0, nchunk, step=2)
        def _(c0):
            for s in range(2):
                ci = c0 + s

                @pl.when(ci + 1 < nchunk)
                def _():
                    @pl.when(ci >= 1)
                    def _():
                        drain(1 - s)
                    load(ci + 1, 1 - s)

                scatter(ci, s)

        drain(0)
        drain(1)

    return k(h2_flat, dest)


def _gather_sum(obuf, dest, wgt):
    _, dh = obuf.shape
    kk_n, t = dest.shape
    r, lanes = SC_SUM_TOKENS, SC_LANES
    assert r * lanes == 128
    per_w = t // SC_WORKERS
    nchunk = per_w // r
    idx_rows = per_w * kk_n // 128
    assert per_w % (2 * r) == 0 and idx_rows % 8 == 0
    nj = dh // lanes
    dest_c = dest.reshape(kk_n, t // r, r).transpose(1, 0, 2).reshape(t * kk_n // 128, 128)
    w_c = jnp.broadcast_to(wgt.reshape(kk_n, t // r, r).transpose(1, 0, 2)[..., None],
                           (t // r, kk_n, r, lanes)).reshape(t // r, kk_n, r * lanes)

    @functools.partial(
        pl.kernel, mesh=_sc_mesh(),
        out_type=jax.ShapeDtypeStruct((t, 2 * dh), F32),
        scratch_types=[pltpu.VMEM((idx_rows, 128), jnp.int32), pltpu.VMEM((2, kk_n, r, dh), obuf.dtype),
                       pltpu.VMEM((2, kk_n, r * lanes), F32), pltpu.VMEM((2, r, 2 * dh), F32),
                       pltpu.SemaphoreType.DMA((2,)), pltpu.SemaphoreType.DMA((2,))],
        compiler_params=pltpu.CompilerParams(needs_layout_passes=False),
        name="sc_gather_sum",
    )
    def k(table_hbm, idx_hbm, w_hbm, out_hbm, idx_v, rows_v, w_v, out_v, gsem, wsem):
        wid = lax.axis_index("s") * SC_CORES + lax.axis_index("c")
        base = wid * per_w
        cbase = wid * nchunk
        pltpu.sync_copy(idx_hbm.at[pl.ds(wid * idx_rows, idx_rows)], idx_v)

        def idx_list(ci, kk):
            off = (ci * kk_n + kk) * r
            return idx_v.at[off // 128, pl.ds(pl.multiple_of(off % 128, 8), r)]

        def copies(ci, slot):
            yield pltpu.make_async_copy(w_hbm.at[cbase + ci], w_v.at[slot], gsem.at[slot])
            for kk in range(kk_n):
                yield pltpu.make_async_copy(table_hbm.at[idx_list(ci, kk)], rows_v.at[slot, kk], gsem.at[slot])

        def out_copy(ci, slot):
            return pltpu.make_async_copy(out_v.at[slot], out_hbm.at[pl.ds(base + ci * r, r)], wsem.at[slot])

        def compute(slot):
            @pl.loop(0, r)
            def _(i):
                ws = [w_v[slot, kk, pl.ds(pl.multiple_of(i * lanes, lanes), lanes)] for kk in range(kk_n)]

                @plsc.parallel_loop(0, nj, unroll=4)
                def _(j):
                    col = pl.multiple_of(j * lanes, lanes)
                    lo = jnp.zeros((lanes,), F32)
                    hi = jnp.zeros((lanes,), F32)
                    for kk in range(kk_n):
                        v = rows_v[slot, kk, i, pl.ds(col, lanes)]
                        lo = lo + ws[kk] * plsc.bitcast(v << 16, F32)
                        hi = hi + ws[kk] * plsc.bitcast(v & jnp.uint32(0xFFFF0000), F32)
                    out_v[slot, i, pl.ds(col, lanes)] = lo
                    out_v[slot, i, pl.ds(dh + col, lanes)] = hi

        for c in copies(0, 0):
            c.start()

        @pl.loop(0, nchunk, step=2)
        def _(c0):
            for s in range(2):
                ci = c0 + s

                @pl.when(ci + 1 < nchunk)
                def _():
                    for c in copies(ci + 1, 1 - s):
                        c.start()

                for c in copies(ci, s):
                    c.wait()

                @pl.when(ci >= 2)
                def _():
                    out_copy(ci - 2, s).wait()

                compute(s)
                out_copy(ci, s).start()

        out_copy(nchunk - 2, 0).wait()
        out_copy(nchunk - 1, 1).wait()

    return k(obuf, dest_c, w_c)


def _experts_kernel(be_ref, nvalid_ref, run_ref, nxt_ref, nused_ref, x_ref, wg_hbm, wu_hbm, wd_hbm, o_ref,
                    wg_f, wu_f, wd_f, wg_s, wu_s, wd_s, sem):
    b = pl.program_id(0)

    def weight_copies(e, slot):
        return (pltpu.make_async_copy(wg_hbm.at[e], wg_f.at[slot], sem.at[slot]),
                pltpu.make_async_copy(wu_hbm.at[e], wu_f.at[slot], sem.at[slot]),
                pltpu.make_async_copy(wd_hbm.at[e], wd_f.at[slot], sem.at[slot]))

    @pl.when(b < nused_ref[0])
    def _():
        e = be_ref[b]
        slot = run_ref[b] % 2

        @pl.when(b == 0)
        def _():
            for c in weight_copies(e, slot):
                c.start()

        @pl.when((b == 0) | (e != be_ref[jnp.maximum(b - 1, 0)]))
        def _():
            for c in weight_copies(e, slot):
                c.wait()

            @pl.when(nxt_ref[b] >= 0)
            def _():
                for c in weight_copies(nxt_ref[b], 1 - slot):
                    c.start()

            wg_s[...] = wg_f[slot].astype(BF16)
            wu_s[...] = wu_f[slot].astype(BF16)
            wd_s[...] = wd_f[slot].astype(BF16)

        rows = lax.broadcasted_iota(jnp.int32, x_ref.shape, 0)
        xp = jnp.where(rows < nvalid_ref[b], x_ref[...], jnp.uint32(0))
        x = _unpack_bf16_pairs(xp).astype(BF16)
        g = jnp.dot(x, wg_s[...], preferred_element_type=F32)
        u = jnp.dot(x, wu_s[...], preferred_element_type=F32)
        hid = (_silu(g) * u).astype(BF16)
        o_ref[...] = _pack_bf16_pairs(jnp.dot(hid, wd_s[...], preferred_element_type=F32))

    @pl.when(b >= nused_ref[0])
    def _():
        o_ref[...] = jnp.zeros_like(o_ref)


def _experts(xbuf, block_e, nvalid, run, nxt, nused, w_gate, w_up, w_down, bm):
    p, dh = xbuf.shape
    nb = p // bm
    d, de = w_gate.shape[1:]
    hbm = pl.BlockSpec(memory_space=pl.ANY)
    grid_spec = pltpu.PrefetchScalarGridSpec(
        num_scalar_prefetch=5,
        grid=(nb,),
        in_specs=[pl.BlockSpec((bm, dh), lambda b, be, nv, rn, nx, nu: (jnp.minimum(b, nu[0] - 1), 0)),
                  hbm, hbm, hbm],
        out_specs=pl.BlockSpec((bm, dh), lambda b, be, nv, rn, nx, nu: (b, 0)),
        scratch_shapes=[pltpu.VMEM((2, d, de), F32), pltpu.VMEM((2, d, de), F32), pltpu.VMEM((2, de, d), F32),
                        pltpu.VMEM((d, de), BF16), pltpu.VMEM((d, de), BF16), pltpu.VMEM((de, d), BF16),
                        pltpu.SemaphoreType.DMA((2,))],
    )
    return pl.pallas_call(
        _experts_kernel,
        grid_spec=grid_spec,
        out_shape=jax.ShapeDtypeStruct((p, dh), jnp.uint32),
        compiler_params=_cparams(("arbitrary",)),
        name="experts",
    )(block_e, nvalid, run, nxt, nused, xbuf, w_gate, w_up, w_down)


def _combine_kernel(y_ref, x1_ref, ysh_ref, mod_ref, g_ref, out_ref):
    y = y_ref[...] + _unpack_bf16_pairs(ysh_ref[...])
    out_ref[...] = x1_ref[...] + mod_ref[0, 5:6, :] * (_rms(y) * g_ref[...])


def _combine(y_routed, x1_flat, ysh_flat, mod3, g_post, s, tc):
    t, d = x1_flat.shape
    per_seq = s // tc
    tok = pl.BlockSpec((tc, d), lambda i: (i, 0))
    return pl.pallas_call(
        _combine_kernel,
        grid=(t // tc,),
        in_specs=[tok, tok, pl.BlockSpec((tc, d // 2), lambda i: (i, 0)),
                  pl.BlockSpec((1, 6, d), lambda i: (i // per_seq, 0, 0)),
                  pl.BlockSpec((1, d), lambda i: (0, 0))],
        out_specs=tok,
        out_shape=jax.ShapeDtypeStruct((t, d), F32),
        compiler_params=_cparams(("parallel",)),
        name="combine",
    )(y_routed, x1_flat, ysh_flat, mod3, g_post)


def _channel_dft_table():
    c = np.arange(GROUP_DIM)
    ang = 2.0 * np.pi * ((c[:, None] * c[None, :]) % GROUP_DIM) / GROUP_DIM
    eye = np.eye(D_FNET // GROUP_DIM)
    scale = 1.0 / np.sqrt(GROUP_DIM)
    cos_m, sin_m = np.kron(eye, np.cos(ang)) * scale, np.kron(eye, np.sin(ang)) * scale
    cols = [m[:, c * DFT_CH:(c + 1) * DFT_CH] for c in range(D_FNET // DFT_CH) for m in (cos_m, sin_m)]
    return jnp.asarray(np.concatenate(cols, axis=1), BF16)


EXPERT_BLOCK_MAX = 1280
EXPERT_BLOCK_ALIGN = 64


def _expert_block_rows(n_assign):
    target = max(n_assign // N_EXPERTS * 9 // 8, 2 * EXPERT_BLOCK_ALIGN)
    k = -(-target // EXPERT_BLOCK_MAX)
    return -(-target // (k * EXPERT_BLOCK_ALIGN)) * EXPERT_BLOCK_ALIGN


def _layer(x, mod, p):
    b, s, d = x.shape
    t = b * s
    mod3 = mod.reshape(b, 6, d)
    g_tab, h_tab = _dft_tables(s)

    v, zcs = _inproj(x, mod3, p["g_mix_pre"], p["w_in_b"], _channel_dft_table())
    cn = _conv(v, p["conv_w"], p["conv_b"], p["conv_ln_g"], p["conv_ln_b"], p["g_conv_out"], min(256, s))
    fy = _seqdft(zcs, g_tab, h_tab)
    x1, h2, ysh = _mix(x, cn, fy, mod3, p["g_fnet_out"], p["w_out_b"], p["g_mix_post"], p["g_ffn_pre"],
                       p["wsg_b"], p["wsu_b"], p["wsd_b"], min(256, s))
    idx, rank, wgt, cnt = _router(h2, p["wr_t_b"], p["b_router_col"], min(256, s))

    n = t * TOP_K
    bm = _expert_block_rows(n)
    counts = cnt[:, 0]
    pcounts = (counts + bm - 1) // bm * bm
    pends = jnp.cumsum(pcounts)
    pstarts = pends - pcounts
    dest = _dest(pstarts.astype(jnp.int32), idx, rank, min(2048, t))
    nb = (n + N_EXPERTS * (bm - 1) + bm - 1) // bm
    nused = (pends[-1] // bm).astype(jnp.int32)
    blk = jnp.minimum(jnp.arange(nb, dtype=jnp.int32), nused - 1) * bm
    block_e = jnp.sum((pends[None, :] <= blk[:, None]).astype(jnp.int32), axis=1)
    block_e = jnp.minimum(block_e, N_EXPERTS - 1)
    nvalid = jnp.clip(pstarts[block_e] + counts[block_e] - blk, 0, bm).astype(jnp.int32)
    first = jnp.concatenate([jnp.ones((1,), jnp.int32), (block_e[1:] != block_e[:-1]).astype(jnp.int32)])
    run = jnp.cumsum(first) - 1
    eid = jnp.arange(N_EXPERTS, dtype=jnp.int32)
    later = lax.cummin(jnp.where(pcounts > 0, eid, N_EXPERTS)[::-1])[::-1]
    nxt_e = jnp.concatenate([later[1:], jnp.full((1,), N_EXPERTS, jnp.int32)])
    nxt = jnp.where(nxt_e < N_EXPERTS, nxt_e, -1)[block_e].astype(jnp.int32)

    xbuf = _dispatch(h2.reshape(t, d // 2), dest, nb * bm)
    obuf = _experts(xbuf, block_e, nvalid, run.astype(jnp.int32), nxt, nused.reshape(1),
                    p["w_gate"], p["w_up"], p["w_down"], bm)
    y_routed = _gather_sum(obuf, dest, wgt)
    out = _combine(y_routed, x1.reshape(t, d), ysh.reshape(t, d // 2), mod3, p["g_ffn_post"], s, min(512, s))
    return out.reshape(b, s, d)


def kernel(x_prompt, x_sample, c_prompt, c_sample, w_ada, b_ada, g_mix_pre, w_in, conv_w, conv_b, conv_ln_g, conv_ln_b, g_conv_out, g_fnet_out, w_out, g_mix_post, g_ffn_pre, w_router, b_router, w_gate, w_up, w_down, ws_gate, ws_up, ws_down, g_ffn_post):
    assert w_ada.shape[0] == 1, "single-layer kernel"
    bp, bs = c_prompt.shape[0], c_sample.shape[0]
    rows = -(-(bp + bs) // 8) * 8
    c_all = jnp.zeros((rows, D_MODEL), F32).at[:bp].set(c_prompt).at[bp:bp + bs].set(c_sample)
    mod = _ada(c_all, w_ada[0], b_ada)
    p = {
        "g_mix_pre": g_mix_pre, "w_in_b": w_in[0].astype(BF16),
        "conv_w": conv_w[0], "conv_b": conv_b, "conv_ln_g": conv_ln_g, "conv_ln_b": conv_ln_b,
        "g_conv_out": g_conv_out, "g_fnet_out": g_fnet_out, "w_out_b": w_out[0].astype(BF16),
        "g_mix_post": g_mix_post, "g_ffn_pre": g_ffn_pre,
        "wr_t_b": w_router[0].T.astype(BF16), "b_router_col": b_router[0][:, None],
        "w_gate": w_gate[0], "w_up": w_up[0], "w_down": w_down[0],
        "wsg_b": ws_gate[0].astype(BF16), "wsu_b": ws_up[0].astype(BF16), "wsd_b": ws_down[0].astype(BF16),
        "g_ffn_post": g_ffn_post,
    }
    y_prompt = _layer(x_prompt, mod[:bp], p)
    y_sample = _layer(x_sample, mod[bp:bp + bs], p)
    return (y_prompt, y_sample)
```

```python
import functools

import numpy as np
import jax
import jax.numpy as jnp
from jax import lax
from jax.experimental import pallas as pl
from jax.experimental.pallas import tpu as pltpu
from jax.experimental.pallas import tpu_sc as plsc

F32 = jnp.float32
BF16 = jnp.bfloat16

D_MODEL = 1024
D_CONV = 512
D_FNET = 512
GROUP_DIM = 64
CONV_WIDTH = 31
N_EXPERTS = 256
TOP_K = 8
N_EXPERT_GROUPS = 8
GROUP_SIZE = N_EXPERTS // N_EXPERT_GROUPS
TOPK_GROUPS = 4
ROUTED_SCALE = 2.5
EPS = 1e-6

DFT_S1 = 128
DFT_CH = 128
HALO = 16
VMEM_LIMIT = 56 * 1024 * 1024
SC_CORES = 2
SC_SUBCORES = 16
SC_WORKERS = SC_CORES * SC_SUBCORES
SC_ROWS = 64
SC_LANES = 16
SC_SUM_TOKENS = 8
SC_PACK_WORDS = 32768


def _cparams(sem, vmem=None):
    return pltpu.CompilerParams(dimension_semantics=sem, vmem_limit_bytes=vmem or VMEM_LIMIT)


def _rms(x):
    return x * lax.rsqrt(jnp.mean(x * x, axis=-1, keepdims=True) + EPS)


def _silu(x):
    return x * jax.nn.sigmoid(x)


def _pack_bf16_pairs(x):
    c = x.shape[-1] // 2
    bits = lax.bitcast_convert_type(x.astype(BF16).astype(F32), jnp.uint32)
    return (bits[:, :c] >> 16) | bits[:, c:]


def _unpack_bf16_pairs(p):
    lo = lax.bitcast_convert_type(p << 16, F32)
    hi = lax.bitcast_convert_type(p & jnp.uint32(0xFFFF0000), F32)
    return jnp.concatenate([lo, hi], axis=-1)


def _ada_kernel(c_ref, w_ref, b_ref, o_ref):
    o_ref[...] = jnp.dot(_silu(c_ref[...]), w_ref[...], preferred_element_type=F32) + b_ref[...]


def _ada(c, w_ada, b_ada):
    bp, d = c.shape
    n = w_ada.shape[1]
    return pl.pallas_call(
        _ada_kernel,
        grid=(n // d,),
        in_specs=[pl.BlockSpec((bp, d), lambda j: (0, 0)),
                  pl.BlockSpec((d, d), lambda j: (0, j)),
                  pl.BlockSpec((1, d), lambda j: (0, j))],
        out_specs=pl.BlockSpec((bp, d), lambda j: (0, j)),
        out_shape=jax.ShapeDtypeStruct((bp, n), F32),
        compiler_params=_cparams(("parallel",)),
        name="ada",
    )(c, w_ada, b_ada)


def _inproj_kernel(x_ref, mod_ref, g_ref, win_ref, cs_ref, perm_ref, v_ref, z_ref, *, s2, p1l):
    x = x_ref[0]
    h = _rms(x) * g_ref[...]
    h = h * (1.0 + mod_ref[0, 1:2, :]) + mod_ref[0, 0:1, :]
    u = jnp.dot(h.astype(BF16), win_ref[...], preferred_element_type=F32)
    a = u[:, :D_CONV]
    gt = u[:, D_CONV:2 * D_CONV]
    z = u[:, 2 * D_CONV:]
    v_ref[0] = a * jax.nn.sigmoid(gt)
    zp = jnp.dot(perm_ref[...], z.astype(BF16), preferred_element_type=F32).astype(BF16)
    zcs = jnp.dot(zp, cs_ref[...], preferred_element_type=F32).astype(BF16)
    z_ref[0] = zcs.reshape(s2, p1l, zcs.shape[-1])


def _inproj(x, mod3, g_mix_pre, w_in_b, cs_b):
    b, s, d = x.shape
    s2 = s // DFT_S1
    p1l = 16
    ts = p1l * s2
    r = np.arange(ts)
    perm = np.zeros((ts, ts), np.float32)
    perm[(r % s2) * p1l + r // s2, r] = 1.0
    perm = jnp.asarray(perm, BF16)
    return pl.pallas_call(
        functools.partial(_inproj_kernel, s2=s2, p1l=p1l),
        grid=(b, s // ts),
        in_specs=[pl.BlockSpec((1, ts, d), lambda i, t: (i, t, 0)),
                  pl.BlockSpec((1, 6, d), lambda i, t: (i, 0, 0)),
                  pl.BlockSpec((1, d), lambda i, t: (0, 0)),
                  pl.BlockSpec(w_in_b.shape, lambda i, t: (0, 0)),
                  pl.BlockSpec(cs_b.shape, lambda i, t: (0, 0)),
                  pl.BlockSpec((ts, ts), lambda i, t: (0, 0))],
        out_specs=[pl.BlockSpec((1, ts, D_CONV), lambda i, t: (i, t, 0)),
                   pl.BlockSpec((1, s2, p1l, 2 * D_FNET), lambda i, t: (i, 0, t, 0))],
        out_shape=[jax.ShapeDtypeStruct((b, s, D_CONV), F32),
                   jax.ShapeDtypeStruct((b, s2, DFT_S1, 2 * D_FNET), BF16)],
        compiler_params=_cparams(("parallel", "parallel")),
        name="inproj",
    )(x, mod3, g_mix_pre, w_in_b, cs_b, perm)


def _conv_kernel(vp_ref, v_ref, vn_ref, w_ref, b_ref, lg_ref, lb_ref, go_ref, o_ref, pad_ref, sh_ref, *, ts, rc):
    t = pl.program_id(1)
    nt = pl.num_programs(1)
    pad_ref[0:HALO, :] = jnp.where(t > 0, vp_ref[0], 0.0)
    pad_ref[HALO:HALO + ts, :] = v_ref[0]
    pad_ref[HALO + ts:HALO + ts + HALO, :] = jnp.where(t < nt - 1, vn_ref[0], 0.0)
    span = ts + 2 * HALO - 8
    for m in range(8):
        sh_ref[m] = pad_ref[m:m + span, :]
    off = HALO - CONV_WIDTH // 2
    for c in range(ts // rc):
        r0 = c * rc
        acc = jnp.zeros((rc // 8, 8, D_CONV), F32)
        for j in range(CONV_WIDTH):
            m, q = (off + j) % 8, (off + j) // 8
            tap = sh_ref[m, r0 + 8 * q:r0 + 8 * q + rc, :].reshape(rc // 8, 8, D_CONV)
            acc = acc + tap * w_ref[j][None]
        acc = acc.reshape(rc, D_CONV) + b_ref[...]
        mu = jnp.mean(acc, axis=-1, keepdims=True)
        xc = acc - mu
        var = jnp.mean(xc * xc, axis=-1, keepdims=True)
        y = xc * lax.rsqrt(var + EPS) * lg_ref[...] + lb_ref[...]
        y = _silu(y)
        y = _rms(y) * go_ref[...]
        o_ref[0, r0:r0 + rc, :] = y.astype(o_ref.dtype)


def _conv(v, conv_w, conv_b, ln_g, ln_b, g_out, ts, rc=32):
    b, s, c = v.shape
    hb = ts // HALO
    nh = s // HALO
    vec = pl.BlockSpec((1, c), lambda i, t: (0, 0))
    return pl.pallas_call(
        functools.partial(_conv_kernel, ts=ts, rc=rc),
        grid=(b, s // ts),
        in_specs=[pl.BlockSpec((1, HALO, c), lambda i, t: (i, jnp.maximum(t * hb - 1, 0), 0)),
                  pl.BlockSpec((1, ts, c), lambda i, t: (i, t, 0)),
                  pl.BlockSpec((1, HALO, c), lambda i, t: (i, jnp.minimum((t + 1) * hb, nh - 1), 0)),
                  pl.BlockSpec((CONV_WIDTH, 8, c), lambda i, t: (0, 0, 0)),
                  vec, vec, vec, vec],
        out_specs=pl.BlockSpec((1, ts, c), lambda i, t: (i, t, 0)),
        out_shape=jax.ShapeDtypeStruct((b, s, c), BF16),
        scratch_shapes=[pltpu.VMEM((ts + 2 * HALO, c), F32), pltpu.VMEM((8, ts + 2 * HALO - 8, c), F32)],
        compiler_params=_cparams(("parallel", "parallel")),
        name="conv",
    )(v, v, v, jnp.broadcast_to(conv_w[:, None, :], (CONV_WIDTH, 8, c)), conv_b, ln_g, ln_b, g_out)


def _dft_tables(s):
    s1 = DFT_S1
    s2 = s // s1
    k1 = np.arange(s1)[None, :, None]
    p1 = np.arange(s1)[None, None, :]
    p2 = np.arange(s2)[:, None, None]
    ang = 2.0 * np.pi * ((k1 * (s2 * p1 + p2)) % s) / s
    g = np.concatenate([np.cos(ang), np.sin(ang)], axis=1)
    k2 = np.arange(s2)[:, None]
    q2 = np.arange(s2)[None, :]
    ang2 = 2.0 * np.pi * ((k2 * q2) % s2) / s2
    h = np.concatenate([np.cos(ang2), np.sin(ang2)], axis=1) / np.sqrt(s)
    return jnp.asarray(g, BF16), jnp.asarray(h, BF16)


def _seqdft_kernel(z_ref, g_ref, h_ref, o_ref, scr_ref, *, s1, s2):
    ch = DFT_CH
    for p2 in range(s2):
        r = jnp.dot(g_ref[p2], z_ref[0, p2], preferred_element_type=F32)
        a_re = r[:s1, :ch] - r[s1:, ch:]
        a_im = -(r[:s1, ch:] + r[s1:, :ch])
        scr_ref[:, p2, :] = a_re
        scr_ref[:, s2 + p2, :] = a_im
    hmat = h_ref[...]
    for k1 in range(s1):
        y = jnp.dot(hmat, scr_ref[k1].astype(BF16), preferred_element_type=F32)
        o_ref[0, pl.ds(k1, s2, stride=s1), :] = y


def _seqdft(zp, g_tab, h_tab):
    b, s2, s1, _ = zp.shape
    s = s1 * s2
    nch = D_FNET // DFT_CH
    return pl.pallas_call(
        functools.partial(_seqdft_kernel, s1=s1, s2=s2),
        grid=(b, nch),
        in_specs=[pl.BlockSpec((1, s2, s1, 2 * DFT_CH), lambda i, c: (i, 0, 0, c)),
                  pl.BlockSpec(g_tab.shape, lambda i, c: (0, 0, 0)),
                  pl.BlockSpec(h_tab.shape, lambda i, c: (0, 0))],
        out_specs=pl.BlockSpec((1, s, DFT_CH), lambda i, c: (i, 0, c)),
        out_shape=jax.ShapeDtypeStruct((b, s, D_FNET), F32),
        scratch_shapes=[pltpu.VMEM((s1, 2 * s2, DFT_CH), F32)],
        compiler_params=_cparams(("parallel", "parallel")),
        name="seqdft",
    )(zp, g_tab, h_tab)


def _mix_kernel(x_ref, cn_ref, fy_ref, mod_ref, gf_ref, wout_ref, gpost_ref, gpre_ref,
                wsg_ref, wsu_ref, wsd_ref, x1_ref, h2_ref, ysh_ref):
    fn = _rms(fy_ref[0]) * gf_ref[...]
    mixed = jnp.dot(cn_ref[0], wout_ref[:D_CONV, :], preferred_element_type=F32)
    mixed = mixed + jnp.dot(fn.astype(BF16), wout_ref[D_CONV:, :], preferred_element_type=F32)
    x1 = x_ref[0] + mod_ref[0, 2:3, :] * (_rms(mixed) * gpost_ref[...])
    x1_ref[0] = x1
    h2 = _rms(x1) * gpre_ref[...]
    h2 = h2 * (1.0 + mod_ref[0, 4:5, :]) + mod_ref[0, 3:4, :]
    h2_ref[0] = _pack_bf16_pairs(h2)
    hb = h2.astype(BF16)
    hid = _silu(jnp.dot(hb, wsg_ref[...], preferred_element_type=F32))
    hid = hid * jnp.dot(hb, wsu_ref[...], preferred_element_type=F32)
    ysh_ref[0] = _pack_bf16_pairs(jnp.dot(hid.astype(BF16), wsd_ref[...], preferred_element_type=F32))


def _mix(x, cn, fy, mod3, g_fnet, w_out_b, g_post, g_pre, wsg_b, wsu_b, wsd_b, ts):
    b, s, d = x.shape
    tok = lambda c: pl.BlockSpec((1, ts, c), lambda i, t: (i, t, 0))
    full = lambda a: pl.BlockSpec(a.shape, lambda i, t: (0,) * a.ndim)
    return pl.pallas_call(
        _mix_kernel,
        grid=(b, s // ts),
        in_specs=[tok(d), tok(D_CONV), tok(D_FNET),
                  pl.BlockSpec((1, 6, d), lambda i, t: (i, 0, 0)),
                  full(g_fnet), full(w_out_b), full(g_post), full(g_pre),
                  full(wsg_b), full(wsu_b), full(wsd_b)],
        out_specs=[tok(d), tok(d // 2), tok(d // 2)],
        out_shape=[jax.ShapeDtypeStruct((b, s, d), F32),
                   jax.ShapeDtypeStruct((b, s, d // 2), jnp.uint32),
                   jax.ShapeDtypeStruct((b, s, d // 2), jnp.uint32)],
        compiler_params=_cparams(("parallel", "parallel")),
        name="mix",
    )(x, cn, fy, mod3, g_fnet, w_out_b, g_post, g_pre, wsg_b, wsu_b, wsd_b)


def _router_kernel(h_ref, wr_ref, br_ref, u_ref, idx_ref, rank_ref, wgt_ref, cnt_ref, carry_ref, *, tr):
    e = N_EXPERTS

    @pl.when((pl.program_id(0) == 0) & (pl.program_id(1) == 0))
    def _():
        carry_ref[...] = jnp.zeros_like(carry_ref)

    logits = lax.dot_general(wr_ref[...], _unpack_bf16_pairs(h_ref[0]).astype(BF16), (((1,), (1,)), ((), ())),
                             preferred_element_type=F32)
    sc = jax.nn.sigmoid(logits)
    sb = sc + br_ref[...]
    ninf = jnp.float32(-jnp.inf)

    io_g = lax.broadcasted_iota(jnp.int32, (GROUP_SIZE, tr), 0).astype(F32)
    gs = []
    for g in range(N_EXPERT_GROUPS):
        blk = sb[g * GROUP_SIZE:(g + 1) * GROUP_SIZE]
        m1 = jnp.max(blk, axis=0, keepdims=True)
        i1 = jnp.min(jnp.where(blk == m1, io_g, float(GROUP_SIZE)), axis=0, keepdims=True)
        m2 = jnp.max(jnp.where(io_g == i1, ninf, blk), axis=0, keepdims=True)
        gs.append(m1 + m2)
    masked = []
    for g in range(N_EXPERT_GROUPS):
        beat = jnp.zeros((1, tr), F32)
        for o in range(N_EXPERT_GROUPS):
            if o == g:
                continue
            wins = (gs[o] > gs[g]) | ((gs[o] == gs[g]) & (o < g))
            beat = beat + wins.astype(F32)
        keep = beat < float(TOPK_GROUPS)
        masked.append(jnp.where(keep, sb[g * GROUP_SIZE:(g + 1) * GROUP_SIZE], ninf))
    v = jnp.concatenate(masked, axis=0)

    io_e = lax.broadcasted_iota(jnp.int32, (e, tr), 0).astype(F32)
    ids, ws = [], []
    sel = jnp.zeros((e, tr), F32)
    for _ in range(TOP_K):
        m = jnp.max(v, axis=0, keepdims=True)
        i = jnp.min(jnp.where(v == m, io_e, float(e)), axis=0, keepdims=True)
        oh = io_e == i
        ids.append(i)
        ws.append(jnp.sum(jnp.where(oh, sc, 0.0), axis=0, keepdims=True))
        v = jnp.where(oh, ninf, v)
        sel = sel + oh.astype(F32)

    wsum = ws[0]
    for k in range(1, TOP_K):
        wsum = wsum + ws[k]
    wgt_ref[...] = jnp.concatenate([w / wsum * ROUTED_SCALE for w in ws], axis=0)
    idx_ref[...] = jnp.concatenate(ids, axis=0).astype(jnp.int32)

    excl = jnp.dot(sel.astype(BF16), u_ref[...], preferred_element_type=F32)
    base = carry_ref[:, 0:1]
    rank_full = base + excl
    ranks = [jnp.sum(jnp.where(io_e == ids[k], rank_full, 0.0), axis=0, keepdims=True)
             for k in range(TOP_K)]
    rank_ref[...] = jnp.concatenate(ranks, axis=0).astype(jnp.int32)
    new = base + jnp.sum(sel, axis=1, keepdims=True)
    carry_ref[...] = jnp.broadcast_to(new, carry_ref.shape)
    cnt_ref[...] = jnp.broadcast_to(new, cnt_ref.shape).astype(jnp.int32)


def _router(h2p, wr_t_b, b_router_col, tr):
    b, s, d = h2p.shape
    t = b * s
    nt = s // tr
    u = jnp.asarray(np.triu(np.ones((tr, tr), np.float32), k=1), BF16)
    col = lambda i, j: (0, i * nt + j)
    return pl.pallas_call(
        functools.partial(_router_kernel, tr=tr),
        grid=(b, nt),
        in_specs=[pl.BlockSpec((1, tr, d), lambda i, j: (i, j, 0)),
                  pl.BlockSpec(wr_t_b.shape, lambda i, j: (0, 0)),
                  pl.BlockSpec((N_EXPERTS, 1), lambda i, j: (0, 0)),
                  pl.BlockSpec((tr, tr), lambda i, j: (0, 0))],
        out_specs=[pl.BlockSpec((TOP_K, tr), col), pl.BlockSpec((TOP_K, tr), col),
                   pl.BlockSpec((TOP_K, tr), col),
                   pl.BlockSpec((N_EXPERTS, 128), lambda i, j: (0, 0))],
        out_shape=[jax.ShapeDtypeStruct((TOP_K, t), jnp.int32),
                   jax.ShapeDtypeStruct((TOP_K, t), jnp.int32),
                   jax.ShapeDtypeStruct((TOP_K, t), F32),
                   jax.ShapeDtypeStruct((N_EXPERTS, 128), jnp.int32)],
        scratch_shapes=[pltpu.VMEM((N_EXPERTS, 128), F32)],
        compiler_params=_cparams(("arbitrary", "arbitrary")),
        name="router",
    )(h2p, wr_t_b, b_router_col, u)


def _dest_kernel(pstart_ref, idx_ref, rank_ref, dest_ref):
    idx = idx_ref[...]

    def body(g, acc):
        for j in range(8):
            e = g * 8 + j
            acc = jnp.where(idx == e, pstart_ref[e], acc)
        return acc

    dest_ref[...] = lax.fori_loop(0, N_EXPERTS // 8, body, jnp.zeros_like(idx)) + rank_ref[...]


def _dest(pstarts, idx, rank, tl):
    k, t = idx.shape
    grid_spec = pltpu.PrefetchScalarGridSpec(
        num_scalar_prefetch=1,
        grid=(t // tl,),
        in_specs=[pl.BlockSpec((k, tl), lambda i, ps: (0, i)), pl.BlockSpec((k, tl), lambda i, ps: (0, i))],
        out_specs=pl.BlockSpec((k, tl), lambda i, ps: (0, i)),
    )
    return pl.pallas_call(
        _dest_kernel,
        grid_spec=grid_spec,
        out_shape=jax.ShapeDtypeStruct((k, t), jnp.int32),
        compiler_params=_cparams(("parallel",)),
        name="dest",
    )(pstarts, idx, rank)


def _sc_mesh():
    return plsc.VectorSubcoreMesh(core_axis_name="c", subcore_axis_name="s",
                                  num_cores=SC_CORES, num_subcores=SC_SUBCORES)


def _sc_worker_base(per_worker):
    return (lax.axis_index("s") * SC_CORES + lax.axis_index("c")) * per_worker


def _pack_weights(w):
    e, r, c = w.shape
    n, ch, lanes = e * r, c // 2, SC_LANES
    rc = SC_PACK_WORDS // c
    per_w = n // SC_WORKERS
    nchunk = per_w // rc
    assert per_w % (2 * rc) == 0 and ch % lanes == 0
    bits = lax.bitcast_convert_type(w, jnp.uint32).reshape(n, c)

    def to_bf16_bits(u):
        rounded = u + jnp.uint32(0x7FFF) + ((u >> 16) & jnp.uint32(1))
        is_nan = (u & jnp.uint32(0x7FFFFFFF)) > jnp.uint32(0x7F800000)
        is_subnormal = (u & jnp.uint32(0x7F800000)) == jnp.uint32(0)
        out = jnp.where(is_nan, u | jnp.uint32(0x00400000), rounded)
        return jnp.where(is_subnormal, u & jnp.uint32(0x80000000), out)

    @functools.partial(
        pl.kernel, mesh=_sc_mesh(),
        out_type=jax.ShapeDtypeStruct((n, ch), jnp.uint32),
        scratch_types=[pltpu.VMEM((2, rc, c), jnp.uint32), pltpu.VMEM((2, rc, ch), jnp.uint32),
                       pltpu.SemaphoreType.DMA((2,)), pltpu.SemaphoreType.DMA((2,))],
        compiler_params=pltpu.CompilerParams(needs_layout_passes=False),
        name="sc_pack_weights",
    )
    def k(w_hbm, out_hbm, in_v, out_v, lsem, ssem):
        base = _sc_worker_base(per_w)

        def in_copy(ci, slot):
            return pltpu.make_async_copy(w_hbm.at[pl.ds(base + ci * rc, rc)], in_v.at[slot], lsem.at[slot])

        def out_copy(ci, slot):
            return pltpu.make_async_copy(out_v.at[slot], out_hbm.at[pl.ds(base + ci * rc, rc)], ssem.at[slot])

        def compute(slot):
            @pl.loop(0, rc)
            def _(i):
                @plsc.parallel_loop(0, ch // lanes, unroll=4)
                def _(j):
                    col = pl.multiple_of(j * lanes, lanes)
                    lo = to_bf16_bits(in_v[slot, i, pl.ds(col, lanes)])
                    hi = to_bf16_bits(in_v[slot, i, pl.ds(ch + col, lanes)])
                    out_v[slot, i, pl.ds(col, lanes)] = (lo >> 16) | (hi & jnp.uint32(0xFFFF0000))

        in_copy(0, 0).start()

        @pl.loop(0, nchunk, step=2)
        def _(c0):
            for s in range(2):
                ci = c0 + s

                @pl.when(ci + 1 < nchunk)
                def _():
                    in_copy(ci + 1, 1 - s).start()

                in_copy(ci, s).wait()

                @pl.when(ci >= 2)
                def _():
                    out_copy(ci - 2, s).wait()

                compute(s)
                out_copy(ci, s).start()

        out_copy(nchunk - 2, 0).wait()
        out_copy(nchunk - 1, 1).wait()

    return k(bits).reshape(e, r, ch)


def _dispatch(h2_flat, dest, p_rows):
    t, dh = h2_flat.shape
    r = SC_ROWS
    per_w = t // SC_WORKERS
    nchunk = per_w // r
    assert per_w % (2 * r) == 0

    @functools.partial(
        pl.kernel, mesh=_sc_mesh(),
        out_type=jax.ShapeDtypeStruct((p_rows, dh), h2_flat.dtype),
        scratch_types=[pltpu.VMEM((2, TOP_K, r), jnp.int32), pltpu.VMEM((2, r, dh), h2_flat.dtype),
                       pltpu.SemaphoreType.DMA((2,)), pltpu.SemaphoreType.DMA((2,))],
        name="sc_dispatch",
    )
    def k(rows_hbm, dest_hbm, out_hbm, idx_v, rows_v, lsem, ssem):
        base = _sc_worker_base(per_w)

        def load(ci, slot):
            t0 = base + ci * r
            for kk in range(TOP_K):
                pltpu.sync_copy(dest_hbm.at[kk, pl.ds(t0, r)], idx_v.at[slot, kk])
            pltpu.async_copy(rows_hbm.at[pl.ds(t0, r)], rows_v.at[slot], lsem.at[slot])

        def scatter(ci, slot):
            t0 = base + ci * r
            pltpu.make_async_copy(rows_hbm.at[pl.ds(t0, r)], rows_v.at[slot], lsem.at[slot]).wait()
            for kk in range(TOP_K):
                pltpu.async_copy(rows_v.at[slot], out_hbm.at[idx_v.at[slot, kk]], ssem.at[slot])

        def drain(slot):
            for kk in range(TOP_K):
                pltpu.make_async_copy(rows_v.at[slot], out_hbm.at[idx_v.at[slot, kk]], ssem.at[slot]).wait()

        load(0, 0)

        @pl.loop(0, nchunk, step=2)
        def _(c0):
            for s in range(2):
                ci = c0 + s

                @pl.when(ci + 1 < nchunk)
                def _():
                    @pl.when(ci >= 1)
                    def _():
                        drain(1 - s)
                    load(ci + 1, 1 - s)

                scatter(ci, s)

        drain(0)
        drain(1)

    return k(h2_flat, dest)


def _gather_sum(obuf, dest, wgt):
    _, dh = obuf.shape
    kk_n, t = dest.shape
    r, lanes = SC_SUM_TOKENS, SC_LANES
    assert r * lanes == 128
    per_w = t // SC_WORKERS
    nchunk = per_w // r
    idx_rows = per_w * kk_n // 128
    assert per_w % (2 * r) == 0 and idx_rows % 8 == 0
    nj = dh // lanes
    dest_c = dest.reshape(kk_n, t // r, r).transpose(1, 0, 2).reshape(t * kk_n // 128, 128)
    w_c = jnp.broadcast_to(wgt.reshape(kk_n, t // r, r).transpose(1, 0, 2)[..., None],
                           (t // r, kk_n, r, lanes)).reshape(t // r, kk_n, r * lanes)

    @functools.partial(
        pl.kernel, mesh=_sc_mesh(),
        out_type=jax.ShapeDtypeStruct((t, 2 * dh), F32),
        scratch_types=[pltpu.VMEM((idx_rows, 128), jnp.int32), pltpu.VMEM((2, kk_n, r, dh), obuf.dtype),
                       pltpu.VMEM((2, kk_n, r * lanes), F32), pltpu.VMEM((2, r, 2 * dh), F32),
                       pltpu.SemaphoreType.DMA((2,)), pltpu.SemaphoreType.DMA((2,))],
        compiler_params=pltpu.CompilerParams(needs_layout_passes=False),
        name="sc_gather_sum",
    )
    def k(table_hbm, idx_hbm, w_hbm, out_hbm, idx_v, rows_v, w_v, out_v, gsem, wsem):
        wid = lax.axis_index("s") * SC_CORES + lax.axis_index("c")
        base = wid * per_w
        cbase = wid * nchunk
        pltpu.sync_copy(idx_hbm.at[pl.ds(wid * idx_rows, idx_rows)], idx_v)

        def idx_list(ci, kk):
            off = (ci * kk_n + kk) * r
            return idx_v.at[off // 128, pl.ds(pl.multiple_of(off % 128, 8), r)]

        def copies(ci, slot):
            yield pltpu.make_async_copy(w_hbm.at[cbase + ci], w_v.at[slot], gsem.at[slot])
            for kk in range(kk_n):
                yield pltpu.make_async_copy(table_hbm.at[idx_list(ci, kk)], rows_v.at[slot, kk], gsem.at[slot])

        def out_copy(ci, slot):
            return pltpu.make_async_copy(out_v.at[slot], out_hbm.at[pl.ds(base + ci * r, r)], wsem.at[slot])

        def compute(slot):
            @pl.loop(0, r)
            def _(i):
                ws = [w_v[slot, kk, pl.ds(pl.multiple_of(i * lanes, lanes), lanes)] for kk in range(kk_n)]

                @plsc.parallel_loop(0, nj, unroll=4)
                def _(j):
                    col = pl.multiple_of(j * lanes, lanes)
                    lo = jnp.zeros((lanes,), F32)
                    hi = jnp.zeros((lanes,), F32)
                    for kk in range(kk_n):
                        v = rows_v[slot, kk, i, pl.ds(col, lanes)]
                        lo = lo + ws[kk] * plsc.bitcast(v << 16, F32)
                        hi = hi + ws[kk] * plsc.bitcast(v & jnp.uint32(0xFFFF0000), F32)
                    out_v[slot, i, pl.ds(col, lanes)] = lo
                    out_v[slot, i, pl.ds(dh + col, lanes)] = hi

        for c in copies(0, 0):
            c.start()

        @pl.loop(0, nchunk, step=2)
        def _(c0):
            for s in range(2):
                ci = c0 + s

                @pl.when(ci + 1 < nchunk)
                def _():
                    for c in copies(ci + 1, 1 - s):
                        c.start()

                for c in copies(ci, s):
                    c.wait()

                @pl.when(ci >= 2)
                def _():
                    out_copy(ci - 2, s).wait()

                compute(s)
                out_copy(ci, s).start()

        out_copy(nchunk - 2, 0).wait()
        out_copy(nchunk - 1, 1).wait()

    return k(obuf, dest_c, w_c)


def _experts_kernel(be_ref, nvalid_ref, run_ref, nxt_ref, nused_ref, x_ref, wg_hbm, wu_hbm, wd_hbm, o_ref,
                    wg_f, wu_f, wd_f, wg_s, wu_s, wd_s, sem):
    b = pl.program_id(0)

    def weight_copies(e, slot):
        return (pltpu.make_async_copy(wg_hbm.at[e], wg_f.at[slot], sem.at[slot]),
                pltpu.make_async_copy(wu_hbm.at[e], wu_f.at[slot], sem.at[slot]),
                pltpu.make_async_copy(wd_hbm.at[e], wd_f.at[slot], sem.at[slot]))

    @pl.when(b < nused_ref[0])
    def _():
        e = be_ref[b]
        slot = run_ref[b] % 2

        @pl.when(b == 0)
        def _():
            for c in weight_copies(e, slot):
                c.start()

        @pl.when((b == 0) | (e != be_ref[jnp.maximum(b - 1, 0)]))
        def _():
            for c in weight_copies(e, slot):
                c.wait()

            @pl.when(nxt_ref[b] >= 0)
            def _():
                for c in weight_copies(nxt_ref[b], 1 - slot):
                    c.start()

            wg_s[...] = _unpack_bf16_pairs(wg_f[slot]).astype(BF16)
            wu_s[...] = _unpack_bf16_pairs(wu_f[slot]).astype(BF16)
            wd_s[...] = _unpack_bf16_pairs(wd_f[slot]).astype(BF16)

        rows = lax.broadcasted_iota(jnp.int32, x_ref.shape, 0)
        xp = jnp.where(rows < nvalid_ref[b], x_ref[...], jnp.uint32(0))
        x = _unpack_bf16_pairs(xp).astype(BF16)
        g = jnp.dot(x, wg_s[...], preferred_element_type=F32)
        u = jnp.dot(x, wu_s[...], preferred_element_type=F32)
        hid = (_silu(g) * u).astype(BF16)
        o_ref[...] = _pack_bf16_pairs(jnp.dot(hid, wd_s[...], preferred_element_type=F32))

    @pl.when(b >= nused_ref[0])
    def _():
        o_ref[...] = jnp.zeros_like(o_ref)


def _experts(xbuf, block_e, nvalid, run, nxt, nused, w_gate, w_up, w_down, bm):
    p, dh = xbuf.shape
    nb = p // bm
    d, de = w_gate.shape[1], w_down.shape[1]
    hbm = pl.BlockSpec(memory_space=pl.ANY)
    grid_spec = pltpu.PrefetchScalarGridSpec(
        num_scalar_prefetch=5,
        grid=(nb,),
        in_specs=[pl.BlockSpec((bm, dh), lambda b, be, nv, rn, nx, nu: (jnp.minimum(b, nu[0] - 1), 0)),
                  hbm, hbm, hbm],
        out_specs=pl.BlockSpec((bm, dh), lambda b, be, nv, rn, nx, nu: (b, 0)),
        scratch_shapes=[pltpu.VMEM((2, d, de // 2), jnp.uint32), pltpu.VMEM((2, d, de // 2), jnp.uint32),
                        pltpu.VMEM((2, de, d // 2), jnp.uint32),
                        pltpu.VMEM((d, de), BF16), pltpu.VMEM((d, de), BF16), pltpu.VMEM((de, d), BF16),
                        pltpu.SemaphoreType.DMA((2,))],
    )
    return pl.pallas_call(
        _experts_kernel,
        grid_spec=grid_spec,
        out_shape=jax.ShapeDtypeStruct((p, dh), jnp.uint32),
        compiler_params=_cparams(("arbitrary",)),
        name="experts",
    )(block_e, nvalid, run, nxt, nused, xbuf, w_gate, w_up, w_down)


def _combine_kernel(y_ref, x1_ref, ysh_ref, mod_ref, g_ref, out_ref):
    y = y_ref[...] + _unpack_bf16_pairs(ysh_ref[...])
    out_ref[...] = x1_ref[...] + mod_ref[0, 5:6, :] * (_rms(y) * g_ref[...])


def _combine(y_routed, x1_flat, ysh_flat, mod3, g_post, s, tc):
    t, d = x1_flat.shape
    per_seq = s // tc
    tok = pl.BlockSpec((tc, d), lambda i: (i, 0))
    return pl.pallas_call(
        _combine_kernel,
        grid=(t // tc,),
        in_specs=[tok, tok, pl.BlockSpec((tc, d // 2), lambda i: (i, 0)),
                  pl.BlockSpec((1, 6, d), lambda i: (i // per_seq, 0, 0)),
                  pl.BlockSpec((1, d), lambda i: (0, 0))],
        out_specs=tok,
        out_shape=jax.ShapeDtypeStruct((t, d), F32),
        compiler_params=_cparams(("parallel",)),
        name="combine",
    )(y_routed, x1_flat, ysh_flat, mod3, g_post)


def _channel_dft_table():
    c = np.arange(GROUP_DIM)
    ang = 2.0 * np.pi * ((c[:, None] * c[None, :]) % GROUP_DIM) / GROUP_DIM
    eye = np.eye(D_FNET // GROUP_DIM)
    scale = 1.0 / np.sqrt(GROUP_DIM)
    cos_m, sin_m = np.kron(eye, np.cos(ang)) * scale, np.kron(eye, np.sin(ang)) * scale
    cols = [m[:, c * DFT_CH:(c + 1) * DFT_CH] for c in range(D_FNET // DFT_CH) for m in (cos_m, sin_m)]
    return jnp.asarray(np.concatenate(cols, axis=1), BF16)


EXPERT_BLOCK_MAX = 1280
EXPERT_BLOCK_ALIGN = 64


def _expert_block_rows(n_assign):
    target = max(n_assign // N_EXPERTS * 9 // 8, 2 * EXPERT_BLOCK_ALIGN)
    k = -(-target // EXPERT_BLOCK_MAX)
    return -(-target // (k * EXPERT_BLOCK_ALIGN)) * EXPERT_BLOCK_ALIGN


def _layer(x, mod, p):
    b, s, d = x.shape
    t = b * s
    mod3 = mod.reshape(b, 6, d)
    g_tab, h_tab = _dft_tables(s)

    v, zcs = _inproj(x, mod3, p["g_mix_pre"], p["w_in_b"], _channel_dft_table())
    cn = _conv(v, p["conv_w"], p["conv_b"], p["conv_ln_g"], p["conv_ln_b"], p["g_conv_out"], min(256, s))
    fy = _seqdft(zcs, g_tab, h_tab)
    x1, h2, ysh = _mix(x, cn, fy, mod3, p["g_fnet_out"], p["w_out_b"], p["g_mix_post"], p["g_ffn_pre"],
                       p["wsg_b"], p["wsu_b"], p["wsd_b"], min(256, s))
    idx, rank, wgt, cnt = _router(h2, p["wr_t_b"], p["b_router_col"], min(256, s))

    n = t * TOP_K
    bm = _expert_block_rows(n)
    counts = cnt[:, 0]
    pcounts = (counts + bm - 1) // bm * bm
    pends = jnp.cumsum(pcounts)
    pstarts = pends - pcounts
    dest = _dest(pstarts.astype(jnp.int32), idx, rank, min(2048, t))
    nb = (n + N_EXPERTS * (bm - 1) + bm - 1) // bm
    nused = (pends[-1] // bm).astype(jnp.int32)
    blk = jnp.minimum(jnp.arange(nb, dtype=jnp.int32), nused - 1) * bm
    block_e = jnp.sum((pends[None, :] <= blk[:, None]).astype(jnp.int32), axis=1)
    block_e = jnp.minimum(block_e, N_EXPERTS - 1)
    nvalid = jnp.clip(pstarts[block_e] + counts[block_e] - blk, 0, bm).astype(jnp.int32)
    first = jnp.concatenate([jnp.ones((1,), jnp.int32), (block_e[1:] != block_e[:-1]).astype(jnp.int32)])
    run = jnp.cumsum(first) - 1
    eid = jnp.arange(N_EXPERTS, dtype=jnp.int32)
    later = lax.cummin(jnp.where(pcounts > 0, eid, N_EXPERTS)[::-1])[::-1]
    nxt_e = jnp.concatenate([later[1:], jnp.full((1,), N_EXPERTS, jnp.int32)])
    nxt = jnp.where(nxt_e < N_EXPERTS, nxt_e, -1)[block_e].astype(jnp.int32)

    xbuf = _dispatch(h2.reshape(t, d // 2), dest, nb * bm)
    obuf = _experts(xbuf, block_e, nvalid, run.astype(jnp.int32), nxt, nused.reshape(1),
                    p["w_gate"], p["w_up"], p["w_down"], bm)
    y_routed = _gather_sum(obuf, dest, wgt)
    out = _combine(y_routed, x1.reshape(t, d), ysh.reshape(t, d // 2), mod3, p["g_ffn_post"], s, min(512, s))
    return out.reshape(b, s, d)


def kernel(x_prompt, x_sample, c_prompt, c_sample, w_ada, b_ada, g_mix_pre, w_in, conv_w, conv_b, conv_ln_g, conv_ln_b, g_conv_out, g_fnet_out, w_out, g_mix_post, g_ffn_pre, w_router, b_router, w_gate, w_up, w_down, ws_gate, ws_up, ws_down, g_ffn_post):
    assert w_ada.shape[0] == 1, "single-layer kernel"
    bp, bs = c_prompt.shape[0], c_sample.shape[0]
    rows = -(-(bp + bs) // 8) * 8
    c_all = jnp.zeros((rows, D_MODEL), F32).at[:bp].set(c_prompt).at[bp:bp + bs].set(c_sample)
    mod = _ada(c_all, w_ada[0], b_ada)
    p = {
        "g_mix_pre": g_mix_pre, "w_in_b": w_in[0].astype(BF16),
        "conv_w": conv_w[0], "conv_b": conv_b, "conv_ln_g": conv_ln_g, "conv_ln_b": conv_ln_b,
        "g_conv_out": g_conv_out, "g_fnet_out": g_fnet_out, "w_out_b": w_out[0].astype(BF16),
        "g_mix_post": g_mix_post, "g_ffn_pre": g_ffn_pre,
        "wr_t_b": w_router[0].T.astype(BF16), "b_router_col": b_router[0][:, None],
        "w_gate": _pack_weights(w_gate[0]), "w_up": _pack_weights(w_up[0]), "w_down": _pack_weights(w_down[0]),
        "wsg_b": ws_gate[0].astype(BF16), "wsu_b": ws_up[0].astype(BF16), "wsd_b": ws_down[0].astype(BF16),
        "g_ffn_post": g_ffn_post,
    }
    y_prompt = _layer(x_prompt, mod[:bp], p)
    y_sample = _layer(x_sample, mod[bp:bp + bs], p)
    return (y_prompt, y_sample)
```

```python
import functools

import numpy as np
import jax
import jax.numpy as jnp
from jax import lax
from jax.experimental import pallas as pl
from jax.experimental.pallas import tpu as pltpu
from jax.experimental.pallas import tpu_sc as plsc

F32 = jnp.float32
BF16 = jnp.bfloat16

D_MODEL = 1024
D_CONV = 512
D_FNET = 512
GROUP_DIM = 64
CONV_WIDTH = 31
N_EXPERTS = 256
TOP_K = 8
N_EXPERT_GROUPS = 8
GROUP_SIZE = N_EXPERTS // N_EXPERT_GROUPS
TOPK_GROUPS = 4
ROUTED_SCALE = 2.5
EPS = 1e-6

DFT_S1 = 128
DFT_CH = 128
HALO = 16
VMEM_LIMIT = 56 * 1024 * 1024
SC_CORES = 2
SC_SUBCORES = 16
SC_WORKERS = SC_CORES * SC_SUBCORES
SC_ROWS = 64
SC_LANES = 16
SC_SUM_TOKENS = 8
EXPERT_STREAMS = 4

def _cparams(sem, vmem=None):
    return pltpu.CompilerParams(dimension_semantics=sem, vmem_limit_bytes=vmem or VMEM_LIMIT)


def _rms(x):
    return x * lax.rsqrt(jnp.mean(x * x, axis=-1, keepdims=True) + EPS)


def _silu(x):
    return x * jax.nn.sigmoid(x)


def _pack_bf16_pairs(x):
    c = x.shape[-1] // 2
    bits = lax.bitcast_convert_type(x.astype(BF16).astype(F32), jnp.uint32)
    return (bits[:, :c] >> 16) | bits[:, c:]


def _unpack_bf16_pairs(p):
    lo = lax.bitcast_convert_type(p << 16, F32)
    hi = lax.bitcast_convert_type(p & jnp.uint32(0xFFFF0000), F32)
    return jnp.concatenate([lo, hi], axis=-1)


def _ada_kernel(c_ref, w_ref, b_ref, o_ref):
    o_ref[...] = jnp.dot(_silu(c_ref[...]), w_ref[...], preferred_element_type=F32) + b_ref[...]


def _ada(c, w_ada, b_ada):
    bp, d = c.shape
    n = w_ada.shape[1]
    return pl.pallas_call(
        _ada_kernel,
        grid=(n // d,),
        in_specs=[pl.BlockSpec((bp, d), lambda j: (0, 0)),
                  pl.BlockSpec((d, d), lambda j: (0, j)),
                  pl.BlockSpec((1, d), lambda j: (0, j))],
        out_specs=pl.BlockSpec((bp, d), lambda j: (0, j)),
        out_shape=jax.ShapeDtypeStruct((bp, n), F32),
        compiler_params=_cparams(("parallel",)),
        name="ada",
    )(c, w_ada, b_ada)


def _inproj_kernel(x_ref, mod_ref, g_ref, win_ref, cs_ref, perm_ref, v_ref, z_ref, *, s2, p1l):
    x = x_ref[0]
    h = _rms(x) * g_ref[...]
    h = h * (1.0 + mod_ref[0, 1:2, :]) + mod_ref[0, 0:1, :]
    u = jnp.dot(h.astype(BF16), win_ref[...], preferred_element_type=F32)
    a = u[:, :D_CONV]
    gt = u[:, D_CONV:2 * D_CONV]
    z = u[:, 2 * D_CONV:]
    v_ref[0] = a * jax.nn.sigmoid(gt)
    zp = jnp.dot(perm_ref[...], z.astype(BF16), preferred_element_type=F32).astype(BF16)
    zcs = jnp.dot(zp, cs_ref[...], preferred_element_type=F32).astype(BF16)
    z_ref[0] = zcs.reshape(s2, p1l, zcs.shape[-1])


def _inproj(x, mod3, g_mix_pre, w_in_b, cs_b):
    b, s, d = x.shape
    s2 = s // DFT_S1
    p1l = 16
    ts = p1l * s2
    r = np.arange(ts)
    perm = np.zeros((ts, ts), np.float32)
    perm[(r % s2) * p1l + r // s2, r] = 1.0
    perm = jnp.asarray(perm, BF16)
    return pl.pallas_call(
        functools.partial(_inproj_kernel, s2=s2, p1l=p1l),
        grid=(b, s // ts),
        in_specs=[pl.BlockSpec((1, ts, d), lambda i, t: (i, t, 0)),
                  pl.BlockSpec((1, 6, d), lambda i, t: (i, 0, 0)),
                  pl.BlockSpec((1, d), lambda i, t: (0, 0)),
                  pl.BlockSpec(w_in_b.shape, lambda i, t: (0, 0)),
                  pl.BlockSpec(cs_b.shape, lambda i, t: (0, 0)),
                  pl.BlockSpec((ts, ts), lambda i, t: (0, 0))],
        out_specs=[pl.BlockSpec((1, ts, D_CONV), lambda i, t: (i, t, 0)),
                   pl.BlockSpec((1, s2, p1l, 2 * D_FNET), lambda i, t: (i, 0, t, 0))],
        out_shape=[jax.ShapeDtypeStruct((b, s, D_CONV), F32),
                   jax.ShapeDtypeStruct((b, s2, DFT_S1, 2 * D_FNET), BF16)],
        compiler_params=_cparams(("parallel", "parallel")),
        name="inproj",
    )(x, mod3, g_mix_pre, w_in_b, cs_b, perm)


def _conv_kernel(vp_ref, v_ref, vn_ref, w_ref, b_ref, lg_ref, lb_ref, go_ref, o_ref, pad_ref, sh_ref, *, ts, rc):
    t = pl.program_id(1)
    nt = pl.num_programs(1)
    pad_ref[0:HALO, :] = jnp.where(t > 0, vp_ref[0], 0.0)
    pad_ref[HALO:HALO + ts, :] = v_ref[0]
    pad_ref[HALO + ts:HALO + ts + HALO, :] = jnp.where(t < nt - 1, vn_ref[0], 0.0)
    span = ts + 2 * HALO - 8
    for m in range(8):
        sh_ref[m] = pad_ref[m:m + span, :]
    off = HALO - CONV_WIDTH // 2
    for c in range(ts // rc):
        r0 = c * rc
        acc = jnp.zeros((rc // 8, 8, D_CONV), F32)
        for j in range(CONV_WIDTH):
            m, q = (off + j) % 8, (off + j) // 8
            tap = sh_ref[m, r0 + 8 * q:r0 + 8 * q + rc, :].reshape(rc // 8, 8, D_CONV)
            acc = acc + tap * w_ref[j][None]
        acc = acc.reshape(rc, D_CONV) + b_ref[...]
        mu = jnp.mean(acc, axis=-1, keepdims=True)
        xc = acc - mu
        var = jnp.mean(xc * xc, axis=-1, keepdims=True)
        y = xc * lax.rsqrt(var + EPS) * lg_ref[...] + lb_ref[...]
        y = _silu(y)
        y = _rms(y) * go_ref[...]
        o_ref[0, r0:r0 + rc, :] = y.astype(o_ref.dtype)


def _conv(v, conv_w, conv_b, ln_g, ln_b, g_out, ts, rc=32):
    b, s, c = v.shape
    hb = ts // HALO
    nh = s // HALO
    vec = pl.BlockSpec((1, c), lambda i, t: (0, 0))
    return pl.pallas_call(
        functools.partial(_conv_kernel, ts=ts, rc=rc),
        grid=(b, s // ts),
        in_specs=[pl.BlockSpec((1, HALO, c), lambda i, t: (i, jnp.maximum(t * hb - 1, 0), 0)),
                  pl.BlockSpec((1, ts, c), lambda i, t: (i, t, 0)),
                  pl.BlockSpec((1, HALO, c), lambda i, t: (i, jnp.minimum((t + 1) * hb, nh - 1), 0)),
                  pl.BlockSpec((CONV_WIDTH, 8, c), lambda i, t: (0, 0, 0)),
                  vec, vec, vec, vec],
        out_specs=pl.BlockSpec((1, ts, c), lambda i, t: (i, t, 0)),
        out_shape=jax.ShapeDtypeStruct((b, s, c), BF16),
        scratch_shapes=[pltpu.VMEM((ts + 2 * HALO, c), F32), pltpu.VMEM((8, ts + 2 * HALO - 8, c), F32)],
        compiler_params=_cparams(("parallel", "parallel")),
        name="conv",
    )(v, v, v, jnp.broadcast_to(conv_w[:, None, :], (CONV_WIDTH, 8, c)), conv_b, ln_g, ln_b, g_out)


def _dft_tables(s):
    s1 = DFT_S1
    s2 = s // s1
    k1 = np.arange(s1)[None, :, None]
    p1 = np.arange(s1)[None, None, :]
    p2 = np.arange(s2)[:, None, None]
    ang = 2.0 * np.pi * ((k1 * (s2 * p1 + p2)) % s) / s
    g = np.concatenate([np.cos(ang), np.sin(ang)], axis=1)
    k2 = np.arange(s2)[:, None]
    q2 = np.arange(s2)[None, :]
    ang2 = 2.0 * np.pi * ((k2 * q2) % s2) / s2
    h = np.concatenate([np.cos(ang2), np.sin(ang2)], axis=1) / np.sqrt(s)
    return jnp.asarray(g, BF16), jnp.asarray(h, BF16)


def _seqdft_kernel(z_ref, g_ref, h_ref, o_ref, scr_ref, *, s1, s2):
    ch = DFT_CH
    for p2 in range(s2):
        r = jnp.dot(g_ref[p2], z_ref[0, p2], preferred_element_type=F32)
        a_re = r[:s1, :ch] - r[s1:, ch:]
        a_im = -(r[:s1, ch:] + r[s1:, :ch])
        scr_ref[:, p2, :] = a_re
        scr_ref[:, s2 + p2, :] = a_im
    hmat = h_ref[...]
    for k1 in range(s1):
        y = jnp.dot(hmat, scr_ref[k1].astype(BF16), preferred_element_type=F32)
        o_ref[0, pl.ds(k1, s2, stride=s1), :] = y


def _seqdft(zp, g_tab, h_tab):
    b, s2, s1, _ = zp.shape
    s = s1 * s2
    nch = D_FNET // DFT_CH
    return pl.pallas_call(
        functools.partial(_seqdft_kernel, s1=s1, s2=s2),
        grid=(b, nch),
        in_specs=[pl.BlockSpec((1, s2, s1, 2 * DFT_CH), lambda i, c: (i, 0, 0, c)),
                  pl.BlockSpec(g_tab.shape, lambda i, c: (0, 0, 0)),
                  pl.BlockSpec(h_tab.shape, lambda i, c: (0, 0))],
        out_specs=pl.BlockSpec((1, s, DFT_CH), lambda i, c: (i, 0, c)),
        out_shape=jax.ShapeDtypeStruct((b, s, D_FNET), F32),
        scratch_shapes=[pltpu.VMEM((s1, 2 * s2, DFT_CH), F32)],
        compiler_params=_cparams(("parallel", "parallel")),
        name="seqdft",
    )(zp, g_tab, h_tab)


def _mix_kernel(x_ref, cn_ref, fy_ref, mod_ref, gf_ref, wout_ref, gpost_ref, gpre_ref,
                wsg_ref, wsu_ref, wsd_ref, x1_ref, h2_ref, ysh_ref):
    fn = _rms(fy_ref[0]) * gf_ref[...]
    mixed = jnp.dot(cn_ref[0], wout_ref[:D_CONV, :], preferred_element_type=F32)
    mixed = mixed + jnp.dot(fn.astype(BF16), wout_ref[D_CONV:, :], preferred_element_type=F32)
    x1 = x_ref[0] + mod_ref[0, 2:3, :] * (_rms(mixed) * gpost_ref[...])
    x1_ref[0] = x1
    h2 = _rms(x1) * gpre_ref[...]
    h2 = h2 * (1.0 + mod_ref[0, 4:5, :]) + mod_ref[0, 3:4, :]
    h2_ref[0] = _pack_bf16_pairs(h2)
    hb = h2.astype(BF16)
    hid = _silu(jnp.dot(hb, wsg_ref[...], preferred_element_type=F32))
    hid = hid * jnp.dot(hb, wsu_ref[...], preferred_element_type=F32)
    ysh_ref[0] = _pack_bf16_pairs(jnp.dot(hid.astype(BF16), wsd_ref[...], preferred_element_type=F32))


def _mix(x, cn, fy, mod3, g_fnet, w_out_b, g_post, g_pre, wsg_b, wsu_b, wsd_b, ts):
    b, s, d = x.shape
    tok = lambda c: pl.BlockSpec((1, ts, c), lambda i, t: (i, t, 0))
    full = lambda a: pl.BlockSpec(a.shape, lambda i, t: (0,) * a.ndim)
    return pl.pallas_call(
        _mix_kernel,
        grid=(b, s // ts),
        in_specs=[tok(d), tok(D_CONV), tok(D_FNET),
                  pl.BlockSpec((1, 6, d), lambda i, t: (i, 0, 0)),
                  full(g_fnet), full(w_out_b), full(g_post), full(g_pre),
                  full(wsg_b), full(wsu_b), full(wsd_b)],
        out_specs=[tok(d), tok(d // 2), tok(d // 2)],
        out_shape=[jax.ShapeDtypeStruct((b, s, d), F32),
                   jax.ShapeDtypeStruct((b, s, d // 2), jnp.uint32),
                   jax.ShapeDtypeStruct((b, s, d // 2), jnp.uint32)],
        compiler_params=_cparams(("parallel", "parallel")),
        name="mix",
    )(x, cn, fy, mod3, g_fnet, w_out_b, g_post, g_pre, wsg_b, wsu_b, wsd_b)


def _router_kernel(h_ref, wr_ref, br_ref, u_ref, idx_ref, rank_ref, wgt_ref, cnt_ref, carry_ref, *, tr):
    e = N_EXPERTS

    @pl.when((pl.program_id(0) == 0) & (pl.program_id(1) == 0))
    def _():
        carry_ref[...] = jnp.zeros_like(carry_ref)

    logits = lax.dot_general(wr_ref[...], _unpack_bf16_pairs(h_ref[0]).astype(BF16), (((1,), (1,)), ((), ())),
                             preferred_element_type=F32)
    sc = jax.nn.sigmoid(logits)
    sb = sc + br_ref[...]
    ninf = jnp.float32(-jnp.inf)

    io_g = lax.broadcasted_iota(jnp.int32, (GROUP_SIZE, tr), 0).astype(F32)
    gs = []
    for g in range(N_EXPERT_GROUPS):
        blk = sb[g * GROUP_SIZE:(g + 1) * GROUP_SIZE]
        m1 = jnp.max(blk, axis=0, keepdims=True)
        i1 = jnp.min(jnp.where(blk == m1, io_g, float(GROUP_SIZE)), axis=0, keepdims=True)
        m2 = jnp.max(jnp.where(io_g == i1, ninf, blk), axis=0, keepdims=True)
        gs.append(m1 + m2)
    masked = []
    for g in range(N_EXPERT_GROUPS):
        beat = jnp.zeros((1, tr), F32)
        for o in range(N_EXPERT_GROUPS):
            if o == g:
                continue
            wins = (gs[o] > gs[g]) | ((gs[o] == gs[g]) & (o < g))
            beat = beat + wins.astype(F32)
        keep = beat < float(TOPK_GROUPS)
        masked.append(jnp.where(keep, sb[g * GROUP_SIZE:(g + 1) * GROUP_SIZE], ninf))
    v = jnp.concatenate(masked, axis=0)

    io_e = lax.broadcasted_iota(jnp.int32, (e, tr), 0).astype(F32)
    ids, ws = [], []
    sel = jnp.zeros((e, tr), F32)
    for _ in range(TOP_K):
        m = jnp.max(v, axis=0, keepdims=True)
        i = jnp.min(jnp.where(v == m, io_e, float(e)), axis=0, keepdims=True)
        oh = io_e == i
        ids.append(i)
        ws.append(jnp.sum(jnp.where(oh, sc, 0.0), axis=0, keepdims=True))
        v = jnp.where(oh, ninf, v)
        sel = sel + oh.astype(F32)

    wsum = ws[0]
    for k in range(1, TOP_K):
        wsum = wsum + ws[k]
    wgt_ref[...] = jnp.concatenate([w / wsum * ROUTED_SCALE for w in ws], axis=0)
    idx_ref[...] = jnp.concatenate(ids, axis=0).astype(jnp.int32)

    excl = jnp.dot(sel.astype(BF16), u_ref[...], preferred_element_type=F32)
    base = carry_ref[:, 0:1]
    rank_full = base + excl
    ranks = [jnp.sum(jnp.where(io_e == ids[k], rank_full, 0.0), axis=0, keepdims=True)
             for k in range(TOP_K)]
    rank_ref[...] = jnp.concatenate(ranks, axis=0).astype(jnp.int32)
    new = base + jnp.sum(sel, axis=1, keepdims=True)
    carry_ref[...] = jnp.broadcast_to(new, carry_ref.shape)
    cnt_ref[...] = jnp.broadcast_to(new, cnt_ref.shape).astype(jnp.int32)


def _router(h2p, wr_t_b, b_router_col, tr):
    b, s, d = h2p.shape
    t = b * s
    nt = s // tr
    u = jnp.asarray(np.triu(np.ones((tr, tr), np.float32), k=1), BF16)
    col = lambda i, j: (0, i * nt + j)
    return pl.pallas_call(
        functools.partial(_router_kernel, tr=tr),
        grid=(b, nt),
        in_specs=[pl.BlockSpec((1, tr, d), lambda i, j: (i, j, 0)),
                  pl.BlockSpec(wr_t_b.shape, lambda i, j: (0, 0)),
                  pl.BlockSpec((N_EXPERTS, 1), lambda i, j: (0, 0)),
                  pl.BlockSpec((tr, tr), lambda i, j: (0, 0))],
        out_specs=[pl.BlockSpec((TOP_K, tr), col), pl.BlockSpec((TOP_K, tr), col),
                   pl.BlockSpec((TOP_K, tr), col),
                   pl.BlockSpec((N_EXPERTS, 128), lambda i, j: (0, 0))],
        out_shape=[jax.ShapeDtypeStruct((TOP_K, t), jnp.int32),
                   jax.ShapeDtypeStruct((TOP_K, t), jnp.int32),
                   jax.ShapeDtypeStruct((TOP_K, t), F32),
                   jax.ShapeDtypeStruct((N_EXPERTS, 128), jnp.int32)],
        scratch_shapes=[pltpu.VMEM((N_EXPERTS, 128), F32)],
        compiler_params=_cparams(("arbitrary", "arbitrary")),
        name="router",
    )(h2p, wr_t_b, b_router_col, u)


def _dest_kernel(pstart_ref, idx_ref, rank_ref, dest_ref):
    idx = idx_ref[...]

    def body(g, acc):
        for j in range(8):
            e = g * 8 + j
            acc = jnp.where(idx == e, pstart_ref[e], acc)
        return acc

    dest_ref[...] = lax.fori_loop(0, N_EXPERTS // 8, body, jnp.zeros_like(idx)) + rank_ref[...]


def _dest(pstarts, idx, rank, tl):
    k, t = idx.shape
    grid_spec = pltpu.PrefetchScalarGridSpec(
        num_scalar_prefetch=1,
        grid=(t // tl,),
        in_specs=[pl.BlockSpec((k, tl), lambda i, ps: (0, i)), pl.BlockSpec((k, tl), lambda i, ps: (0, i))],
        out_specs=pl.BlockSpec((k, tl), lambda i, ps: (0, i)),
    )
    return pl.pallas_call(
        _dest_kernel,
        grid_spec=grid_spec,
        out_shape=jax.ShapeDtypeStruct((k, t), jnp.int32),
        compiler_params=_cparams(("parallel",)),
        name="dest",
    )(pstarts, idx, rank)


def _sc_mesh():
    return plsc.VectorSubcoreMesh(core_axis_name="c", subcore_axis_name="s",
                                  num_cores=SC_CORES, num_subcores=SC_SUBCORES)


def _sc_worker_base(per_worker):
    return (lax.axis_index("s") * SC_CORES + lax.axis_index("c")) * per_worker


def _dispatch(h2_flat, dest, p_rows):
    t, dh = h2_flat.shape
    r = SC_ROWS
    per_w = t // SC_WORKERS
    nchunk = per_w // r
    assert per_w % (2 * r) == 0

    @functools.partial(
        pl.kernel, mesh=_sc_mesh(),
        out_type=jax.ShapeDtypeStruct((p_rows, dh), h2_flat.dtype),
        scratch_types=[pltpu.VMEM((2, TOP_K, r), jnp.int32), pltpu.VMEM((2, r, dh), h2_flat.dtype),
                       pltpu.SemaphoreType.DMA((2,)), pltpu.SemaphoreType.DMA((2,))],
        name="sc_dispatch",
    )
    def k(rows_hbm, dest_hbm, out_hbm, idx_v, rows_v, lsem, ssem):
        base = _sc_worker_base(per_w)

        def load(ci, slot):
            t0 = base + ci * r
            for kk in range(TOP_K):
                pltpu.sync_copy(dest_hbm.at[kk, pl.ds(t0, r)], idx_v.at[slot, kk])
            pltpu.async_copy(rows_hbm.at[pl.ds(t0, r)], rows_v.at[slot], lsem.at[slot])

        def scatter(ci, slot):
            t0 = base + ci * r
            pltpu.make_async_copy(rows_hbm.at[pl.ds(t0, r)], rows_v.at[slot], lsem.at[slot]).wait()
            for kk in range(TOP_K):
                pltpu.async_copy(rows_v.at[slot], out_hbm.at[idx_v.at[slot, kk]], ssem.at[slot])

        def drain(slot):
            for kk in range(TOP_K):
                pltpu.make_async_copy(rows_v.at[slot], out_hbm.at[idx_v.at[slot, kk]], ssem.at[slot]).wait()

        load(0, 0)

        @pl.loop(0, nchunk, step=2)
        def _(c0):
            for s in range(2):
                ci = c0 + s

                @pl.when(ci + 1 < nchunk)
                def _():
                    @pl.when(ci >= 1)
                    def _():
                        drain(1 - s)
                    load(ci + 1, 1 - s)

                scatter(ci, s)

        drain(0)
        drain(1)

    return k(h2_flat, dest)


def _gather_sum(obuf, dest, wgt):
    _, dh = obuf.shape
    kk_n, t = dest.shape
    r, lanes = SC_SUM_TOKENS, SC_LANES
    assert r * lanes == 128
    per_w = t // SC_WORKERS
    nchunk = per_w // r
    idx_rows = per_w * kk_n // 128
    assert per_w % (2 * r) == 0 and idx_rows % 8 == 0
    nj = dh // lanes
    dest_c = dest.reshape(kk_n, t // r, r).transpose(1, 0, 2).reshape(t * kk_n // 128, 128)
    w_c = jnp.broadcast_to(wgt.reshape(kk_n, t // r, r).transpose(1, 0, 2)[..., None],
                           (t // r, kk_n, r, lanes)).reshape(t // r, kk_n, r * lanes)

    @functools.partial(
        pl.kernel, mesh=_sc_mesh(),
        out_type=jax.ShapeDtypeStruct((t, 2 * dh), F32),
        scratch_types=[pltpu.VMEM((idx_rows, 128), jnp.int32), pltpu.VMEM((2, kk_n, r, dh), obuf.dtype),
                       pltpu.VMEM((2, kk_n, r * lanes), F32), pltpu.VMEM((2, r, 2 * dh), F32),
                       pltpu.SemaphoreType.DMA((2,)), pltpu.SemaphoreType.DMA((2,))],
        compiler_params=pltpu.CompilerParams(needs_layout_passes=False),
        name="sc_gather_sum",
    )
    def k(table_hbm, idx_hbm, w_hbm, out_hbm, idx_v, rows_v, w_v, out_v, gsem, wsem):
        wid = lax.axis_index("s") * SC_CORES + lax.axis_index("c")
        base = wid * per_w
        cbase = wid * nchunk
        pltpu.sync_copy(idx_hbm.at[pl.ds(wid * idx_rows, idx_rows)], idx_v)

        def idx_list(ci, kk):
            off = (ci * kk_n + kk) * r
            return idx_v.at[off // 128, pl.ds(pl.multiple_of(off % 128, 8), r)]

        def copies(ci, slot):
            yield pltpu.make_async_copy(w_hbm.at[cbase + ci], w_v.at[slot], gsem.at[slot])
            for kk in range(kk_n):
                yield pltpu.make_async_copy(table_hbm.at[idx_list(ci, kk)], rows_v.at[slot, kk], gsem.at[slot])

        def out_copy(ci, slot):
            return pltpu.make_async_copy(out_v.at[slot], out_hbm.at[pl.ds(base + ci * r, r)], wsem.at[slot])

        def compute(slot):
            @pl.loop(0, r)
            def _(i):
                ws = [w_v[slot, kk, pl.ds(pl.multiple_of(i * lanes, lanes), lanes)] for kk in range(kk_n)]

                @plsc.parallel_loop(0, nj, unroll=4)
                def _(j):
                    col = pl.multiple_of(j * lanes, lanes)
                    lo = jnp.zeros((lanes,), F32)
                    hi = jnp.zeros((lanes,), F32)
                    for kk in range(kk_n):
                        v = rows_v[slot, kk, i, pl.ds(col, lanes)]
                        lo = lo + ws[kk] * plsc.bitcast(v << 16, F32)
                        hi = hi + ws[kk] * plsc.bitcast(v & jnp.uint32(0xFFFF0000), F32)
                    out_v[slot, i, pl.ds(col, lanes)] = lo
                    out_v[slot, i, pl.ds(dh + col, lanes)] = hi

        for c in copies(0, 0):
            c.start()

        @pl.loop(0, nchunk, step=2)
        def _(c0):
            for s in range(2):
                ci = c0 + s

                @pl.when(ci + 1 < nchunk)
                def _():
                    for c in copies(ci + 1, 1 - s):
                        c.start()

                for c in copies(ci, s):
                    c.wait()

                @pl.when(ci >= 2)
                def _():
                    out_copy(ci - 2, s).wait()

                compute(s)
                out_copy(ci, s).start()

        out_copy(nchunk - 2, 0).wait()
        out_copy(nchunk - 1, 1).wait()

    return k(obuf, dest_c, w_c)


def _experts_kernel(be_ref, nvalid_ref, run_ref, nxt_ref, nused_ref, *refs, bm):
    ns = EXPERT_STREAMS
    xq_refs = refs[:ns]
    wg_hbm, wu_hbm, wd_hbm, o_hbm, wg_f, wu_f, wd_f, wg_s, wu_s, wd_s, xs, ob, sem, osem = refs[ns:]
    b = pl.program_id(0)
    nb = pl.num_programs(0)
    nused = nused_ref[0]
    bq = bm // ns

    def weight_copies(e, slot):
        return (pltpu.make_async_copy(wg_hbm.at[e], wg_f.at[slot], sem.at[slot]),
                pltpu.make_async_copy(wu_hbm.at[e], wu_f.at[slot], sem.at[slot]),
                pltpu.make_async_copy(wd_hbm.at[e], wd_f.at[slot], sem.at[slot]))

    def out_copies(blk, slot):
        return [pltpu.make_async_copy(ob.at[slot, pl.ds(q * bq, bq)],
                                      o_hbm.at[pl.ds(pl.multiple_of(blk * bm + q * bq, 8), bq)], osem.at[slot])
                for q in range(ns)]

    @pl.when((b >= 2) & (b - 2 < nused))
    def _():
        for c in out_copies(b - 2, b % 2):
            c.wait()

    @pl.when(b < nused)
    def _():
        e = be_ref[b]
        slot = run_ref[b] % 2

        @pl.when(b == 0)
        def _():
            for c in weight_copies(e, slot):
                c.start()

        @pl.when((b == 0) | (e != be_ref[jnp.maximum(b - 1, 0)]))
        def _():
            for c in weight_copies(e, slot):
                c.wait()

            @pl.when(nxt_ref[b] >= 0)
            def _():
                for c in weight_copies(nxt_ref[b], 1 - slot):
                    c.start()

            wg_s[...] = wg_f[slot].astype(BF16)
            wu_s[...] = wu_f[slot].astype(BF16)
            wd_s[...] = wd_f[slot].astype(BF16)

        rows = lax.broadcasted_iota(jnp.int32, xq_refs[0].shape, 0)
        for q in range(ns):
            xp = jnp.where(rows + q * bq < nvalid_ref[b], xq_refs[q][...], jnp.uint32(0))
            xs[q * bq:(q + 1) * bq, :] = _unpack_bf16_pairs(xp).astype(BF16)
        x = xs[...]
        g = jnp.dot(x, wg_s[...], preferred_element_type=F32)
        u = jnp.dot(x, wu_s[...], preferred_element_type=F32)
        hid = (_silu(g) * u).astype(BF16)
        ob[b % 2] = _pack_bf16_pairs(jnp.dot(hid, wd_s[...], preferred_element_type=F32))
        for c in out_copies(b, b % 2):
            c.start()

    @pl.when(b == nb - 1)
    def _():
        @pl.when((b >= 1) & (b - 1 < nused))
        def _():
            for c in out_copies(b - 1, 1 - b % 2):
                c.wait()

        @pl.when(b < nused)
        def _():
            for c in out_copies(b, b % 2):
                c.wait()


def _experts(xbuf, block_e, nvalid, run, nxt, nused, w_gate, w_up, w_down, bm):
    p, dh = xbuf.shape
    nb = p // bm
    d, de = w_gate.shape[1:]
    ns = EXPERT_STREAMS
    bq = bm // ns
    hbm = pl.BlockSpec(memory_space=pl.ANY)

    def quarter(q):
        return pl.BlockSpec((bq, dh), lambda b, be, nv, rn, nx, nu: (jnp.minimum(b, nu[0] - 1) * ns + q, 0))

    grid_spec = pltpu.PrefetchScalarGridSpec(
        num_scalar_prefetch=5,
        grid=(nb,),
        in_specs=[quarter(q) for q in range(ns)] + [hbm, hbm, hbm],
        out_specs=hbm,
        scratch_shapes=[pltpu.VMEM((2, d, de), F32), pltpu.VMEM((2, d, de), F32), pltpu.VMEM((2, de, d), F32),
                        pltpu.VMEM((d, de), BF16), pltpu.VMEM((d, de), BF16), pltpu.VMEM((de, d), BF16),
                        pltpu.VMEM((bm, 2 * dh), BF16), pltpu.VMEM((2, bm, dh), jnp.uint32),
                        pltpu.SemaphoreType.DMA((2,)), pltpu.SemaphoreType.DMA((2,))],
    )
    return pl.pallas_call(
        functools.partial(_experts_kernel, bm=bm),
        grid_spec=grid_spec,
        out_shape=jax.ShapeDtypeStruct((p, dh), jnp.uint32),
        compiler_params=_cparams(("arbitrary",)),
        name="experts",
    )(block_e, nvalid, run, nxt, nused, *([xbuf] * ns), w_gate, w_up, w_down)


def _combine_kernel(y_ref, x1_ref, ysh_ref, mod_ref, g_ref, out_ref):
    y = y_ref[...] + _unpack_bf16_pairs(ysh_ref[...])
    out_ref[...] = x1_ref[...] + mod_ref[0, 5:6, :] * (_rms(y) * g_ref[...])


def _combine(y_routed, x1_flat, ysh_flat, mod3, g_post, s, tc):
    t, d = x1_flat.shape
    per_seq = s // tc
    tok = pl.BlockSpec((tc, d), lambda i: (i, 0))
    return pl.pallas_call(
        _combine_kernel,
        grid=(t // tc,),
        in_specs=[tok, tok, pl.BlockSpec((tc, d // 2), lambda i: (i, 0)),
                  pl.BlockSpec((1, 6, d), lambda i: (i // per_seq, 0, 0)),
                  pl.BlockSpec((1, d), lambda i: (0, 0))],
        out_specs=tok,
        out_shape=jax.ShapeDtypeStruct((t, d), F32),
        compiler_params=_cparams(("parallel",)),
        name="combine",
    )(y_routed, x1_flat, ysh_flat, mod3, g_post)


def _channel_dft_table():
    c = np.arange(GROUP_DIM)
    ang = 2.0 * np.pi * ((c[:, None] * c[None, :]) % GROUP_DIM) / GROUP_DIM
    eye = np.eye(D_FNET // GROUP_DIM)
    scale = 1.0 / np.sqrt(GROUP_DIM)
    cos_m, sin_m = np.kron(eye, np.cos(ang)) * scale, np.kron(eye, np.sin(ang)) * scale
    cols = [m[:, c * DFT_CH:(c + 1) * DFT_CH] for c in range(D_FNET // DFT_CH) for m in (cos_m, sin_m)]
    return jnp.asarray(np.concatenate(cols, axis=1), BF16)


EXPERT_BLOCK_MAX = 1280
EXPERT_BLOCK_ALIGN = 64


def _expert_block_rows(n_assign):
    target = max(n_assign // N_EXPERTS * 9 // 8, 2 * EXPERT_BLOCK_ALIGN)
    k = -(-target // EXPERT_BLOCK_MAX)
    return -(-target // (k * EXPERT_BLOCK_ALIGN)) * EXPERT_BLOCK_ALIGN


def _layer(x, mod, p):
    b, s, d = x.shape
    t = b * s
    mod3 = mod.reshape(b, 6, d)
    g_tab, h_tab = _dft_tables(s)

    v, zcs = _inproj(x, mod3, p["g_mix_pre"], p["w_in_b"], _channel_dft_table())
    cn = _conv(v, p["conv_w"], p["conv_b"], p["conv_ln_g"], p["conv_ln_b"], p["g_conv_out"], min(256, s))
    fy = _seqdft(zcs, g_tab, h_tab)
    x1, h2, ysh = _mix(x, cn, fy, mod3, p["g_fnet_out"], p["w_out_b"], p["g_mix_post"], p["g_ffn_pre"],
                       p["wsg_b"], p["wsu_b"], p["wsd_b"], min(256, s))
    idx, rank, wgt, cnt = _router(h2, p["wr_t_b"], p["b_router_col"], min(256, s))

    n = t * TOP_K
    bm = _expert_block_rows(n)
    counts = cnt[:, 0]
    pcounts = (counts + bm - 1) // bm * bm
    pends = jnp.cumsum(pcounts)
    pstarts = pends - pcounts
    dest = _dest(pstarts.astype(jnp.int32), idx, rank, min(2048, t))
    nb = (n + N_EXPERTS * (bm - 1) + bm - 1) // bm
    nused = (pends[-1] // bm).astype(jnp.int32)
    blk = jnp.minimum(jnp.arange(nb, dtype=jnp.int32), nused - 1) * bm
    block_e = jnp.sum((pends[None, :] <= blk[:, None]).astype(jnp.int32), axis=1)
    block_e = jnp.minimum(block_e, N_EXPERTS - 1)
    nvalid = jnp.clip(pstarts[block_e] + counts[block_e] - blk, 0, bm).astype(jnp.int32)
    first = jnp.concatenate([jnp.ones((1,), jnp.int32), (block_e[1:] != block_e[:-1]).astype(jnp.int32)])
    run = jnp.cumsum(first) - 1
    eid = jnp.arange(N_EXPERTS, dtype=jnp.int32)
    later = lax.cummin(jnp.where(pcounts > 0, eid, N_EXPERTS)[::-1])[::-1]
    nxt_e = jnp.concatenate([later[1:], jnp.full((1,), N_EXPERTS, jnp.int32)])
    nxt = jnp.where(nxt_e < N_EXPERTS, nxt_e, -1)[block_e].astype(jnp.int32)

    xbuf = _dispatch(h2.reshape(t, d // 2), dest, nb * bm)
    obuf = _experts(xbuf, block_e, nvalid, run.astype(jnp.int32), nxt, nused.reshape(1),
                    p["w_gate"], p["w_up"], p["w_down"], bm)
    y_routed = _gather_sum(obuf, dest, wgt)
    out = _combine(y_routed, x1.reshape(t, d), ysh.reshape(t, d // 2), mod3, p["g_ffn_post"], s, min(512, s))
    return out.reshape(b, s, d)


def kernel(x_prompt, x_sample, c_prompt, c_sample, w_ada, b_ada, g_mix_pre, w_in, conv_w, conv_b, conv_ln_g, conv_ln_b, g_conv_out, g_fnet_out, w_out, g_mix_post, g_ffn_pre, w_router, b_router, w_gate, w_up, w_down, ws_gate, ws_up, ws_down, g_ffn_post):
    assert w_ada.shape[0] == 1, "single-layer kernel"
    bp, bs = c_prompt.shape[0], c_sample.shape[0]
    rows = -(-(bp + bs) // 8) * 8
    c_all = jnp.zeros((rows, D_MODEL), F32).at[:bp].set(c_prompt).at[bp:bp + bs].set(c_sample)
    mod = _ada(c_all, w_ada[0], b_ada)
    p = {
        "g_mix_pre": g_mix_pre, "w_in_b": w_in[0].astype(BF16),
        "conv_w": conv_w[0], "conv_b": conv_b, "conv_ln_g": conv_ln_g, "conv_ln_b": conv_ln_b,
        "g_conv_out": g_conv_out, "g_fnet_out": g_fnet_out, "w_out_b": w_out[0].astype(BF16),
        "g_mix_post": g_mix_post, "g_ffn_pre": g_ffn_pre,
        "wr_t_b": w_router[0].T.astype(BF16), "b_router_col": b_router[0][:, None],
        "w_gate": w_gate[0], "w_up": w_up[0], "w_down": w_down[0],
        "wsg_b": ws_gate[0].astype(BF16), "wsu_b": ws_up[0].astype(BF16), "wsd_b": ws_down[0].astype(BF16),
        "g_ffn_post": g_ffn_post,
    }
    y_prompt = _layer(x_prompt, mod[:bp], p)
    y_sample = _layer(x_sample, mod[bp:bp + bs], p)
    return (y_prompt, y_sample)
```

```python
import functools

import numpy as np
import jax
import jax.numpy as jnp
from jax import lax
from jax.experimental import pallas as pl
from jax.experimental.pallas import tpu as pltpu
from jax.experimental.pallas import tpu_sc as plsc

F32 = jnp.float32
BF16 = jnp.bfloat16

D_MODEL = 1024
D_CONV = 512
D_FNET = 512
GROUP_DIM = 64
CONV_WIDTH = 31
N_EXPERTS = 256
TOP_K = 8
N_EXPERT_GROUPS = 8
GROUP_SIZE = N_EXPERTS // N_EXPERT_GROUPS
TOPK_GROUPS = 4
ROUTED_SCALE = 2.5
EPS = 1e-6

DFT_S1 = 128
DFT_CH = 128
HALO = 16
VMEM_LIMIT = 56 * 1024 * 1024
SC_CORES = 2
SC_SUBCORES = 16
SC_WORKERS = SC_CORES * SC_SUBCORES
SC_ROWS = 64
SC_LANES = 16
SC_SUM_TOKENS = 8
EXPERT_STREAMS = 4

def _cparams(sem, vmem=None):
    return pltpu.CompilerParams(dimension_semantics=sem, vmem_limit_bytes=vmem or VMEM_LIMIT)


def _rms(x):
    return x * lax.rsqrt(jnp.mean(x * x, axis=-1, keepdims=True) + EPS)


def _silu(x):
    return x * jax.nn.sigmoid(x)


def _pack_bf16_pairs(x):
    c = x.shape[-1] // 2
    bits = lax.bitcast_convert_type(x.astype(BF16).astype(F32), jnp.uint32)
    return (bits[:, :c] >> 16) | bits[:, c:]


def _unpack_bf16_pairs(p):
    lo = lax.bitcast_convert_type(p << 16, F32)
    hi = lax.bitcast_convert_type(p & jnp.uint32(0xFFFF0000), F32)
    return jnp.concatenate([lo, hi], axis=-1)


def _ada_kernel(c_ref, w_ref, b_ref, o_ref):
    o_ref[...] = jnp.dot(_silu(c_ref[...]), w_ref[...], preferred_element_type=F32) + b_ref[...]


def _ada(c, w_ada, b_ada):
    bp, d = c.shape
    n = w_ada.shape[1]
    return pl.pallas_call(
        _ada_kernel,
        grid=(n // d,),
        in_specs=[pl.BlockSpec((bp, d), lambda j: (0, 0)),
                  pl.BlockSpec((d, d), lambda j: (0, j)),
                  pl.BlockSpec((1, d), lambda j: (0, j))],
        out_specs=pl.BlockSpec((bp, d), lambda j: (0, j)),
        out_shape=jax.ShapeDtypeStruct((bp, n), F32),
        compiler_params=_cparams(("parallel",)),
        name="ada",
    )(c, w_ada, b_ada)


def _inproj_kernel(x_ref, mod_ref, g_ref, win_ref, cs_ref, perm_ref, v_ref, z_ref, *, s2, p1l):
    x = x_ref[0]
    h = _rms(x) * g_ref[...]
    h = h * (1.0 + mod_ref[0, 1:2, :]) + mod_ref[0, 0:1, :]
    u = jnp.dot(h.astype(BF16), win_ref[...], preferred_element_type=F32)
    a = u[:, :D_CONV]
    gt = u[:, D_CONV:2 * D_CONV]
    z = u[:, 2 * D_CONV:]
    v_ref[0] = a * jax.nn.sigmoid(gt)
    zp = jnp.dot(perm_ref[...], z.astype(BF16), preferred_element_type=F32).astype(BF16)
    zcs = jnp.dot(zp, cs_ref[...], preferred_element_type=F32).astype(BF16)
    z_ref[0] = zcs.reshape(s2, p1l, zcs.shape[-1])


def _inproj(x, mod3, g_mix_pre, w_in_b, cs_b):
    b, s, d = x.shape
    s2 = s // DFT_S1
    p1l = 16
    ts = p1l * s2
    r = np.arange(ts)
    perm = np.zeros((ts, ts), np.float32)
    perm[(r % s2) * p1l + r // s2, r] = 1.0
    perm = jnp.asarray(perm, BF16)
    return pl.pallas_call(
        functools.partial(_inproj_kernel, s2=s2, p1l=p1l),
        grid=(b, s // ts),
        in_specs=[pl.BlockSpec((1, ts, d), lambda i, t: (i, t, 0)),
                  pl.BlockSpec((1, 6, d), lambda i, t: (i, 0, 0)),
                  pl.BlockSpec((1, d), lambda i, t: (0, 0)),
                  pl.BlockSpec(w_in_b.shape, lambda i, t: (0, 0)),
                  pl.BlockSpec(cs_b.shape, lambda i, t: (0, 0)),
                  pl.BlockSpec((ts, ts), lambda i, t: (0, 0))],
        out_specs=[pl.BlockSpec((1, ts, D_CONV), lambda i, t: (i, t, 0)),
                   pl.BlockSpec((1, s2, p1l, 2 * D_FNET), lambda i, t: (i, 0, t, 0))],
        out_shape=[jax.ShapeDtypeStruct((b, s, D_CONV), F32),
                   jax.ShapeDtypeStruct((b, s2, DFT_S1, 2 * D_FNET), BF16)],
        compiler_params=_cparams(("parallel", "parallel")),
        name="inproj",
    )(x, mod3, g_mix_pre, w_in_b, cs_b, perm)


def _conv_kernel(vp_ref, v_ref, vn_ref, w_ref, b_ref, lg_ref, lb_ref, go_ref, o_ref, pad_ref, sh_ref, *, ts, rc):
    t = pl.program_id(1)
    nt = pl.num_programs(1)
    pad_ref[0:HALO, :] = jnp.where(t > 0, vp_ref[0], 0.0)
    pad_ref[HALO:HALO + ts, :] = v_ref[0]
    pad_ref[HALO + ts:HALO + ts + HALO, :] = jnp.where(t < nt - 1, vn_ref[0], 0.0)
    span = ts + 2 * HALO - 8
    for m in range(8):
        sh_ref[m] = pad_ref[m:m + span, :]
    off = HALO - CONV_WIDTH // 2
    for c in range(ts // rc):
        r0 = c * rc
        acc = jnp.zeros((rc // 8, 8, D_CONV), F32)
        for j in range(CONV_WIDTH):
            m, q = (off + j) % 8, (off + j) // 8
            tap = sh_ref[m, r0 + 8 * q:r0 + 8 * q + rc, :].reshape(rc // 8, 8, D_CONV)
            acc = acc + tap * w_ref[j][None]
        acc = acc.reshape(rc, D_CONV) + b_ref[...]
        mu = jnp.mean(acc, axis=-1, keepdims=True)
        xc = acc - mu
        var = jnp.mean(xc * xc, axis=-1, keepdims=True)
        y = xc * lax.rsqrt(var + EPS) * lg_ref[...] + lb_ref[...]
        y = _silu(y)
        y = _rms(y) * go_ref[...]
        o_ref[0, r0:r0 + rc, :] = y.astype(o_ref.dtype)


def _conv(v, conv_w, conv_b, ln_g, ln_b, g_out, ts, rc=32):
    b, s, c = v.shape
    hb = ts // HALO
    nh = s // HALO
    vec = pl.BlockSpec((1, c), lambda i, t: (0, 0))
    return pl.pallas_call(
        functools.partial(_conv_kernel, ts=ts, rc=rc),
        grid=(b, s // ts),
        in_specs=[pl.BlockSpec((1, HALO, c), lambda i, t: (i, jnp.maximum(t * hb - 1, 0), 0)),
                  pl.BlockSpec((1, ts, c), lambda i, t: (i, t, 0)),
                  pl.BlockSpec((1, HALO, c), lambda i, t: (i, jnp.minimum((t + 1) * hb, nh - 1), 0)),
                  pl.BlockSpec((CONV_WIDTH, 8, c), lambda i, t: (0, 0, 0)),
                  vec, vec, vec, vec],
        out_specs=pl.BlockSpec((1, ts, c), lambda i, t: (i, t, 0)),
        out_shape=jax.ShapeDtypeStruct((b, s, c), BF16),
        scratch_shapes=[pltpu.VMEM((ts + 2 * HALO, c), F32), pltpu.VMEM((8, ts + 2 * HALO - 8, c), F32)],
        compiler_params=_cparams(("parallel", "parallel")),
        name="conv",
    )(v, v, v, jnp.broadcast_to(conv_w[:, None, :], (CONV_WIDTH, 8, c)), conv_b, ln_g, ln_b, g_out)


def _dft_tables(s):
    s1 = DFT_S1
    s2 = s // s1
    k1 = np.arange(s1)[None, :, None]
    p1 = np.arange(s1)[None, None, :]
    p2 = np.arange(s2)[:, None, None]
    ang = 2.0 * np.pi * ((k1 * (s2 * p1 + p2)) % s) / s
    g = np.concatenate([np.cos(ang), np.sin(ang)], axis=1)
    k2 = np.arange(s2)[:, None]
    q2 = np.arange(s2)[None, :]
    ang2 = 2.0 * np.pi * ((k2 * q2) % s2) / s2
    h = np.concatenate([np.cos(ang2), np.sin(ang2)], axis=1) / np.sqrt(s)
    return jnp.asarray(g, BF16), jnp.asarray(h, BF16)


def _seqdft_kernel(z_ref, g_ref, h_ref, o_ref, scr_ref, *, s1, s2):
    ch = DFT_CH
    for p2 in range(s2):
        r = jnp.dot(g_ref[p2], z_ref[0, p2], preferred_element_type=F32)
        a_re = r[:s1, :ch] - r[s1:, ch:]
        a_im = -(r[:s1, ch:] + r[s1:, :ch])
        scr_ref[:, p2, :] = a_re
        scr_ref[:, s2 + p2, :] = a_im
    hmat = h_ref[...]
    for k1 in range(s1):
        y = jnp.dot(hmat, scr_ref[k1].astype(BF16), preferred_element_type=F32)
        o_ref[0, pl.ds(k1, s2, stride=s1), :] = y


def _seqdft(zp, g_tab, h_tab):
    b, s2, s1, _ = zp.shape
    s = s1 * s2
    nch = D_FNET // DFT_CH
    return pl.pallas_call(
        functools.partial(_seqdft_kernel, s1=s1, s2=s2),
        grid=(b, nch),
        in_specs=[pl.BlockSpec((1, s2, s1, 2 * DFT_CH), lambda i, c: (i, 0, 0, c)),
                  pl.BlockSpec(g_tab.shape, lambda i, c: (0, 0, 0)),
                  pl.BlockSpec(h_tab.shape, lambda i, c: (0, 0))],
        out_specs=pl.BlockSpec((1, s, DFT_CH), lambda i, c: (i, 0, c)),
        out_shape=jax.ShapeDtypeStruct((b, s, D_FNET), F32),
        scratch_shapes=[pltpu.VMEM((s1, 2 * s2, DFT_CH), F32)],
        compiler_params=_cparams(("parallel", "parallel")),
        name="seqdft",
    )(zp, g_tab, h_tab)


def _mix_kernel(x_ref, cn_ref, fy_ref, mod_ref, gf_ref, wout_ref, gpost_ref, gpre_ref,
                wsg_ref, wsu_ref, wsd_ref, x1_ref, h2_ref, ysh_ref):
    fn = _rms(fy_ref[0]) * gf_ref[...]
    mixed = jnp.dot(cn_ref[0], wout_ref[:D_CONV, :], preferred_element_type=F32)
    mixed = mixed + jnp.dot(fn.astype(BF16), wout_ref[D_CONV:, :], preferred_element_type=F32)
    x1 = x_ref[0] + mod_ref[0, 2:3, :] * (_rms(mixed) * gpost_ref[...])
    x1_ref[0] = x1
    h2 = _rms(x1) * gpre_ref[...]
    h2 = h2 * (1.0 + mod_ref[0, 4:5, :]) + mod_ref[0, 3:4, :]
    h2_ref[0] = _pack_bf16_pairs(h2)
    hb = h2.astype(BF16)
    hid = _silu(jnp.dot(hb, wsg_ref[...], preferred_element_type=F32))
    hid = hid * jnp.dot(hb, wsu_ref[...], preferred_element_type=F32)
    ysh_ref[0] = _pack_bf16_pairs(jnp.dot(hid.astype(BF16), wsd_ref[...], preferred_element_type=F32))


def _mix(x, cn, fy, mod3, g_fnet, w_out_b, g_post, g_pre, wsg_b, wsu_b, wsd_b, ts):
    b, s, d = x.shape
    tok = lambda c: pl.BlockSpec((1, ts, c), lambda i, t: (i, t, 0))
    full = lambda a: pl.BlockSpec(a.shape, lambda i, t: (0,) * a.ndim)
    return pl.pallas_call(
        _mix_kernel,
        grid=(b, s // ts),
        in_specs=[tok(d), tok(D_CONV), tok(D_FNET),
                  pl.BlockSpec((1, 6, d), lambda i, t: (i, 0, 0)),
                  full(g_fnet), full(w_out_b), full(g_post), full(g_pre),
                  full(wsg_b), full(wsu_b), full(wsd_b)],
        out_specs=[tok(d), tok(d // 2), tok(d // 2)],
        out_shape=[jax.ShapeDtypeStruct((b, s, d), F32),
                   jax.ShapeDtypeStruct((b, s, d // 2), jnp.uint32),
                   jax.ShapeDtypeStruct((b, s, d // 2), jnp.uint32)],
        compiler_params=_cparams(("parallel", "parallel")),
        name="mix",
    )(x, cn, fy, mod3, g_fnet, w_out_b, g_post, g_pre, wsg_b, wsu_b, wsd_b)


def _router_kernel(h_ref, wr_ref, br_ref, u_ref, idx_ref, rank_ref, wgt_ref, cnt_ref, carry_ref, *, tr):
    e = N_EXPERTS

    @pl.when((pl.program_id(0) == 0) & (pl.program_id(1) == 0))
    def _():
        carry_ref[...] = jnp.zeros_like(carry_ref)

    logits = lax.dot_general(wr_ref[...], _unpack_bf16_pairs(h_ref[0]).astype(BF16), (((1,), (1,)), ((), ())),
                             preferred_element_type=F32)
    sc = jax.nn.sigmoid(logits)
    sb = sc + br_ref[...]
    ninf = jnp.float32(-jnp.inf)

    io_g = lax.broadcasted_iota(jnp.int32, (GROUP_SIZE, tr), 0).astype(F32)
    gs = []
    for g in range(N_EXPERT_GROUPS):
        blk = sb[g * GROUP_SIZE:(g + 1) * GROUP_SIZE]
        m1 = jnp.max(blk, axis=0, keepdims=True)
        i1 = jnp.min(jnp.where(blk == m1, io_g, float(GROUP_SIZE)), axis=0, keepdims=True)
        m2 = jnp.max(jnp.where(io_g == i1, ninf, blk), axis=0, keepdims=True)
        gs.append(m1 + m2)
    masked = []
    for g in range(N_EXPERT_GROUPS):
        beat = jnp.zeros((1, tr), F32)
        for o in range(N_EXPERT_GROUPS):
            if o == g:
                continue
            wins = (gs[o] > gs[g]) | ((gs[o] == gs[g]) & (o < g))
            beat = beat + wins.astype(F32)
        keep = beat < float(TOPK_GROUPS)
        masked.append(jnp.where(keep, sb[g * GROUP_SIZE:(g + 1) * GROUP_SIZE], ninf))
    v = jnp.concatenate(masked, axis=0)

    io_e = lax.broadcasted_iota(jnp.int32, (e, tr), 0).astype(F32)
    ids, ws = [], []
    sel = jnp.zeros((e, tr), F32)
    for _ in range(TOP_K):
        m = jnp.max(v, axis=0, keepdims=True)
        i = jnp.min(jnp.where(v == m, io_e, float(e)), axis=0, keepdims=True)
        oh = io_e == i
        ids.append(i)
        ws.append(jnp.sum(jnp.where(oh, sc, 0.0), axis=0, keepdims=True))
        v = jnp.where(oh, ninf, v)
        sel = sel + oh.astype(F32)

    wsum = ws[0]
    for k in range(1, TOP_K):
        wsum = wsum + ws[k]
    wgt_ref[...] = jnp.concatenate([w / wsum * ROUTED_SCALE for w in ws], axis=0)
    idx_ref[...] = jnp.concatenate(ids, axis=0).astype(jnp.int32)

    excl = jnp.dot(sel.astype(BF16), u_ref[...], preferred_element_type=F32)
    base = carry_ref[:, 0:1]
    rank_full = base + excl
    ranks = [jnp.sum(jnp.where(io_e == ids[k], rank_full, 0.0), axis=0, keepdims=True)
             for k in range(TOP_K)]
    rank_ref[...] = jnp.concatenate(ranks, axis=0).astype(jnp.int32)
    new = base + jnp.sum(sel, axis=1, keepdims=True)
    carry_ref[...] = jnp.broadcast_to(new, carry_ref.shape)
    cnt_ref[...] = jnp.broadcast_to(new, cnt_ref.shape).astype(jnp.int32)


def _router(h2p, wr_t_b, b_router_col, tr):
    b, s, d = h2p.shape
    t = b * s
    nt = s // tr
    u = jnp.asarray(np.triu(np.ones((tr, tr), np.float32), k=1), BF16)
    col = lambda i, j: (0, i * nt + j)
    return pl.pallas_call(
        functools.partial(_router_kernel, tr=tr),
        grid=(b, nt),
        in_specs=[pl.BlockSpec((1, tr, d), lambda i, j: (i, j, 0)),
                  pl.BlockSpec(wr_t_b.shape, lambda i, j: (0, 0)),
                  pl.BlockSpec((N_EXPERTS, 1), lambda i, j: (0, 0)),
                  pl.BlockSpec((tr, tr), lambda i, j: (0, 0))],
        out_specs=[pl.BlockSpec((TOP_K, tr), col), pl.BlockSpec((TOP_K, tr), col),
                   pl.BlockSpec((TOP_K, tr), col),
                   pl.BlockSpec((N_EXPERTS, 128), lambda i, j: (0, 0))],
        out_shape=[jax.ShapeDtypeStruct((TOP_K, t), jnp.int32),
                   jax.ShapeDtypeStruct((TOP_K, t), jnp.int32),
                   jax.ShapeDtypeStruct((TOP_K, t), F32),
                   jax.ShapeDtypeStruct((N_EXPERTS, 128), jnp.int32)],
        scratch_shapes=[pltpu.VMEM((N_EXPERTS, 128), F32)],
        compiler_params=_cparams(("arbitrary", "arbitrary")),
        name="router",
    )(h2p, wr_t_b, b_router_col, u)


def _dest_kernel(pstart_ref, idx_ref, rank_ref, dest_ref):
    idx = idx_ref[...]

    def body(g, acc):
        for j in range(8):
            e = g * 8 + j
            acc = jnp.where(idx == e, pstart_ref[e], acc)
        return acc

    dest_ref[...] = lax.fori_loop(0, N_EXPERTS // 8, body, jnp.zeros_like(idx)) + rank_ref[...]


def _dest(pstarts, idx, rank, tl):
    k, t = idx.shape
    grid_spec = pltpu.PrefetchScalarGridSpec(
        num_scalar_prefetch=1,
        grid=(t // tl,),
        in_specs=[pl.BlockSpec((k, tl), lambda i, ps: (0, i)), pl.BlockSpec((k, tl), lambda i, ps: (0, i))],
        out_specs=pl.BlockSpec((k, tl), lambda i, ps: (0, i)),
    )
    return pl.pallas_call(
        _dest_kernel,
        grid_spec=grid_spec,
        out_shape=jax.ShapeDtypeStruct((k, t), jnp.int32),
        compiler_params=_cparams(("parallel",)),
        name="dest",
    )(pstarts, idx, rank)


def _sc_mesh():
    return plsc.VectorSubcoreMesh(core_axis_name="c", subcore_axis_name="s",
                                  num_cores=SC_CORES, num_subcores=SC_SUBCORES)


def _sc_worker_base(per_worker):
    return (lax.axis_index("s") * SC_CORES + lax.axis_index("c")) * per_worker


def _dispatch(h2_flat, dest, p_rows):
    t, dh = h2_flat.shape
    r = SC_ROWS
    per_w = t // SC_WORKERS
    nchunk = per_w // r
    assert per_w % (2 * r) == 0

    @functools.partial(
        pl.kernel, mesh=_sc_mesh(),
        out_type=jax.ShapeDtypeStruct((p_rows, dh), h2_flat.dtype),
        scratch_types=[pltpu.VMEM((2, TOP_K, r), jnp.int32), pltpu.VMEM((2, r, dh), h2_flat.dtype),
                       pltpu.SemaphoreType.DMA((2,)), pltpu.SemaphoreType.DMA((2,))],
        name="sc_dispatch",
    )
    def k(rows_hbm, dest_hbm, out_hbm, idx_v, rows_v, lsem, ssem):
        base = _sc_worker_base(per_w)

        def load(ci, slot):
            t0 = base + ci * r
            for kk in range(TOP_K):
                pltpu.sync_copy(dest_hbm.at[kk, pl.ds(t0, r)], idx_v.at[slot, kk])
            pltpu.async_copy(rows_hbm.at[pl.ds(t0, r)], rows_v.at[slot], lsem.at[slot])

        def scatter(ci, slot):
            t0 = base + ci * r
            pltpu.make_async_copy(rows_hbm.at[pl.ds(t0, r)], rows_v.at[slot], lsem.at[slot]).wait()
            for kk in range(TOP_K):
                pltpu.async_copy(rows_v.at[slot], out_hbm.at[idx_v.at[slot, kk]], ssem.at[slot])

        def drain(slot):
            for kk in range(TOP_K):
                pltpu.make_async_copy(rows_v.at[slot], out_hbm.at[idx_v.at[slot, kk]], ssem.at[slot]).wait()

        load(0, 0)

        @pl.loop(0, nchunk, step=2)
        def _(c0):
            for s in range(2):
                ci = c0 + s

                @pl.when(ci + 1 < nchunk)
                def _():
                    @pl.when(ci >= 1)
                    def _():
                        drain(1 - s)
                    load(ci + 1, 1 - s)

                scatter(ci, s)

        drain(0)
        drain(1)

    return k(h2_flat, dest)


def _gather_sum(obuf, dest, wgt):
    _, dh = obuf.shape
    kk_n, t = dest.shape
    r, lanes = SC_SUM_TOKENS, SC_LANES
    assert r * lanes == 128
    per_w = t // SC_WORKERS
    nchunk = per_w // r
    idx_rows = per_w * kk_n // 128
    assert per_w % (2 * r) == 0 and idx_rows % 8 == 0
    nj = dh // lanes
    dest_c = dest.reshape(kk_n, t // r, r).transpose(1, 0, 2).reshape(t * kk_n // 128, 128)
    w_c = jnp.broadcast_to(wgt.reshape(kk_n, t // r, r).transpose(1, 0, 2)[..., None],
                           (t // r, kk_n, r, lanes)).reshape(t // r, kk_n, r * lanes)

    @functools.partial(
        pl.kernel, mesh=_sc_mesh(),
        out_type=jax.ShapeDtypeStruct((t, 2 * dh), F32),
        scratch_types=[pltpu.VMEM((idx_rows, 128), jnp.int32), pltpu.VMEM((2, kk_n, r, dh), obuf.dtype),
                       pltpu.VMEM((2, kk_n, r * lanes), F32), pltpu.VMEM((2, r, 2 * dh), F32),
                       pltpu.SemaphoreType.DMA((2,)), pltpu.SemaphoreType.DMA((2,))],
        compiler_params=pltpu.CompilerParams(needs_layout_passes=False),
        name="sc_gather_sum",
    )
    def k(table_hbm, idx_hbm, w_hbm, out_hbm, idx_v, rows_v, w_v, out_v, gsem, wsem):
        wid = lax.axis_index("s") * SC_CORES + lax.axis_index("c")
        base = wid * per_w
        cbase = wid * nchunk
        pltpu.sync_copy(idx_hbm.at[pl.ds(wid * idx_rows, idx_rows)], idx_v)

        def idx_list(ci, kk):
            off = (ci * kk_n + kk) * r
            return idx_v.at[off // 128, pl.ds(pl.multiple_of(off % 128, 8), r)]

        def copies(ci, slot):
            yield pltpu.make_async_copy(w_hbm.at[cbase + ci], w_v.at[slot], gsem.at[slot])
            for kk in range(kk_n):
                yield pltpu.make_async_copy(table_hbm.at[idx_list(ci, kk)], rows_v.at[slot, kk], gsem.at[slot])

        def out_copy(ci, slot):
            return pltpu.make_async_copy(out_v.at[slot], out_hbm.at[pl.ds(base + ci * r, r)], wsem.at[slot])

        def compute(slot):
            @pl.loop(0, r)
            def _(i):
                ws = [w_v[slot, kk, pl.ds(pl.multiple_of(i * lanes, lanes), lanes)] for kk in range(kk_n)]

                @plsc.parallel_loop(0, nj, unroll=4)
                def _(j):
                    col = pl.multiple_of(j * lanes, lanes)
                    lo = jnp.zeros((lanes,), F32)
                    hi = jnp.zeros((lanes,), F32)
                    for kk in range(kk_n):
                        v = rows_v[slot, kk, i, pl.ds(col, lanes)]
                        lo = lo + ws[kk] * plsc.bitcast(v << 16, F32)
                        hi = hi + ws[kk] * plsc.bitcast(v & jnp.uint32(0xFFFF0000), F32)
                    out_v[slot, i, pl.ds(col, lanes)] = lo
                    out_v[slot, i, pl.ds(dh + col, lanes)] = hi

        for c in copies(0, 0):
            c.start()

        @pl.loop(0, nchunk, step=2)
        def _(c0):
            for s in range(2):
                ci = c0 + s

                @pl.when(ci + 1 < nchunk)
                def _():
                    for c in copies(ci + 1, 1 - s):
                        c.start()

                for c in copies(ci, s):
                    c.wait()

                @pl.when(ci >= 2)
                def _():
                    out_copy(ci - 2, s).wait()

                compute(s)
                out_copy(ci, s).start()

        out_copy(nchunk - 2, 0).wait()
        out_copy(nchunk - 1, 1).wait()

    return k(obuf, dest_c, w_c)


def _experts_kernel(be_ref, nvalid_ref, run_ref, nxt_ref, nused_ref, *refs, bm):
    ns = EXPERT_STREAMS
    xq_refs = refs[:ns]
    wg_hbm, wu_hbm, wd_hbm, o_hbm, wg_f, wu_f, wd_f, wg_s, wu_s, wd_s, xs, ob, sem, osem = refs[ns:]
    b = pl.program_id(0)
    nb = pl.num_programs(0)
    nused = nused_ref[0]
    bq = bm // ns

    def weight_copies(e, slot):
        copies = []
        for src, dst in ((wg_hbm, wg_f), (wu_hbm, wu_f), (wd_hbm, wd_f)):
            rows = src.shape[1] // ns
            for q in range(ns):
                copies.append(pltpu.make_async_copy(src.at[e, pl.ds(q * rows, rows)],
                                                    dst.at[slot, pl.ds(q * rows, rows)], sem.at[slot]))
        return copies

    def out_copies(blk, slot):
        return [pltpu.make_async_copy(ob.at[slot, pl.ds(q * bq, bq)],
                                      o_hbm.at[pl.ds(pl.multiple_of(blk * bm + q * bq, 8), bq)], osem.at[slot])
                for q in range(ns)]

    @pl.when((b >= 2) & (b - 2 < nused))
    def _():
        for c in out_copies(b - 2, b % 2):
            c.wait()

    @pl.when(b < nused)
    def _():
        e = be_ref[b]
        slot = run_ref[b] % 2

        @pl.when(b == 0)
        def _():
            for c in weight_copies(e, slot):
                c.start()

        @pl.when((b == 0) | (e != be_ref[jnp.maximum(b - 1, 0)]))
        def _():
            for c in weight_copies(e, slot):
                c.wait()

            @pl.when(nxt_ref[b] >= 0)
            def _():
                for c in weight_copies(nxt_ref[b], 1 - slot):
                    c.start()

            wg_s[...] = wg_f[slot].astype(BF16)
            wu_s[...] = wu_f[slot].astype(BF16)
            wd_s[...] = wd_f[slot].astype(BF16)

        rows = lax.broadcasted_iota(jnp.int32, xq_refs[0].shape, 0)
        for q in range(ns):
            xp = jnp.where(rows + q * bq < nvalid_ref[b], xq_refs[q][...], jnp.uint32(0))
            xs[q * bq:(q + 1) * bq, :] = _unpack_bf16_pairs(xp).astype(BF16)
        x = xs[...]
        g = jnp.dot(x, wg_s[...], preferred_element_type=F32)
        u = jnp.dot(x, wu_s[...], preferred_element_type=F32)
        hid = (_silu(g) * u).astype(BF16)
        ob[b % 2] = _pack_bf16_pairs(jnp.dot(hid, wd_s[...], preferred_element_type=F32))
        for c in out_copies(b, b % 2):
            c.start()

    @pl.when(b == nb - 1)
    def _():
        @pl.when((b >= 1) & (b - 1 < nused))
        def _():
            for c in out_copies(b - 1, 1 - b % 2):
                c.wait()

        @pl.when(b < nused)
        def _():
            for c in out_copies(b, b % 2):
                c.wait()


def _experts(xbuf, block_e, nvalid, run, nxt, nused, w_gate, w_up, w_down, bm):
    p, dh = xbuf.shape
    nb = p // bm
    d, de = w_gate.shape[1:]
    ns = EXPERT_STREAMS
    bq = bm // ns
    hbm = pl.BlockSpec(memory_space=pl.ANY)

    def quarter(q):
        return pl.BlockSpec((bq, dh), lambda b, be, nv, rn, nx, nu: (jnp.minimum(b, nu[0] - 1) * ns + q, 0))

    grid_spec = pltpu.PrefetchScalarGridSpec(
        num_scalar_prefetch=5,
        grid=(nb,),
        in_specs=[quarter(q) for q in range(ns)] + [hbm, hbm, hbm],
        out_specs=hbm,
        scratch_shapes=[pltpu.VMEM((2, d, de), F32), pltpu.VMEM((2, d, de), F32), pltpu.VMEM((2, de, d), F32),
                        pltpu.VMEM((d, de), BF16), pltpu.VMEM((d, de), BF16), pltpu.VMEM((de, d), BF16),
                        pltpu.VMEM((bm, 2 * dh), BF16), pltpu.VMEM((2, bm, dh), jnp.uint32),
                        pltpu.SemaphoreType.DMA((2,)), pltpu.SemaphoreType.DMA((2,))],
    )
    return pl.pallas_call(
        functools.partial(_experts_kernel, bm=bm),
        grid_spec=grid_spec,
        out_shape=jax.ShapeDtypeStruct((p, dh), jnp.uint32),
        compiler_params=_cparams(("arbitrary",)),
        name="experts",
    )(block_e, nvalid, run, nxt, nused, *([xbuf] * ns), w_gate, w_up, w_down)


def _combine_kernel(y_ref, x1_ref, ysh_ref, mod_ref, g_ref, out_ref):
    y = y_ref[...] + _unpack_bf16_pairs(ysh_ref[...])
    out_ref[...] = x1_ref[...] + mod_ref[0, 5:6, :] * (_rms(y) * g_ref[...])


def _combine(y_routed, x1_flat, ysh_flat, mod3, g_post, s, tc):
    t, d = x1_flat.shape
    per_seq = s // tc
    tok = pl.BlockSpec((tc, d), lambda i: (i, 0))
    return pl.pallas_call(
        _combine_kernel,
        grid=(t // tc,),
        in_specs=[tok, tok, pl.BlockSpec((tc, d // 2), lambda i: (i, 0)),
                  pl.BlockSpec((1, 6, d), lambda i: (i // per_seq, 0, 0)),
                  pl.BlockSpec((1, d), lambda i: (0, 0))],
        out_specs=tok,
        out_shape=jax.ShapeDtypeStruct((t, d), F32),
        compiler_params=_cparams(("parallel",)),
        name="combine",
    )(y_routed, x1_flat, ysh_flat, mod3, g_post)


def _channel_dft_table():
    c = np.arange(GROUP_DIM)
    ang = 2.0 * np.pi * ((c[:, None] * c[None, :]) % GROUP_DIM) / GROUP_DIM
    eye = np.eye(D_FNET // GROUP_DIM)
    scale = 1.0 / np.sqrt(GROUP_DIM)
    cos_m, sin_m = np.kron(eye, np.cos(ang)) * scale, np.kron(eye, np.sin(ang)) * scale
    cols = [m[:, c * DFT_CH:(c + 1) * DFT_CH] for c in range(D_FNET // DFT_CH) for m in (cos_m, sin_m)]
    return jnp.asarray(np.concatenate(cols, axis=1), BF16)


EXPERT_BLOCK_MAX = 1280
EXPERT_BLOCK_ALIGN = 64


def _expert_block_rows(n_assign):
    target = max(n_assign // N_EXPERTS * 9 // 8, 2 * EXPERT_BLOCK_ALIGN)
    k = -(-target // EXPERT_BLOCK_MAX)
    return -(-target // (k * EXPERT_BLOCK_ALIGN)) * EXPERT_BLOCK_ALIGN


def _layer(x, mod, p):
    b, s, d = x.shape
    t = b * s
    mod3 = mod.reshape(b, 6, d)
    g_tab, h_tab = _dft_tables(s)

    v, zcs = _inproj(x, mod3, p["g_mix_pre"], p["w_in_b"], _channel_dft_table())
    cn = _conv(v, p["conv_w"], p["conv_b"], p["conv_ln_g"], p["conv_ln_b"], p["g_conv_out"], min(256, s))
    fy = _seqdft(zcs, g_tab, h_tab)
    x1, h2, ysh = _mix(x, cn, fy, mod3, p["g_fnet_out"], p["w_out_b"], p["g_mix_post"], p["g_ffn_pre"],
                       p["wsg_b"], p["wsu_b"], p["wsd_b"], min(256, s))
    idx, rank, wgt, cnt = _router(h2, p["wr_t_b"], p["b_router_col"], min(256, s))

    n = t * TOP_K
    bm = _expert_block_rows(n)
    counts = cnt[:, 0]
    pcounts = (counts + bm - 1) // bm * bm
    pends = jnp.cumsum(pcounts)
    pstarts = pends - pcounts
    dest = _dest(pstarts.astype(jnp.int32), idx, rank, min(2048, t))
    nb = (n + N_EXPERTS * (bm - 1) + bm - 1) // bm
    nused = (pends[-1] // bm).astype(jnp.int32)
    blk = jnp.minimum(jnp.arange(nb, dtype=jnp.int32), nused - 1) * bm
    block_e = jnp.sum((pends[None, :] <= blk[:, None]).astype(jnp.int32), axis=1)
    block_e = jnp.minimum(block_e, N_EXPERTS - 1)
    nvalid = jnp.clip(pstarts[block_e] + counts[block_e] - blk, 0, bm).astype(jnp.int32)
    first = jnp.concatenate([jnp.ones((1,), jnp.int32), (block_e[1:] != block_e[:-1]).astype(jnp.int32)])
    run = jnp.cumsum(first) - 1
    eid = jnp.arange(N_EXPERTS, dtype=jnp.int32)
    later = lax.cummin(jnp.where(pcounts > 0, eid, N_EXPERTS)[::-1])[::-1]
    nxt_e = jnp.concatenate([later[1:], jnp.full((1,), N_EXPERTS, jnp.int32)])
    nxt = jnp.where(nxt_e < N_EXPERTS, nxt_e, -1)[block_e].astype(jnp.int32)

    xbuf = _dispatch(h2.reshape(t, d // 2), dest, nb * bm)
    obuf = _experts(xbuf, block_e, nvalid, run.astype(jnp.int32), nxt, nused.reshape(1),
                    p["w_gate"], p["w_up"], p["w_down"], bm)
    y_routed = _gather_sum(obuf, dest, wgt)
    out = _combine(y_routed, x1.reshape(t, d), ysh.reshape(t, d // 2), mod3, p["g_ffn_post"], s, min(512, s))
    return out.reshape(b, s, d)


def kernel(x_prompt, x_sample, c_prompt, c_sample, w_ada, b_ada, g_mix_pre, w_in, conv_w, conv_b, conv_ln_g, conv_ln_b, g_conv_out, g_fnet_out, w_out, g_mix_post, g_ffn_pre, w_router, b_router, w_gate, w_up, w_down, ws_gate, ws_up, ws_down, g_ffn_post):
    assert w_ada.shape[0] == 1, "single-layer kernel"
    bp, bs = c_prompt.shape[0], c_sample.shape[0]
    rows = -(-(bp + bs) // 8) * 8
    c_all = jnp.zeros((rows, D_MODEL), F32).at[:bp].set(c_prompt).at[bp:bp + bs].set(c_sample)
    mod = _ada(c_all, w_ada[0], b_ada)
    p = {
        "g_mix_pre": g_mix_pre, "w_in_b": w_in[0].astype(BF16),
        "conv_w": conv_w[0], "conv_b": conv_b, "conv_ln_g": conv_ln_g, "conv_ln_b": conv_ln_b,
        "g_conv_out": g_conv_out, "g_fnet_out": g_fnet_out, "w_out_b": w_out[0].astype(BF16),
        "g_mix_post": g_mix_post, "g_ffn_pre": g_ffn_pre,
        "wr_t_b": w_router[0].T.astype(BF16), "b_router_col": b_router[0][:, None],
        "w_gate": w_gate[0], "w_up": w_up[0], "w_down": w_down[0],
        "wsg_b": ws_gate[0].astype(BF16), "wsu_b": ws_up[0].astype(BF16), "wsd_b": ws_down[0].astype(BF16),
        "g_ffn_post": g_ffn_post,
    }
    y_prompt = _layer(x_prompt, mod[:bp], p)
    y_sample = _layer(x_sample, mod[bp:bp + bs], p)
    return (y_prompt, y_sample)
```

```python
import functools

import numpy as np
import jax
import jax.numpy as jnp
from jax import lax
from jax.experimental import pallas as pl
from jax.experimental.pallas import tpu as pltpu
from jax.experimental.pallas import tpu_sc as plsc

F32 = jnp.float32
BF16 = jnp.bfloat16

D_MODEL = 1024
D_CONV = 512
D_FNET = 512
GROUP_DIM = 64
CONV_WIDTH = 31
N_EXPERTS = 256
TOP_K = 8
N_EXPERT_GROUPS = 8
GROUP_SIZE = N_EXPERTS // N_EXPERT_GROUPS
TOPK_GROUPS = 4
ROUTED_SCALE = 2.5
EPS = 1e-6

DFT_S1 = 128
DFT_CH = 128
HALO = 16
VMEM_LIMIT = 56 * 1024 * 1024
TOKEN_TILE = 256
COMBINE_TILE = 512
DEST_TILE = 2048
SC_CORES = 2
SC_SUBCORES = 16
SC_WORKERS = SC_CORES * SC_SUBCORES
SC_ROWS = 64
SC_LANES = 16
SC_SUM_TOKENS = 8
EXPERT_STREAMS = 4

def _cparams(sem, vmem=None):
    return pltpu.CompilerParams(dimension_semantics=sem, vmem_limit_bytes=vmem or VMEM_LIMIT)


def _rms(x):
    return x * lax.rsqrt(jnp.mean(x * x, axis=-1, keepdims=True) + EPS)


def _silu(x):
    return x * jax.nn.sigmoid(x)


def _pack_bf16_pairs(x):
    c = x.shape[-1] // 2
    bits = lax.bitcast_convert_type(x.astype(BF16).astype(F32), jnp.uint32)
    return (bits[:, :c] >> 16) | bits[:, c:]


def _unpack_bf16_pairs(p):
    lo = lax.bitcast_convert_type(p << 16, F32)
    hi = lax.bitcast_convert_type(p & jnp.uint32(0xFFFF0000), F32)
    return jnp.concatenate([lo, hi], axis=-1)


def _ada_kernel(c_ref, w_ref, b_ref, o_ref):
    o_ref[...] = jnp.dot(_silu(c_ref[...]), w_ref[...], preferred_element_type=F32) + b_ref[...]


def _ada(c, w_ada, b_ada):
    bp, d = c.shape
    n = w_ada.shape[1]
    return pl.pallas_call(
        _ada_kernel,
        grid=(n // d,),
        in_specs=[pl.BlockSpec((bp, d), lambda j: (0, 0)),
                  pl.BlockSpec((d, d), lambda j: (0, j)),
                  pl.BlockSpec((1, d), lambda j: (0, j))],
        out_specs=pl.BlockSpec((bp, d), lambda j: (0, j)),
        out_shape=jax.ShapeDtypeStruct((bp, n), F32),
        compiler_params=_cparams(("parallel",)),
        name="ada",
    )(c, w_ada, b_ada)


def _inproj_kernel(x_ref, mod_ref, g_ref, win_ref, cs_ref, perm_ref, v_ref, z_ref, *, s2, p1l):
    x = x_ref[0]
    h = _rms(x) * g_ref[...]
    h = h * (1.0 + mod_ref[0, 1:2, :]) + mod_ref[0, 0:1, :]
    u = jnp.dot(h.astype(BF16), win_ref[...], preferred_element_type=F32)
    a = u[:, :D_CONV]
    gt = u[:, D_CONV:2 * D_CONV]
    z = u[:, 2 * D_CONV:]
    v_ref[0] = (a * jax.nn.sigmoid(gt)).astype(v_ref.dtype)
    zp = jnp.dot(perm_ref[...], z.astype(BF16), preferred_element_type=F32).astype(BF16)
    zcs = jnp.dot(zp, cs_ref[...], preferred_element_type=F32).astype(BF16)
    z_ref[0] = zcs.reshape(s2, p1l, zcs.shape[-1])


def _inproj(x, mod3, g_mix_pre, w_in_b, cs_b):
    b, s, d = x.shape
    s2 = s // DFT_S1
    p1l = 16
    ts = p1l * s2
    r = np.arange(ts)
    perm = np.zeros((ts, ts), np.float32)
    perm[(r % s2) * p1l + r // s2, r] = 1.0
    perm = jnp.asarray(perm, BF16)
    return pl.pallas_call(
        functools.partial(_inproj_kernel, s2=s2, p1l=p1l),
        grid=(b, s // ts),
        in_specs=[pl.BlockSpec((1, ts, d), lambda i, t: (i, t, 0)),
                  pl.BlockSpec((1, 6, d), lambda i, t: (i, 0, 0)),
                  pl.BlockSpec((1, d), lambda i, t: (0, 0)),
                  pl.BlockSpec(w_in_b.shape, lambda i, t: (0, 0)),
                  pl.BlockSpec(cs_b.shape, lambda i, t: (0, 0)),
                  pl.BlockSpec((ts, ts), lambda i, t: (0, 0))],
        out_specs=[pl.BlockSpec((1, ts, D_CONV), lambda i, t: (i, t, 0)),
                   pl.BlockSpec((1, s2, p1l, 2 * D_FNET), lambda i, t: (i, 0, t, 0))],
        out_shape=[jax.ShapeDtypeStruct((b, s, D_CONV), BF16),
                   jax.ShapeDtypeStruct((b, s2, DFT_S1, 2 * D_FNET), BF16)],
        compiler_params=_cparams(("parallel", "parallel")),
        name="inproj",
    )(x, mod3, g_mix_pre, w_in_b, cs_b, perm)


def _conv_kernel(vp_ref, v_ref, vn_ref, w_ref, b_ref, lg_ref, lb_ref, go_ref, o_ref, pad_ref, sh_ref, *, ts, rc):
    t = pl.program_id(1)
    nt = pl.num_programs(1)
    pad_ref[0:HALO, :] = jnp.where(t > 0, vp_ref[0].astype(F32), 0.0)
    pad_ref[HALO:HALO + ts, :] = v_ref[0].astype(F32)
    pad_ref[HALO + ts:HALO + ts + HALO, :] = jnp.where(t < nt - 1, vn_ref[0].astype(F32), 0.0)
    span = ts + 2 * HALO - 8
    for m in range(8):
        sh_ref[m] = pad_ref[m:m + span, :]
    off = HALO - CONV_WIDTH // 2
    for c in range(ts // rc):
        r0 = c * rc
        acc = jnp.zeros((rc // 8, 8, D_CONV), F32)
        for j in range(CONV_WIDTH):
            m, q = (off + j) % 8, (off + j) // 8
            tap = sh_ref[m, r0 + 8 * q:r0 + 8 * q + rc, :].reshape(rc // 8, 8, D_CONV)
            acc = acc + tap * w_ref[j][None]
        acc = acc.reshape(rc, D_CONV) + b_ref[...]
        mu = jnp.mean(acc, axis=-1, keepdims=True)
        xc = acc - mu
        var = jnp.mean(xc * xc, axis=-1, keepdims=True)
        y = xc * lax.rsqrt(var + EPS) * lg_ref[...] + lb_ref[...]
        y = _silu(y)
        y = _rms(y) * go_ref[...]
        o_ref[0, r0:r0 + rc, :] = y.astype(o_ref.dtype)


def _conv(v, conv_w, conv_b, ln_g, ln_b, g_out, ts, rc=32):
    b, s, c = v.shape
    hb = ts // HALO
    nh = s // HALO
    vec = pl.BlockSpec((1, c), lambda i, t: (0, 0))
    return pl.pallas_call(
        functools.partial(_conv_kernel, ts=ts, rc=rc),
        grid=(b, s // ts),
        in_specs=[pl.BlockSpec((1, HALO, c), lambda i, t: (i, jnp.maximum(t * hb - 1, 0), 0)),
                  pl.BlockSpec((1, ts, c), lambda i, t: (i, t, 0)),
                  pl.BlockSpec((1, HALO, c), lambda i, t: (i, jnp.minimum((t + 1) * hb, nh - 1), 0)),
                  pl.BlockSpec((CONV_WIDTH, 8, c), lambda i, t: (0, 0, 0)),
                  vec, vec, vec, vec],
        out_specs=pl.BlockSpec((1, ts, c), lambda i, t: (i, t, 0)),
        out_shape=jax.ShapeDtypeStruct((b, s, c), BF16),
        scratch_shapes=[pltpu.VMEM((ts + 2 * HALO, c), F32), pltpu.VMEM((8, ts + 2 * HALO - 8, c), F32)],
        compiler_params=_cparams(("parallel", "parallel")),
        name="conv",
    )(v, v, v, jnp.broadcast_to(conv_w[:, None, :], (CONV_WIDTH, 8, c)), conv_b, ln_g, ln_b, g_out)


def _dft_tables(s):
    s1 = DFT_S1
    s2 = s // s1
    k1 = np.arange(s1)[None, :, None]
    p1 = np.arange(s1)[None, None, :]
    p2 = np.arange(s2)[:, None, None]
    ang = 2.0 * np.pi * ((k1 * (s2 * p1 + p2)) % s) / s
    g = np.concatenate([np.cos(ang), np.sin(ang)], axis=1)
    k2 = np.arange(s2)[:, None]
    q2 = np.arange(s2)[None, :]
    ang2 = 2.0 * np.pi * ((k2 * q2) % s2) / s2
    h = np.concatenate([np.cos(ang2), np.sin(ang2)], axis=1) / np.sqrt(s)
    return jnp.asarray(g, BF16), jnp.asarray(h, BF16)


def _seqdft_kernel(z_ref, g_ref, h_ref, o_ref, scr_ref, *, s1, s2):
    ch = DFT_CH
    for p2 in range(s2):
        r = jnp.dot(g_ref[p2], z_ref[0, p2], preferred_element_type=F32)
        a_re = r[:s1, :ch] - r[s1:, ch:]
        a_im = -(r[:s1, ch:] + r[s1:, :ch])
        scr_ref[:, p2, :] = a_re
        scr_ref[:, s2 + p2, :] = a_im
    hmat = h_ref[...]
    for k1 in range(s1):
        y = jnp.dot(hmat, scr_ref[k1].astype(BF16), preferred_element_type=F32)
        o_ref[0, pl.ds(k1, s2, stride=s1), :] = y


def _seqdft(zp, g_tab, h_tab):
    b, s2, s1, _ = zp.shape
    s = s1 * s2
    nch = D_FNET // DFT_CH
    return pl.pallas_call(
        functools.partial(_seqdft_kernel, s1=s1, s2=s2),
        grid=(b, nch),
        in_specs=[pl.BlockSpec((1, s2, s1, 2 * DFT_CH), lambda i, c: (i, 0, 0, c)),
                  pl.BlockSpec(g_tab.shape, lambda i, c: (0, 0, 0)),
                  pl.BlockSpec(h_tab.shape, lambda i, c: (0, 0))],
        out_specs=pl.BlockSpec((1, s, DFT_CH), lambda i, c: (i, 0, c)),
        out_shape=jax.ShapeDtypeStruct((b, s, D_FNET), F32),
        scratch_shapes=[pltpu.VMEM((s1, 2 * s2, DFT_CH), F32)],
        compiler_params=_cparams(("parallel", "parallel")),
        name="seqdft",
    )(zp, g_tab, h_tab)


def _mix_kernel(x_ref, cn_ref, fy_ref, mod_ref, gf_ref, wout_ref, gpost_ref, gpre_ref,
                wsg_ref, wsu_ref, wsd_ref, x1_ref, h2_ref, ysh_ref):
    fn = _rms(fy_ref[0]) * gf_ref[...]
    mixed = jnp.dot(cn_ref[0], wout_ref[:D_CONV, :], preferred_element_type=F32)
    mixed = mixed + jnp.dot(fn.astype(BF16), wout_ref[D_CONV:, :], preferred_element_type=F32)
    x1 = x_ref[0] + mod_ref[0, 2:3, :] * (_rms(mixed) * gpost_ref[...])
    x1_ref[0] = x1
    h2 = _rms(x1) * gpre_ref[...]
    h2 = h2 * (1.0 + mod_ref[0, 4:5, :]) + mod_ref[0, 3:4, :]
    h2_ref[0] = _pack_bf16_pairs(h2)
    hb = h2.astype(BF16)
    hid = _silu(jnp.dot(hb, wsg_ref[...], preferred_element_type=F32))
    hid = hid * jnp.dot(hb, wsu_ref[...], preferred_element_type=F32)
    ysh_ref[0] = _pack_bf16_pairs(jnp.dot(hid.astype(BF16), wsd_ref[...], preferred_element_type=F32))


def _mix(x, cn, fy, mod3, g_fnet, w_out_b, g_post, g_pre, wsg_b, wsu_b, wsd_b, ts):
    b, s, d = x.shape
    tok = lambda c: pl.BlockSpec((1, ts, c), lambda i, t: (i, t, 0))
    full = lambda a: pl.BlockSpec(a.shape, lambda i, t: (0,) * a.ndim)
    return pl.pallas_call(
        _mix_kernel,
        grid=(b, s // ts),
        in_specs=[tok(d), tok(D_CONV), tok(D_FNET),
                  pl.BlockSpec((1, 6, d), lambda i, t: (i, 0, 0)),
                  full(g_fnet), full(w_out_b), full(g_post), full(g_pre),
                  full(wsg_b), full(wsu_b), full(wsd_b)],
        out_specs=[tok(d), tok(d // 2), tok(d // 2)],
        out_shape=[jax.ShapeDtypeStruct((b, s, d), F32),
                   jax.ShapeDtypeStruct((b, s, d // 2), jnp.uint32),
                   jax.ShapeDtypeStruct((b, s, d // 2), jnp.uint32)],
        compiler_params=_cparams(("parallel", "parallel")),
        name="mix",
    )(x, cn, fy, mod3, g_fnet, w_out_b, g_post, g_pre, wsg_b, wsu_b, wsd_b)


def _router_kernel(h_ref, wr_ref, br_ref, u_ref, idx_ref, rank_ref, wgt_ref, cnt_ref, carry_ref, *, tr):
    e = N_EXPERTS

    @pl.when((pl.program_id(0) == 0) & (pl.program_id(1) == 0))
    def _():
        carry_ref[...] = jnp.zeros_like(carry_ref)

    logits = lax.dot_general(wr_ref[...], _unpack_bf16_pairs(h_ref[0]).astype(BF16), (((1,), (1,)), ((), ())),
                             preferred_element_type=F32)
    sc = jax.nn.sigmoid(logits)
    sb = sc + br_ref[...]
    ninf = jnp.float32(-jnp.inf)

    io_g = lax.broadcasted_iota(jnp.int32, (GROUP_SIZE, tr), 0).astype(F32)
    gs = []
    for g in range(N_EXPERT_GROUPS):
        blk = sb[g * GROUP_SIZE:(g + 1) * GROUP_SIZE]
        m1 = jnp.max(blk, axis=0, keepdims=True)
        i1 = jnp.min(jnp.where(blk == m1, io_g, float(GROUP_SIZE)), axis=0, keepdims=True)
        m2 = jnp.max(jnp.where(io_g == i1, ninf, blk), axis=0, keepdims=True)
        gs.append(m1 + m2)
    masked = []
    for g in range(N_EXPERT_GROUPS):
        beat = jnp.zeros((1, tr), F32)
        for o in range(N_EXPERT_GROUPS):
            if o == g:
                continue
            wins = (gs[o] > gs[g]) | ((gs[o] == gs[g]) & (o < g))
            beat = beat + wins.astype(F32)
        keep = beat < float(TOPK_GROUPS)
        masked.append(jnp.where(keep, sb[g * GROUP_SIZE:(g + 1) * GROUP_SIZE], ninf))
    v = jnp.concatenate(masked, axis=0)

    io_e = lax.broadcasted_iota(jnp.int32, (e, tr), 0).astype(F32)
    ids, ws = [], []
    sel = jnp.zeros((e, tr), F32)
    for _ in range(TOP_K):
        m = jnp.max(v, axis=0, keepdims=True)
        i = jnp.min(jnp.where(v == m, io_e, float(e)), axis=0, keepdims=True)
        oh = io_e == i
        ids.append(i)
        ws.append(jnp.sum(jnp.where(oh, sc, 0.0), axis=0, keepdims=True))
        v = jnp.where(oh, ninf, v)
        sel = sel + oh.astype(F32)

    wsum = ws[0]
    for k in range(1, TOP_K):
        wsum = wsum + ws[k]
    wgt_ref[...] = jnp.concatenate([w / wsum * ROUTED_SCALE for w in ws], axis=0)
    idx_ref[...] = jnp.concatenate(ids, axis=0).astype(jnp.int32)

    excl = jnp.dot(sel.astype(BF16), u_ref[...], preferred_element_type=F32)
    base = carry_ref[:, 0:1]
    rank_full = base + excl
    ranks = [jnp.sum(jnp.where(io_e == ids[k], rank_full, 0.0), axis=0, keepdims=True)
             for k in range(TOP_K)]
    rank_ref[...] = jnp.concatenate(ranks, axis=0).astype(jnp.int32)
    new = base + jnp.sum(sel, axis=1, keepdims=True)
    carry_ref[...] = jnp.broadcast_to(new, carry_ref.shape)
    cnt_ref[...] = jnp.broadcast_to(new, cnt_ref.shape).astype(jnp.int32)


def _router(h2p, wr_t_b, b_router_col, tr):
    b, s, d = h2p.shape
    t = b * s
    nt = s // tr
    u = jnp.asarray(np.triu(np.ones((tr, tr), np.float32), k=1), BF16)
    col = lambda i, j: (0, i * nt + j)
    return pl.pallas_call(
        functools.partial(_router_kernel, tr=tr),
        grid=(b, nt),
        in_specs=[pl.BlockSpec((1, tr, d), lambda i, j: (i, j, 0)),
                  pl.BlockSpec(wr_t_b.shape, lambda i, j: (0, 0)),
                  pl.BlockSpec((N_EXPERTS, 1), lambda i, j: (0, 0)),
                  pl.BlockSpec((tr, tr), lambda i, j: (0, 0))],
        out_specs=[pl.BlockSpec((TOP_K, tr), col), pl.BlockSpec((TOP_K, tr), col),
                   pl.BlockSpec((TOP_K, tr), col),
                   pl.BlockSpec((N_EXPERTS, 128), lambda i, j: (0, 0))],
        out_shape=[jax.ShapeDtypeStruct((TOP_K, t), jnp.int32),
                   jax.ShapeDtypeStruct((TOP_K, t), jnp.int32),
                   jax.ShapeDtypeStruct((TOP_K, t), F32),
                   jax.ShapeDtypeStruct((N_EXPERTS, 128), jnp.int32)],
        scratch_shapes=[pltpu.VMEM((N_EXPERTS, 128), F32)],
        compiler_params=_cparams(("arbitrary", "arbitrary")),
        name="router",
    )(h2p, wr_t_b, b_router_col, u)


def _dest_kernel(pstart_ref, idx_ref, rank_ref, dest_ref):
    idx = idx_ref[...]

    def body(g, acc):
        for j in range(8):
            e = g * 8 + j
            acc = jnp.where(idx == e, pstart_ref[e], acc)
        return acc

    dest_ref[...] = lax.fori_loop(0, N_EXPERTS // 8, body, jnp.zeros_like(idx)) + rank_ref[...]


def _dest(pstarts, idx, rank, tl):
    k, t = idx.shape
    grid_spec = pltpu.PrefetchScalarGridSpec(
        num_scalar_prefetch=1,
        grid=(t // tl,),
        in_specs=[pl.BlockSpec((k, tl), lambda i, ps: (0, i)), pl.BlockSpec((k, tl), lambda i, ps: (0, i))],
        out_specs=pl.BlockSpec((k, tl), lambda i, ps: (0, i)),
    )
    return pl.pallas_call(
        _dest_kernel,
        grid_spec=grid_spec,
        out_shape=jax.ShapeDtypeStruct((k, t), jnp.int32),
        compiler_params=_cparams(("parallel",)),
        name="dest",
    )(pstarts, idx, rank)


def _sc_mesh():
    return plsc.VectorSubcoreMesh(core_axis_name="c", subcore_axis_name="s",
                                  num_cores=SC_CORES, num_subcores=SC_SUBCORES)


def _sc_worker_base(per_worker):
    return (lax.axis_index("s") * SC_CORES + lax.axis_index("c")) * per_worker


def _dispatch(h2_flat, dest, p_rows):
    t, dh = h2_flat.shape
    r = SC_ROWS
    per_w = t // SC_WORKERS
    nchunk = per_w // r
    assert per_w % (2 * r) == 0

    @functools.partial(
        pl.kernel, mesh=_sc_mesh(),
        out_type=jax.ShapeDtypeStruct((p_rows, dh), h2_flat.dtype),
        scratch_types=[pltpu.VMEM((2, TOP_K, r), jnp.int32), pltpu.VMEM((2, r, dh), h2_flat.dtype),
                       pltpu.SemaphoreType.DMA((2,)), pltpu.SemaphoreType.DMA((2,))],
        name="sc_dispatch",
    )
    def k(rows_hbm, dest_hbm, out_hbm, idx_v, rows_v, lsem, ssem):
        base = _sc_worker_base(per_w)

        def load(ci, slot):
            t0 = base + ci * r
            for kk in range(TOP_K):
                pltpu.sync_copy(dest_hbm.at[kk, pl.ds(t0, r)], idx_v.at[slot, kk])
            pltpu.async_copy(rows_hbm.at[pl.ds(t0, r)], rows_v.at[slot], lsem.at[slot])

        def scatter(ci, slot):
            t0 = base + ci * r
            pltpu.make_async_copy(rows_hbm.at[pl.ds(t0, r)], rows_v.at[slot], lsem.at[slot]).wait()
            for kk in range(TOP_K):
                pltpu.async_copy(rows_v.at[slot], out_hbm.at[idx_v.at[slot, kk]], ssem.at[slot])

        def drain(slot):
            for kk in range(TOP_K):
                pltpu.make_async_copy(rows_v.at[slot], out_hbm.at[idx_v.at[slot, kk]], ssem.at[slot]).wait()

        load(0, 0)

        @pl.loop(0, nchunk, step=2)
        def _(c0):
            for s in range(2):
                ci = c0 + s

                @pl.when(ci + 1 < nchunk)
                def _():
                    @pl.when(ci >= 1)
                    def _():
                        drain(1 - s)
                    load(ci + 1, 1 - s)

                scatter(ci, s)

        drain(0)
        drain(1)

    return k(h2_flat, dest)


def _gather_sum(obuf, dest, wgt):
    _, dh = obuf.shape
    kk_n, t = dest.shape
    r, lanes = SC_SUM_TOKENS, SC_LANES
    assert r * lanes == 128
    per_w = t // SC_WORKERS
    nchunk = per_w // r
    idx_rows = per_w * kk_n // 128
    assert per_w % (2 * r) == 0 and idx_rows % 8 == 0
    nj = dh // lanes
    dest_c = dest.reshape(kk_n, t // r, r).transpose(1, 0, 2).reshape(t * kk_n // 128, 128)
    w_c = jnp.broadcast_to(wgt.reshape(kk_n, t // r, r).transpose(1, 0, 2)[..., None],
                           (t // r, kk_n, r, lanes)).reshape(t // r, kk_n, r * lanes)

    @functools.partial(
        pl.kernel, mesh=_sc_mesh(),
        out_type=jax.ShapeDtypeStruct((t, 2 * dh), F32),
        scratch_types=[pltpu.VMEM((idx_rows, 128), jnp.int32), pltpu.VMEM((2, kk_n, r, dh), obuf.dtype),
                       pltpu.VMEM((2, kk_n, r * lanes), F32), pltpu.VMEM((2, r, 2 * dh), F32),
                       pltpu.SemaphoreType.DMA((2,)), pltpu.SemaphoreType.DMA((2,))],
        compiler_params=pltpu.CompilerParams(needs_layout_passes=False),
        name="sc_gather_sum",
    )
    def k(table_hbm, idx_hbm, w_hbm, out_hbm, idx_v, rows_v, w_v, out_v, gsem, wsem):
        wid = lax.axis_index("s") * SC_CORES + lax.axis_index("c")
        base = wid * per_w
        cbase = wid * nchunk
        pltpu.sync_copy(idx_hbm.at[pl.ds(wid * idx_rows, idx_rows)], idx_v)

        def idx_list(ci, kk):
            off = (ci * kk_n + kk) * r
            return idx_v.at[off // 128, pl.ds(pl.multiple_of(off % 128, 8), r)]

        def copies(ci, slot):
            yield pltpu.make_async_copy(w_hbm.at[cbase + ci], w_v.at[slot], gsem.at[slot])
            for kk in range(kk_n):
                yield pltpu.make_async_copy(table_hbm.at[idx_list(ci, kk)], rows_v.at[slot, kk], gsem.at[slot])

        def out_copy(ci, slot):
            return pltpu.make_async_copy(out_v.at[slot], out_hbm.at[pl.ds(base + ci * r, r)], wsem.at[slot])

        def compute(slot):
            @pl.loop(0, r)
            def _(i):
                ws = [w_v[slot, kk, pl.ds(pl.multiple_of(i * lanes, lanes), lanes)] for kk in range(kk_n)]

                @plsc.parallel_loop(0, nj, unroll=4)
                def _(j):
                    col = pl.multiple_of(j * lanes, lanes)
                    lo = jnp.zeros((lanes,), F32)
                    hi = jnp.zeros((lanes,), F32)
                    for kk in range(kk_n):
                        v = rows_v[slot, kk, i, pl.ds(col, lanes)]
                        lo = lo + ws[kk] * plsc.bitcast(v << 16, F32)
                        hi = hi + ws[kk] * plsc.bitcast(v & jnp.uint32(0xFFFF0000), F32)
                    out_v[slot, i, pl.ds(col, lanes)] = lo
                    out_v[slot, i, pl.ds(dh + col, lanes)] = hi

        for c in copies(0, 0):
            c.start()

        @pl.loop(0, nchunk, step=2)
        def _(c0):
            for s in range(2):
                ci = c0 + s

                @pl.when(ci + 1 < nchunk)
                def _():
                    for c in copies(ci + 1, 1 - s):
                        c.start()

                for c in copies(ci, s):
                    c.wait()

                @pl.when(ci >= 2)
                def _():
                    out_copy(ci - 2, s).wait()

                compute(s)
                out_copy(ci, s).start()

        out_copy(nchunk - 2, 0).wait()
        out_copy(nchunk - 1, 1).wait()

    return k(obuf, dest_c, w_c)


def _experts_kernel(be_ref, nvalid_ref, run_ref, nxt_ref, nused_ref, *refs, bm):
    ns = EXPERT_STREAMS
    xq_refs = refs[:ns]
    wg_hbm, wu_hbm, wd_hbm, o_hbm, wg_f, wu_f, wd_f, wg_s, wu_s, wd_s, xs, ob, sem, osem = refs[ns:]
    b = pl.program_id(0)
    nb = pl.num_programs(0)
    nused = nused_ref[0]
    bq = bm // ns

    def weight_copies(e, slot):
        return (pltpu.make_async_copy(wg_hbm.at[e], wg_f.at[slot], sem.at[slot]),
                pltpu.make_async_copy(wu_hbm.at[e], wu_f.at[slot], sem.at[slot]),
                pltpu.make_async_copy(wd_hbm.at[e], wd_f.at[slot], sem.at[slot]))

    def out_copies(blk, slot):
        return [pltpu.make_async_copy(ob.at[slot, pl.ds(q * bq, bq)],
                                      o_hbm.at[pl.ds(pl.multiple_of(blk * bm + q * bq, 8), bq)], osem.at[slot])
                for q in range(ns)]

    @pl.when((b >= 2) & (b - 2 < nused))
    def _():
        for c in out_copies(b - 2, b % 2):
            c.wait()

    @pl.when(b < nused)
    def _():
        e = be_ref[b]
        slot = run_ref[b] % 2

        @pl.when(b == 0)
        def _():
            for c in weight_copies(e, slot):
                c.start()

        @pl.when((b == 0) | (e != be_ref[jnp.maximum(b - 1, 0)]))
        def _():
            for c in weight_copies(e, slot):
                c.wait()

            @pl.when(nxt_ref[b] >= 0)
            def _():
                for c in weight_copies(nxt_ref[b], 1 - slot):
                    c.start()

            wg_s[...] = wg_f[slot].astype(BF16)
            wu_s[...] = wu_f[slot].astype(BF16)
            wd_s[...] = wd_f[slot].astype(BF16)

        rows = lax.broadcasted_iota(jnp.int32, xq_refs[0].shape, 0)
        for q in range(ns):
            xp = jnp.where(rows + q * bq < nvalid_ref[b], xq_refs[q][...], jnp.uint32(0))
            xs[q * bq:(q + 1) * bq, :] = _unpack_bf16_pairs(xp).astype(BF16)
        x = xs[...]
        g = jnp.dot(x, wg_s[...], preferred_element_type=F32)
        u = jnp.dot(x, wu_s[...], preferred_element_type=F32)
        hid = (_silu(g) * u).astype(BF16)
        ob[b % 2] = _pack_bf16_pairs(jnp.dot(hid, wd_s[...], preferred_element_type=F32))
        for c in out_copies(b, b % 2):
            c.start()

    @pl.when(b == nb - 1)
    def _():
        @pl.when((b >= 1) & (b - 1 < nused))
        def _():
            for c in out_copies(b - 1, 1 - b % 2):
                c.wait()

        @pl.when(b < nused)
        def _():
            for c in out_copies(b, b % 2):
                c.wait()


def _experts(xbuf, block_e, nvalid, run, nxt, nused, w_gate, w_up, w_down, bm):
    p, dh = xbuf.shape
    nb = p // bm
    d, de = w_gate.shape[1:]
    ns = EXPERT_STREAMS
    bq = bm // ns
    hbm = pl.BlockSpec(memory_space=pl.ANY)

    def quarter(q):
        return pl.BlockSpec((bq, dh), lambda b, be, nv, rn, nx, nu: (jnp.minimum(b, nu[0] - 1) * ns + q, 0))

    grid_spec = pltpu.PrefetchScalarGridSpec(
        num_scalar_prefetch=5,
        grid=(nused[0],),
        in_specs=[quarter(q) for q in range(ns)] + [hbm, hbm, hbm],
        out_specs=hbm,
        scratch_shapes=[pltpu.VMEM((2, d, de), F32), pltpu.VMEM((2, d, de), F32), pltpu.VMEM((2, de, d), F32),
                        pltpu.VMEM((d, de), BF16), pltpu.VMEM((d, de), BF16), pltpu.VMEM((de, d), BF16),
                        pltpu.VMEM((bm, 2 * dh), BF16), pltpu.VMEM((2, bm, dh), jnp.uint32),
                        pltpu.SemaphoreType.DMA((2,)), pltpu.SemaphoreType.DMA((2,))],
    )
    return pl.pallas_call(
        functools.partial(_experts_kernel, bm=bm),
        grid_spec=grid_spec,
        out_shape=jax.ShapeDtypeStruct((p, dh), jnp.uint32),
        compiler_params=_cparams(("arbitrary",)),
        name="experts",
    )(block_e, nvalid, run, nxt, nused, *([xbuf] * ns), w_gate, w_up, w_down)


def _combine_kernel(y_ref, x1_ref, ysh_ref, mod_ref, g_ref, out_ref):
    y = y_ref[...] + _unpack_bf16_pairs(ysh_ref[...])
    out_ref[...] = x1_ref[...] + mod_ref[0, 5:6, :] * (_rms(y) * g_ref[...])


def _combine(y_routed, x1_flat, ysh_flat, mod3, g_post, s, tc):
    t, d = x1_flat.shape
    per_seq = s // tc
    tok = pl.BlockSpec((tc, d), lambda i: (i, 0))
    return pl.pallas_call(
        _combine_kernel,
        grid=(t // tc,),
        in_specs=[tok, tok, pl.BlockSpec((tc, d // 2), lambda i: (i, 0)),
                  pl.BlockSpec((1, 6, d), lambda i: (i // per_seq, 0, 0)),
                  pl.BlockSpec((1, d), lambda i: (0, 0))],
        out_specs=tok,
        out_shape=jax.ShapeDtypeStruct((t, d), F32),
        compiler_params=_cparams(("parallel",)),
        name="combine",
    )(y_routed, x1_flat, ysh_flat, mod3, g_post)


def _channel_dft_table():
    c = np.arange(GROUP_DIM)
    ang = 2.0 * np.pi * ((c[:, None] * c[None, :]) % GROUP_DIM) / GROUP_DIM
    eye = np.eye(D_FNET // GROUP_DIM)
    scale = 1.0 / np.sqrt(GROUP_DIM)
    cos_m, sin_m = np.kron(eye, np.cos(ang)) * scale, np.kron(eye, np.sin(ang)) * scale
    cols = [m[:, c * DFT_CH:(c + 1) * DFT_CH] for c in range(D_FNET // DFT_CH) for m in (cos_m, sin_m)]
    return jnp.asarray(np.concatenate(cols, axis=1), BF16)


EXPERT_BLOCK_MAX = 1280
EXPERT_BLOCK_ALIGN = 64


def _expert_block_rows(n_assign):
    target = max(n_assign // N_EXPERTS * 9 // 8, 2 * EXPERT_BLOCK_ALIGN)
    k = -(-target // EXPERT_BLOCK_MAX)
    return -(-target // (k * EXPERT_BLOCK_ALIGN)) * EXPERT_BLOCK_ALIGN


def _layer(x, mod, p):
    b, s, d = x.shape
    t = b * s
    mod3 = mod.reshape(b, 6, d)
    g_tab, h_tab = _dft_tables(s)

    v, zcs = _inproj(x, mod3, p["g_mix_pre"], p["w_in_b"], _channel_dft_table())
    tile = min(TOKEN_TILE, s)
    cn = _conv(v, p["conv_w"], p["conv_b"], p["conv_ln_g"], p["conv_ln_b"], p["g_conv_out"], tile)
    fy = _seqdft(zcs, g_tab, h_tab)
    x1, h2, ysh = _mix(x, cn, fy, mod3, p["g_fnet_out"], p["w_out_b"], p["g_mix_post"], p["g_ffn_pre"],
                       p["wsg_b"], p["wsu_b"], p["wsd_b"], tile)
    idx, rank, wgt, cnt = _router(h2, p["wr_t_b"], p["b_router_col"], tile)

    n = t * TOP_K
    bm = _expert_block_rows(n)
    counts = cnt[:, 0]
    pcounts = (counts + bm - 1) // bm * bm
    pends = jnp.cumsum(pcounts)
    pstarts = pends - pcounts
    dest = _dest(pstarts.astype(jnp.int32), idx, rank, min(DEST_TILE, t))
    nb = (n + N_EXPERTS * (bm - 1) + bm - 1) // bm
    nused = (pends[-1] // bm).astype(jnp.int32)
    blk = jnp.minimum(jnp.arange(nb, dtype=jnp.int32), nused - 1) * bm
    block_e = jnp.sum((pends[None, :] <= blk[:, None]).astype(jnp.int32), axis=1)
    block_e = jnp.minimum(block_e, N_EXPERTS - 1)
    nvalid = jnp.clip(pstarts[block_e] + counts[block_e] - blk, 0, bm).astype(jnp.int32)
    first = jnp.concatenate([jnp.ones((1,), jnp.int32), (block_e[1:] != block_e[:-1]).astype(jnp.int32)])
    run = jnp.cumsum(first) - 1
    eid = jnp.arange(N_EXPERTS, dtype=jnp.int32)
    later = lax.cummin(jnp.where(pcounts > 0, eid, N_EXPERTS)[::-1])[::-1]
    nxt_e = jnp.concatenate([later[1:], jnp.full((1,), N_EXPERTS, jnp.int32)])
    nxt = jnp.where(nxt_e < N_EXPERTS, nxt_e, -1)[block_e].astype(jnp.int32)

    xbuf = _dispatch(h2.reshape(t, d // 2), dest, nb * bm)
    obuf = _experts(xbuf, block_e, nvalid, run.astype(jnp.int32), nxt, nused.reshape(1),
                    p["w_gate"], p["w_up"], p["w_down"], bm)
    y_routed = _gather_sum(obuf, dest, wgt)
    out = _combine(y_routed, x1.reshape(t, d), ysh.reshape(t, d // 2), mod3, p["g_ffn_post"], s,
                   min(COMBINE_TILE, s))
    return out.reshape(b, s, d)


def kernel(x_prompt, x_sample, c_prompt, c_sample, w_ada, b_ada, g_mix_pre, w_in, conv_w, conv_b, conv_ln_g, conv_ln_b, g_conv_out, g_fnet_out, w_out, g_mix_post, g_ffn_pre, w_router, b_router, w_gate, w_up, w_down, ws_gate, ws_up, ws_down, g_ffn_post):
    assert w_ada.shape[0] == 1, "single-layer kernel"
    bp, bs = c_prompt.shape[0], c_sample.shape[0]
    rows = -(-(bp + bs) // 8) * 8
    c_all = jnp.zeros((rows, D_MODEL), F32).at[:bp].set(c_prompt).at[bp:bp + bs].set(c_sample)
    mod = _ada(c_all, w_ada[0], b_ada)
    p = {
        "g_mix_pre": g_mix_pre, "w_in_b": w_in[0].astype(BF16),
        "conv_w": conv_w[0], "conv_b": conv_b, "conv_ln_g": conv_ln_g, "conv_ln_b": conv_ln_b,
        "g_conv_out": g_conv_out, "g_fnet_out": g_fnet_out, "w_out_b": w_out[0].astype(BF16),
        "g_mix_post": g_mix_post, "g_ffn_pre": g_ffn_pre,
        "wr_t_b": w_router[0].T.astype(BF16), "b_router_col": b_router[0][:, None],
        "w_gate": w_gate[0], "w_up": w_up[0], "w_down": w_down[0],
        "wsg_b": ws_gate[0].astype(BF16), "wsu_b": ws_up[0].astype(BF16), "wsd_b": ws_down[0].astype(BF16),
        "g_ffn_post": g_ffn_post,
    }
    y_prompt = _layer(x_prompt, mod[:bp], p)
    y_sample = _layer(x_sample, mod[bp:bp + bs], p)
    return (y_prompt, y_sample)
```

```python
import functools

import numpy as np
import jax
import jax.numpy as jnp
from jax import lax
from jax.experimental import pallas as pl
from jax.experimental.pallas import tpu as pltpu
from jax.experimental.pallas import tpu_sc as plsc

F32 = jnp.float32
BF16 = jnp.bfloat16

D_MODEL = 1024
D_CONV = 512
D_FNET = 512
GROUP_DIM = 64
CONV_WIDTH = 31
N_EXPERTS = 256
TOP_K = 8
N_EXPERT_GROUPS = 8
GROUP_SIZE = N_EXPERTS // N_EXPERT_GROUPS
TOPK_GROUPS = 4
ROUTED_SCALE = 2.5
EPS = 1e-6

DFT_S1 = 128
DFT_CH = 128
HALO = 16
VMEM_LIMIT = 56 * 1024 * 1024
TOKEN_TILE = 512
COMBINE_TILE = 512
DEST_TILE = 2048
SC_CORES = 2
SC_SUBCORES = 16
SC_WORKERS = SC_CORES * SC_SUBCORES
SC_ROWS = 64
SC_LANES = 16
SC_SUM_TOKENS = 8
EXPERT_STREAMS = 4

def _cparams(sem, vmem=None):
    return pltpu.CompilerParams(dimension_semantics=sem, vmem_limit_bytes=vmem or VMEM_LIMIT)


def _rms(x):
    return x * lax.rsqrt(jnp.mean(x * x, axis=-1, keepdims=True) + EPS)


def _silu(x):
    return x * jax.nn.sigmoid(x)


def _pack_bf16_pairs(x):
    c = x.shape[-1] // 2
    bits = lax.bitcast_convert_type(x.astype(BF16).astype(F32), jnp.uint32)
    return (bits[:, :c] >> 16) | bits[:, c:]


def _unpack_bf16_pairs(p):
    lo = lax.bitcast_convert_type(p << 16, F32)
    hi = lax.bitcast_convert_type(p & jnp.uint32(0xFFFF0000), F32)
    return jnp.concatenate([lo, hi], axis=-1)


def _ada_kernel(c_ref, w_ref, b_ref, o_ref):
    o_ref[...] = jnp.dot(_silu(c_ref[...]), w_ref[...], preferred_element_type=F32) + b_ref[...]


def _ada(c, w_ada, b_ada):
    bp, d = c.shape
    n = w_ada.shape[1]
    return pl.pallas_call(
        _ada_kernel,
        grid=(n // d,),
        in_specs=[pl.BlockSpec((bp, d), lambda j: (0, 0)),
                  pl.BlockSpec((d, d), lambda j: (0, j)),
                  pl.BlockSpec((1, d), lambda j: (0, j))],
        out_specs=pl.BlockSpec((bp, d), lambda j: (0, j)),
        out_shape=jax.ShapeDtypeStruct((bp, n), F32),
        compiler_params=_cparams(("parallel",)),
        name="ada",
    )(c, w_ada, b_ada)


def _inproj_kernel(x_ref, mod_ref, g_ref, win_ref, cs_ref, perm_ref, v_ref, z_ref, *, s2, p1l):
    x = x_ref[0]
    h = _rms(x) * g_ref[...]
    h = h * (1.0 + mod_ref[0, 1:2, :]) + mod_ref[0, 0:1, :]
    u = jnp.dot(h.astype(BF16), win_ref[...], preferred_element_type=F32)
    a = u[:, :D_CONV]
    gt = u[:, D_CONV:2 * D_CONV]
    z = u[:, 2 * D_CONV:]
    v_ref[0] = (a * jax.nn.sigmoid(gt)).astype(v_ref.dtype)
    zp = jnp.dot(perm_ref[...], z.astype(BF16), preferred_element_type=F32).astype(BF16)
    zcs = jnp.dot(zp, cs_ref[...], preferred_element_type=F32).astype(BF16)
    z_ref[0] = zcs.reshape(s2, p1l, zcs.shape[-1])


def _inproj(x, mod3, g_mix_pre, w_in_b, cs_b):
    b, s, d = x.shape
    s2 = s // DFT_S1
    p1l = 16
    ts = p1l * s2
    r = np.arange(ts)
    perm = np.zeros((ts, ts), np.float32)
    perm[(r % s2) * p1l + r // s2, r] = 1.0
    perm = jnp.asarray(perm, BF16)
    return pl.pallas_call(
        functools.partial(_inproj_kernel, s2=s2, p1l=p1l),
        grid=(b, s // ts),
        in_specs=[pl.BlockSpec((1, ts, d), lambda i, t: (i, t, 0)),
                  pl.BlockSpec((1, 6, d), lambda i, t: (i, 0, 0)),
                  pl.BlockSpec((1, d), lambda i, t: (0, 0)),
                  pl.BlockSpec(w_in_b.shape, lambda i, t: (0, 0)),
                  pl.BlockSpec(cs_b.shape, lambda i, t: (0, 0)),
                  pl.BlockSpec((ts, ts), lambda i, t: (0, 0))],
        out_specs=[pl.BlockSpec((1, ts, D_CONV), lambda i, t: (i, t, 0)),
                   pl.BlockSpec((1, s2, p1l, 2 * D_FNET), lambda i, t: (i, 0, t, 0))],
        out_shape=[jax.ShapeDtypeStruct((b, s, D_CONV), BF16),
                   jax.ShapeDtypeStruct((b, s2, DFT_S1, 2 * D_FNET), BF16)],
        compiler_params=_cparams(("parallel", "parallel")),
        name="inproj",
    )(x, mod3, g_mix_pre, w_in_b, cs_b, perm)


def _conv_kernel(vp_ref, v_ref, vn_ref, w_ref, b_ref, lg_ref, lb_ref, go_ref, o_ref, pad_ref, sh_ref, *, ts, rc):
    t = pl.program_id(1)
    nt = pl.num_programs(1)
    pad_ref[0:HALO, :] = jnp.where(t > 0, vp_ref[0].astype(F32), 0.0)
    pad_ref[HALO:HALO + ts, :] = v_ref[0].astype(F32)
    pad_ref[HALO + ts:HALO + ts + HALO, :] = jnp.where(t < nt - 1, vn_ref[0].astype(F32), 0.0)
    span = ts + 2 * HALO - 8
    for m in range(8):
        sh_ref[m] = pad_ref[m:m + span, :]
    off = HALO - CONV_WIDTH // 2
    for c in range(ts // rc):
        r0 = c * rc
        acc = jnp.zeros((rc // 8, 8, D_CONV), F32)
        for j in range(CONV_WIDTH):
            m, q = (off + j) % 8, (off + j) // 8
            tap = sh_ref[m, r0 + 8 * q:r0 + 8 * q + rc, :].reshape(rc // 8, 8, D_CONV)
            acc = acc + tap * w_ref[j][None]
        acc = acc.reshape(rc, D_CONV) + b_ref[...]
        mu = jnp.mean(acc, axis=-1, keepdims=True)
        xc = acc - mu
        var = jnp.mean(xc * xc, axis=-1, keepdims=True)
        y = xc * lax.rsqrt(var + EPS) * lg_ref[...] + lb_ref[...]
        y = _silu(y)
        y = _rms(y) * go_ref[...]
        o_ref[0, r0:r0 + rc, :] = y.astype(o_ref.dtype)


def _conv(v, conv_w, conv_b, ln_g, ln_b, g_out, ts, rc=32):
    b, s, c = v.shape
    hb = ts // HALO
    nh = s // HALO
    vec = pl.BlockSpec((1, c), lambda i, t: (0, 0))
    return pl.pallas_call(
        functools.partial(_conv_kernel, ts=ts, rc=rc),
        grid=(b, s // ts),
        in_specs=[pl.BlockSpec((1, HALO, c), lambda i, t: (i, jnp.maximum(t * hb - 1, 0), 0)),
                  pl.BlockSpec((1, ts, c), lambda i, t: (i, t, 0)),
                  pl.BlockSpec((1, HALO, c), lambda i, t: (i, jnp.minimum((t + 1) * hb, nh - 1), 0)),
                  pl.BlockSpec((CONV_WIDTH, 8, c), lambda i, t: (0, 0, 0)),
                  vec, vec, vec, vec],
        out_specs=pl.BlockSpec((1, ts, c), lambda i, t: (i, t, 0)),
        out_shape=jax.ShapeDtypeStruct((b, s, c), BF16),
        scratch_shapes=[pltpu.VMEM((ts + 2 * HALO, c), F32), pltpu.VMEM((8, ts + 2 * HALO - 8, c), F32)],
        compiler_params=_cparams(("parallel", "parallel")),
        name="conv",
    )(v, v, v, jnp.broadcast_to(conv_w[:, None, :], (CONV_WIDTH, 8, c)), conv_b, ln_g, ln_b, g_out)


def _dft_tables(s):
    s1 = DFT_S1
    s2 = s // s1
    k1 = np.arange(s1)[None, :, None]
    p1 = np.arange(s1)[None, None, :]
    p2 = np.arange(s2)[:, None, None]
    ang = 2.0 * np.pi * ((k1 * (s2 * p1 + p2)) % s) / s
    g = np.concatenate([np.cos(ang), np.sin(ang)], axis=1)
    k2 = np.arange(s2)[:, None]
    q2 = np.arange(s2)[None, :]
    ang2 = 2.0 * np.pi * ((k2 * q2) % s2) / s2
    h = np.concatenate([np.cos(ang2), np.sin(ang2)], axis=1) / np.sqrt(s)
    return jnp.asarray(g, BF16), jnp.asarray(h, BF16)


def _seqdft_kernel(z_ref, g_ref, h_ref, o_ref, scr_ref, *, s1, s2):
    ch = DFT_CH
    for p2 in range(s2):
        r = jnp.dot(g_ref[p2], z_ref[0, p2], preferred_element_type=F32)
        a_re = r[:s1, :ch] - r[s1:, ch:]
        a_im = -(r[:s1, ch:] + r[s1:, :ch])
        scr_ref[:, p2, :] = a_re
        scr_ref[:, s2 + p2, :] = a_im
    hmat = h_ref[...]
    for k1 in range(s1):
        y = jnp.dot(hmat, scr_ref[k1].astype(BF16), preferred_element_type=F32)
        o_ref[0, pl.ds(k1, s2, stride=s1), :] = y


def _seqdft(zp, g_tab, h_tab):
    b, s2, s1, _ = zp.shape
    s = s1 * s2
    nch = D_FNET // DFT_CH
    return pl.pallas_call(
        functools.partial(_seqdft_kernel, s1=s1, s2=s2),
        grid=(b, nch),
        in_specs=[pl.BlockSpec((1, s2, s1, 2 * DFT_CH), lambda i, c: (i, 0, 0, c)),
                  pl.BlockSpec(g_tab.shape, lambda i, c: (0, 0, 0)),
                  pl.BlockSpec(h_tab.shape, lambda i, c: (0, 0))],
        out_specs=pl.BlockSpec((1, s, DFT_CH), lambda i, c: (i, 0, c)),
        out_shape=jax.ShapeDtypeStruct((b, s, D_FNET), F32),
        scratch_shapes=[pltpu.VMEM((s1, 2 * s2, DFT_CH), F32)],
        compiler_params=_cparams(("parallel", "parallel")),
        name="seqdft",
    )(zp, g_tab, h_tab)


def _mix_kernel(x_ref, cn_ref, fy_ref, mod_ref, gf_ref, wout_ref, gpost_ref, gpre_ref,
                wsg_ref, wsu_ref, wsd_ref, x1_ref, h2_ref, ysh_ref):
    fn = _rms(fy_ref[0]) * gf_ref[...]
    mixed = jnp.dot(cn_ref[0], wout_ref[:D_CONV, :], preferred_element_type=F32)
    mixed = mixed + jnp.dot(fn.astype(BF16), wout_ref[D_CONV:, :], preferred_element_type=F32)
    x1 = x_ref[0] + mod_ref[0, 2:3, :] * (_rms(mixed) * gpost_ref[...])
    x1_ref[0] = x1
    h2 = _rms(x1) * gpre_ref[...]
    h2 = h2 * (1.0 + mod_ref[0, 4:5, :]) + mod_ref[0, 3:4, :]
    h2_ref[0] = _pack_bf16_pairs(h2)
    hb = h2.astype(BF16)
    hid = _silu(jnp.dot(hb, wsg_ref[...], preferred_element_type=F32))
    hid = hid * jnp.dot(hb, wsu_ref[...], preferred_element_type=F32)
    ysh_ref[0] = _pack_bf16_pairs(jnp.dot(hid.astype(BF16), wsd_ref[...], preferred_element_type=F32))


def _mix(x, cn, fy, mod3, g_fnet, w_out_b, g_post, g_pre, wsg_b, wsu_b, wsd_b, ts):
    b, s, d = x.shape
    tok = lambda c: pl.BlockSpec((1, ts, c), lambda i, t: (i, t, 0))
    full = lambda a: pl.BlockSpec(a.shape, lambda i, t: (0,) * a.ndim)
    return pl.pallas_call(
        _mix_kernel,
        grid=(b, s // ts),
        in_specs=[tok(d), tok(D_CONV), tok(D_FNET),
                  pl.BlockSpec((1, 6, d), lambda i, t: (i, 0, 0)),
                  full(g_fnet), full(w_out_b), full(g_post), full(g_pre),
                  full(wsg_b), full(wsu_b), full(wsd_b)],
        out_specs=[tok(d), tok(d // 2), tok(d // 2)],
        out_shape=[jax.ShapeDtypeStruct((b, s, d), F32),
                   jax.ShapeDtypeStruct((b, s, d // 2), jnp.uint32),
                   jax.ShapeDtypeStruct((b, s, d // 2), jnp.uint32)],
        compiler_params=_cparams(("parallel", "parallel")),
        name="mix",
    )(x, cn, fy, mod3, g_fnet, w_out_b, g_post, g_pre, wsg_b, wsu_b, wsd_b)


def _router_kernel(h_ref, wr_ref, br_ref, u_ref, idx_ref, rank_ref, wgt_ref, cnt_ref, carry_ref, *, tr):
    e = N_EXPERTS

    @pl.when((pl.program_id(0) == 0) & (pl.program_id(1) == 0))
    def _():
        carry_ref[...] = jnp.zeros_like(carry_ref)

    logits = lax.dot_general(wr_ref[...], _unpack_bf16_pairs(h_ref[0]).astype(BF16), (((1,), (1,)), ((), ())),
                             preferred_element_type=F32)
    sc = jax.nn.sigmoid(logits)
    sb = sc + br_ref[...]
    ninf = jnp.float32(-jnp.inf)

    io_g = lax.broadcasted_iota(jnp.int32, (GROUP_SIZE, tr), 0).astype(F32)
    gs = []
    for g in range(N_EXPERT_GROUPS):
        blk = sb[g * GROUP_SIZE:(g + 1) * GROUP_SIZE]
        m1 = jnp.max(blk, axis=0, keepdims=True)
        i1 = jnp.min(jnp.where(blk == m1, io_g, float(GROUP_SIZE)), axis=0, keepdims=True)
        m2 = jnp.max(jnp.where(io_g == i1, ninf, blk), axis=0, keepdims=True)
        gs.append(m1 + m2)
    masked = []
    for g in range(N_EXPERT_GROUPS):
        beat = jnp.zeros((1, tr), F32)
        for o in range(N_EXPERT_GROUPS):
            if o == g:
                continue
            wins = (gs[o] > gs[g]) | ((gs[o] == gs[g]) & (o < g))
            beat = beat + wins.astype(F32)
        keep = beat < float(TOPK_GROUPS)
        masked.append(jnp.where(keep, sb[g * GROUP_SIZE:(g + 1) * GROUP_SIZE], ninf))
    v = jnp.concatenate(masked, axis=0)

    io_e = lax.broadcasted_iota(jnp.int32, (e, tr), 0).astype(F32)
    ids, ws = [], []
    sel = jnp.zeros((e, tr), F32)
    for _ in range(TOP_K):
        m = jnp.max(v, axis=0, keepdims=True)
        i = jnp.min(jnp.where(v == m, io_e, float(e)), axis=0, keepdims=True)
        oh = io_e == i
        ids.append(i)
        ws.append(jnp.sum(jnp.where(oh, sc, 0.0), axis=0, keepdims=True))
        v = jnp.where(oh, ninf, v)
        sel = sel + oh.astype(F32)

    wsum = ws[0]
    for k in range(1, TOP_K):
        wsum = wsum + ws[k]
    wgt_ref[...] = jnp.concatenate([w / wsum * ROUTED_SCALE for w in ws], axis=0)
    idx_ref[...] = jnp.concatenate(ids, axis=0).astype(jnp.int32)

    excl = jnp.dot(sel.astype(BF16), u_ref[...], preferred_element_type=F32)
    base = carry_ref[:, 0:1]
    rank_full = base + excl
    ranks = [jnp.sum(jnp.where(io_e == ids[k], rank_full, 0.0), axis=0, keepdims=True)
             for k in range(TOP_K)]
    rank_ref[...] = jnp.concatenate(ranks, axis=0).astype(jnp.int32)
    new = base + jnp.sum(sel, axis=1, keepdims=True)
    carry_ref[...] = jnp.broadcast_to(new, carry_ref.shape)
    cnt_ref[...] = jnp.broadcast_to(new, cnt_ref.shape).astype(jnp.int32)


def _router(h2p, wr_t_b, b_router_col, tr):
    b, s, d = h2p.shape
    t = b * s
    nt = s // tr
    u = jnp.asarray(np.triu(np.ones((tr, tr), np.float32), k=1), BF16)
    col = lambda i, j: (0, i * nt + j)
    return pl.pallas_call(
        functools.partial(_router_kernel, tr=tr),
        grid=(b, nt),
        in_specs=[pl.BlockSpec((1, tr, d), lambda i, j: (i, j, 0)),
                  pl.BlockSpec(wr_t_b.shape, lambda i, j: (0, 0)),
                  pl.BlockSpec((N_EXPERTS, 1), lambda i, j: (0, 0)),
                  pl.BlockSpec((tr, tr), lambda i, j: (0, 0))],
        out_specs=[pl.BlockSpec((TOP_K, tr), col), pl.BlockSpec((TOP_K, tr), col),
                   pl.BlockSpec((TOP_K, tr), col),
                   pl.BlockSpec((N_EXPERTS, 128), lambda i, j: (0, 0))],
        out_shape=[jax.ShapeDtypeStruct((TOP_K, t), jnp.int32),
                   jax.ShapeDtypeStruct((TOP_K, t), jnp.int32),
                   jax.ShapeDtypeStruct((TOP_K, t), F32),
                   jax.ShapeDtypeStruct((N_EXPERTS, 128), jnp.int32)],
        scratch_shapes=[pltpu.VMEM((N_EXPERTS, 128), F32)],
        compiler_params=_cparams(("arbitrary", "arbitrary")),
        name="router",
    )(h2p, wr_t_b, b_router_col, u)


def _dest_kernel(pstart_ref, idx_ref, rank_ref, dest_ref):
    idx = idx_ref[...]

    def body(g, acc):
        for j in range(8):
            e = g * 8 + j
            acc = jnp.where(idx == e, pstart_ref[e], acc)
        return acc

    dest_ref[...] = lax.fori_loop(0, N_EXPERTS // 8, body, jnp.zeros_like(idx)) + rank_ref[...]


def _dest(pstarts, idx, rank, tl):
    k, t = idx.shape
    grid_spec = pltpu.PrefetchScalarGridSpec(
        num_scalar_prefetch=1,
        grid=(t // tl,),
        in_specs=[pl.BlockSpec((k, tl), lambda i, ps: (0, i)), pl.BlockSpec((k, tl), lambda i, ps: (0, i))],
        out_specs=pl.BlockSpec((k, tl), lambda i, ps: (0, i)),
    )
    return pl.pallas_call(
        _dest_kernel,
        grid_spec=grid_spec,
        out_shape=jax.ShapeDtypeStruct((k, t), jnp.int32),
        compiler_params=_cparams(("parallel",)),
        name="dest",
    )(pstarts, idx, rank)


def _sc_mesh():
    return plsc.VectorSubcoreMesh(core_axis_name="c", subcore_axis_name="s",
                                  num_cores=SC_CORES, num_subcores=SC_SUBCORES)


def _sc_worker_base(per_worker):
    return (lax.axis_index("s") * SC_CORES + lax.axis_index("c")) * per_worker


def _dispatch(h2_flat, dest, p_rows):
    t, dh = h2_flat.shape
    r = SC_ROWS
    per_w = t // SC_WORKERS
    nchunk = per_w // r
    assert per_w % (2 * r) == 0

    @functools.partial(
        pl.kernel, mesh=_sc_mesh(),
        out_type=jax.ShapeDtypeStruct((p_rows, dh), h2_flat.dtype),
        scratch_types=[pltpu.VMEM((2, TOP_K, r), jnp.int32), pltpu.VMEM((2, r, dh), h2_flat.dtype),
                       pltpu.SemaphoreType.DMA((2,)), pltpu.SemaphoreType.DMA((2,))],
        name="sc_dispatch",
    )
    def k(rows_hbm, dest_hbm, out_hbm, idx_v, rows_v, lsem, ssem):
        base = _sc_worker_base(per_w)

        def load(ci, slot):
            t0 = base + ci * r
            for kk in range(TOP_K):
                pltpu.sync_copy(dest_hbm.at[kk, pl.ds(t0, r)], idx_v.at[slot, kk])
            pltpu.async_copy(rows_hbm.at[pl.ds(t0, r)], rows_v.at[slot], lsem.at[slot])

        def scatter(ci, slot):
            t0 = base + ci * r
            pltpu.make_async_copy(rows_hbm.at[pl.ds(t0, r)], rows_v.at[slot], lsem.at[slot]).wait()
            for kk in range(TOP_K):
                pltpu.async_copy(rows_v.at[slot], out_hbm.at[idx_v.at[slot, kk]], ssem.at[slot])

        def drain(slot):
            for kk in range(TOP_K):
                pltpu.make_async_copy(rows_v.at[slot], out_hbm.at[idx_v.at[slot, kk]], ssem.at[slot]).wait()

        load(0, 0)

        @pl.loop(0, nchunk, step=2)
        def _(c0):
            for s in range(2):
                ci = c0 + s

                @pl.when(ci + 1 < nchunk)
                def _():
                    @pl.when(ci >= 1)
                    def _():
                        drain(1 - s)
                    load(ci + 1, 1 - s)

                scatter(ci, s)

        drain(0)
        drain(1)

    return k(h2_flat, dest)


def _gather_sum(obuf, dest, wgt):
    _, dh = obuf.shape
    kk_n, t = dest.shape
    r, lanes = SC_SUM_TOKENS, SC_LANES
    assert r * lanes == 128
    per_w = t // SC_WORKERS
    nchunk = per_w // r
    idx_rows = per_w * kk_n // 128
    assert per_w % (2 * r) == 0 and idx_rows % 8 == 0
    nj = dh // lanes
    dest_c = dest.reshape(kk_n, t // r, r).transpose(1, 0, 2).reshape(t * kk_n // 128, 128)
    w_c = jnp.broadcast_to(wgt.reshape(kk_n, t // r, r).transpose(1, 0, 2)[..., None],
                           (t // r, kk_n, r, lanes)).reshape(t // r, kk_n, r * lanes)

    @functools.partial(
        pl.kernel, mesh=_sc_mesh(),
        out_type=jax.ShapeDtypeStruct((t, 2 * dh), F32),
        scratch_types=[pltpu.VMEM((idx_rows, 128), jnp.int32), pltpu.VMEM((2, kk_n, r, dh), obuf.dtype),
                       pltpu.VMEM((2, kk_n, r * lanes), F32), pltpu.VMEM((2, r, 2 * dh), F32),
                       pltpu.SemaphoreType.DMA((2,)), pltpu.SemaphoreType.DMA((2,))],
        compiler_params=pltpu.CompilerParams(needs_layout_passes=False),
        name="sc_gather_sum",
    )
    def k(table_hbm, idx_hbm, w_hbm, out_hbm, idx_v, rows_v, w_v, out_v, gsem, wsem):
        wid = lax.axis_index("s") * SC_CORES + lax.axis_index("c")
        base = wid * per_w
        cbase = wid * nchunk
        pltpu.sync_copy(idx_hbm.at[pl.ds(wid * idx_rows, idx_rows)], idx_v)

        def idx_list(ci, kk):
            off = (ci * kk_n + kk) * r
            return idx_v.at[off // 128, pl.ds(pl.multiple_of(off % 128, 8), r)]

        def copies(ci, slot):
            yield pltpu.make_async_copy(w_hbm.at[cbase + ci], w_v.at[slot], gsem.at[slot])
            for kk in range(kk_n):
                yield pltpu.make_async_copy(table_hbm.at[idx_list(ci, kk)], rows_v.at[slot, kk], gsem.at[slot])

        def out_copy(ci, slot):
            return pltpu.make_async_copy(out_v.at[slot], out_hbm.at[pl.ds(base + ci * r, r)], wsem.at[slot])

        def compute(slot):
            @pl.loop(0, r)
            def _(i):
                ws = [w_v[slot, kk, pl.ds(pl.multiple_of(i * lanes, lanes), lanes)] for kk in range(kk_n)]

                @plsc.parallel_loop(0, nj, unroll=4)
                def _(j):
                    col = pl.multiple_of(j * lanes, lanes)
                    lo = jnp.zeros((lanes,), F32)
                    hi = jnp.zeros((lanes,), F32)
                    for kk in range(kk_n):
                        v = rows_v[slot, kk, i, pl.ds(col, lanes)]
                        lo = lo + ws[kk] * plsc.bitcast(v << 16, F32)
                        hi = hi + ws[kk] * plsc.bitcast(v & jnp.uint32(0xFFFF0000), F32)
                    out_v[slot, i, pl.ds(col, lanes)] = lo
                    out_v[slot, i, pl.ds(dh + col, lanes)] = hi

        for c in copies(0, 0):
            c.start()

        @pl.loop(0, nchunk, step=2)
        def _(c0):
            for s in range(2):
                ci = c0 + s

                @pl.when(ci + 1 < nchunk)
                def _():
                    for c in copies(ci + 1, 1 - s):
                        c.start()

                for c in copies(ci, s):
                    c.wait()

                @pl.when(ci >= 2)
                def _():
                    out_copy(ci - 2, s).wait()

                compute(s)
                out_copy(ci, s).start()

        out_copy(nchunk - 2, 0).wait()
        out_copy(nchunk - 1, 1).wait()

    return k(obuf, dest_c, w_c)


def _experts_kernel(be_ref, nvalid_ref, run_ref, nxt_ref, nused_ref, *refs, bm):
    ns = EXPERT_STREAMS
    xq_refs = refs[:ns]
    wg_hbm, wu_hbm, wd_hbm, o_hbm, wg_f, wu_f, wd_f, wg_s, wu_s, wd_s, xs, ob, sem, osem = refs[ns:]
    b = pl.program_id(0)
    nb = pl.num_programs(0)
    nused = nused_ref[0]
    bq = bm // ns

    def weight_copies(e, slot):
        return (pltpu.make_async_copy(wg_hbm.at[e], wg_f.at[slot], sem.at[slot]),
                pltpu.make_async_copy(wu_hbm.at[e], wu_f.at[slot], sem.at[slot]),
                pltpu.make_async_copy(wd_hbm.at[e], wd_f.at[slot], sem.at[slot]))

    def out_copies(blk, slot):
        return [pltpu.make_async_copy(ob.at[slot, pl.ds(q * bq, bq)],
                                      o_hbm.at[pl.ds(pl.multiple_of(blk * bm + q * bq, 8), bq)], osem.at[slot])
                for q in range(ns)]

    @pl.when((b >= 2) & (b - 2 < nused))
    def _():
        for c in out_copies(b - 2, b % 2):
            c.wait()

    @pl.when(b < nused)
    def _():
        e = be_ref[b]
        slot = run_ref[b] % 2

        @pl.when(b == 0)
        def _():
            for c in weight_copies(e, slot):
                c.start()

        @pl.when((b == 0) | (e != be_ref[jnp.maximum(b - 1, 0)]))
        def _():
            for c in weight_copies(e, slot):
                c.wait()

            @pl.when(nxt_ref[b] >= 0)
            def _():
                for c in weight_copies(nxt_ref[b], 1 - slot):
                    c.start()

            wg_s[...] = wg_f[slot].astype(BF16)
            wu_s[...] = wu_f[slot].astype(BF16)
            wd_s[...] = wd_f[slot].astype(BF16)

        rows = lax.broadcasted_iota(jnp.int32, xq_refs[0].shape, 0)
        for q in range(ns):
            xp = jnp.where(rows + q * bq < nvalid_ref[b], xq_refs[q][...], jnp.uint32(0))
            xs[q * bq:(q + 1) * bq, :] = _unpack_bf16_pairs(xp).astype(BF16)
        x = xs[...]
        g = jnp.dot(x, wg_s[...], preferred_element_type=F32)
        u = jnp.dot(x, wu_s[...], preferred_element_type=F32)
        hid = (_silu(g) * u).astype(BF16)
        ob[b % 2] = _pack_bf16_pairs(jnp.dot(hid, wd_s[...], preferred_element_type=F32))
        for c in out_copies(b, b % 2):
            c.start()

    @pl.when(b == nb - 1)
    def _():
        @pl.when((b >= 1) & (b - 1 < nused))
        def _():
            for c in out_copies(b - 1, 1 - b % 2):
                c.wait()

        @pl.when(b < nused)
        def _():
            for c in out_copies(b, b % 2):
                c.wait()


def _experts(xbuf, block_e, nvalid, run, nxt, nused, w_gate, w_up, w_down, bm):
    p, dh = xbuf.shape
    nb = p // bm
    d, de = w_gate.shape[1:]
    ns = EXPERT_STREAMS
    bq = bm // ns
    hbm = pl.BlockSpec(memory_space=pl.ANY)

    def quarter(q):
        return pl.BlockSpec((bq, dh), lambda b, be, nv, rn, nx, nu: (jnp.minimum(b, nu[0] - 1) * ns + q, 0))

    grid_spec = pltpu.PrefetchScalarGridSpec(
        num_scalar_prefetch=5,
        grid=(nused[0],),
        in_specs=[quarter(q) for q in range(ns)] + [hbm, hbm, hbm],
        out_specs=hbm,
        scratch_shapes=[pltpu.VMEM((2, d, de), F32), pltpu.VMEM((2, d, de), F32), pltpu.VMEM((2, de, d), F32),
                        pltpu.VMEM((d, de), BF16), pltpu.VMEM((d, de), BF16), pltpu.VMEM((de, d), BF16),
                        pltpu.VMEM((bm, 2 * dh), BF16), pltpu.VMEM((2, bm, dh), jnp.uint32),
                        pltpu.SemaphoreType.DMA((2,)), pltpu.SemaphoreType.DMA((2,))],
    )
    return pl.pallas_call(
        functools.partial(_experts_kernel, bm=bm),
        grid_spec=grid_spec,
        out_shape=jax.ShapeDtypeStruct((p, dh), jnp.uint32),
        compiler_params=_cparams(("arbitrary",)),
        name="experts",
    )(block_e, nvalid, run, nxt, nused, *([xbuf] * ns), w_gate, w_up, w_down)


def _combine_kernel(y_ref, x1_ref, ysh_ref, mod_ref, g_ref, out_ref):
    y = y_ref[...] + _unpack_bf16_pairs(ysh_ref[...])
    out_ref[...] = x1_ref[...] + mod_ref[0, 5:6, :] * (_rms(y) * g_ref[...])


def _combine(y_routed, x1_flat, ysh_flat, mod3, g_post, s, tc):
    t, d = x1_flat.shape
    per_seq = s // tc
    tok = pl.BlockSpec((tc, d), lambda i: (i, 0))
    return pl.pallas_call(
        _combine_kernel,
        grid=(t // tc,),
        in_specs=[tok, tok, pl.BlockSpec((tc, d // 2), lambda i: (i, 0)),
                  pl.BlockSpec((1, 6, d), lambda i: (i // per_seq, 0, 0)),
                  pl.BlockSpec((1, d), lambda i: (0, 0))],
        out_specs=tok,
        out_shape=jax.ShapeDtypeStruct((t, d), F32),
        compiler_params=_cparams(("parallel",)),
        name="combine",
    )(y_routed, x1_flat, ysh_flat, mod3, g_post)


def _channel_dft_table():
    c = np.arange(GROUP_DIM)
    ang = 2.0 * np.pi * ((c[:, None] * c[None, :]) % GROUP_DIM) / GROUP_DIM
    eye = np.eye(D_FNET // GROUP_DIM)
    scale = 1.0 / np.sqrt(GROUP_DIM)
    cos_m, sin_m = np.kron(eye, np.cos(ang)) * scale, np.kron(eye, np.sin(ang)) * scale
    cols = [m[:, c * DFT_CH:(c + 1) * DFT_CH] for c in range(D_FNET // DFT_CH) for m in (cos_m, sin_m)]
    return jnp.asarray(np.concatenate(cols, axis=1), BF16)


EXPERT_BLOCK_MAX = 1280
EXPERT_BLOCK_ALIGN = 64


def _expert_block_rows(n_assign):
    target = max(n_assign // N_EXPERTS * 9 // 8, 2 * EXPERT_BLOCK_ALIGN)
    k = -(-target // EXPERT_BLOCK_MAX)
    return -(-target // (k * EXPERT_BLOCK_ALIGN)) * EXPERT_BLOCK_ALIGN


def _layer(x, mod, p):
    b, s, d = x.shape
    t = b * s
    mod3 = mod.reshape(b, 6, d)
    g_tab, h_tab = _dft_tables(s)

    v, zcs = _inproj(x, mod3, p["g_mix_pre"], p["w_in_b"], _channel_dft_table())
    tile = min(TOKEN_TILE, s)
    cn = _conv(v, p["conv_w"], p["conv_b"], p["conv_ln_g"], p["conv_ln_b"], p["g_conv_out"], tile)
    fy = _seqdft(zcs, g_tab, h_tab)
    x1, h2, ysh = _mix(x, cn, fy, mod3, p["g_fnet_out"], p["w_out_b"], p["g_mix_post"], p["g_ffn_pre"],
                       p["wsg_b"], p["wsu_b"], p["wsd_b"], tile)
    idx, rank, wgt, cnt = _router(h2, p["wr_t_b"], p["b_router_col"], tile)

    n = t * TOP_K
    bm = _expert_block_rows(n)
    counts = cnt[:, 0]
    pcounts = (counts + bm - 1) // bm * bm
    pends = jnp.cumsum(pcounts)
    pstarts = pends - pcounts
    dest = _dest(pstarts.astype(jnp.int32), idx, rank, min(DEST_TILE, t))
    nb = (n + N_EXPERTS * (bm - 1) + bm - 1) // bm
    nused = (pends[-1] // bm).astype(jnp.int32)
    blk = jnp.minimum(jnp.arange(nb, dtype=jnp.int32), nused - 1) * bm
    block_e = jnp.sum((pends[None, :] <= blk[:, None]).astype(jnp.int32), axis=1)
    block_e = jnp.minimum(block_e, N_EXPERTS - 1)
    nvalid = jnp.clip(pstarts[block_e] + counts[block_e] - blk, 0, bm).astype(jnp.int32)
    first = jnp.concatenate([jnp.ones((1,), jnp.int32), (block_e[1:] != block_e[:-1]).astype(jnp.int32)])
    run = jnp.cumsum(first) - 1
    eid = jnp.arange(N_EXPERTS, dtype=jnp.int32)
    later = lax.cummin(jnp.where(pcounts > 0, eid, N_EXPERTS)[::-1])[::-1]
    nxt_e = jnp.concatenate([later[1:], jnp.full((1,), N_EXPERTS, jnp.int32)])
    nxt = jnp.where(nxt_e < N_EXPERTS, nxt_e, -1)[block_e].astype(jnp.int32)

    xbuf = _dispatch(h2.reshape(t, d // 2), dest, nb * bm)
    obuf = _experts(xbuf, block_e, nvalid, run.astype(jnp.int32), nxt, nused.reshape(1),
                    p["w_gate"], p["w_up"], p["w_down"], bm)
    y_routed = _gather_sum(obuf, dest, wgt)
    out = _combine(y_routed, x1.reshape(t, d), ysh.reshape(t, d // 2), mod3, p["g_ffn_post"], s,
                   min(COMBINE_TILE, s))
    return out.reshape(b, s, d)


def kernel(x_prompt, x_sample, c_prompt, c_sample, w_ada, b_ada, g_mix_pre, w_in, conv_w, conv_b, conv_ln_g, conv_ln_b, g_conv_out, g_fnet_out, w_out, g_mix_post, g_ffn_pre, w_router, b_router, w_gate, w_up, w_down, ws_gate, ws_up, ws_down, g_ffn_post):
    assert w_ada.shape[0] == 1, "single-layer kernel"
    bp, bs = c_prompt.shape[0], c_sample.shape[0]
    rows = -(-(bp + bs) // 8) * 8
    c_all = jnp.zeros((rows, D_MODEL), F32).at[:bp].set(c_prompt).at[bp:bp + bs].set(c_sample)
    mod = _ada(c_all, w_ada[0], b_ada)
    p = {
        "g_mix_pre": g_mix_pre, "w_in_b": w_in[0].astype(BF16),
        "conv_w": conv_w[0], "conv_b": conv_b, "conv_ln_g": conv_ln_g, "conv_ln_b": conv_ln_b,
        "g_conv_out": g_conv_out, "g_fnet_out": g_fnet_out, "w_out_b": w_out[0].astype(BF16),
        "g_mix_post": g_mix_post, "g_ffn_pre": g_ffn_pre,
        "wr_t_b": w_router[0].T.astype(BF16), "b_router_col": b_router[0][:, None],
        "w_gate": w_gate[0], "w_up": w_up[0], "w_down": w_down[0],
        "wsg_b": ws_gate[0].astype(BF16), "wsu_b": ws_up[0].astype(BF16), "wsd_b": ws_down[0].astype(BF16),
        "g_ffn_post": g_ffn_post,
    }
    y_prompt = _layer(x_prompt, mod[:bp], p)
    y_sample = _layer(x_sample, mod[bp:bp + bs], p)
    return (y_prompt, y_sample)
```

```python
import functools

import numpy as np
import jax
import jax.numpy as jnp
from jax import lax
from jax.experimental import pallas as pl
from jax.experimental.pallas import tpu as pltpu
from jax.experimental.pallas import tpu_sc as plsc

F32 = jnp.float32
BF16 = jnp.bfloat16

D_MODEL = 1024
D_CONV = 512
D_FNET = 512
GROUP_DIM = 64
CONV_WIDTH = 31
N_EXPERTS = 256
TOP_K = 8
N_EXPERT_GROUPS = 8
GROUP_SIZE = N_EXPERTS // N_EXPERT_GROUPS
TOPK_GROUPS = 4
ROUTED_SCALE = 2.5
EPS = 1e-6

DFT_S1 = 128
DFT_CH = 128
HALO = 16
VMEM_LIMIT = 56 * 1024 * 1024
TOKEN_TILE = 1024
COMBINE_TILE = 1024
DEST_TILE = 2048
SC_CORES = 2
SC_SUBCORES = 16
SC_WORKERS = SC_CORES * SC_SUBCORES
SC_ROWS = 64
SC_LANES = 16
SC_SUM_TOKENS = 8
EXPERT_STREAMS = 4

def _cparams(sem, vmem=None):
    return pltpu.CompilerParams(dimension_semantics=sem, vmem_limit_bytes=vmem or VMEM_LIMIT)


def _rms(x):
    return x * lax.rsqrt(jnp.mean(x * x, axis=-1, keepdims=True) + EPS)


def _silu(x):
    return x * jax.nn.sigmoid(x)


def _pack_bf16_pairs(x):
    c = x.shape[-1] // 2
    bits = lax.bitcast_convert_type(x.astype(BF16).astype(F32), jnp.uint32)
    return (bits[:, :c] >> 16) | bits[:, c:]


def _unpack_bf16_pairs(p):
    lo = lax.bitcast_convert_type(p << 16, F32)
    hi = lax.bitcast_convert_type(p & jnp.uint32(0xFFFF0000), F32)
    return jnp.concatenate([lo, hi], axis=-1)


def _ada_kernel(c_ref, w_ref, b_ref, o_ref):
    o_ref[...] = jnp.dot(_silu(c_ref[...]), w_ref[...], preferred_element_type=F32) + b_ref[...]


def _ada(c, w_ada, b_ada):
    bp, d = c.shape
    n = w_ada.shape[1]
    return pl.pallas_call(
        _ada_kernel,
        grid=(n // d,),
        in_specs=[pl.BlockSpec((bp, d), lambda j: (0, 0)),
                  pl.BlockSpec((d, d), lambda j: (0, j)),
                  pl.BlockSpec((1, d), lambda j: (0, j))],
        out_specs=pl.BlockSpec((bp, d), lambda j: (0, j)),
        out_shape=jax.ShapeDtypeStruct((bp, n), F32),
        compiler_params=_cparams(("parallel",)),
        name="ada",
    )(c, w_ada, b_ada)


def _inproj_kernel(x_ref, mod_ref, g_ref, win_ref, cs_ref, perm_ref, v_ref, z_ref, *, s2, p1l):
    x = x_ref[0]
    h = _rms(x) * g_ref[...]
    h = h * (1.0 + mod_ref[0, 1:2, :]) + mod_ref[0, 0:1, :]
    u = jnp.dot(h.astype(BF16), win_ref[...], preferred_element_type=F32)
    a = u[:, :D_CONV]
    gt = u[:, D_CONV:2 * D_CONV]
    z = u[:, 2 * D_CONV:]
    v_ref[0] = (a * jax.nn.sigmoid(gt)).astype(v_ref.dtype)
    zp = jnp.dot(perm_ref[...], z.astype(BF16), preferred_element_type=F32).astype(BF16)
    zcs = jnp.dot(zp, cs_ref[...], preferred_element_type=F32).astype(BF16)
    z_ref[0] = zcs.reshape(s2, p1l, zcs.shape[-1])


def _inproj(x, mod3, g_mix_pre, w_in_b, cs_b):
    b, s, d = x.shape
    s2 = s // DFT_S1
    p1l = 16
    ts = p1l * s2
    r = np.arange(ts)
    perm = np.zeros((ts, ts), np.float32)
    perm[(r % s2) * p1l + r // s2, r] = 1.0
    perm = jnp.asarray(perm, BF16)
    return pl.pallas_call(
        functools.partial(_inproj_kernel, s2=s2, p1l=p1l),
        grid=(b, s // ts),
        in_specs=[pl.BlockSpec((1, ts, d), lambda i, t: (i, t, 0)),
                  pl.BlockSpec((1, 6, d), lambda i, t: (i, 0, 0)),
                  pl.BlockSpec((1, d), lambda i, t: (0, 0)),
                  pl.BlockSpec(w_in_b.shape, lambda i, t: (0, 0)),
                  pl.BlockSpec(cs_b.shape, lambda i, t: (0, 0)),
                  pl.BlockSpec((ts, ts), lambda i, t: (0, 0))],
        out_specs=[pl.BlockSpec((1, ts, D_CONV), lambda i, t: (i, t, 0)),
                   pl.BlockSpec((1, s2, p1l, 2 * D_FNET), lambda i, t: (i, 0, t, 0))],
        out_shape=[jax.ShapeDtypeStruct((b, s, D_CONV), BF16),
                   jax.ShapeDtypeStruct((b, s2, DFT_S1, 2 * D_FNET), BF16)],
        compiler_params=_cparams(("parallel", "parallel")),
        name="inproj",
    )(x, mod3, g_mix_pre, w_in_b, cs_b, perm)


def _conv_kernel(vp_ref, v_ref, vn_ref, w_ref, b_ref, lg_ref, lb_ref, go_ref, o_ref, pad_ref, sh_ref, *, ts, rc):
    t = pl.program_id(1)
    nt = pl.num_programs(1)
    pad_ref[0:HALO, :] = jnp.where(t > 0, vp_ref[0].astype(F32), 0.0)
    pad_ref[HALO:HALO + ts, :] = v_ref[0].astype(F32)
    pad_ref[HALO + ts:HALO + ts + HALO, :] = jnp.where(t < nt - 1, vn_ref[0].astype(F32), 0.0)
    span = ts + 2 * HALO - 8
    for m in range(8):
        sh_ref[m] = pad_ref[m:m + span, :]
    off = HALO - CONV_WIDTH // 2
    for c in range(ts // rc):
        r0 = c * rc
        acc = jnp.zeros((rc // 8, 8, D_CONV), F32)
        for j in range(CONV_WIDTH):
            m, q = (off + j) % 8, (off + j) // 8
            tap = sh_ref[m, r0 + 8 * q:r0 + 8 * q + rc, :].reshape(rc // 8, 8, D_CONV)
            acc = acc + tap * w_ref[j][None]
        acc = acc.reshape(rc, D_CONV) + b_ref[...]
        mu = jnp.mean(acc, axis=-1, keepdims=True)
        xc = acc - mu
        var = jnp.mean(xc * xc, axis=-1, keepdims=True)
        y = xc * lax.rsqrt(var + EPS) * lg_ref[...] + lb_ref[...]
        y = _silu(y)
        y = _rms(y) * go_ref[...]
        o_ref[0, r0:r0 + rc, :] = y.astype(o_ref.dtype)


def _conv(v, conv_w, conv_b, ln_g, ln_b, g_out, ts, rc=32):
    b, s, c = v.shape
    hb = ts // HALO
    nh = s // HALO
    vec = pl.BlockSpec((1, c), lambda i, t: (0, 0))
    return pl.pallas_call(
        functools.partial(_conv_kernel, ts=ts, rc=rc),
        grid=(b, s // ts),
        in_specs=[pl.BlockSpec((1, HALO, c), lambda i, t: (i, jnp.maximum(t * hb - 1, 0), 0)),
                  pl.BlockSpec((1, ts, c), lambda i, t: (i, t, 0)),
                  pl.BlockSpec((1, HALO, c), lambda i, t: (i, jnp.minimum((t + 1) * hb, nh - 1), 0)),
                  pl.BlockSpec((CONV_WIDTH, 8, c), lambda i, t: (0, 0, 0)),
                  vec, vec, vec, vec],
        out_specs=pl.BlockSpec((1, ts, c), lambda i, t: (i, t, 0)),
        out_shape=jax.ShapeDtypeStruct((b, s, c), BF16),
        scratch_shapes=[pltpu.VMEM((ts + 2 * HALO, c), F32), pltpu.VMEM((8, ts + 2 * HALO - 8, c), F32)],
        compiler_params=_cparams(("parallel", "parallel")),
        name="conv",
    )(v, v, v, jnp.broadcast_to(conv_w[:, None, :], (CONV_WIDTH, 8, c)), conv_b, ln_g, ln_b, g_out)


def _dft_tables(s):
    s1 = DFT_S1
    s2 = s // s1
    k1 = np.arange(s1)[None, :, None]
    p1 = np.arange(s1)[None, None, :]
    p2 = np.arange(s2)[:, None, None]
    ang = 2.0 * np.pi * ((k1 * (s2 * p1 + p2)) % s) / s
    g = np.concatenate([np.cos(ang), np.sin(ang)], axis=1)
    k2 = np.arange(s2)[:, None]
    q2 = np.arange(s2)[None, :]
    ang2 = 2.0 * np.pi * ((k2 * q2) % s2) / s2
    h = np.concatenate([np.cos(ang2), np.sin(ang2)], axis=1) / np.sqrt(s)
    return jnp.asarray(g, BF16), jnp.asarray(h, BF16)


def _seqdft_kernel(z_ref, g_ref, h_ref, o_ref, scr_ref, *, s1, s2):
    ch = DFT_CH
    for p2 in range(s2):
        r = jnp.dot(g_ref[p2], z_ref[0, p2], preferred_element_type=F32)
        a_re = r[:s1, :ch] - r[s1:, ch:]
        a_im = -(r[:s1, ch:] + r[s1:, :ch])
        scr_ref[:, p2, :] = a_re
        scr_ref[:, s2 + p2, :] = a_im
    hmat = h_ref[...]
    for k1 in range(s1):
        y = jnp.dot(hmat, scr_ref[k1].astype(BF16), preferred_element_type=F32)
        o_ref[0, pl.ds(k1, s2, stride=s1), :] = y


def _seqdft(zp, g_tab, h_tab):
    b, s2, s1, _ = zp.shape
    s = s1 * s2
    nch = D_FNET // DFT_CH
    return pl.pallas_call(
        functools.partial(_seqdft_kernel, s1=s1, s2=s2),
        grid=(b, nch),
        in_specs=[pl.BlockSpec((1, s2, s1, 2 * DFT_CH), lambda i, c: (i, 0, 0, c)),
                  pl.BlockSpec(g_tab.shape, lambda i, c: (0, 0, 0)),
                  pl.BlockSpec(h_tab.shape, lambda i, c: (0, 0))],
        out_specs=pl.BlockSpec((1, s, DFT_CH), lambda i, c: (i, 0, c)),
        out_shape=jax.ShapeDtypeStruct((b, s, D_FNET), F32),
        scratch_shapes=[pltpu.VMEM((s1, 2 * s2, DFT_CH), F32)],
        compiler_params=_cparams(("parallel", "parallel")),
        name="seqdft",
    )(zp, g_tab, h_tab)


def _mix_kernel(x_ref, cn_ref, fy_ref, mod_ref, gf_ref, wout_ref, gpost_ref, gpre_ref,
                wsg_ref, wsu_ref, wsd_ref, x1_ref, h2_ref, ysh_ref):
    fn = _rms(fy_ref[0]) * gf_ref[...]
    mixed = jnp.dot(cn_ref[0], wout_ref[:D_CONV, :], preferred_element_type=F32)
    mixed = mixed + jnp.dot(fn.astype(BF16), wout_ref[D_CONV:, :], preferred_element_type=F32)
    x1 = x_ref[0] + mod_ref[0, 2:3, :] * (_rms(mixed) * gpost_ref[...])
    x1_ref[0] = x1
    h2 = _rms(x1) * gpre_ref[...]
    h2 = h2 * (1.0 + mod_ref[0, 4:5, :]) + mod_ref[0, 3:4, :]
    h2_ref[0] = _pack_bf16_pairs(h2)
    hb = h2.astype(BF16)
    hid = _silu(jnp.dot(hb, wsg_ref[...], preferred_element_type=F32))
    hid = hid * jnp.dot(hb, wsu_ref[...], preferred_element_type=F32)
    ysh_ref[0] = _pack_bf16_pairs(jnp.dot(hid.astype(BF16), wsd_ref[...], preferred_element_type=F32))


def _mix(x, cn, fy, mod3, g_fnet, w_out_b, g_post, g_pre, wsg_b, wsu_b, wsd_b, ts):
    b, s, d = x.shape
    tok = lambda c: pl.BlockSpec((1, ts, c), lambda i, t: (i, t, 0))
    full = lambda a: pl.BlockSpec(a.shape, lambda i, t: (0,) * a.ndim)
    return pl.pallas_call(
        _mix_kernel,
        grid=(b, s // ts),
        in_specs=[tok(d), tok(D_CONV), tok(D_FNET),
                  pl.BlockSpec((1, 6, d), lambda i, t: (i, 0, 0)),
                  full(g_fnet), full(w_out_b), full(g_post), full(g_pre),
                  full(wsg_b), full(wsu_b), full(wsd_b)],
        out_specs=[tok(d), tok(d // 2), tok(d // 2)],
        out_shape=[jax.ShapeDtypeStruct((b, s, d), F32),
                   jax.ShapeDtypeStruct((b, s, d // 2), jnp.uint32),
                   jax.ShapeDtypeStruct((b, s, d // 2), jnp.uint32)],
        compiler_params=_cparams(("parallel", "parallel")),
        name="mix",
    )(x, cn, fy, mod3, g_fnet, w_out_b, g_post, g_pre, wsg_b, wsu_b, wsd_b)


def _router_kernel(h_ref, wr_ref, br_ref, u_ref, idx_ref, rank_ref, wgt_ref, cnt_ref, carry_ref, *, tr):
    e = N_EXPERTS

    @pl.when((pl.program_id(0) == 0) & (pl.program_id(1) == 0))
    def _():
        carry_ref[...] = jnp.zeros_like(carry_ref)

    logits = lax.dot_general(wr_ref[...], _unpack_bf16_pairs(h_ref[0]).astype(BF16), (((1,), (1,)), ((), ())),
                             preferred_element_type=F32)
    sc = jax.nn.sigmoid(logits)
    sb = sc + br_ref[...]
    ninf = jnp.float32(-jnp.inf)

    io_g = lax.broadcasted_iota(jnp.int32, (GROUP_SIZE, tr), 0).astype(F32)
    gs = []
    for g in range(N_EXPERT_GROUPS):
        blk = sb[g * GROUP_SIZE:(g + 1) * GROUP_SIZE]
        m1 = jnp.max(blk, axis=0, keepdims=True)
        i1 = jnp.min(jnp.where(blk == m1, io_g, float(GROUP_SIZE)), axis=0, keepdims=True)
        m2 = jnp.max(jnp.where(io_g == i1, ninf, blk), axis=0, keepdims=True)
        gs.append(m1 + m2)
    masked = []
    for g in range(N_EXPERT_GROUPS):
        beat = jnp.zeros((1, tr), F32)
        for o in range(N_EXPERT_GROUPS):
            if o == g:
                continue
            wins = (gs[o] > gs[g]) | ((gs[o] == gs[g]) & (o < g))
            beat = beat + wins.astype(F32)
        keep = beat < float(TOPK_GROUPS)
        masked.append(jnp.where(keep, sb[g * GROUP_SIZE:(g + 1) * GROUP_SIZE], ninf))
    v = jnp.concatenate(masked, axis=0)

    io_e = lax.broadcasted_iota(jnp.int32, (e, tr), 0).astype(F32)
    ids, ws = [], []
    sel = jnp.zeros((e, tr), F32)
    for _ in range(TOP_K):
        m = jnp.max(v, axis=0, keepdims=True)
        i = jnp.min(jnp.where(v == m, io_e, float(e)), axis=0, keepdims=True)
        oh = io_e == i
        ids.append(i)
        ws.append(jnp.sum(jnp.where(oh, sc, 0.0), axis=0, keepdims=True))
        v = jnp.where(oh, ninf, v)
        sel = sel + oh.astype(F32)

    wsum = ws[0]
    for k in range(1, TOP_K):
        wsum = wsum + ws[k]
    wgt_ref[...] = jnp.concatenate([w / wsum * ROUTED_SCALE for w in ws], axis=0)
    idx_ref[...] = jnp.concatenate(ids, axis=0).astype(jnp.int32)

    excl = jnp.dot(sel.astype(BF16), u_ref[...], preferred_element_type=F32)
    base = carry_ref[:, 0:1]
    rank_full = base + excl
    ranks = [jnp.sum(jnp.where(io_e == ids[k], rank_full, 0.0), axis=0, keepdims=True)
             for k in range(TOP_K)]
    rank_ref[...] = jnp.concatenate(ranks, axis=0).astype(jnp.int32)
    new = base + jnp.sum(sel, axis=1, keepdims=True)
    carry_ref[...] = jnp.broadcast_to(new, carry_ref.shape)
    cnt_ref[...] = jnp.broadcast_to(new, cnt_ref.shape).astype(jnp.int32)


def _router(h2p, wr_t_b, b_router_col, tr):
    b, s, d = h2p.shape
    t = b * s
    nt = s // tr
    u = jnp.asarray(np.triu(np.ones((tr, tr), np.float32), k=1), BF16)
    col = lambda i, j: (0, i * nt + j)
    return pl.pallas_call(
        functools.partial(_router_kernel, tr=tr),
        grid=(b, nt),
        in_specs=[pl.BlockSpec((1, tr, d), lambda i, j: (i, j, 0)),
                  pl.BlockSpec(wr_t_b.shape, lambda i, j: (0, 0)),
                  pl.BlockSpec((N_EXPERTS, 1), lambda i, j: (0, 0)),
                  pl.BlockSpec((tr, tr), lambda i, j: (0, 0))],
        out_specs=[pl.BlockSpec((TOP_K, tr), col), pl.BlockSpec((TOP_K, tr), col),
                   pl.BlockSpec((TOP_K, tr), col),
                   pl.BlockSpec((N_EXPERTS, 128), lambda i, j: (0, 0))],
        out_shape=[jax.ShapeDtypeStruct((TOP_K, t), jnp.int32),
                   jax.ShapeDtypeStruct((TOP_K, t), jnp.int32),
                   jax.ShapeDtypeStruct((TOP_K, t), F32),
                   jax.ShapeDtypeStruct((N_EXPERTS, 128), jnp.int32)],
        scratch_shapes=[pltpu.VMEM((N_EXPERTS, 128), F32)],
        compiler_params=_cparams(("arbitrary", "arbitrary")),
        name="router",
    )(h2p, wr_t_b, b_router_col, u)


def _dest_kernel(pstart_ref, idx_ref, rank_ref, dest_ref):
    idx = idx_ref[...]

    def body(g, acc):
        for j in range(8):
            e = g * 8 + j
            acc = jnp.where(idx == e, pstart_ref[e], acc)
        return acc

    dest_ref[...] = lax.fori_loop(0, N_EXPERTS // 8, body, jnp.zeros_like(idx)) + rank_ref[...]


def _dest(pstarts, idx, rank, tl):
    k, t = idx.shape
    grid_spec = pltpu.PrefetchScalarGridSpec(
        num_scalar_prefetch=1,
        grid=(t // tl,),
        in_specs=[pl.BlockSpec((k, tl), lambda i, ps: (0, i)), pl.BlockSpec((k, tl), lambda i, ps: (0, i))],
        out_specs=pl.BlockSpec((k, tl), lambda i, ps: (0, i)),
    )
    return pl.pallas_call(
        _dest_kernel,
        grid_spec=grid_spec,
        out_shape=jax.ShapeDtypeStruct((k, t), jnp.int32),
        compiler_params=_cparams(("parallel",)),
        name="dest",
    )(pstarts, idx, rank)


def _sc_mesh():
    return plsc.VectorSubcoreMesh(core_axis_name="c", subcore_axis_name="s",
                                  num_cores=SC_CORES, num_subcores=SC_SUBCORES)


def _sc_worker_base(per_worker):
    return (lax.axis_index("s") * SC_CORES + lax.axis_index("c")) * per_worker


def _dispatch(h2_flat, dest, p_rows):
    t, dh = h2_flat.shape
    r = SC_ROWS
    per_w = t // SC_WORKERS
    nchunk = per_w // r
    assert per_w % (2 * r) == 0

    @functools.partial(
        pl.kernel, mesh=_sc_mesh(),
        out_type=jax.ShapeDtypeStruct((p_rows, dh), h2_flat.dtype),
        scratch_types=[pltpu.VMEM((2, TOP_K, r), jnp.int32), pltpu.VMEM((2, r, dh), h2_flat.dtype),
                       pltpu.SemaphoreType.DMA((2,)), pltpu.SemaphoreType.DMA((2,))],
        name="sc_dispatch",
    )
    def k(rows_hbm, dest_hbm, out_hbm, idx_v, rows_v, lsem, ssem):
        base = _sc_worker_base(per_w)

        def load(ci, slot):
            t0 = base + ci * r
            for kk in range(TOP_K):
                pltpu.sync_copy(dest_hbm.at[kk, pl.ds(t0, r)], idx_v.at[slot, kk])
            pltpu.async_copy(rows_hbm.at[pl.ds(t0, r)], rows_v.at[slot], lsem.at[slot])

        def scatter(ci, slot):
            t0 = base + ci * r
            pltpu.make_async_copy(rows_hbm.at[pl.ds(t0, r)], rows_v.at[slot], lsem.at[slot]).wait()
            for kk in range(TOP_K):
                pltpu.async_copy(rows_v.at[slot], out_hbm.at[idx_v.at[slot, kk]], ssem.at[slot])

        def drain(slot):
            for kk in range(TOP_K):
                pltpu.make_async_copy(rows_v.at[slot], out_hbm.at[idx_v.at[slot, kk]], ssem.at[slot]).wait()

        load(0, 0)

        @pl.loop(0, nchunk, step=2)
        def _(c0):
            for s in range(2):
                ci = c0 + s

                @pl.when(ci + 1 < nchunk)
                def _():
                    @pl.when(ci >= 1)
                    def _():
                        drain(1 - s)
                    load(ci + 1, 1 - s)

                scatter(ci, s)

        drain(0)
        drain(1)

    return k(h2_flat, dest)


def _gather_sum(obuf, dest, wgt):
    _, dh = obuf.shape
    kk_n, t = dest.shape
    r, lanes = SC_SUM_TOKENS, SC_LANES
    assert r * lanes == 128
    per_w = t // SC_WORKERS
    nchunk = per_w // r
    idx_rows = per_w * kk_n // 128
    assert per_w % (2 * r) == 0 and idx_rows % 8 == 0
    nj = dh // lanes
    dest_c = dest.reshape(kk_n, t // r, r).transpose(1, 0, 2).reshape(t * kk_n // 128, 128)
    w_c = jnp.broadcast_to(wgt.reshape(kk_n, t // r, r).transpose(1, 0, 2)[..., None],
                           (t // r, kk_n, r, lanes)).reshape(t // r, kk_n, r * lanes)

    @functools.partial(
        pl.kernel, mesh=_sc_mesh(),
        out_type=jax.ShapeDtypeStruct((t, 2 * dh), F32),
        scratch_types=[pltpu.VMEM((idx_rows, 128), jnp.int32), pltpu.VMEM((2, kk_n, r, dh), obuf.dtype),
                       pltpu.VMEM((2, kk_n, r * lanes), F32), pltpu.VMEM((2, r, 2 * dh), F32),
                       pltpu.SemaphoreType.DMA((2,)), pltpu.SemaphoreType.DMA((2,))],
        compiler_params=pltpu.CompilerParams(needs_layout_passes=False),
        name="sc_gather_sum",
    )
    def k(table_hbm, idx_hbm, w_hbm, out_hbm, idx_v, rows_v, w_v, out_v, gsem, wsem):
        wid = lax.axis_index("s") * SC_CORES + lax.axis_index("c")
        base = wid * per_w
        cbase = wid * nchunk
        pltpu.sync_copy(idx_hbm.at[pl.ds(wid * idx_rows, idx_rows)], idx_v)

        def idx_list(ci, kk):
            off = (ci * kk_n + kk) * r
            return idx_v.at[off // 128, pl.ds(pl.multiple_of(off % 128, 8), r)]

        def copies(ci, slot):
            yield pltpu.make_async_copy(w_hbm.at[cbase + ci], w_v.at[slot], gsem.at[slot])
            for kk in range(kk_n):
                yield pltpu.make_async_copy(table_hbm.at[idx_list(ci, kk)], rows_v.at[slot, kk], gsem.at[slot])

        def out_copy(ci, slot):
            return pltpu.make_async_copy(out_v.at[slot], out_hbm.at[pl.ds(base + ci * r, r)], wsem.at[slot])

        def compute(slot):
            @pl.loop(0, r)
            def _(i):
                ws = [w_v[slot, kk, pl.ds(pl.multiple_of(i * lanes, lanes), lanes)] for kk in range(kk_n)]

                @plsc.parallel_loop(0, nj, unroll=4)
                def _(j):
                    col = pl.multiple_of(j * lanes, lanes)
                    lo = jnp.zeros((lanes,), F32)
                    hi = jnp.zeros((lanes,), F32)
                    for kk in range(kk_n):
                        v = rows_v[slot, kk, i, pl.ds(col, lanes)]
                        lo = lo + ws[kk] * plsc.bitcast(v << 16, F32)
                        hi = hi + ws[kk] * plsc.bitcast(v & jnp.uint32(0xFFFF0000), F32)
                    out_v[slot, i, pl.ds(col, lanes)] = lo
                    out_v[slot, i, pl.ds(dh + col, lanes)] = hi

        for c in copies(0, 0):
            c.start()

        @pl.loop(0, nchunk, step=2)
        def _(c0):
            for s in range(2):
                ci = c0 + s

                @pl.when(ci + 1 < nchunk)
                def _():
                    for c in copies(ci + 1, 1 - s):
                        c.start()

                for c in copies(ci, s):
                    c.wait()

                @pl.when(ci >= 2)
                def _():
                    out_copy(ci - 2, s).wait()

                compute(s)
                out_copy(ci, s).start()

        out_copy(nchunk - 2, 0).wait()
        out_copy(nchunk - 1, 1).wait()

    return k(obuf, dest_c, w_c)


def _experts_kernel(be_ref, nvalid_ref, run_ref, nxt_ref, nused_ref, *refs, bm):
    ns = EXPERT_STREAMS
    xq_refs = refs[:ns]
    wg_hbm, wu_hbm, wd_hbm, o_hbm, wg_f, wu_f, wd_f, wg_s, wu_s, wd_s, xs, ob, sem, osem = refs[ns:]
    b = pl.program_id(0)
    nb = pl.num_programs(0)
    nused = nused_ref[0]
    bq = bm // ns

    def weight_copies(e, slot):
        return (pltpu.make_async_copy(wg_hbm.at[e], wg_f.at[slot], sem.at[slot]),
                pltpu.make_async_copy(wu_hbm.at[e], wu_f.at[slot], sem.at[slot]),
                pltpu.make_async_copy(wd_hbm.at[e], wd_f.at[slot], sem.at[slot]))

    def out_copies(blk, slot):
        return [pltpu.make_async_copy(ob.at[slot, pl.ds(q * bq, bq)],
                                      o_hbm.at[pl.ds(pl.multiple_of(blk * bm + q * bq, 8), bq)], osem.at[slot])
                for q in range(ns)]

    @pl.when((b >= 2) & (b - 2 < nused))
    def _():
        for c in out_copies(b - 2, b % 2):
            c.wait()

    @pl.when(b < nused)
    def _():
        e = be_ref[b]
        slot = run_ref[b] % 2

        @pl.when(b == 0)
        def _():
            for c in weight_copies(e, slot):
                c.start()

        @pl.when((b == 0) | (e != be_ref[jnp.maximum(b - 1, 0)]))
        def _():
            for c in weight_copies(e, slot):
                c.wait()

            @pl.when(nxt_ref[b] >= 0)
            def _():
                for c in weight_copies(nxt_ref[b], 1 - slot):
                    c.start()

            wg_s[...] = wg_f[slot].astype(BF16)
            wu_s[...] = wu_f[slot].astype(BF16)
            wd_s[...] = wd_f[slot].astype(BF16)

        rows = lax.broadcasted_iota(jnp.int32, xq_refs[0].shape, 0)
        for q in range(ns):
            xp = jnp.where(rows + q * bq < nvalid_ref[b], xq_refs[q][...], jnp.uint32(0))
            xs[q * bq:(q + 1) * bq, :] = _unpack_bf16_pairs(xp).astype(BF16)
        x = xs[...]
        g = jnp.dot(x, wg_s[...], preferred_element_type=F32)
        u = jnp.dot(x, wu_s[...], preferred_element_type=F32)
        hid = (_silu(g) * u).astype(BF16)
        ob[b % 2] = _pack_bf16_pairs(jnp.dot(hid, wd_s[...], preferred_element_type=F32))
        for c in out_copies(b, b % 2):
            c.start()

    @pl.when(b == nb - 1)
    def _():
        @pl.when((b >= 1) & (b - 1 < nused))
        def _():
            for c in out_copies(b - 1, 1 - b % 2):
                c.wait()

        @pl.when(b < nused)
        def _():
            for c in out_copies(b, b % 2):
                c.wait()


def _experts(xbuf, block_e, nvalid, run, nxt, nused, w_gate, w_up, w_down, bm):
    p, dh = xbuf.shape
    nb = p // bm
    d, de = w_gate.shape[1:]
    ns = EXPERT_STREAMS
    bq = bm // ns
    hbm = pl.BlockSpec(memory_space=pl.ANY)

    def quarter(q):
        return pl.BlockSpec((bq, dh), lambda b, be, nv, rn, nx, nu: (jnp.minimum(b, nu[0] - 1) * ns + q, 0))

    grid_spec = pltpu.PrefetchScalarGridSpec(
        num_scalar_prefetch=5,
        grid=(nused[0],),
        in_specs=[quarter(q) for q in range(ns)] + [hbm, hbm, hbm],
        out_specs=hbm,
        scratch_shapes=[pltpu.VMEM((2, d, de), F32), pltpu.VMEM((2, d, de), F32), pltpu.VMEM((2, de, d), F32),
                        pltpu.VMEM((d, de), BF16), pltpu.VMEM((d, de), BF16), pltpu.VMEM((de, d), BF16),
                        pltpu.VMEM((bm, 2 * dh), BF16), pltpu.VMEM((2, bm, dh), jnp.uint32),
                        pltpu.SemaphoreType.DMA((2,)), pltpu.SemaphoreType.DMA((2,))],
    )
    return pl.pallas_call(
        functools.partial(_experts_kernel, bm=bm),
        grid_spec=grid_spec,
        out_shape=jax.ShapeDtypeStruct((p, dh), jnp.uint32),
        compiler_params=_cparams(("arbitrary",)),
        name="experts",
    )(block_e, nvalid, run, nxt, nused, *([xbuf] * ns), w_gate, w_up, w_down)


def _combine_kernel(y_ref, x1_ref, ysh_ref, mod_ref, g_ref, out_ref):
    y = y_ref[...] + _unpack_bf16_pairs(ysh_ref[...])
    out_ref[...] = x1_ref[...] + mod_ref[0, 5:6, :] * (_rms(y) * g_ref[...])


def _combine(y_routed, x1_flat, ysh_flat, mod3, g_post, s, tc):
    t, d = x1_flat.shape
    per_seq = s // tc
    tok = pl.BlockSpec((tc, d), lambda i: (i, 0))
    return pl.pallas_call(
        _combine_kernel,
        grid=(t // tc,),
        in_specs=[tok, tok, pl.BlockSpec((tc, d // 2), lambda i: (i, 0)),
                  pl.BlockSpec((1, 6, d), lambda i: (i // per_seq, 0, 0)),
                  pl.BlockSpec((1, d), lambda i: (0, 0))],
        out_specs=tok,
        out_shape=jax.ShapeDtypeStruct((t, d), F32),
        compiler_params=_cparams(("parallel",)),
        name="combine",
    )(y_routed, x1_flat, ysh_flat, mod3, g_post)


def _channel_dft_table():
    c = np.arange(GROUP_DIM)
    ang = 2.0 * np.pi * ((c[:, None] * c[None, :]) % GROUP_DIM) / GROUP_DIM
    eye = np.eye(D_FNET // GROUP_DIM)
    scale = 1.0 / np.sqrt(GROUP_DIM)
    cos_m, sin_m = np.kron(eye, np.cos(ang)) * scale, np.kron(eye, np.sin(ang)) * scale
    cols = [m[:, c * DFT_CH:(c + 1) * DFT_CH] for c in range(D_FNET // DFT_CH) for m in (cos_m, sin_m)]
    return jnp.asarray(np.concatenate(cols, axis=1), BF16)


EXPERT_BLOCK_MAX = 1280
EXPERT_BLOCK_ALIGN = 64


def _expert_block_rows(n_assign):
    target = max(n_assign // N_EXPERTS * 9 // 8, 2 * EXPERT_BLOCK_ALIGN)
    k = -(-target // EXPERT_BLOCK_MAX)
    return -(-target // (k * EXPERT_BLOCK_ALIGN)) * EXPERT_BLOCK_ALIGN


def _layer(x, mod, p):
    b, s, d = x.shape
    t = b * s
    mod3 = mod.reshape(b, 6, d)
    g_tab, h_tab = _dft_tables(s)

    v, zcs = _inproj(x, mod3, p["g_mix_pre"], p["w_in_b"], _channel_dft_table())
    tile = min(TOKEN_TILE, s)
    cn = _conv(v, p["conv_w"], p["conv_b"], p["conv_ln_g"], p["conv_ln_b"], p["g_conv_out"], tile)
    fy = _seqdft(zcs, g_tab, h_tab)
    x1, h2, ysh = _mix(x, cn, fy, mod3, p["g_fnet_out"], p["w_out_b"], p["g_mix_post"], p["g_ffn_pre"],
                       p["wsg_b"], p["wsu_b"], p["wsd_b"], tile)
    idx, rank, wgt, cnt = _router(h2, p["wr_t_b"], p["b_router_col"], tile)

    n = t * TOP_K
    bm = _expert_block_rows(n)
    counts = cnt[:, 0]
    pcounts = (counts + bm - 1) // bm * bm
    pends = jnp.cumsum(pcounts)
    pstarts = pends - pcounts
    dest = _dest(pstarts.astype(jnp.int32), idx, rank, min(DEST_TILE, t))
    nb = (n + N_EXPERTS * (bm - 1) + bm - 1) // bm
    nused = (pends[-1] // bm).astype(jnp.int32)
    blk = jnp.minimum(jnp.arange(nb, dtype=jnp.int32), nused - 1) * bm
    block_e = jnp.sum((pends[None, :] <= blk[:, None]).astype(jnp.int32), axis=1)
    block_e = jnp.minimum(block_e, N_EXPERTS - 1)
    nvalid = jnp.clip(pstarts[block_e] + counts[block_e] - blk, 0, bm).astype(jnp.int32)
    first = jnp.concatenate([jnp.ones((1,), jnp.int32), (block_e[1:] != block_e[:-1]).astype(jnp.int32)])
    run = jnp.cumsum(first) - 1
    eid = jnp.arange(N_EXPERTS, dtype=jnp.int32)
    later = lax.cummin(jnp.where(pcounts > 0, eid, N_EXPERTS)[::-1])[::-1]
    nxt_e = jnp.concatenate([later[1:], jnp.full((1,), N_EXPERTS, jnp.int32)])
    nxt = jnp.where(nxt_e < N_EXPERTS, nxt_e, -1)[block_e].astype(jnp.int32)

    xbuf = _dispatch(h2.reshape(t, d // 2), dest, nb * bm)
    obuf = _experts(xbuf, block_e, nvalid, run.astype(jnp.int32), nxt, nused.reshape(1),
                    p["w_gate"], p["w_up"], p["w_down"], bm)
    y_routed = _gather_sum(obuf, dest, wgt)
    out = _combine(y_routed, x1.reshape(t, d), ysh.reshape(t, d // 2), mod3, p["g_ffn_post"], s,
                   min(COMBINE_TILE, s))
    return out.reshape(b, s, d)


def kernel(x_prompt, x_sample, c_prompt, c_sample, w_ada, b_ada, g_mix_pre, w_in, conv_w, conv_b, conv_ln_g, conv_ln_b, g_conv_out, g_fnet_out, w_out, g_mix_post, g_ffn_pre, w_router, b_router, w_gate, w_up, w_down, ws_gate, ws_up, ws_down, g_ffn_post):
    assert w_ada.shape[0] == 1, "single-layer kernel"
    bp, bs = c_prompt.shape[0], c_sample.shape[0]
    rows = -(-(bp + bs) // 8) * 8
    c_all = jnp.zeros((rows, D_MODEL), F32).at[:bp].set(c_prompt).at[bp:bp + bs].set(c_sample)
    mod = _ada(c_all, w_ada[0], b_ada)
    p = {
        "g_mix_pre": g_mix_pre, "w_in_b": w_in[0].astype(BF16),
        "conv_w": conv_w[0], "conv_b": conv_b, "conv_ln_g": conv_ln_g, "conv_ln_b": conv_ln_b,
        "g_conv_out": g_conv_out, "g_fnet_out": g_fnet_out, "w_out_b": w_out[0].astype(BF16),
        "g_mix_post": g_mix_post, "g_ffn_pre": g_ffn_pre,
        "wr_t_b": w_router[0].T.astype(BF16), "b_router_col": b_router[0][:, None],
        "w_gate": w_gate[0], "w_up": w_up[0], "w_down": w_down[0],
        "wsg_b": ws_gate[0].astype(BF16), "wsu_b": ws_up[0].astype(BF16), "wsd_b": ws_down[0].astype(BF16),
        "g_ffn_post": g_ffn_post,
    }
    y_prompt = _layer(x_prompt, mod[:bp], p)
    y_sample = _layer(x_sample, mod[bp:bp + bs], p)
    return (y_prompt, y_sample)
```

```python
import functools

import numpy as np
import jax
import jax.numpy as jnp
from jax import lax
from jax.experimental import pallas as pl
from jax.experimental.pallas import tpu as pltpu
from jax.experimental.pallas import tpu_sc as plsc

F32 = jnp.float32
BF16 = jnp.bfloat16

D_MODEL = 1024
D_CONV = 512
D_FNET = 512
GROUP_DIM = 64
CONV_WIDTH = 31
N_EXPERTS = 256
TOP_K = 8
N_EXPERT_GROUPS = 8
GROUP_SIZE = N_EXPERTS // N_EXPERT_GROUPS
TOPK_GROUPS = 4
ROUTED_SCALE = 2.5
EPS = 1e-6

DFT_S1 = 128
DFT_CH = 128
HALO = 16
VMEM_LIMIT = 56 * 1024 * 1024
TOKEN_TILE = 1024
COMBINE_TILE = 1024
DEST_TILE = 2048
SC_CORES = 2
SC_SUBCORES = 16
SC_WORKERS = SC_CORES * SC_SUBCORES
SC_ROWS = 64
SC_LANES = 16
SC_SUM_TOKENS = 8
EXPERT_STREAMS = 4

def _cparams(sem, vmem=None):
    return pltpu.CompilerParams(dimension_semantics=sem, vmem_limit_bytes=vmem or VMEM_LIMIT)


def _rms(x):
    return x * lax.rsqrt(jnp.mean(x * x, axis=-1, keepdims=True) + EPS)


def _silu(x):
    return x * jax.nn.sigmoid(x)


def _pack_bf16_pairs(x):
    c = x.shape[-1] // 2
    bits = lax.bitcast_convert_type(x.astype(BF16).astype(F32), jnp.uint32)
    return (bits[:, :c] >> 16) | bits[:, c:]


def _unpack_bf16_pairs(p):
    lo = lax.bitcast_convert_type(p << 16, F32)
    hi = lax.bitcast_convert_type(p & jnp.uint32(0xFFFF0000), F32)
    return jnp.concatenate([lo, hi], axis=-1)


def _ada_kernel(c_ref, w_ref, b_ref, o_ref):
    o_ref[...] = jnp.dot(_silu(c_ref[...]), w_ref[...], preferred_element_type=F32) + b_ref[...]


def _ada(c, w_ada, b_ada):
    bp, d = c.shape
    n = w_ada.shape[1]
    return pl.pallas_call(
        _ada_kernel,
        grid=(n // d,),
        in_specs=[pl.BlockSpec((bp, d), lambda j: (0, 0)),
                  pl.BlockSpec((d, d), lambda j: (0, j)),
                  pl.BlockSpec((1, d), lambda j: (0, j))],
        out_specs=pl.BlockSpec((bp, d), lambda j: (0, j)),
        out_shape=jax.ShapeDtypeStruct((bp, n), F32),
        compiler_params=_cparams(("parallel",)),
        name="ada",
    )(c, w_ada, b_ada)


def _inproj_kernel(x_ref, mod_ref, g_ref, win_ref, cs_ref, perm_ref, v_ref, z_ref, *, s2, p1l):
    x = x_ref[0]
    h = _rms(x) * g_ref[...]
    h = h * (1.0 + mod_ref[0, 1:2, :]) + mod_ref[0, 0:1, :]
    u = jnp.dot(h.astype(BF16), win_ref[...], preferred_element_type=F32)
    a = u[:, :D_CONV]
    gt = u[:, D_CONV:2 * D_CONV]
    z = u[:, 2 * D_CONV:]
    v_ref[0] = (a * jax.nn.sigmoid(gt)).astype(v_ref.dtype)
    zp = jnp.dot(perm_ref[...], z.astype(BF16), preferred_element_type=F32).astype(BF16)
    zcs = jnp.dot(zp, cs_ref[...], preferred_element_type=F32).astype(BF16)
    z_ref[0] = zcs.reshape(s2, p1l, zcs.shape[-1])


def _inproj(x, mod3, g_mix_pre, w_in_b, cs_b):
    b, s, d = x.shape
    s2 = s // DFT_S1
    p1l = 16
    ts = p1l * s2
    r = np.arange(ts)
    perm = np.zeros((ts, ts), np.float32)
    perm[(r % s2) * p1l + r // s2, r] = 1.0
    perm = jnp.asarray(perm, BF16)
    return pl.pallas_call(
        functools.partial(_inproj_kernel, s2=s2, p1l=p1l),
        grid=(b, s // ts),
        in_specs=[pl.BlockSpec((1, ts, d), lambda i, t: (i, t, 0)),
                  pl.BlockSpec((1, 6, d), lambda i, t: (i, 0, 0)),
                  pl.BlockSpec((1, d), lambda i, t: (0, 0)),
                  pl.BlockSpec(w_in_b.shape, lambda i, t: (0, 0)),
                  pl.BlockSpec(cs_b.shape, lambda i, t: (0, 0)),
                  pl.BlockSpec((ts, ts), lambda i, t: (0, 0))],
        out_specs=[pl.BlockSpec((1, ts, D_CONV), lambda i, t: (i, t, 0)),
                   pl.BlockSpec((1, s2, p1l, 2 * D_FNET), lambda i, t: (i, 0, t, 0))],
        out_shape=[jax.ShapeDtypeStruct((b, s, D_CONV), BF16),
                   jax.ShapeDtypeStruct((b, s2, DFT_S1, 2 * D_FNET), BF16)],
        compiler_params=_cparams(("parallel", "parallel")),
        name="inproj",
    )(x, mod3, g_mix_pre, w_in_b, cs_b, perm)


def _conv_kernel(vp_ref, v_ref, vn_ref, w_ref, b_ref, lg_ref, lb_ref, go_ref, o_ref, pad_ref, sh_ref, *, ts, rc):
    t = pl.program_id(1)
    nt = pl.num_programs(1)
    pad_ref[0:HALO, :] = jnp.where(t > 0, vp_ref[0].astype(F32), 0.0)
    pad_ref[HALO:HALO + ts, :] = v_ref[0].astype(F32)
    pad_ref[HALO + ts:HALO + ts + HALO, :] = jnp.where(t < nt - 1, vn_ref[0].astype(F32), 0.0)
    span = ts + 2 * HALO - 8
    for m in range(8):
        sh_ref[m] = pad_ref[m:m + span, :]
    off = HALO - CONV_WIDTH // 2
    for c in range(ts // rc):
        r0 = c * rc
        acc = jnp.zeros((rc // 8, 8, D_CONV), F32)
        for j in range(CONV_WIDTH):
            m, q = (off + j) % 8, (off + j) // 8
            tap = sh_ref[m, r0 + 8 * q:r0 + 8 * q + rc, :].reshape(rc // 8, 8, D_CONV)
            acc = acc + tap * w_ref[j][None]
        acc = acc.reshape(rc, D_CONV) + b_ref[...]
        mu = jnp.mean(acc, axis=-1, keepdims=True)
        xc = acc - mu
        var = jnp.mean(xc * xc, axis=-1, keepdims=True)
        y = xc * lax.rsqrt(var + EPS) * lg_ref[...] + lb_ref[...]
        y = _silu(y)
        y = _rms(y) * go_ref[...]
        o_ref[0, r0:r0 + rc, :] = y.astype(o_ref.dtype)


def _conv(v, conv_w, conv_b, ln_g, ln_b, g_out, ts, rc=32):
    b, s, c = v.shape
    hb = ts // HALO
    nh = s // HALO
    vec = pl.BlockSpec((1, c), lambda i, t: (0, 0))
    return pl.pallas_call(
        functools.partial(_conv_kernel, ts=ts, rc=rc),
        grid=(b, s // ts),
        in_specs=[pl.BlockSpec((1, HALO, c), lambda i, t: (i, jnp.maximum(t * hb - 1, 0), 0)),
                  pl.BlockSpec((1, ts, c), lambda i, t: (i, t, 0)),
                  pl.BlockSpec((1, HALO, c), lambda i, t: (i, jnp.minimum((t + 1) * hb, nh - 1), 0)),
                  pl.BlockSpec((CONV_WIDTH, 8, c), lambda i, t: (0, 0, 0)),
                  vec, vec, vec, vec],
        out_specs=pl.BlockSpec((1, ts, c), lambda i, t: (i, t, 0)),
        out_shape=jax.ShapeDtypeStruct((b, s, c), BF16),
        scratch_shapes=[pltpu.VMEM((ts + 2 * HALO, c), F32), pltpu.VMEM((8, ts + 2 * HALO - 8, c), F32)],
        compiler_params=_cparams(("parallel", "parallel")),
        name="conv",
    )(v, v, v, jnp.broadcast_to(conv_w[:, None, :], (CONV_WIDTH, 8, c)), conv_b, ln_g, ln_b, g_out)


def _dft_tables(s):
    s1 = DFT_S1
    s2 = s // s1
    k1 = np.arange(s1)[None, :, None]
    p1 = np.arange(s1)[None, None, :]
    p2 = np.arange(s2)[:, None, None]
    ang = 2.0 * np.pi * ((k1 * (s2 * p1 + p2)) % s) / s
    g = np.concatenate([np.cos(ang), np.sin(ang)], axis=1)
    k2 = np.arange(s2)[:, None]
    q2 = np.arange(s2)[None, :]
    ang2 = 2.0 * np.pi * ((k2 * q2) % s2) / s2
    h = np.concatenate([np.cos(ang2), np.sin(ang2)], axis=1) / np.sqrt(s)
    return jnp.asarray(g, BF16), jnp.asarray(h, BF16)


def _seqdft_kernel(z_ref, g_ref, h_ref, o_ref, scr_ref, *, s1, s2):
    ch = DFT_CH
    for p2 in range(s2):
        r = jnp.dot(g_ref[p2], z_ref[0, p2], preferred_element_type=F32)
        a_re = r[:s1, :ch] - r[s1:, ch:]
        a_im = -(r[:s1, ch:] + r[s1:, :ch])
        scr_ref[:, p2, :] = a_re
        scr_ref[:, s2 + p2, :] = a_im
    hmat = h_ref[...]
    for k1 in range(s1):
        y = jnp.dot(hmat, scr_ref[k1].astype(BF16), preferred_element_type=F32)
        o_ref[0, pl.ds(k1, s2, stride=s1), :] = y


def _seqdft(zp, g_tab, h_tab):
    b, s2, s1, _ = zp.shape
    s = s1 * s2
    nch = D_FNET // DFT_CH
    return pl.pallas_call(
        functools.partial(_seqdft_kernel, s1=s1, s2=s2),
        grid=(b, nch),
        in_specs=[pl.BlockSpec((1, s2, s1, 2 * DFT_CH), lambda i, c: (i, 0, 0, c)),
                  pl.BlockSpec(g_tab.shape, lambda i, c: (0, 0, 0)),
                  pl.BlockSpec(h_tab.shape, lambda i, c: (0, 0))],
        out_specs=pl.BlockSpec((1, s, DFT_CH), lambda i, c: (i, 0, c)),
        out_shape=jax.ShapeDtypeStruct((b, s, D_FNET), F32),
        scratch_shapes=[pltpu.VMEM((s1, 2 * s2, DFT_CH), F32)],
        compiler_params=_cparams(("parallel", "parallel")),
        name="seqdft",
    )(zp, g_tab, h_tab)


def _mix_kernel(x_ref, cn_ref, fy_ref, mod_ref, gf_ref, wout_ref, gpost_ref, gpre_ref,
                wsg_ref, wsu_ref, wsd_ref, x1_ref, h2_ref, ysh_ref):
    fn = _rms(fy_ref[0]) * gf_ref[...]
    mixed = jnp.dot(cn_ref[0], wout_ref[:D_CONV, :], preferred_element_type=F32)
    mixed = mixed + jnp.dot(fn.astype(BF16), wout_ref[D_CONV:, :], preferred_element_type=F32)
    x1 = x_ref[0] + mod_ref[0, 2:3, :] * (_rms(mixed) * gpost_ref[...])
    x1_ref[0] = x1
    h2 = _rms(x1) * gpre_ref[...]
    h2 = h2 * (1.0 + mod_ref[0, 4:5, :]) + mod_ref[0, 3:4, :]
    h2_ref[0] = _pack_bf16_pairs(h2)
    hb = h2.astype(BF16)
    hid = _silu(jnp.dot(hb, wsg_ref[...], preferred_element_type=F32))
    hid = hid * jnp.dot(hb, wsu_ref[...], preferred_element_type=F32)
    ysh_ref[0] = _pack_bf16_pairs(jnp.dot(hid.astype(BF16), wsd_ref[...], preferred_element_type=F32))


def _mix(x, cn, fy, mod3, g_fnet, w_out_b, g_post, g_pre, wsg_b, wsu_b, wsd_b, ts):
    b, s, d = x.shape
    tok = lambda c: pl.BlockSpec((1, ts, c), lambda i, t: (i, t, 0))
    full = lambda a: pl.BlockSpec(a.shape, lambda i, t: (0,) * a.ndim)
    return pl.pallas_call(
        _mix_kernel,
        grid=(b, s // ts),
        in_specs=[tok(d), tok(D_CONV), tok(D_FNET),
                  pl.BlockSpec((1, 6, d), lambda i, t: (i, 0, 0)),
                  full(g_fnet), full(w_out_b), full(g_post), full(g_pre),
                  full(wsg_b), full(wsu_b), full(wsd_b)],
        out_specs=[tok(d), tok(d // 2), tok(d // 2)],
        out_shape=[jax.ShapeDtypeStruct((b, s, d), F32),
                   jax.ShapeDtypeStruct((b, s, d // 2), jnp.uint32),
                   jax.ShapeDtypeStruct((b, s, d // 2), jnp.uint32)],
        compiler_params=_cparams(("parallel", "parallel")),
        name="mix",
    )(x, cn, fy, mod3, g_fnet, w_out_b, g_post, g_pre, wsg_b, wsu_b, wsd_b)


def _router_kernel(h_ref, wr_ref, br_ref, u_ref, idx_ref, rank_ref, wgt_ref, cnt_ref, carry_ref, *, tr):
    e = N_EXPERTS

    @pl.when((pl.program_id(0) == 0) & (pl.program_id(1) == 0))
    def _():
        carry_ref[...] = jnp.zeros_like(carry_ref)

    logits = lax.dot_general(wr_ref[...], _unpack_bf16_pairs(h_ref[0]).astype(BF16), (((1,), (1,)), ((), ())),
                             preferred_element_type=F32)
    sc = jax.nn.sigmoid(logits)
    sb = sc + br_ref[...]
    ninf = jnp.float32(-jnp.inf)

    io_g = lax.broadcasted_iota(jnp.int32, (GROUP_SIZE, tr), 0).astype(F32)
    gs = []
    for g in range(N_EXPERT_GROUPS):
        blk = sb[g * GROUP_SIZE:(g + 1) * GROUP_SIZE]
        m1 = jnp.max(blk, axis=0, keepdims=True)
        i1 = jnp.min(jnp.where(blk == m1, io_g, float(GROUP_SIZE)), axis=0, keepdims=True)
        m2 = jnp.max(jnp.where(io_g == i1, ninf, blk), axis=0, keepdims=True)
        gs.append(m1 + m2)
    masked = []
    for g in range(N_EXPERT_GROUPS):
        beat = jnp.zeros((1, tr), F32)
        for o in range(N_EXPERT_GROUPS):
            if o == g:
                continue
            wins = (gs[o] > gs[g]) | ((gs[o] == gs[g]) & (o < g))
            beat = beat + wins.astype(F32)
        keep = beat < float(TOPK_GROUPS)
        masked.append(jnp.where(keep, sb[g * GROUP_SIZE:(g + 1) * GROUP_SIZE], ninf))
    v = jnp.concatenate(masked, axis=0)

    io_e = lax.broadcasted_iota(jnp.int32, (e, tr), 0).astype(F32)
    ids, ws = [], []
    sel = jnp.zeros((e, tr), F32)
    for _ in range(TOP_K):
        m = jnp.max(v, axis=0, keepdims=True)
        i = jnp.min(jnp.where(v == m, io_e, float(e)), axis=0, keepdims=True)
        oh = io_e == i
        ids.append(i)
        ws.append(jnp.sum(jnp.where(oh, sc, 0.0), axis=0, keepdims=True))
        v = jnp.where(oh, ninf, v)
        sel = sel + oh.astype(F32)

    wsum = ws[0]
    for k in range(1, TOP_K):
        wsum = wsum + ws[k]
    wgt_ref[...] = jnp.concatenate([w / wsum * ROUTED_SCALE for w in ws], axis=0)
    idx_ref[...] = jnp.concatenate(ids, axis=0).astype(jnp.int32)

    excl = jnp.dot(sel.astype(BF16), u_ref[...], preferred_element_type=F32)
    base = carry_ref[:, 0:1]
    rank_full = base + excl
    ranks = [jnp.sum(jnp.where(io_e == ids[k], rank_full, 0.0), axis=0, keepdims=True)
             for k in range(TOP_K)]
    rank_ref[...] = jnp.concatenate(ranks, axis=0).astype(jnp.int32)
    new = base + jnp.sum(sel, axis=1, keepdims=True)
    carry_ref[...] = jnp.broadcast_to(new, carry_ref.shape)
    cnt_ref[...] = jnp.broadcast_to(new, cnt_ref.shape).astype(jnp.int32)


def _router(h2p, wr_t_b, b_router_col, tr):
    b, s, d = h2p.shape
    t = b * s
    nt = s // tr
    u = jnp.asarray(np.triu(np.ones((tr, tr), np.float32), k=1), BF16)
    col = lambda i, j: (0, i * nt + j)
    return pl.pallas_call(
        functools.partial(_router_kernel, tr=tr),
        grid=(b, nt),
        in_specs=[pl.BlockSpec((1, tr, d), lambda i, j: (i, j, 0)),
                  pl.BlockSpec(wr_t_b.shape, lambda i, j: (0, 0)),
                  pl.BlockSpec((N_EXPERTS, 1), lambda i, j: (0, 0)),
                  pl.BlockSpec((tr, tr), lambda i, j: (0, 0))],
        out_specs=[pl.BlockSpec((TOP_K, tr), col), pl.BlockSpec((TOP_K, tr), col),
                   pl.BlockSpec((TOP_K, tr), col),
                   pl.BlockSpec((N_EXPERTS, 128), lambda i, j: (0, 0))],
        out_shape=[jax.ShapeDtypeStruct((TOP_K, t), jnp.int32),
                   jax.ShapeDtypeStruct((TOP_K, t), jnp.int32),
                   jax.ShapeDtypeStruct((TOP_K, t), F32),
                   jax.ShapeDtypeStruct((N_EXPERTS, 128), jnp.int32)],
        scratch_shapes=[pltpu.VMEM((N_EXPERTS, 128), F32)],
        compiler_params=_cparams(("arbitrary", "arbitrary")),
        name="router",
    )(h2p, wr_t_b, b_router_col, u)


def _dest_kernel(pstart_ref, idx_ref, rank_ref, dest_ref):
    idx = idx_ref[...]

    def body(g, acc):
        for j in range(8):
            e = g * 8 + j
            acc = jnp.where(idx == e, pstart_ref[e], acc)
        return acc

    dest_ref[...] = lax.fori_loop(0, N_EXPERTS // 8, body, jnp.zeros_like(idx)) + rank_ref[...]


def _dest(pstarts, idx, rank, tl):
    k, t = idx.shape
    grid_spec = pltpu.PrefetchScalarGridSpec(
        num_scalar_prefetch=1,
        grid=(t // tl,),
        in_specs=[pl.BlockSpec((k, tl), lambda i, ps: (0, i)), pl.BlockSpec((k, tl), lambda i, ps: (0, i))],
        out_specs=pl.BlockSpec((k, tl), lambda i, ps: (0, i)),
    )
    return pl.pallas_call(
        _dest_kernel,
        grid_spec=grid_spec,
        out_shape=jax.ShapeDtypeStruct((k, t), jnp.int32),
        compiler_params=_cparams(("parallel",)),
        name="dest",
    )(pstarts, idx, rank)


def _sc_mesh():
    return plsc.VectorSubcoreMesh(core_axis_name="c", subcore_axis_name="s",
                                  num_cores=SC_CORES, num_subcores=SC_SUBCORES)


def _sc_worker_base(per_worker):
    return (lax.axis_index("s") * SC_CORES + lax.axis_index("c")) * per_worker


def _dispatch(h2_flat, dest, p_rows):
    t, dh = h2_flat.shape
    r = SC_ROWS
    per_w = t // SC_WORKERS
    nchunk = per_w // r
    assert per_w % (2 * r) == 0

    @functools.partial(
        pl.kernel, mesh=_sc_mesh(),
        out_type=jax.ShapeDtypeStruct((p_rows, dh), h2_flat.dtype),
        scratch_types=[pltpu.VMEM((2, TOP_K, r), jnp.int32), pltpu.VMEM((2, r, dh), h2_flat.dtype),
                       pltpu.SemaphoreType.DMA((2,)), pltpu.SemaphoreType.DMA((2,))],
        name="sc_dispatch",
    )
    def k(rows_hbm, dest_hbm, out_hbm, idx_v, rows_v, lsem, ssem):
        base = _sc_worker_base(per_w)

        def load(ci, slot):
            t0 = base + ci * r
            for kk in range(TOP_K):
                pltpu.sync_copy(dest_hbm.at[kk, pl.ds(t0, r)], idx_v.at[slot, kk])
            pltpu.async_copy(rows_hbm.at[pl.ds(t0, r)], rows_v.at[slot], lsem.at[slot])

        def scatter(ci, slot):
            t0 = base + ci * r
            pltpu.make_async_copy(rows_hbm.at[pl.ds(t0, r)], rows_v.at[slot], lsem.at[slot]).wait()
            for kk in range(TOP_K):
                pltpu.async_copy(rows_v.at[slot], out_hbm.at[idx_v.at[slot, kk]], ssem.at[slot])

        def drain(slot):
            for kk in range(TOP_K):
                pltpu.make_async_copy(rows_v.at[slot], out_hbm.at[idx_v.at[slot, kk]], ssem.at[slot]).wait()

        load(0, 0)

        @pl.loop(0, nchunk, step=2)
        def _(c0):
            for s in range(2):
                ci = c0 + s

                @pl.when(ci + 1 < nchunk)
                def _():
                    @pl.when(ci >= 1)
                    def _():
                        drain(1 - s)
                    load(ci + 1, 1 - s)

                scatter(ci, s)

        drain(0)
        drain(1)

    return k(h2_flat, dest)


def _gather_sum(obuf, dest, wgt):
    _, dh = obuf.shape
    kk_n, t = dest.shape
    r, lanes = SC_SUM_TOKENS, SC_LANES
    assert r * lanes == 128
    per_w = t // SC_WORKERS
    nchunk = per_w // r
    idx_rows = per_w * kk_n // 128
    assert per_w % (2 * r) == 0 and idx_rows % 8 == 0
    nj = dh // lanes
    dest_c = dest.reshape(kk_n, t // r, r).transpose(1, 0, 2).reshape(t * kk_n // 128, 128)
    w_c = jnp.take(wgt.reshape(kk_n, t // r, r).transpose(1, 0, 2), jnp.arange(r * lanes) // lanes, axis=-1)

    @functools.partial(
        pl.kernel, mesh=_sc_mesh(),
        out_type=jax.ShapeDtypeStruct((t, 2 * dh), F32),
        scratch_types=[pltpu.VMEM((idx_rows, 128), jnp.int32), pltpu.VMEM((2, kk_n, r, dh), obuf.dtype),
                       pltpu.VMEM((2, kk_n, r * lanes), F32), pltpu.VMEM((2, r, 2 * dh), F32),
                       pltpu.SemaphoreType.DMA((2,)), pltpu.SemaphoreType.DMA((2,))],
        compiler_params=pltpu.CompilerParams(needs_layout_passes=False),
        name="sc_gather_sum",
    )
    def k(table_hbm, idx_hbm, w_hbm, out_hbm, idx_v, rows_v, w_v, out_v, gsem, wsem):
        wid = lax.axis_index("s") * SC_CORES + lax.axis_index("c")
        base = wid * per_w
        cbase = wid * nchunk
        pltpu.sync_copy(idx_hbm.at[pl.ds(wid * idx_rows, idx_rows)], idx_v)

        def idx_list(ci, kk):
            off = (ci * kk_n + kk) * r
            return idx_v.at[off // 128, pl.ds(pl.multiple_of(off % 128, 8), r)]

        def copies(ci, slot):
            yield pltpu.make_async_copy(w_hbm.at[cbase + ci], w_v.at[slot], gsem.at[slot])
            for kk in range(kk_n):
                yield pltpu.make_async_copy(table_hbm.at[idx_list(ci, kk)], rows_v.at[slot, kk], gsem.at[slot])

        def out_copy(ci, slot):
            return pltpu.make_async_copy(out_v.at[slot], out_hbm.at[pl.ds(base + ci * r, r)], wsem.at[slot])

        def compute(slot):
            @pl.loop(0, r)
            def _(i):
                ws = [w_v[slot, kk, pl.ds(pl.multiple_of(i * lanes, lanes), lanes)] for kk in range(kk_n)]

                @plsc.parallel_loop(0, nj, unroll=4)
                def _(j):
                    col = pl.multiple_of(j * lanes, lanes)
                    lo = jnp.zeros((lanes,), F32)
                    hi = jnp.zeros((lanes,), F32)
                    for kk in range(kk_n):
                        v = rows_v[slot, kk, i, pl.ds(col, lanes)]
                        lo = lo + ws[kk] * plsc.bitcast(v << 16, F32)
                        hi = hi + ws[kk] * plsc.bitcast(v & jnp.uint32(0xFFFF0000), F32)
                    out_v[slot, i, pl.ds(col, lanes)] = lo
                    out_v[slot, i, pl.ds(dh + col, lanes)] = hi

        for c in copies(0, 0):
            c.start()

        @pl.loop(0, nchunk, step=2)
        def _(c0):
            for s in range(2):
                ci = c0 + s

                @pl.when(ci + 1 < nchunk)
                def _():
                    for c in copies(ci + 1, 1 - s):
                        c.start()

                for c in copies(ci, s):
                    c.wait()

                @pl.when(ci >= 2)
                def _():
                    out_copy(ci - 2, s).wait()

                compute(s)
                out_copy(ci, s).start()

        out_copy(nchunk - 2, 0).wait()
        out_copy(nchunk - 1, 1).wait()

    return k(obuf, dest_c, w_c)


def _experts_kernel(be_ref, nvalid_ref, run_ref, nxt_ref, nused_ref, *refs, bm):
    ns = EXPERT_STREAMS
    xq_refs = refs[:ns]
    wg_hbm, wu_hbm, wd_hbm, o_hbm, wg_f, wu_f, wd_f, wg_s, wu_s, wd_s, xs, ob, sem, osem = refs[ns:]
    b = pl.program_id(0)
    nb = pl.num_programs(0)
    nused = nused_ref[0]
    bq = bm // ns

    def weight_copies(e, slot):
        return (pltpu.make_async_copy(wg_hbm.at[e], wg_f.at[slot], sem.at[slot]),
                pltpu.make_async_copy(wu_hbm.at[e], wu_f.at[slot], sem.at[slot]),
                pltpu.make_async_copy(wd_hbm.at[e], wd_f.at[slot], sem.at[slot]))

    def out_copies(blk, slot):
        return [pltpu.make_async_copy(ob.at[slot, pl.ds(q * bq, bq)],
                                      o_hbm.at[pl.ds(pl.multiple_of(blk * bm + q * bq, 8), bq)], osem.at[slot])
                for q in range(ns)]

    @pl.when((b >= 2) & (b - 2 < nused))
    def _():
        for c in out_copies(b - 2, b % 2):
            c.wait()

    @pl.when(b < nused)
    def _():
        e = be_ref[b]
        slot = run_ref[b] % 2

        @pl.when(b == 0)
        def _():
            for c in weight_copies(e, slot):
                c.start()

        @pl.when((b == 0) | (e != be_ref[jnp.maximum(b - 1, 0)]))
        def _():
            for c in weight_copies(e, slot):
                c.wait()

            @pl.when(nxt_ref[b] >= 0)
            def _():
                for c in weight_copies(nxt_ref[b], 1 - slot):
                    c.start()

            wg_s[...] = wg_f[slot].astype(BF16)
            wu_s[...] = wu_f[slot].astype(BF16)
            wd_s[...] = wd_f[slot].astype(BF16)

        rows = lax.broadcasted_iota(jnp.int32, xq_refs[0].shape, 0)
        for q in range(ns):
            xp = jnp.where(rows + q * bq < nvalid_ref[b], xq_refs[q][...], jnp.uint32(0))
            xs[q * bq:(q + 1) * bq, :] = _unpack_bf16_pairs(xp).astype(BF16)
        x = xs[...]
        g = jnp.dot(x, wg_s[...], preferred_element_type=F32)
        u = jnp.dot(x, wu_s[...], preferred_element_type=F32)
        hid = (_silu(g) * u).astype(BF16)
        ob[b % 2] = _pack_bf16_pairs(jnp.dot(hid, wd_s[...], preferred_element_type=F32))
        for c in out_copies(b, b % 2):
            c.start()

    @pl.when(b == nb - 1)
    def _():
        @pl.when((b >= 1) & (b - 1 < nused))
        def _():
            for c in out_copies(b - 1, 1 - b % 2):
                c.wait()

        @pl.when(b < nused)
        def _():
            for c in out_copies(b, b % 2):
                c.wait()


def _experts(xbuf, block_e, nvalid, run, nxt, nused, w_gate, w_up, w_down, bm):
    p, dh = xbuf.shape
    nb = p // bm
    d, de = w_gate.shape[1:]
    ns = EXPERT_STREAMS
    bq = bm // ns
    hbm = pl.BlockSpec(memory_space=pl.ANY)

    def quarter(q):
        return pl.BlockSpec((bq, dh), lambda b, be, nv, rn, nx, nu: (jnp.minimum(b, nu[0] - 1) * ns + q, 0))

    grid_spec = pltpu.PrefetchScalarGridSpec(
        num_scalar_prefetch=5,
        grid=(nused[0],),
        in_specs=[quarter(q) for q in range(ns)] + [hbm, hbm, hbm],
        out_specs=hbm,
        scratch_shapes=[pltpu.VMEM((2, d, de), F32), pltpu.VMEM((2, d, de), F32), pltpu.VMEM((2, de, d), F32),
                        pltpu.VMEM((d, de), BF16), pltpu.VMEM((d, de), BF16), pltpu.VMEM((de, d), BF16),
                        pltpu.VMEM((bm, 2 * dh), BF16), pltpu.VMEM((2, bm, dh), jnp.uint32),
                        pltpu.SemaphoreType.DMA((2,)), pltpu.SemaphoreType.DMA((2,))],
    )
    return pl.pallas_call(
        functools.partial(_experts_kernel, bm=bm),
        grid_spec=grid_spec,
        out_shape=jax.ShapeDtypeStruct((p, dh), jnp.uint32),
        compiler_params=_cparams(("arbitrary",)),
        name="experts",
    )(block_e, nvalid, run, nxt, nused, *([xbuf] * ns), w_gate, w_up, w_down)


def _combine_kernel(y_ref, x1_ref, ysh_ref, mod_ref, g_ref, out_ref):
    y = y_ref[...] + _unpack_bf16_pairs(ysh_ref[...])
    out_ref[...] = x1_ref[...] + mod_ref[0, 5:6, :] * (_rms(y) * g_ref[...])


def _combine(y_routed, x1_flat, ysh_flat, mod3, g_post, s, tc):
    t, d = x1_flat.shape
    per_seq = s // tc
    tok = pl.BlockSpec((tc, d), lambda i: (i, 0))
    return pl.pallas_call(
        _combine_kernel,
        grid=(t // tc,),
        in_specs=[tok, tok, pl.BlockSpec((tc, d // 2), lambda i: (i, 0)),
                  pl.BlockSpec((1, 6, d), lambda i: (i // per_seq, 0, 0)),
                  pl.BlockSpec((1, d), lambda i: (0, 0))],
        out_specs=tok,
        out_shape=jax.ShapeDtypeStruct((t, d), F32),
        compiler_params=_cparams(("parallel",)),
        name="combine",
    )(y_routed, x1_flat, ysh_flat, mod3, g_post)


def _channel_dft_table():
    c = np.arange(GROUP_DIM)
    ang = 2.0 * np.pi * ((c[:, None] * c[None, :]) % GROUP_DIM) / GROUP_DIM
    eye = np.eye(D_FNET // GROUP_DIM)
    scale = 1.0 / np.sqrt(GROUP_DIM)
    cos_m, sin_m = np.kron(eye, np.cos(ang)) * scale, np.kron(eye, np.sin(ang)) * scale
    cols = [m[:, c * DFT_CH:(c + 1) * DFT_CH] for c in range(D_FNET // DFT_CH) for m in (cos_m, sin_m)]
    return jnp.asarray(np.concatenate(cols, axis=1), BF16)


EXPERT_BLOCK_MAX = 1280
EXPERT_BLOCK_ALIGN = 64


def _expert_block_rows(n_assign):
    target = max(n_assign // N_EXPERTS * 9 // 8, 2 * EXPERT_BLOCK_ALIGN)
    k = -(-target // EXPERT_BLOCK_MAX)
    return -(-target // (k * EXPERT_BLOCK_ALIGN)) * EXPERT_BLOCK_ALIGN


def _layer(x, mod, p):
    b, s, d = x.shape
    t = b * s
    mod3 = mod.reshape(b, 6, d)
    g_tab, h_tab = _dft_tables(s)

    v, zcs = _inproj(x, mod3, p["g_mix_pre"], p["w_in_b"], _channel_dft_table())
    tile = min(TOKEN_TILE, s)
    cn = _conv(v, p["conv_w"], p["conv_b"], p["conv_ln_g"], p["conv_ln_b"], p["g_conv_out"], tile)
    fy = _seqdft(zcs, g_tab, h_tab)
    x1, h2, ysh = _mix(x, cn, fy, mod3, p["g_fnet_out"], p["w_out_b"], p["g_mix_post"], p["g_ffn_pre"],
                       p["wsg_b"], p["wsu_b"], p["wsd_b"], tile)
    idx, rank, wgt, cnt = _router(h2, p["wr_t_b"], p["b_router_col"], tile)

    n = t * TOP_K
    bm = _expert_block_rows(n)
    counts = cnt[:, 0]
    pcounts = (counts + bm - 1) // bm * bm
    pends = jnp.cumsum(pcounts)
    pstarts = pends - pcounts
    dest = _dest(pstarts.astype(jnp.int32), idx, rank, min(DEST_TILE, t))
    nb = (n + N_EXPERTS * (bm - 1) + bm - 1) // bm
    nused = (pends[-1] // bm).astype(jnp.int32)
    blk = jnp.minimum(jnp.arange(nb, dtype=jnp.int32), nused - 1) * bm
    block_e = jnp.sum((pends[None, :] <= blk[:, None]).astype(jnp.int32), axis=1)
    block_e = jnp.minimum(block_e, N_EXPERTS - 1)
    nvalid = jnp.clip(pstarts[block_e] + counts[block_e] - blk, 0, bm).astype(jnp.int32)
    first = jnp.concatenate([jnp.ones((1,), jnp.int32), (block_e[1:] != block_e[:-1]).astype(jnp.int32)])
    run = jnp.cumsum(first) - 1
    eid = jnp.arange(N_EXPERTS, dtype=jnp.int32)
    later = lax.cummin(jnp.where(pcounts > 0, eid, N_EXPERTS)[::-1])[::-1]
    nxt_e = jnp.concatenate([later[1:], jnp.full((1,), N_EXPERTS, jnp.int32)])
    nxt = jnp.where(nxt_e < N_EXPERTS, nxt_e, -1)[block_e].astype(jnp.int32)

    xbuf = _dispatch(h2.reshape(t, d // 2), dest, nb * bm)
    obuf = _experts(xbuf, block_e, nvalid, run.astype(jnp.int32), nxt, nused.reshape(1),
                    p["w_gate"], p["w_up"], p["w_down"], bm)
    y_routed = _gather_sum(obuf, dest, wgt)
    out = _combine(y_routed, x1.reshape(t, d), ysh.reshape(t, d // 2), mod3, p["g_ffn_post"], s,
                   min(COMBINE_TILE, s))
    return out.reshape(b, s, d)


def kernel(x_prompt, x_sample, c_prompt, c_sample, w_ada, b_ada, g_mix_pre, w_in, conv_w, conv_b, conv_ln_g, conv_ln_b, g_conv_out, g_fnet_out, w_out, g_mix_post, g_ffn_pre, w_router, b_router, w_gate, w_up, w_down, ws_gate, ws_up, ws_down, g_ffn_post):
    assert w_ada.shape[0] == 1, "single-layer kernel"
    bp, bs = c_prompt.shape[0], c_sample.shape[0]
    rows = -(-(bp + bs) // 8) * 8
    c_all = jnp.zeros((rows, D_MODEL), F32).at[:bp].set(c_prompt).at[bp:bp + bs].set(c_sample)
    mod = _ada(c_all, w_ada[0], b_ada)
    p = {
        "g_mix_pre": g_mix_pre, "w_in_b": w_in[0].astype(BF16),
        "conv_w": conv_w[0], "conv_b": conv_b, "conv_ln_g": conv_ln_g, "conv_ln_b": conv_ln_b,
        "g_conv_out": g_conv_out, "g_fnet_out": g_fnet_out, "w_out_b": w_out[0].astype(BF16),
        "g_mix_post": g_mix_post, "g_ffn_pre": g_ffn_pre,
        "wr_t_b": w_router[0].T.astype(BF16), "b_router_col": b_router[0][:, None],
        "w_gate": w_gate[0], "w_up": w_up[0], "w_down": w_down[0],
        "wsg_b": ws_gate[0].astype(BF16), "wsu_b": ws_up[0].astype(BF16), "wsd_b": ws_down[0].astype(BF16),
        "g_ffn_post": g_ffn_post,
    }
    y_prompt = _layer(x_prompt, mod[:bp], p)
    y_sample = _layer(x_sample, mod[bp:bp + bs], p)
    return (y_prompt, y_sample)
```

```python
import functools

import numpy as np
import jax
import jax.numpy as jnp
from jax import lax
from jax.experimental import pallas as pl
from jax.experimental.pallas import tpu as pltpu
from jax.experimental.pallas import tpu_sc as plsc

F32 = jnp.float32
BF16 = jnp.bfloat16

D_MODEL = 1024
D_CONV = 512
D_FNET = 512
GROUP_DIM = 64
CONV_WIDTH = 31
N_EXPERTS = 256
TOP_K = 8
N_EXPERT_GROUPS = 8
GROUP_SIZE = N_EXPERTS // N_EXPERT_GROUPS
TOPK_GROUPS = 4
ROUTED_SCALE = 2.5
EPS = 1e-6

DFT_S1 = 128
DFT_CH = 128
HALO = 16
VMEM_LIMIT = 56 * 1024 * 1024
TOKEN_TILE = 1024
COMBINE_TILE = 1024
DEST_TILE = 2048
SC_CORES = 2
SC_SUBCORES = 16
SC_WORKERS = SC_CORES * SC_SUBCORES
SC_ROWS = 64
SC_LANES = 16
SC_SUM_TOKENS = 8
EXPERT_STREAMS = 4

def _cparams(sem, vmem=None):
    return pltpu.CompilerParams(dimension_semantics=sem, vmem_limit_bytes=vmem or VMEM_LIMIT)


def _rms(x):
    return x * lax.rsqrt(jnp.mean(x * x, axis=-1, keepdims=True) + EPS)


def _silu(x):
    return x * jax.nn.sigmoid(x)


def _pack_bf16_pairs(x):
    c = x.shape[-1] // 2
    bits = lax.bitcast_convert_type(x.astype(BF16).astype(F32), jnp.uint32)
    return (bits[:, :c] >> 16) | bits[:, c:]


def _unpack_bf16_pairs(p):
    lo = lax.bitcast_convert_type(p << 16, F32)
    hi = lax.bitcast_convert_type(p & jnp.uint32(0xFFFF0000), F32)
    return jnp.concatenate([lo, hi], axis=-1)


def _ada_kernel(c_ref, w_ref, b_ref, o_ref):
    o_ref[...] = jnp.dot(_silu(c_ref[...]), w_ref[...], preferred_element_type=F32) + b_ref[...]


def _ada(c, w_ada, b_ada):
    bp, d = c.shape
    n = w_ada.shape[1]
    return pl.pallas_call(
        _ada_kernel,
        grid=(n // d,),
        in_specs=[pl.BlockSpec((bp, d), lambda j: (0, 0)),
                  pl.BlockSpec((d, d), lambda j: (0, j)),
                  pl.BlockSpec((1, d), lambda j: (0, j))],
        out_specs=pl.BlockSpec((bp, d), lambda j: (0, j)),
        out_shape=jax.ShapeDtypeStruct((bp, n), F32),
        compiler_params=_cparams(("parallel",)),
        name="ada",
    )(c, w_ada, b_ada)


def _inproj_kernel(x_ref, mod_ref, g_ref, win_ref, cs_ref, perm_ref, v_ref, z_ref, *, s2, p1l):
    x = x_ref[0]
    h = _rms(x) * g_ref[...]
    h = h * (1.0 + mod_ref[0, 1:2, :]) + mod_ref[0, 0:1, :]
    u = jnp.dot(h.astype(BF16), win_ref[...], preferred_element_type=F32)
    a = u[:, :D_CONV]
    gt = u[:, D_CONV:2 * D_CONV]
    z = u[:, 2 * D_CONV:]
    v_ref[0] = (a * jax.nn.sigmoid(gt)).astype(v_ref.dtype)
    zp = jnp.dot(perm_ref[...], z.astype(BF16), preferred_element_type=F32).astype(BF16)
    zcs = jnp.dot(zp, cs_ref[...], preferred_element_type=F32).astype(BF16)
    z_ref[0] = zcs.reshape(s2, p1l, zcs.shape[-1])


def _inproj(x, mod3, g_mix_pre, w_in_b, cs_b):
    b, s, d = x.shape
    s2 = s // DFT_S1
    p1l = 16
    ts = p1l * s2
    r = np.arange(ts)
    perm = np.zeros((ts, ts), np.float32)
    perm[(r % s2) * p1l + r // s2, r] = 1.0
    perm = jnp.asarray(perm, BF16)
    return pl.pallas_call(
        functools.partial(_inproj_kernel, s2=s2, p1l=p1l),
        grid=(b, s // ts),
        in_specs=[pl.BlockSpec((1, ts, d), lambda i, t: (i, t, 0)),
                  pl.BlockSpec((1, 6, d), lambda i, t: (i, 0, 0)),
                  pl.BlockSpec((1, d), lambda i, t: (0, 0)),
                  pl.BlockSpec(w_in_b.shape, lambda i, t: (0, 0)),
                  pl.BlockSpec(cs_b.shape, lambda i, t: (0, 0)),
                  pl.BlockSpec((ts, ts), lambda i, t: (0, 0))],
        out_specs=[pl.BlockSpec((1, ts, D_CONV), lambda i, t: (i, t, 0)),
                   pl.BlockSpec((1, s2, p1l, 2 * D_FNET), lambda i, t: (i, 0, t, 0))],
        out_shape=[jax.ShapeDtypeStruct((b, s, D_CONV), BF16),
                   jax.ShapeDtypeStruct((b, s2, DFT_S1, 2 * D_FNET), BF16)],
        compiler_params=_cparams(("parallel", "parallel")),
        name="inproj",
    )(x, mod3, g_mix_pre, w_in_b, cs_b, perm)


def _conv_kernel(vp_ref, v_ref, vn_ref, w_ref, b_ref, lg_ref, lb_ref, go_ref, o_ref, pad_ref, sh_ref, *, ts, rc):
    t = pl.program_id(1)
    nt = pl.num_programs(1)
    pad_ref[0:HALO, :] = jnp.where(t > 0, vp_ref[0].astype(F32), 0.0)
    pad_ref[HALO:HALO + ts, :] = v_ref[0].astype(F32)
    pad_ref[HALO + ts:HALO + ts + HALO, :] = jnp.where(t < nt - 1, vn_ref[0].astype(F32), 0.0)
    span = ts + 2 * HALO - 8
    for m in range(8):
        sh_ref[m] = pad_ref[m:m + span, :]
    off = HALO - CONV_WIDTH // 2
    for c in range(ts // rc):
        r0 = c * rc
        acc = jnp.zeros((rc // 8, 8, D_CONV), F32)
        for j in range(CONV_WIDTH):
            m, q = (off + j) % 8, (off + j) // 8
            tap = sh_ref[m, r0 + 8 * q:r0 + 8 * q + rc, :].reshape(rc // 8, 8, D_CONV)
            acc = acc + tap * w_ref[j][None]
        acc = acc.reshape(rc, D_CONV) + b_ref[...]
        mu = jnp.mean(acc, axis=-1, keepdims=True)
        xc = acc - mu
        var = jnp.mean(xc * xc, axis=-1, keepdims=True)
        y = xc * lax.rsqrt(var + EPS) * lg_ref[...] + lb_ref[...]
        y = _silu(y)
        y = _rms(y) * go_ref[...]
        o_ref[0, r0:r0 + rc, :] = y.astype(o_ref.dtype)


def _conv(v, conv_w, conv_b, ln_g, ln_b, g_out, ts, rc=32):
    b, s, c = v.shape
    hb = ts // HALO
    nh = s // HALO
    vec = pl.BlockSpec((1, c), lambda i, t: (0, 0))
    return pl.pallas_call(
        functools.partial(_conv_kernel, ts=ts, rc=rc),
        grid=(b, s // ts),
        in_specs=[pl.BlockSpec((1, HALO, c), lambda i, t: (i, jnp.maximum(t * hb - 1, 0), 0)),
                  pl.BlockSpec((1, ts, c), lambda i, t: (i, t, 0)),
                  pl.BlockSpec((1, HALO, c), lambda i, t: (i, jnp.minimum((t + 1) * hb, nh - 1), 0)),
                  pl.BlockSpec((CONV_WIDTH, 8, c), lambda i, t: (0, 0, 0)),
                  vec, vec, vec, vec],
        out_specs=pl.BlockSpec((1, ts, c), lambda i, t: (i, t, 0)),
        out_shape=jax.ShapeDtypeStruct((b, s, c), BF16),
        scratch_shapes=[pltpu.VMEM((ts + 2 * HALO, c), F32), pltpu.VMEM((8, ts + 2 * HALO - 8, c), F32)],
        compiler_params=_cparams(("parallel", "parallel")),
        name="conv",
    )(v, v, v, jnp.broadcast_to(conv_w[:, None, :], (CONV_WIDTH, 8, c)), conv_b, ln_g, ln_b, g_out)


def _dft_tables(s):
    s1 = DFT_S1
    s2 = s // s1
    k1 = np.arange(s1)[None, :, None]
    p1 = np.arange(s1)[None, None, :]
    p2 = np.arange(s2)[:, None, None]
    ang = 2.0 * np.pi * ((k1 * (s2 * p1 + p2)) % s) / s
    g = np.concatenate([np.cos(ang), np.sin(ang)], axis=1)
    k2 = np.arange(s2)[:, None]
    q2 = np.arange(s2)[None, :]
    ang2 = 2.0 * np.pi * ((k2 * q2) % s2) / s2
    h = np.concatenate([np.cos(ang2), np.sin(ang2)], axis=1) / np.sqrt(s)
    return jnp.asarray(g, BF16), jnp.asarray(h, BF16)


def _seqdft_kernel(z_ref, g_ref, h_ref, o_ref, scr_ref, *, s1, s2):
    ch = DFT_CH
    for p2 in range(s2):
        r = jnp.dot(g_ref[p2], z_ref[0, p2], preferred_element_type=F32)
        a_re = r[:s1, :ch] - r[s1:, ch:]
        a_im = -(r[:s1, ch:] + r[s1:, :ch])
        scr_ref[:, p2, :] = a_re
        scr_ref[:, s2 + p2, :] = a_im
    hmat = h_ref[...]
    for k1 in range(s1):
        y = jnp.dot(hmat, scr_ref[k1].astype(BF16), preferred_element_type=F32)
        o_ref[0, pl.ds(k1, s2, stride=s1), :] = y


def _seqdft(zp, g_tab, h_tab):
    b, s2, s1, _ = zp.shape
    s = s1 * s2
    nch = D_FNET // DFT_CH
    return pl.pallas_call(
        functools.partial(_seqdft_kernel, s1=s1, s2=s2),
        grid=(b, nch),
        in_specs=[pl.BlockSpec((1, s2, s1, 2 * DFT_CH), lambda i, c: (i, 0, 0, c)),
                  pl.BlockSpec(g_tab.shape, lambda i, c: (0, 0, 0)),
                  pl.BlockSpec(h_tab.shape, lambda i, c: (0, 0))],
        out_specs=pl.BlockSpec((1, s, DFT_CH), lambda i, c: (i, 0, c)),
        out_shape=jax.ShapeDtypeStruct((b, s, D_FNET), F32),
        scratch_shapes=[pltpu.VMEM((s1, 2 * s2, DFT_CH), F32)],
        compiler_params=_cparams(("parallel", "parallel")),
        name="seqdft",
    )(zp, g_tab, h_tab)


def _mix_kernel(x_ref, cn_ref, fy_ref, mod_ref, gf_ref, wout_ref, gpost_ref, gpre_ref,
                wsg_ref, wsu_ref, wsd_ref, x1_ref, h2_ref, ysh_ref):
    fn = _rms(fy_ref[0]) * gf_ref[...]
    mixed = jnp.dot(cn_ref[0], wout_ref[:D_CONV, :], preferred_element_type=F32)
    mixed = mixed + jnp.dot(fn.astype(BF16), wout_ref[D_CONV:, :], preferred_element_type=F32)
    x1 = x_ref[0] + mod_ref[0, 2:3, :] * (_rms(mixed) * gpost_ref[...])
    x1_ref[0] = x1
    h2 = _rms(x1) * gpre_ref[...]
    h2 = h2 * (1.0 + mod_ref[0, 4:5, :]) + mod_ref[0, 3:4, :]
    h2_ref[0] = _pack_bf16_pairs(h2)
    hb = h2.astype(BF16)
    hid = _silu(jnp.dot(hb, wsg_ref[...], preferred_element_type=F32))
    hid = hid * jnp.dot(hb, wsu_ref[...], preferred_element_type=F32)
    ysh_ref[0] = _pack_bf16_pairs(jnp.dot(hid.astype(BF16), wsd_ref[...], preferred_element_type=F32))


def _mix(x, cn, fy, mod3, g_fnet, w_out_b, g_post, g_pre, wsg_b, wsu_b, wsd_b, ts):
    b, s, d = x.shape
    tok = lambda c: pl.BlockSpec((1, ts, c), lambda i, t: (i, t, 0))
    full = lambda a: pl.BlockSpec(a.shape, lambda i, t: (0,) * a.ndim)
    return pl.pallas_call(
        _mix_kernel,
        grid=(b, s // ts),
        in_specs=[tok(d), tok(D_CONV), tok(D_FNET),
                  pl.BlockSpec((1, 6, d), lambda i, t: (i, 0, 0)),
                  full(g_fnet), full(w_out_b), full(g_post), full(g_pre),
                  full(wsg_b), full(wsu_b), full(wsd_b)],
        out_specs=[tok(d), tok(d // 2), tok(d // 2)],
        out_shape=[jax.ShapeDtypeStruct((b, s, d), F32),
                   jax.ShapeDtypeStruct((b, s, d // 2), jnp.uint32),
                   jax.ShapeDtypeStruct((b, s, d // 2), jnp.uint32)],
        compiler_params=_cparams(("parallel", "parallel")),
        name="mix",
    )(x, cn, fy, mod3, g_fnet, w_out_b, g_post, g_pre, wsg_b, wsu_b, wsd_b)


def _router_kernel(h_ref, wr_ref, br_ref, u_ref, idx_ref, rank_ref, wgt_ref, cnt_ref, carry_ref, *, tr):
    e = N_EXPERTS

    @pl.when((pl.program_id(0) == 0) & (pl.program_id(1) == 0))
    def _():
        carry_ref[...] = jnp.zeros_like(carry_ref)

    logits = lax.dot_general(wr_ref[...], _unpack_bf16_pairs(h_ref[0]).astype(BF16), (((1,), (1,)), ((), ())),
                             preferred_element_type=F32)
    sc = jax.nn.sigmoid(logits)
    sb = sc + br_ref[...]
    ninf = jnp.float32(-jnp.inf)

    io_g = lax.broadcasted_iota(jnp.int32, (GROUP_SIZE, tr), 0).astype(F32)
    gs = []
    for g in range(N_EXPERT_GROUPS):
        blk = sb[g * GROUP_SIZE:(g + 1) * GROUP_SIZE]
        m1 = jnp.max(blk, axis=0, keepdims=True)
        i1 = jnp.min(jnp.where(blk == m1, io_g, float(GROUP_SIZE)), axis=0, keepdims=True)
        m2 = jnp.max(jnp.where(io_g == i1, ninf, blk), axis=0, keepdims=True)
        gs.append(m1 + m2)
    masked = []
    for g in range(N_EXPERT_GROUPS):
        beat = jnp.zeros((1, tr), F32)
        for o in range(N_EXPERT_GROUPS):
            if o == g:
                continue
            wins = (gs[o] > gs[g]) | ((gs[o] == gs[g]) & (o < g))
            beat = beat + wins.astype(F32)
        keep = beat < float(TOPK_GROUPS)
        masked.append(jnp.where(keep, sb[g * GROUP_SIZE:(g + 1) * GROUP_SIZE], ninf))
    v = jnp.concatenate(masked, axis=0)

    io_e = lax.broadcasted_iota(jnp.int32, (e, tr), 0).astype(F32)
    ids, ws = [], []
    sel = jnp.zeros((e, tr), F32)
    for _ in range(TOP_K):
        m = jnp.max(v, axis=0, keepdims=True)
        i = jnp.min(jnp.where(v == m, io_e, float(e)), axis=0, keepdims=True)
        oh = io_e == i
        ids.append(i)
        ws.append(jnp.sum(jnp.where(oh, sc, 0.0), axis=0, keepdims=True))
        v = jnp.where(oh, ninf, v)
        sel = sel + oh.astype(F32)

    wsum = ws[0]
    for k in range(1, TOP_K):
        wsum = wsum + ws[k]
    wgt_ref[...] = jnp.concatenate([w / wsum * ROUTED_SCALE for w in ws], axis=0)
    idx_ref[...] = jnp.concatenate(ids, axis=0).astype(jnp.int32)

    excl = jnp.dot(sel.astype(BF16), u_ref[...], preferred_element_type=F32)
    base = carry_ref[:, 0:1]
    rank_full = base + excl
    ranks = [jnp.sum(jnp.where(io_e == ids[k], rank_full, 0.0), axis=0, keepdims=True)
             for k in range(TOP_K)]
    rank_ref[...] = jnp.concatenate(ranks, axis=0).astype(jnp.int32)
    new = base + jnp.sum(sel, axis=1, keepdims=True)
    carry_ref[...] = jnp.broadcast_to(new, carry_ref.shape)
    cnt_ref[...] = jnp.broadcast_to(new, cnt_ref.shape).astype(jnp.int32)


def _router(h2p, wr_t_b, b_router_col, tr):
    b, s, d = h2p.shape
    t = b * s
    nt = s // tr
    u = jnp.asarray(np.triu(np.ones((tr, tr), np.float32), k=1), BF16)
    col = lambda i, j: (0, i * nt + j)
    return pl.pallas_call(
        functools.partial(_router_kernel, tr=tr),
        grid=(b, nt),
        in_specs=[pl.BlockSpec((1, tr, d), lambda i, j: (i, j, 0)),
                  pl.BlockSpec(wr_t_b.shape, lambda i, j: (0, 0)),
                  pl.BlockSpec((N_EXPERTS, 1), lambda i, j: (0, 0)),
                  pl.BlockSpec((tr, tr), lambda i, j: (0, 0))],
        out_specs=[pl.BlockSpec((TOP_K, tr), col), pl.BlockSpec((TOP_K, tr), col),
                   pl.BlockSpec((TOP_K, tr), col),
                   pl.BlockSpec((N_EXPERTS, 128), lambda i, j: (0, 0))],
        out_shape=[jax.ShapeDtypeStruct((TOP_K, t), jnp.int32),
                   jax.ShapeDtypeStruct((TOP_K, t), jnp.int32),
                   jax.ShapeDtypeStruct((TOP_K, t), F32),
                   jax.ShapeDtypeStruct((N_EXPERTS, 128), jnp.int32)],
        scratch_shapes=[pltpu.VMEM((N_EXPERTS, 128), F32)],
        compiler_params=_cparams(("arbitrary", "arbitrary")),
        name="router",
    )(h2p, wr_t_b, b_router_col, u)


def _dest_kernel(pstart_ref, idx_ref, rank_ref, dest_ref):
    idx = idx_ref[...]

    def body(g, acc):
        for j in range(8):
            e = g * 8 + j
            acc = jnp.where(idx == e, pstart_ref[e], acc)
        return acc

    dest_ref[...] = lax.fori_loop(0, N_EXPERTS // 8, body, jnp.zeros_like(idx)) + rank_ref[...]


def _dest(pstarts, idx, rank, tl):
    k, t = idx.shape
    grid_spec = pltpu.PrefetchScalarGridSpec(
        num_scalar_prefetch=1,
        grid=(t // tl,),
        in_specs=[pl.BlockSpec((k, tl), lambda i, ps: (0, i)), pl.BlockSpec((k, tl), lambda i, ps: (0, i))],
        out_specs=pl.BlockSpec((k, tl), lambda i, ps: (0, i)),
    )
    return pl.pallas_call(
        _dest_kernel,
        grid_spec=grid_spec,
        out_shape=jax.ShapeDtypeStruct((k, t), jnp.int32),
        compiler_params=_cparams(("parallel",)),
        name="dest",
    )(pstarts, idx, rank)


def _sc_mesh():
    return plsc.VectorSubcoreMesh(core_axis_name="c", subcore_axis_name="s",
                                  num_cores=SC_CORES, num_subcores=SC_SUBCORES)


def _sc_worker_base(per_worker):
    return (lax.axis_index("s") * SC_CORES + lax.axis_index("c")) * per_worker


def _dispatch(h2_flat, dest, p_rows):
    t, dh = h2_flat.shape
    r = SC_ROWS
    per_w = t // SC_WORKERS
    nchunk = per_w // r
    assert per_w % (2 * r) == 0

    @functools.partial(
        pl.kernel, mesh=_sc_mesh(),
        out_type=jax.ShapeDtypeStruct((p_rows, dh), h2_flat.dtype),
        scratch_types=[pltpu.VMEM((2, TOP_K, r), jnp.int32), pltpu.VMEM((2, r, dh), h2_flat.dtype),
                       pltpu.SemaphoreType.DMA((2,)), pltpu.SemaphoreType.DMA((2,))],
        name="sc_dispatch",
    )
    def k(rows_hbm, dest_hbm, out_hbm, idx_v, rows_v, lsem, ssem):
        base = _sc_worker_base(per_w)

        def load(ci, slot):
            t0 = base + ci * r
            for kk in range(TOP_K):
                pltpu.sync_copy(dest_hbm.at[kk, pl.ds(t0, r)], idx_v.at[slot, kk])
            pltpu.async_copy(rows_hbm.at[pl.ds(t0, r)], rows_v.at[slot], lsem.at[slot])

        def scatter(ci, slot):
            t0 = base + ci * r
            pltpu.make_async_copy(rows_hbm.at[pl.ds(t0, r)], rows_v.at[slot], lsem.at[slot]).wait()
            for kk in range(TOP_K):
                pltpu.async_copy(rows_v.at[slot], out_hbm.at[idx_v.at[slot, kk]], ssem.at[slot])

        def drain(slot):
            for kk in range(TOP_K):
                pltpu.make_async_copy(rows_v.at[slot], out_hbm.at[idx_v.at[slot, kk]], ssem.at[slot]).wait()

        load(0, 0)

        @pl.loop(0, nchunk, step=2)
        def _(c0):
            for s in range(2):
                ci = c0 + s

                @pl.when(ci + 1 < nchunk)
                def _():
                    @pl.when(ci >= 1)
                    def _():
                        drain(1 - s)
                    load(ci + 1, 1 - s)

                scatter(ci, s)

        drain(0)
        drain(1)

    return k(h2_flat, dest)


def _gather_sum(obuf, dest, wgt):
    _, dh = obuf.shape
    kk_n, t = dest.shape
    r, lanes = SC_SUM_TOKENS, SC_LANES
    per_w = t // SC_WORKERS
    nchunk = per_w // r
    assert per_w % (2 * r) == 0 and per_w % 128 == 0
    nj = dh // lanes

    @functools.partial(
        pl.kernel, mesh=_sc_mesh(),
        out_type=jax.ShapeDtypeStruct((t, 2 * dh), F32),
        scratch_types=[pltpu.VMEM((kk_n, per_w), jnp.int32), pltpu.VMEM((2, kk_n, r, dh), obuf.dtype),
                       pltpu.VMEM((kk_n, per_w), F32), pltpu.VMEM((2, r, 2 * dh), F32),
                       pltpu.SemaphoreType.DMA((2,)), pltpu.SemaphoreType.DMA((2,))],
        compiler_params=pltpu.CompilerParams(needs_layout_passes=False),
        name="sc_gather_sum",
    )
    def k(table_hbm, idx_hbm, w_hbm, out_hbm, idx_v, rows_v, w_v, out_v, gsem, wsem):
        base = _sc_worker_base(per_w)
        pltpu.sync_copy(idx_hbm.at[:, pl.ds(base, per_w)], idx_v)
        pltpu.sync_copy(w_hbm.at[:, pl.ds(base, per_w)], w_v)

        def idx_list(ci, kk):
            return idx_v.at[kk, pl.ds(pl.multiple_of(ci * r, 8), r)]

        def copies(ci, slot):
            for kk in range(kk_n):
                yield pltpu.make_async_copy(table_hbm.at[idx_list(ci, kk)], rows_v.at[slot, kk], gsem.at[slot])

        def out_copy(ci, slot):
            return pltpu.make_async_copy(out_v.at[slot], out_hbm.at[pl.ds(base + ci * r, r)], wsem.at[slot])

        def compute(ci, slot):
            @pl.loop(0, r)
            def _(i):
                tok = jnp.zeros((lanes,), jnp.int32) + (ci * r + i)
                ws = [plsc.load_gather(w_v, [jnp.full((lanes,), kk, jnp.int32), tok]) for kk in range(kk_n)]

                @plsc.parallel_loop(0, nj, unroll=4)
                def _(j):
                    col = pl.multiple_of(j * lanes, lanes)
                    lo = jnp.zeros((lanes,), F32)
                    hi = jnp.zeros((lanes,), F32)
                    for kk in range(kk_n):
                        v = rows_v[slot, kk, i, pl.ds(col, lanes)]
                        lo = lo + ws[kk] * plsc.bitcast(v << 16, F32)
                        hi = hi + ws[kk] * plsc.bitcast(v & jnp.uint32(0xFFFF0000), F32)
                    out_v[slot, i, pl.ds(col, lanes)] = lo
                    out_v[slot, i, pl.ds(dh + col, lanes)] = hi

        for c in copies(0, 0):
            c.start()

        @pl.loop(0, nchunk, step=2)
        def _(c0):
            for s in range(2):
                ci = c0 + s

                @pl.when(ci + 1 < nchunk)
                def _():
                    for c in copies(ci + 1, 1 - s):
                        c.start()

                for c in copies(ci, s):
                    c.wait()

                @pl.when(ci >= 2)
                def _():
                    out_copy(ci - 2, s).wait()

                compute(ci, s)
                out_copy(ci, s).start()

        out_copy(nchunk - 2, 0).wait()
        out_copy(nchunk - 1, 1).wait()

    return k(obuf, dest, wgt)


def _experts_kernel(be_ref, nvalid_ref, run_ref, nxt_ref, nused_ref, *refs, bm):
    ns = EXPERT_STREAMS
    xq_refs = refs[:ns]
    wg_hbm, wu_hbm, wd_hbm, o_hbm, wg_f, wu_f, wd_f, wg_s, wu_s, wd_s, xs, ob, sem, osem = refs[ns:]
    b = pl.program_id(0)
    nb = pl.num_programs(0)
    nused = nused_ref[0]
    bq = bm // ns

    def weight_copies(e, slot):
        return (pltpu.make_async_copy(wg_hbm.at[e], wg_f.at[slot], sem.at[slot]),
                pltpu.make_async_copy(wu_hbm.at[e], wu_f.at[slot], sem.at[slot]),
                pltpu.make_async_copy(wd_hbm.at[e], wd_f.at[slot], sem.at[slot]))

    def out_copies(blk, slot):
        return [pltpu.make_async_copy(ob.at[slot, pl.ds(q * bq, bq)],
                                      o_hbm.at[pl.ds(pl.multiple_of(blk * bm + q * bq, 8), bq)], osem.at[slot])
                for q in range(ns)]

    @pl.when((b >= 2) & (b - 2 < nused))
    def _():
        for c in out_copies(b - 2, b % 2):
            c.wait()

    @pl.when(b < nused)
    def _():
        e = be_ref[b]
        slot = run_ref[b] % 2

        @pl.when(b == 0)
        def _():
            for c in weight_copies(e, slot):
                c.start()

        @pl.when((b == 0) | (e != be_ref[jnp.maximum(b - 1, 0)]))
        def _():
            for c in weight_copies(e, slot):
                c.wait()

            @pl.when(nxt_ref[b] >= 0)
            def _():
                for c in weight_copies(nxt_ref[b], 1 - slot):
                    c.start()

            wg_s[...] = wg_f[slot].astype(BF16)
            wu_s[...] = wu_f[slot].astype(BF16)
            wd_s[...] = wd_f[slot].astype(BF16)

        rows = lax.broadcasted_iota(jnp.int32, xq_refs[0].shape, 0)
        for q in range(ns):
            xp = jnp.where(rows + q * bq < nvalid_ref[b], xq_refs[q][...], jnp.uint32(0))
            xs[q * bq:(q + 1) * bq, :] = _unpack_bf16_pairs(xp).astype(BF16)
        x = xs[...]
        g = jnp.dot(x, wg_s[...], preferred_element_type=F32)
        u = jnp.dot(x, wu_s[...], preferred_element_type=F32)
        hid = (_silu(g) * u).astype(BF16)
        ob[b % 2] = _pack_bf16_pairs(jnp.dot(hid, wd_s[...], preferred_element_type=F32))
        for c in out_copies(b, b % 2):
            c.start()

    @pl.when(b == nb - 1)
    def _():
        @pl.when((b >= 1) & (b - 1 < nused))
        def _():
            for c in out_copies(b - 1, 1 - b % 2):
                c.wait()

        @pl.when(b < nused)
        def _():
            for c in out_copies(b, b % 2):
                c.wait()


def _experts(xbuf, block_e, nvalid, run, nxt, nused, w_gate, w_up, w_down, bm):
    p, dh = xbuf.shape
    nb = p // bm
    d, de = w_gate.shape[1:]
    ns = EXPERT_STREAMS
    bq = bm // ns
    hbm = pl.BlockSpec(memory_space=pl.ANY)

    def quarter(q):
        return pl.BlockSpec((bq, dh), lambda b, be, nv, rn, nx, nu: (jnp.minimum(b, nu[0] - 1) * ns + q, 0))

    grid_spec = pltpu.PrefetchScalarGridSpec(
        num_scalar_prefetch=5,
        grid=(nused[0],),
        in_specs=[quarter(q) for q in range(ns)] + [hbm, hbm, hbm],
        out_specs=hbm,
        scratch_shapes=[pltpu.VMEM((2, d, de), F32), pltpu.VMEM((2, d, de), F32), pltpu.VMEM((2, de, d), F32),
                        pltpu.VMEM((d, de), BF16), pltpu.VMEM((d, de), BF16), pltpu.VMEM((de, d), BF16),
                        pltpu.VMEM((bm, 2 * dh), BF16), pltpu.VMEM((2, bm, dh), jnp.uint32),
                        pltpu.SemaphoreType.DMA((2,)), pltpu.SemaphoreType.DMA((2,))],
    )
    return pl.pallas_call(
        functools.partial(_experts_kernel, bm=bm),
        grid_spec=grid_spec,
        out_shape=jax.ShapeDtypeStruct((p, dh), jnp.uint32),
        compiler_params=_cparams(("arbitrary",)),
        name="experts",
    )(block_e, nvalid, run, nxt, nused, *([xbuf] * ns), w_gate, w_up, w_down)


def _combine_kernel(y_ref, x1_ref, ysh_ref, mod_ref, g_ref, out_ref):
    y = y_ref[...] + _unpack_bf16_pairs(ysh_ref[...])
    out_ref[...] = x1_ref[...] + mod_ref[0, 5:6, :] * (_rms(y) * g_ref[...])


def _combine(y_routed, x1_flat, ysh_flat, mod3, g_post, s, tc):
    t, d = x1_flat.shape
    per_seq = s // tc
    tok = pl.BlockSpec((tc, d), lambda i: (i, 0))
    return pl.pallas_call(
        _combine_kernel,
        grid=(t // tc,),
        in_specs=[tok, tok, pl.BlockSpec((tc, d // 2), lambda i: (i, 0)),
                  pl.BlockSpec((1, 6, d), lambda i: (i // per_seq, 0, 0)),
                  pl.BlockSpec((1, d), lambda i: (0, 0))],
        out_specs=tok,
        out_shape=jax.ShapeDtypeStruct((t, d), F32),
        compiler_params=_cparams(("parallel",)),
        name="combine",
    )(y_routed, x1_flat, ysh_flat, mod3, g_post)


def _channel_dft_table():
    c = np.arange(GROUP_DIM)
    ang = 2.0 * np.pi * ((c[:, None] * c[None, :]) % GROUP_DIM) / GROUP_DIM
    eye = np.eye(D_FNET // GROUP_DIM)
    scale = 1.0 / np.sqrt(GROUP_DIM)
    cos_m, sin_m = np.kron(eye, np.cos(ang)) * scale, np.kron(eye, np.sin(ang)) * scale
    cols = [m[:, c * DFT_CH:(c + 1) * DFT_CH] for c in range(D_FNET // DFT_CH) for m in (cos_m, sin_m)]
    return jnp.asarray(np.concatenate(cols, axis=1), BF16)


EXPERT_BLOCK_MAX = 1280
EXPERT_BLOCK_ALIGN = 64


def _expert_block_rows(n_assign):
    target = max(n_assign // N_EXPERTS * 9 // 8, 2 * EXPERT_BLOCK_ALIGN)
    k = -(-target // EXPERT_BLOCK_MAX)
    return -(-target // (k * EXPERT_BLOCK_ALIGN)) * EXPERT_BLOCK_ALIGN


def _layer(x, mod, p):
    b, s, d = x.shape
    t = b * s
    mod3 = mod.reshape(b, 6, d)
    g_tab, h_tab = _dft_tables(s)

    v, zcs = _inproj(x, mod3, p["g_mix_pre"], p["w_in_b"], _channel_dft_table())
    tile = min(TOKEN_TILE, s)
    cn = _conv(v, p["conv_w"], p["conv_b"], p["conv_ln_g"], p["conv_ln_b"], p["g_conv_out"], tile)
    fy = _seqdft(zcs, g_tab, h_tab)
    x1, h2, ysh = _mix(x, cn, fy, mod3, p["g_fnet_out"], p["w_out_b"], p["g_mix_post"], p["g_ffn_pre"],
                       p["wsg_b"], p["wsu_b"], p["wsd_b"], tile)
    idx, rank, wgt, cnt = _router(h2, p["wr_t_b"], p["b_router_col"], tile)

    n = t * TOP_K
    bm = _expert_block_rows(n)
    counts = cnt[:, 0]
    pcounts = (counts + bm - 1) // bm * bm
    pends = jnp.cumsum(pcounts)
    pstarts = pends - pcounts
    dest = _dest(pstarts.astype(jnp.int32), idx, rank, min(DEST_TILE, t))
    nb = (n + N_EXPERTS * (bm - 1) + bm - 1) // bm
    nused = (pends[-1] // bm).astype(jnp.int32)
    blk = jnp.minimum(jnp.arange(nb, dtype=jnp.int32), nused - 1) * bm
    block_e = jnp.sum((pends[None, :] <= blk[:, None]).astype(jnp.int32), axis=1)
    block_e = jnp.minimum(block_e, N_EXPERTS - 1)
    nvalid = jnp.clip(pstarts[block_e] + counts[block_e] - blk, 0, bm).astype(jnp.int32)
    first = jnp.concatenate([jnp.ones((1,), jnp.int32), (block_e[1:] != block_e[:-1]).astype(jnp.int32)])
    run = jnp.cumsum(first) - 1
    eid = jnp.arange(N_EXPERTS, dtype=jnp.int32)
    later = lax.cummin(jnp.where(pcounts > 0, eid, N_EXPERTS)[::-1])[::-1]
    nxt_e = jnp.concatenate([later[1:], jnp.full((1,), N_EXPERTS, jnp.int32)])
    nxt = jnp.where(nxt_e < N_EXPERTS, nxt_e, -1)[block_e].astype(jnp.int32)

    xbuf = _dispatch(h2.reshape(t, d // 2), dest, nb * bm)
    obuf = _experts(xbuf, block_e, nvalid, run.astype(jnp.int32), nxt, nused.reshape(1),
                    p["w_gate"], p["w_up"], p["w_down"], bm)
    y_routed = _gather_sum(obuf, dest, wgt)
    out = _combine(y_routed, x1.reshape(t, d), ysh.reshape(t, d // 2), mod3, p["g_ffn_post"], s,
                   min(COMBINE_TILE, s))
    return out.reshape(b, s, d)


def kernel(x_prompt, x_sample, c_prompt, c_sample, w_ada, b_ada, g_mix_pre, w_in, conv_w, conv_b, conv_ln_g, conv_ln_b, g_conv_out, g_fnet_out, w_out, g_mix_post, g_ffn_pre, w_router, b_router, w_gate, w_up, w_down, ws_gate, ws_up, ws_down, g_ffn_post):
    assert w_ada.shape[0] == 1, "single-layer kernel"
    bp, bs = c_prompt.shape[0], c_sample.shape[0]
    rows = -(-(bp + bs) // 8) * 8
    c_all = jnp.zeros((rows, D_MODEL), F32).at[:bp].set(c_prompt).at[bp:bp + bs].set(c_sample)
    mod = _ada(c_all, w_ada[0], b_ada)
    p = {
        "g_mix_pre": g_mix_pre, "w_in_b": w_in[0].astype(BF16),
        "conv_w": conv_w[0], "conv_b": conv_b, "conv_ln_g": conv_ln_g, "conv_ln_b": conv_ln_b,
        "g_conv_out": g_conv_out, "g_fnet_out": g_fnet_out, "w_out_b": w_out[0].astype(BF16),
        "g_mix_post": g_mix_post, "g_ffn_pre": g_ffn_pre,
        "wr_t_b": w_router[0].T.astype(BF16), "b_router_col": b_router[0][:, None],
        "w_gate": w_gate[0], "w_up": w_up[0], "w_down": w_down[0],
        "wsg_b": ws_gate[0].astype(BF16), "wsu_b": ws_up[0].astype(BF16), "wsd_b": ws_down[0].astype(BF16),
        "g_ffn_post": g_ffn_post,
    }
    y_prompt = _layer(x_prompt, mod[:bp], p)
    y_sample = _layer(x_sample, mod[bp:bp + bs], p)
    return (y_prompt, y_sample)
```

```python
import functools

import numpy as np
import jax
import jax.numpy as jnp
from jax import lax
from jax.experimental import pallas as pl
from jax.experimental.pallas import tpu as pltpu
from jax.experimental.pallas import tpu_sc as plsc

F32 = jnp.float32
BF16 = jnp.bfloat16

D_MODEL = 1024
D_CONV = 512
D_FNET = 512
GROUP_DIM = 64
CONV_WIDTH = 31
N_EXPERTS = 256
TOP_K = 8
N_EXPERT_GROUPS = 8
GROUP_SIZE = N_EXPERTS // N_EXPERT_GROUPS
TOPK_GROUPS = 4
ROUTED_SCALE = 2.5
EPS = 1e-6

DFT_S1 = 128
DFT_CH = 128
HALO = 16
VMEM_LIMIT = 56 * 1024 * 1024
TOKEN_TILE = 1024
COMBINE_TILE = 1024
DEST_TILE = 2048
SC_CORES = 2
SC_SUBCORES = 16
SC_WORKERS = SC_CORES * SC_SUBCORES
SC_ROWS = 64
SC_LANES = 16
SC_SUM_TOKENS = 8
EXPERT_STREAMS = 4

def _cparams(sem, vmem=None):
    return pltpu.CompilerParams(dimension_semantics=sem, vmem_limit_bytes=vmem or VMEM_LIMIT)


def _rms(x):
    return x * lax.rsqrt(jnp.mean(x * x, axis=-1, keepdims=True) + EPS)


def _silu(x):
    return x * jax.nn.sigmoid(x)


def _pack_bf16_pairs(x):
    c = x.shape[-1] // 2
    bits = lax.bitcast_convert_type(x.astype(BF16).astype(F32), jnp.uint32)
    return (bits[:, :c] >> 16) | bits[:, c:]


def _unpack_bf16_pairs(p):
    lo = lax.bitcast_convert_type(p << 16, F32)
    hi = lax.bitcast_convert_type(p & jnp.uint32(0xFFFF0000), F32)
    return jnp.concatenate([lo, hi], axis=-1)


def _ada_kernel(c_ref, w_ref, b_ref, o_ref):
    o_ref[...] = jnp.dot(_silu(c_ref[...]), w_ref[...], preferred_element_type=F32) + b_ref[...]


def _ada(c, w_ada, b_ada):
    bp, d = c.shape
    n = w_ada.shape[1]
    return pl.pallas_call(
        _ada_kernel,
        grid=(n // d,),
        in_specs=[pl.BlockSpec((bp, d), lambda j: (0, 0)),
                  pl.BlockSpec((d, d), lambda j: (0, j)),
                  pl.BlockSpec((1, d), lambda j: (0, j))],
        out_specs=pl.BlockSpec((bp, d), lambda j: (0, j)),
        out_shape=jax.ShapeDtypeStruct((bp, n), F32),
        compiler_params=_cparams(("parallel",)),
        name="ada",
    )(c, w_ada, b_ada)


def _inproj_kernel(x_ref, mod_ref, g_ref, win_ref, cs_ref, perm_ref, v_ref, z_ref, *, s2, p1l):
    x = x_ref[0]
    h = _rms(x) * g_ref[...]
    h = h * (1.0 + mod_ref[0, 1:2, :]) + mod_ref[0, 0:1, :]
    u = jnp.dot(h.astype(BF16), win_ref[...], preferred_element_type=F32)
    a = u[:, :D_CONV]
    gt = u[:, D_CONV:2 * D_CONV]
    z = u[:, 2 * D_CONV:]
    v_ref[0] = (a * jax.nn.sigmoid(gt)).astype(v_ref.dtype)
    zp = jnp.dot(perm_ref[...], z.astype(BF16), preferred_element_type=F32).astype(BF16)
    zcs = jnp.dot(zp, cs_ref[...], preferred_element_type=F32).astype(BF16)
    z_ref[0] = zcs.reshape(s2, p1l, zcs.shape[-1])


def _inproj(x, mod3, g_mix_pre, w_in_b, cs_b):
    b, s, d = x.shape
    s2 = s // DFT_S1
    p1l = 16
    ts = p1l * s2
    r = np.arange(ts)
    perm = np.zeros((ts, ts), np.float32)
    perm[(r % s2) * p1l + r // s2, r] = 1.0
    perm = jnp.asarray(perm, BF16)
    return pl.pallas_call(
        functools.partial(_inproj_kernel, s2=s2, p1l=p1l),
        grid=(b, s // ts),
        in_specs=[pl.BlockSpec((1, ts, d), lambda i, t: (i, t, 0)),
                  pl.BlockSpec((1, 6, d), lambda i, t: (i, 0, 0)),
                  pl.BlockSpec((1, d), lambda i, t: (0, 0)),
                  pl.BlockSpec(w_in_b.shape, lambda i, t: (0, 0)),
                  pl.BlockSpec(cs_b.shape, lambda i, t: (0, 0)),
                  pl.BlockSpec((ts, ts), lambda i, t: (0, 0))],
        out_specs=[pl.BlockSpec((1, ts, D_CONV), lambda i, t: (i, t, 0)),
                   pl.BlockSpec((1, s2, p1l, 2 * D_FNET), lambda i, t: (i, 0, t, 0))],
        out_shape=[jax.ShapeDtypeStruct((b, s, D_CONV), BF16),
                   jax.ShapeDtypeStruct((b, s2, DFT_S1, 2 * D_FNET), BF16)],
        compiler_params=_cparams(("parallel", "parallel")),
        name="inproj",
    )(x, mod3, g_mix_pre, w_in_b, cs_b, perm)


def _conv_kernel(vp_ref, v_ref, vn_ref, w_ref, b_ref, lg_ref, lb_ref, go_ref, o_ref, pad_ref, sh_ref, *, ts, rc):
    t = pl.program_id(1)
    nt = pl.num_programs(1)
    pad_ref[0:HALO, :] = jnp.where(t > 0, vp_ref[0].astype(F32), 0.0)
    pad_ref[HALO:HALO + ts, :] = v_ref[0].astype(F32)
    pad_ref[HALO + ts:HALO + ts + HALO, :] = jnp.where(t < nt - 1, vn_ref[0].astype(F32), 0.0)
    span = ts + 2 * HALO - 8
    for m in range(8):
        sh_ref[m] = pad_ref[m:m + span, :]
    off = HALO - CONV_WIDTH // 2
    for c in range(ts // rc):
        r0 = c * rc
        acc = jnp.zeros((rc // 8, 8, D_CONV), F32)
        for j in range(CONV_WIDTH):
            m, q = (off + j) % 8, (off + j) // 8
            tap = sh_ref[m, r0 + 8 * q:r0 + 8 * q + rc, :].reshape(rc // 8, 8, D_CONV)
            acc = acc + tap * w_ref[j][None]
        acc = acc.reshape(rc, D_CONV) + b_ref[...]
        mu = jnp.mean(acc, axis=-1, keepdims=True)
        xc = acc - mu
        var = jnp.mean(xc * xc, axis=-1, keepdims=True)
        y = xc * lax.rsqrt(var + EPS) * lg_ref[...] + lb_ref[...]
        y = _silu(y)
        y = _rms(y) * go_ref[...]
        o_ref[0, r0:r0 + rc, :] = y.astype(o_ref.dtype)


def _conv(v, conv_w, conv_b, ln_g, ln_b, g_out, ts, rc=32):
    b, s, c = v.shape
    hb = ts // HALO
    nh = s // HALO
    vec = pl.BlockSpec((1, c), lambda i, t: (0, 0))
    return pl.pallas_call(
        functools.partial(_conv_kernel, ts=ts, rc=rc),
        grid=(b, s // ts),
        in_specs=[pl.BlockSpec((1, HALO, c), lambda i, t: (i, jnp.maximum(t * hb - 1, 0), 0)),
                  pl.BlockSpec((1, ts, c), lambda i, t: (i, t, 0)),
                  pl.BlockSpec((1, HALO, c), lambda i, t: (i, jnp.minimum((t + 1) * hb, nh - 1), 0)),
                  pl.BlockSpec((CONV_WIDTH, 8, c), lambda i, t: (0, 0, 0)),
                  vec, vec, vec, vec],
        out_specs=pl.BlockSpec((1, ts, c), lambda i, t: (i, t, 0)),
        out_shape=jax.ShapeDtypeStruct((b, s, c), BF16),
        scratch_shapes=[pltpu.VMEM((ts + 2 * HALO, c), F32), pltpu.VMEM((8, ts + 2 * HALO - 8, c), F32)],
        compiler_params=_cparams(("parallel", "parallel")),
        name="conv",
    )(v, v, v, jnp.broadcast_to(conv_w[:, None, :], (CONV_WIDTH, 8, c)), conv_b, ln_g, ln_b, g_out)


def _dft_tables(s):
    s1 = DFT_S1
    s2 = s // s1
    k1 = np.arange(s1)[None, :, None]
    p1 = np.arange(s1)[None, None, :]
    p2 = np.arange(s2)[:, None, None]
    ang = 2.0 * np.pi * ((k1 * (s2 * p1 + p2)) % s) / s
    g = np.concatenate([np.cos(ang), np.sin(ang)], axis=1)
    k2 = np.arange(s2)[:, None]
    q2 = np.arange(s2)[None, :]
    ang2 = 2.0 * np.pi * ((k2 * q2) % s2) / s2
    h = np.concatenate([np.cos(ang2), np.sin(ang2)], axis=1) / np.sqrt(s)
    return jnp.asarray(g, BF16), jnp.asarray(h, BF16)


def _seqdft_kernel(z_ref, g_ref, h_ref, o_ref, scr_ref, *, s1, s2):
    ch = DFT_CH
    for p2 in range(s2):
        r = jnp.dot(g_ref[p2], z_ref[0, p2], preferred_element_type=F32)
        a_re = r[:s1, :ch] - r[s1:, ch:]
        a_im = -(r[:s1, ch:] + r[s1:, :ch])
        scr_ref[:, p2, :] = a_re
        scr_ref[:, s2 + p2, :] = a_im
    hmat = h_ref[...]
    for k1 in range(s1):
        y = jnp.dot(hmat, scr_ref[k1].astype(BF16), preferred_element_type=F32)
        o_ref[0, pl.ds(k1, s2, stride=s1), :] = y


def _seqdft(zp, g_tab, h_tab):
    b, s2, s1, _ = zp.shape
    s = s1 * s2
    nch = D_FNET // DFT_CH
    return pl.pallas_call(
        functools.partial(_seqdft_kernel, s1=s1, s2=s2),
        grid=(b, nch),
        in_specs=[pl.BlockSpec((1, s2, s1, 2 * DFT_CH), lambda i, c: (i, 0, 0, c)),
                  pl.BlockSpec(g_tab.shape, lambda i, c: (0, 0, 0)),
                  pl.BlockSpec(h_tab.shape, lambda i, c: (0, 0))],
        out_specs=pl.BlockSpec((1, s, DFT_CH), lambda i, c: (i, 0, c)),
        out_shape=jax.ShapeDtypeStruct((b, s, D_FNET), F32),
        scratch_shapes=[pltpu.VMEM((s1, 2 * s2, DFT_CH), F32)],
        compiler_params=_cparams(("parallel", "parallel")),
        name="seqdft",
    )(zp, g_tab, h_tab)


def _mix_kernel(x_ref, cn_ref, fy_ref, mod_ref, gf_ref, wout_ref, gpost_ref, gpre_ref,
                wsg_ref, wsu_ref, wsd_ref, x1_ref, h2_ref, ysh_ref):
    fn = _rms(fy_ref[0]) * gf_ref[...]
    mixed = jnp.dot(cn_ref[0], wout_ref[:D_CONV, :], preferred_element_type=F32)
    mixed = mixed + jnp.dot(fn.astype(BF16), wout_ref[D_CONV:, :], preferred_element_type=F32)
    x1 = x_ref[0] + mod_ref[0, 2:3, :] * (_rms(mixed) * gpost_ref[...])
    x1_ref[0] = x1
    h2 = _rms(x1) * gpre_ref[...]
    h2 = h2 * (1.0 + mod_ref[0, 4:5, :]) + mod_ref[0, 3:4, :]
    h2_ref[0] = _pack_bf16_pairs(h2)
    hb = h2.astype(BF16)
    hid = _silu(jnp.dot(hb, wsg_ref[...], preferred_element_type=F32))
    hid = hid * jnp.dot(hb, wsu_ref[...], preferred_element_type=F32)
    ysh_ref[0] = _pack_bf16_pairs(jnp.dot(hid.astype(BF16), wsd_ref[...], preferred_element_type=F32))


def _mix(x, cn, fy, mod3, g_fnet, w_out_b, g_post, g_pre, wsg_b, wsu_b, wsd_b, ts):
    b, s, d = x.shape
    tok = lambda c: pl.BlockSpec((1, ts, c), lambda i, t: (i, t, 0))
    full = lambda a: pl.BlockSpec(a.shape, lambda i, t: (0,) * a.ndim)
    return pl.pallas_call(
        _mix_kernel,
        grid=(b, s // ts),
        in_specs=[tok(d), tok(D_CONV), tok(D_FNET),
                  pl.BlockSpec((1, 6, d), lambda i, t: (i, 0, 0)),
                  full(g_fnet), full(w_out_b), full(g_post), full(g_pre),
                  full(wsg_b), full(wsu_b), full(wsd_b)],
        out_specs=[tok(d), tok(d // 2), tok(d // 2)],
        out_shape=[jax.ShapeDtypeStruct((b, s, d), F32),
                   jax.ShapeDtypeStruct((b, s, d // 2), jnp.uint32),
                   jax.ShapeDtypeStruct((b, s, d // 2), jnp.uint32)],
        compiler_params=_cparams(("parallel", "parallel")),
        name="mix",
    )(x, cn, fy, mod3, g_fnet, w_out_b, g_post, g_pre, wsg_b, wsu_b, wsd_b)


def _router_kernel(h_ref, wr_ref, br_ref, u_ref, idx_ref, rank_ref, wgt_ref, cnt_ref, carry_ref, *, tr):
    e = N_EXPERTS

    @pl.when((pl.program_id(0) == 0) & (pl.program_id(1) == 0))
    def _():
        carry_ref[...] = jnp.zeros_like(carry_ref)

    logits = lax.dot_general(wr_ref[...], _unpack_bf16_pairs(h_ref[0]).astype(BF16), (((1,), (1,)), ((), ())),
                             preferred_element_type=F32)
    sc = jax.nn.sigmoid(logits)
    sb = sc + br_ref[...]
    ninf = jnp.float32(-jnp.inf)

    io_g = lax.broadcasted_iota(jnp.int32, (GROUP_SIZE, tr), 0).astype(F32)
    gs = []
    for g in range(N_EXPERT_GROUPS):
        blk = sb[g * GROUP_SIZE:(g + 1) * GROUP_SIZE]
        m1 = jnp.max(blk, axis=0, keepdims=True)
        i1 = jnp.min(jnp.where(blk == m1, io_g, float(GROUP_SIZE)), axis=0, keepdims=True)
        m2 = jnp.max(jnp.where(io_g == i1, ninf, blk), axis=0, keepdims=True)
        gs.append(m1 + m2)
    masked = []
    for g in range(N_EXPERT_GROUPS):
        beat = jnp.zeros((1, tr), F32)
        for o in range(N_EXPERT_GROUPS):
            if o == g:
                continue
            wins = (gs[o] > gs[g]) | ((gs[o] == gs[g]) & (o < g))
            beat = beat + wins.astype(F32)
        keep = beat < float(TOPK_GROUPS)
        masked.append(jnp.where(keep, sb[g * GROUP_SIZE:(g + 1) * GROUP_SIZE], ninf))
    v = jnp.concatenate(masked, axis=0)

    io_e = lax.broadcasted_iota(jnp.int32, (e, tr), 0).astype(F32)
    ids, ws = [], []
    sel = jnp.zeros((e, tr), F32)
    for _ in range(TOP_K):
        m = jnp.max(v, axis=0, keepdims=True)
        i = jnp.min(jnp.where(v == m, io_e, float(e)), axis=0, keepdims=True)
        oh = io_e == i
        ids.append(i)
        ws.append(jnp.sum(jnp.where(oh, sc, 0.0), axis=0, keepdims=True))
        v = jnp.where(oh, ninf, v)
        sel = sel + oh.astype(F32)

    wsum = ws[0]
    for k in range(1, TOP_K):
        wsum = wsum + ws[k]
    wgt_ref[...] = jnp.concatenate([w / wsum * ROUTED_SCALE for w in ws], axis=0)
    idx_ref[...] = jnp.concatenate(ids, axis=0).astype(jnp.int32)

    excl = jnp.dot(sel.astype(BF16), u_ref[...], preferred_element_type=F32)
    base = carry_ref[:, 0:1]
    rank_full = base + excl
    ranks = [jnp.sum(jnp.where(io_e == ids[k], rank_full, 0.0), axis=0, keepdims=True)
             for k in range(TOP_K)]
    rank_ref[...] = jnp.concatenate(ranks, axis=0).astype(jnp.int32)
    new = base + jnp.sum(sel, axis=1, keepdims=True)
    carry_ref[...] = jnp.broadcast_to(new, carry_ref.shape)
    cnt_ref[...] = jnp.broadcast_to(new, cnt_ref.shape).astype(jnp.int32)


def _router(h2p, wr_t_b, b_router_col, tr):
    b, s, d = h2p.shape
    t = b * s
    nt = s // tr
    u = jnp.asarray(np.triu(np.ones((tr, tr), np.float32), k=1), BF16)
    col = lambda i, j: (0, i * nt + j)
    return pl.pallas_call(
        functools.partial(_router_kernel, tr=tr),
        grid=(b, nt),
        in_specs=[pl.BlockSpec((1, tr, d), lambda i, j: (i, j, 0)),
                  pl.BlockSpec(wr_t_b.shape, lambda i, j: (0, 0)),
                  pl.BlockSpec((N_EXPERTS, 1), lambda i, j: (0, 0)),
                  pl.BlockSpec((tr, tr), lambda i, j: (0, 0))],
        out_specs=[pl.BlockSpec((TOP_K, tr), col), pl.BlockSpec((TOP_K, tr), col),
                   pl.BlockSpec((TOP_K, tr), col),
                   pl.BlockSpec((N_EXPERTS, 128), lambda i, j: (0, 0))],
        out_shape=[jax.ShapeDtypeStruct((TOP_K, t), jnp.int32),
                   jax.ShapeDtypeStruct((TOP_K, t), jnp.int32),
                   jax.ShapeDtypeStruct((TOP_K, t), F32),
                   jax.ShapeDtypeStruct((N_EXPERTS, 128), jnp.int32)],
        scratch_shapes=[pltpu.VMEM((N_EXPERTS, 128), F32)],
        compiler_params=_cparams(("arbitrary", "arbitrary")),
        name="router",
    )(h2p, wr_t_b, b_router_col, u)


def _dest_kernel(pstart_ref, idx_ref, rank_ref, dest_ref):
    idx = idx_ref[...]

    def body(g, acc):
        for j in range(8):
            e = g * 8 + j
            acc = jnp.where(idx == e, pstart_ref[e], acc)
        return acc

    dest_ref[...] = lax.fori_loop(0, N_EXPERTS // 8, body, jnp.zeros_like(idx)) + rank_ref[...]


def _dest(pstarts, idx, rank, tl):
    k, t = idx.shape
    grid_spec = pltpu.PrefetchScalarGridSpec(
        num_scalar_prefetch=1,
        grid=(t // tl,),
        in_specs=[pl.BlockSpec((k, tl), lambda i, ps: (0, i)), pl.BlockSpec((k, tl), lambda i, ps: (0, i))],
        out_specs=pl.BlockSpec((k, tl), lambda i, ps: (0, i)),
    )
    return pl.pallas_call(
        _dest_kernel,
        grid_spec=grid_spec,
        out_shape=jax.ShapeDtypeStruct((k, t), jnp.int32),
        compiler_params=_cparams(("parallel",)),
        name="dest",
    )(pstarts, idx, rank)


def _sc_mesh():
    return plsc.VectorSubcoreMesh(core_axis_name="c", subcore_axis_name="s",
                                  num_cores=SC_CORES, num_subcores=SC_SUBCORES)


def _sc_worker_base(per_worker):
    return (lax.axis_index("s") * SC_CORES + lax.axis_index("c")) * per_worker


def _dispatch(h2_flat, dest, p_rows):
    t, dh = h2_flat.shape
    r = SC_ROWS
    per_w = t // SC_WORKERS
    nchunk = per_w // r
    assert per_w % (2 * r) == 0

    @functools.partial(
        pl.kernel, mesh=_sc_mesh(),
        out_type=jax.ShapeDtypeStruct((p_rows, dh), h2_flat.dtype),
        scratch_types=[pltpu.VMEM((2, TOP_K, r), jnp.int32), pltpu.VMEM((2, r, dh), h2_flat.dtype),
                       pltpu.SemaphoreType.DMA((2,)), pltpu.SemaphoreType.DMA((2,))],
        name="sc_dispatch",
    )
    def k(rows_hbm, dest_hbm, out_hbm, idx_v, rows_v, lsem, ssem):
        base = _sc_worker_base(per_w)

        def load(ci, slot):
            t0 = base + ci * r
            for kk in range(TOP_K):
                pltpu.sync_copy(dest_hbm.at[kk, pl.ds(t0, r)], idx_v.at[slot, kk])
            pltpu.async_copy(rows_hbm.at[pl.ds(t0, r)], rows_v.at[slot], lsem.at[slot])

        def scatter(ci, slot):
            t0 = base + ci * r
            pltpu.make_async_copy(rows_hbm.at[pl.ds(t0, r)], rows_v.at[slot], lsem.at[slot]).wait()
            for kk in range(TOP_K):
                pltpu.async_copy(rows_v.at[slot], out_hbm.at[idx_v.at[slot, kk]], ssem.at[slot])

        def drain(slot):
            for kk in range(TOP_K):
                pltpu.make_async_copy(rows_v.at[slot], out_hbm.at[idx_v.at[slot, kk]], ssem.at[slot]).wait()

        load(0, 0)

        @pl.loop(0, nchunk, step=2)
        def _(c0):
            for s in range(2):
                ci = c0 + s

                @pl.when(ci + 1 < nchunk)
                def _():
                    @pl.when(ci >= 1)
                    def _():
                        drain(1 - s)
                    load(ci + 1, 1 - s)

                scatter(ci, s)

        drain(0)
        drain(1)

    return k(h2_flat, dest)


def _gather_sum(obuf, dest, wgt):
    _, dh = obuf.shape
    kk_n, t = dest.shape
    r, lanes = SC_SUM_TOKENS, SC_LANES
    per_w = t // SC_WORKERS
    nchunk = per_w // r
    assert per_w % (2 * r) == 0 and per_w % 128 == 0
    nj = dh // lanes

    @functools.partial(
        pl.kernel, mesh=_sc_mesh(),
        out_type=jax.ShapeDtypeStruct((t, 2 * dh), F32),
        scratch_types=[pltpu.VMEM((kk_n, per_w), jnp.int32), pltpu.VMEM((2, kk_n, r, dh), obuf.dtype),
                       pltpu.VMEM((kk_n, per_w), F32), pltpu.VMEM((2, r, 2 * dh), F32),
                       pltpu.SemaphoreType.DMA((2,)), pltpu.SemaphoreType.DMA((2,))],
        compiler_params=pltpu.CompilerParams(needs_layout_passes=False),
        name="sc_gather_sum",
    )
    def k(table_hbm, idx_hbm, w_hbm, out_hbm, idx_v, rows_v, w_v, out_v, gsem, wsem):
        base = _sc_worker_base(per_w)
        pltpu.sync_copy(idx_hbm.at[:, pl.ds(base, per_w)], idx_v)
        pltpu.sync_copy(w_hbm.at[:, pl.ds(base, per_w)], w_v)

        def idx_list(ci, kk):
            return idx_v.at[kk, pl.ds(pl.multiple_of(ci * r, 8), r)]

        def copies(ci, slot):
            for kk in range(kk_n):
                yield pltpu.make_async_copy(table_hbm.at[idx_list(ci, kk)], rows_v.at[slot, kk], gsem.at[slot])

        def out_copy(ci, slot):
            return pltpu.make_async_copy(out_v.at[slot], out_hbm.at[pl.ds(base + ci * r, r)], wsem.at[slot])

        def compute(ci, slot):
            @pl.loop(0, r)
            def _(i):
                tok = jnp.zeros((lanes,), jnp.int32) + (ci * r + i)
                ws = [plsc.load_gather(w_v, [jnp.full((lanes,), kk, jnp.int32), tok]) for kk in range(kk_n)]

                @plsc.parallel_loop(0, nj, unroll=4)
                def _(j):
                    col = pl.multiple_of(j * lanes, lanes)
                    lo = jnp.zeros((lanes,), F32)
                    hi = jnp.zeros((lanes,), F32)
                    for kk in range(kk_n):
                        v = rows_v[slot, kk, i, pl.ds(col, lanes)]
                        lo = lo + ws[kk] * plsc.bitcast(v << 16, F32)
                        hi = hi + ws[kk] * plsc.bitcast(v & jnp.uint32(0xFFFF0000), F32)
                    out_v[slot, i, pl.ds(col, lanes)] = lo
                    out_v[slot, i, pl.ds(dh + col, lanes)] = hi

        for c in copies(0, 0):
            c.start()

        @pl.loop(0, nchunk, step=2)
        def _(c0):
            for s in range(2):
                ci = c0 + s

                @pl.when(ci + 1 < nchunk)
                def _():
                    for c in copies(ci + 1, 1 - s):
                        c.start()

                for c in copies(ci, s):
                    c.wait()

                @pl.when(ci >= 2)
                def _():
                    out_copy(ci - 2, s).wait()

                compute(ci, s)
                out_copy(ci, s).start()

        out_copy(nchunk - 2, 0).wait()
        out_copy(nchunk - 1, 1).wait()

    return k(obuf, dest, wgt)


def _experts_kernel(be_ref, nvalid_ref, run_ref, nxt_ref, nused_ref, *refs, bm):
    ns = EXPERT_STREAMS
    xq_refs = refs[:ns]
    wg_hbm, wu_hbm, wd_hbm, o_hbm, wg_f, wu_f, wd_f, wg_s, wu_s, wd_s, xs, ob, sem, osem = refs[ns:]
    b = pl.program_id(0)
    nb = pl.num_programs(0)
    nused = nused_ref[0]
    bq = bm // ns

    def weight_copies(e, slot):
        return (pltpu.make_async_copy(wg_hbm.at[e], wg_f.at[slot], sem.at[slot]),
                pltpu.make_async_copy(wu_hbm.at[e], wu_f.at[slot], sem.at[slot]),
                pltpu.make_async_copy(wd_hbm.at[e], wd_f.at[slot], sem.at[slot]))

    def out_copies(blk, slot):
        return [pltpu.make_async_copy(ob.at[slot, pl.ds(q * bq, bq)],
                                      o_hbm.at[pl.ds(pl.multiple_of(blk * bm + q * bq, 8), bq)], osem.at[slot])
                for q in range(ns)]

    @pl.when((b >= 2) & (b - 2 < nused))
    def _():
        for c in out_copies(b - 2, b % 2):
            c.wait()

    @pl.when(b < nused)
    def _():
        e = be_ref[b]
        slot = run_ref[b] % 2

        @pl.when(b == 0)
        def _():
            for c in weight_copies(e, slot):
                c.start()

        @pl.when((b == 0) | (e != be_ref[jnp.maximum(b - 1, 0)]))
        def _():
            for c in weight_copies(e, slot):
                c.wait()

            @pl.when(nxt_ref[b] >= 0)
            def _():
                for c in weight_copies(nxt_ref[b], 1 - slot):
                    c.start()

            wg_s[...] = wg_f[slot].astype(BF16)
            wu_s[...] = wu_f[slot].astype(BF16)
            wd_s[...] = wd_f[slot].astype(BF16)

        rows = lax.broadcasted_iota(jnp.int32, xq_refs[0].shape, 0)
        for q in range(ns):
            xp = jnp.where(rows + q * bq < nvalid_ref[b], xq_refs[q][...], jnp.uint32(0))
            xs[q * bq:(q + 1) * bq, :] = _unpack_bf16_pairs(xp).astype(BF16)
        x = xs[...]
        g = jnp.dot(x, wg_s[...], preferred_element_type=F32)
        u = jnp.dot(x, wu_s[...], preferred_element_type=F32)
        hid = (_silu(g) * u).astype(BF16)
        ob[b % 2] = _pack_bf16_pairs(jnp.dot(hid, wd_s[...], preferred_element_type=F32))
        for c in out_copies(b, b % 2):
            c.start()

    @pl.when(b == nb - 1)
    def _():
        @pl.when((b >= 1) & (b - 1 < nused))
        def _():
            for c in out_copies(b - 1, 1 - b % 2):
                c.wait()

        @pl.when(b < nused)
        def _():
            for c in out_copies(b, b % 2):
                c.wait()


def _experts(xbuf, block_e, nvalid, run, nxt, nused, w_gate, w_up, w_down, bm):
    p, dh = xbuf.shape
    nb = p // bm
    d, de = w_gate.shape[1:]
    ns = EXPERT_STREAMS
    bq = bm // ns
    hbm = pl.BlockSpec(memory_space=pl.ANY)

    def quarter(q):
        return pl.BlockSpec((bq, dh), lambda b, be, nv, rn, nx, nu: (jnp.minimum(b, nu[0] - 1) * ns + q, 0))

    grid_spec = pltpu.PrefetchScalarGridSpec(
        num_scalar_prefetch=5,
        grid=(nused[0],),
        in_specs=[quarter(q) for q in range(ns)] + [hbm, hbm, hbm],
        out_specs=hbm,
        scratch_shapes=[pltpu.VMEM((2, d, de), F32), pltpu.VMEM((2, d, de), F32), pltpu.VMEM((2, de, d), F32),
                        pltpu.VMEM((d, de), BF16), pltpu.VMEM((d, de), BF16), pltpu.VMEM((de, d), BF16),
                        pltpu.VMEM((bm, 2 * dh), BF16), pltpu.VMEM((2, bm, dh), jnp.uint32),
                        pltpu.SemaphoreType.DMA((2,)), pltpu.SemaphoreType.DMA((2,))],
    )
    return pl.pallas_call(
        functools.partial(_experts_kernel, bm=bm),
        grid_spec=grid_spec,
        out_shape=jax.ShapeDtypeStruct((p, dh), jnp.uint32),
        compiler_params=_cparams(("arbitrary",)),
        name="experts",
    )(block_e, nvalid, run, nxt, nused, *([xbuf] * ns), w_gate, w_up, w_down)


def _combine_kernel(y_ref, x1_ref, ysh_ref, mod_ref, g_ref, out_ref):
    y = y_ref[...] + _unpack_bf16_pairs(ysh_ref[...])
    out_ref[...] = x1_ref[...] + mod_ref[0, 5:6, :] * (_rms(y) * g_ref[...])


def _combine(y_routed, x1_flat, ysh_flat, mod3, g_post, s, tc):
    t, d = x1_flat.shape
    per_seq = s // tc
    tok = pl.BlockSpec((tc, d), lambda i: (i, 0))
    return pl.pallas_call(
        _combine_kernel,
        grid=(t // tc,),
        in_specs=[tok, tok, pl.BlockSpec((tc, d // 2), lambda i: (i, 0)),
                  pl.BlockSpec((1, 6, d), lambda i: (i // per_seq, 0, 0)),
                  pl.BlockSpec((1, d), lambda i: (0, 0))],
        out_specs=tok,
        out_shape=jax.ShapeDtypeStruct((t, d), F32),
        compiler_params=_cparams(("parallel",)),
        name="combine",
    )(y_routed, x1_flat, ysh_flat, mod3, g_post)


def _channel_dft_table():
    c = np.arange(GROUP_DIM)
    ang = 2.0 * np.pi * ((c[:, None] * c[None, :]) % GROUP_DIM) / GROUP_DIM
    eye = np.eye(D_FNET // GROUP_DIM)
    scale = 1.0 / np.sqrt(GROUP_DIM)
    cos_m, sin_m = np.kron(eye, np.cos(ang)) * scale, np.kron(eye, np.sin(ang)) * scale
    cols = [m[:, c * DFT_CH:(c + 1) * DFT_CH] for c in range(D_FNET // DFT_CH) for m in (cos_m, sin_m)]
    return jnp.asarray(np.concatenate(cols, axis=1), BF16)


EXPERT_BLOCK_MAX = 2304
EXPERT_BLOCK_ALIGN = 64


def _expert_block_rows(n_assign):
    target = max(n_assign // N_EXPERTS * 9 // 8, 2 * EXPERT_BLOCK_ALIGN)
    k = -(-target // EXPERT_BLOCK_MAX)
    return -(-target // (k * EXPERT_BLOCK_ALIGN)) * EXPERT_BLOCK_ALIGN


def _layer(x, mod, p):
    b, s, d = x.shape
    t = b * s
    mod3 = mod.reshape(b, 6, d)
    g_tab, h_tab = _dft_tables(s)

    v, zcs = _inproj(x, mod3, p["g_mix_pre"], p["w_in_b"], _channel_dft_table())
    tile = min(TOKEN_TILE, s)
    cn = _conv(v, p["conv_w"], p["conv_b"], p["conv_ln_g"], p["conv_ln_b"], p["g_conv_out"], tile)
    fy = _seqdft(zcs, g_tab, h_tab)
    x1, h2, ysh = _mix(x, cn, fy, mod3, p["g_fnet_out"], p["w_out_b"], p["g_mix_post"], p["g_ffn_pre"],
                       p["wsg_b"], p["wsu_b"], p["wsd_b"], tile)
    idx, rank, wgt, cnt = _router(h2, p["wr_t_b"], p["b_router_col"], tile)

    n = t * TOP_K
    bm = _expert_block_rows(n)
    counts = cnt[:, 0]
    pcounts = (counts + bm - 1) // bm * bm
    pends = jnp.cumsum(pcounts)
    pstarts = pends - pcounts
    dest = _dest(pstarts.astype(jnp.int32), idx, rank, min(DEST_TILE, t))
    nb = (n + N_EXPERTS * (bm - 1) + bm - 1) // bm
    nused = (pends[-1] // bm).astype(jnp.int32)
    blk = jnp.minimum(jnp.arange(nb, dtype=jnp.int32), nused - 1) * bm
    block_e = jnp.sum((pends[None, :] <= blk[:, None]).astype(jnp.int32), axis=1)
    block_e = jnp.minimum(block_e, N_EXPERTS - 1)
    nvalid = jnp.clip(pstarts[block_e] + counts[block_e] - blk, 0, bm).astype(jnp.int32)
    first = jnp.concatenate([jnp.ones((1,), jnp.int32), (block_e[1:] != block_e[:-1]).astype(jnp.int32)])
    run = jnp.cumsum(first) - 1
    eid = jnp.arange(N_EXPERTS, dtype=jnp.int32)
    later = lax.cummin(jnp.where(pcounts > 0, eid, N_EXPERTS)[::-1])[::-1]
    nxt_e = jnp.concatenate([later[1:], jnp.full((1,), N_EXPERTS, jnp.int32)])
    nxt = jnp.where(nxt_e < N_EXPERTS, nxt_e, -1)[block_e].astype(jnp.int32)

    xbuf = _dispatch(h2.reshape(t, d // 2), dest, nb * bm)
    obuf = _experts(xbuf, block_e, nvalid, run.astype(jnp.int32), nxt, nused.reshape(1),
                    p["w_gate"], p["w_up"], p["w_down"], bm)
    y_routed = _gather_sum(obuf, dest, wgt)
    out = _combine(y_routed, x1.reshape(t, d), ysh.reshape(t, d // 2), mod3, p["g_ffn_post"], s,
                   min(COMBINE_TILE, s))
    return out.reshape(b, s, d)


def kernel(x_prompt, x_sample, c_prompt, c_sample, w_ada, b_ada, g_mix_pre, w_in, conv_w, conv_b, conv_ln_g, conv_ln_b, g_conv_out, g_fnet_out, w_out, g_mix_post, g_ffn_pre, w_router, b_router, w_gate, w_up, w_down, ws_gate, ws_up, ws_down, g_ffn_post):
    assert w_ada.shape[0] == 1, "single-layer kernel"
    bp, bs = c_prompt.shape[0], c_sample.shape[0]
    rows = -(-(bp + bs) // 8) * 8
    c_all = jnp.zeros((rows, D_MODEL), F32).at[:bp].set(c_prompt).at[bp:bp + bs].set(c_sample)
    mod = _ada(c_all, w_ada[0], b_ada)
    p = {
        "g_mix_pre": g_mix_pre, "w_in_b": w_in[0].astype(BF16),
        "conv_w": conv_w[0], "conv_b": conv_b, "conv_ln_g": conv_ln_g, "conv_ln_b": conv_ln_b,
        "g_conv_out": g_conv_out, "g_fnet_out": g_fnet_out, "w_out_b": w_out[0].astype(BF16),
        "g_mix_post": g_mix_post, "g_ffn_pre": g_ffn_pre,
        "wr_t_b": w_router[0].T.astype(BF16), "b_router_col": b_router[0][:, None],
        "w_gate": w_gate[0], "w_up": w_up[0], "w_down": w_down[0],
        "wsg_b": ws_gate[0].astype(BF16), "wsu_b": ws_up[0].astype(BF16), "wsd_b": ws_down[0].astype(BF16),
        "g_ffn_post": g_ffn_post,
    }
    y_prompt = _layer(x_prompt, mod[:bp], p)
    y_sample = _layer(x_sample, mod[bp:bp + bs], p)
    return (y_prompt, y_sample)
```

```python
import functools

import numpy as np
import jax
import jax.numpy as jnp
from jax import lax
from jax.experimental import pallas as pl
from jax.experimental.pallas import tpu as pltpu
from jax.experimental.pallas import tpu_sc as plsc

F32 = jnp.float32
BF16 = jnp.bfloat16

D_MODEL = 1024
D_CONV = 512
D_FNET = 512
GROUP_DIM = 64
CONV_WIDTH = 31
N_EXPERTS = 256
TOP_K = 8
N_EXPERT_GROUPS = 8
GROUP_SIZE = N_EXPERTS // N_EXPERT_GROUPS
TOPK_GROUPS = 4
ROUTED_SCALE = 2.5
EPS = 1e-6

DFT_S1 = 128
DFT_CH = 128
HALO = 16
VMEM_LIMIT = 56 * 1024 * 1024
TOKEN_TILE = 1024
COMBINE_TILE = 1024
DEST_TILE = 2048
SC_CORES = 2
SC_SUBCORES = 16
SC_WORKERS = SC_CORES * SC_SUBCORES
SC_ROWS = 64
SC_LANES = 16
SC_SUM_TOKENS = 8
EXPERT_STREAMS = 4

def _cparams(sem, vmem=None):
    return pltpu.CompilerParams(dimension_semantics=sem, vmem_limit_bytes=vmem or VMEM_LIMIT)


def _rms(x):
    return x * lax.rsqrt(jnp.mean(x * x, axis=-1, keepdims=True) + EPS)


def _silu(x):
    return x * jax.nn.sigmoid(x)


def _pack_bf16_pairs(x):
    c = x.shape[-1] // 2
    bits = lax.bitcast_convert_type(x.astype(BF16).astype(F32), jnp.uint32)
    return (bits[:, :c] >> 16) | bits[:, c:]


def _unpack_bf16_pairs(p):
    lo = lax.bitcast_convert_type(p << 16, F32)
    hi = lax.bitcast_convert_type(p & jnp.uint32(0xFFFF0000), F32)
    return jnp.concatenate([lo, hi], axis=-1)


def _ada_kernel(c_ref, w_ref, b_ref, o_ref):
    o_ref[...] = jnp.dot(_silu(c_ref[...]), w_ref[...], preferred_element_type=F32) + b_ref[...]


def _ada(c, w_ada, b_ada):
    bp, d = c.shape
    n = w_ada.shape[1]
    return pl.pallas_call(
        _ada_kernel,
        grid=(n // d,),
        in_specs=[pl.BlockSpec((bp, d), lambda j: (0, 0)),
                  pl.BlockSpec((d, d), lambda j: (0, j)),
                  pl.BlockSpec((1, d), lambda j: (0, j))],
        out_specs=pl.BlockSpec((bp, d), lambda j: (0, j)),
        out_shape=jax.ShapeDtypeStruct((bp, n), F32),
        compiler_params=_cparams(("parallel",)),
        name="ada",
    )(c, w_ada, b_ada)


def _inproj_kernel(x_ref, mod_ref, g_ref, win_ref, cs_ref, perm_ref, v_ref, z_ref, *, s2, p1l):
    x = x_ref[0]
    h = _rms(x) * g_ref[...]
    h = h * (1.0 + mod_ref[0, 1:2, :]) + mod_ref[0, 0:1, :]
    u = jnp.dot(h.astype(BF16), win_ref[...], preferred_element_type=F32)
    a = u[:, :D_CONV]
    gt = u[:, D_CONV:2 * D_CONV]
    z = u[:, 2 * D_CONV:]
    v_ref[0] = (a * jax.nn.sigmoid(gt)).astype(v_ref.dtype)
    zp = jnp.dot(perm_ref[...], z.astype(BF16), preferred_element_type=F32).astype(BF16)
    zcs = jnp.dot(zp, cs_ref[...], preferred_element_type=F32).astype(BF16)
    z_ref[0] = zcs.reshape(s2, p1l, zcs.shape[-1])


def _inproj(x, mod3, g_mix_pre, w_in_b, cs_b):
    b, s, d = x.shape
    s2 = s // DFT_S1
    p1l = 16
    ts = p1l * s2
    r = np.arange(ts)
    perm = np.zeros((ts, ts), np.float32)
    perm[(r % s2) * p1l + r // s2, r] = 1.0
    perm = jnp.asarray(perm, BF16)
    return pl.pallas_call(
        functools.partial(_inproj_kernel, s2=s2, p1l=p1l),
        grid=(b, s // ts),
        in_specs=[pl.BlockSpec((1, ts, d), lambda i, t: (i, t, 0)),
                  pl.BlockSpec((1, 6, d), lambda i, t: (i, 0, 0)),
                  pl.BlockSpec((1, d), lambda i, t: (0, 0)),
                  pl.BlockSpec(w_in_b.shape, lambda i, t: (0, 0)),
                  pl.BlockSpec(cs_b.shape, lambda i, t: (0, 0)),
                  pl.BlockSpec((ts, ts), lambda i, t: (0, 0))],
        out_specs=[pl.BlockSpec((1, ts, D_CONV), lambda i, t: (i, t, 0)),
                   pl.BlockSpec((1, s2, p1l, 2 * D_FNET), lambda i, t: (i, 0, t, 0))],
        out_shape=[jax.ShapeDtypeStruct((b, s, D_CONV), BF16),
                   jax.ShapeDtypeStruct((b, s2, DFT_S1, 2 * D_FNET), BF16)],
        compiler_params=_cparams(("parallel", "parallel")),
        name="inproj",
    )(x, mod3, g_mix_pre, w_in_b, cs_b, perm)


def _conv_kernel(vp_ref, v_ref, vn_ref, w_ref, b_ref, lg_ref, lb_ref, go_ref, o_ref, pad_ref, sh_ref, *, ts, rc):
    t = pl.program_id(1)
    nt = pl.num_programs(1)
    pad_ref[0:HALO, :] = jnp.where(t > 0, vp_ref[0].astype(F32), 0.0)
    pad_ref[HALO:HALO + ts, :] = v_ref[0].astype(F32)
    pad_ref[HALO + ts:HALO + ts + HALO, :] = jnp.where(t < nt - 1, vn_ref[0].astype(F32), 0.0)
    span = ts + 2 * HALO - 8
    for m in range(8):
        sh_ref[m] = pad_ref[m:m + span, :]
    off = HALO - CONV_WIDTH // 2
    for c in range(ts // rc):
        r0 = c * rc
        acc = jnp.zeros((rc // 8, 8, D_CONV), F32)
        for j in range(CONV_WIDTH):
            m, q = (off + j) % 8, (off + j) // 8
            tap = sh_ref[m, r0 + 8 * q:r0 + 8 * q + rc, :].reshape(rc // 8, 8, D_CONV)
            acc = acc + tap * w_ref[j][None]
        acc = acc.reshape(rc, D_CONV) + b_ref[...]
        mu = jnp.mean(acc, axis=-1, keepdims=True)
        xc = acc - mu
        var = jnp.mean(xc * xc, axis=-1, keepdims=True)
        y = xc * lax.rsqrt(var + EPS) * lg_ref[...] + lb_ref[...]
        y = _silu(y)
        y = _rms(y) * go_ref[...]
        o_ref[0, r0:r0 + rc, :] = y.astype(o_ref.dtype)


def _conv(v, conv_w, conv_b, ln_g, ln_b, g_out, ts, rc=32):
    b, s, c = v.shape
    hb = ts // HALO
    nh = s // HALO
    vec = pl.BlockSpec((1, c), lambda i, t: (0, 0))
    return pl.pallas_call(
        functools.partial(_conv_kernel, ts=ts, rc=rc),
        grid=(b, s // ts),
        in_specs=[pl.BlockSpec((1, HALO, c), lambda i, t: (i, jnp.maximum(t * hb - 1, 0), 0)),
                  pl.BlockSpec((1, ts, c), lambda i, t: (i, t, 0)),
                  pl.BlockSpec((1, HALO, c), lambda i, t: (i, jnp.minimum((t + 1) * hb, nh - 1), 0)),
                  pl.BlockSpec((CONV_WIDTH, 8, c), lambda i, t: (0, 0, 0)),
                  vec, vec, vec, vec],
        out_specs=pl.BlockSpec((1, ts, c), lambda i, t: (i, t, 0)),
        out_shape=jax.ShapeDtypeStruct((b, s, c), BF16),
        scratch_shapes=[pltpu.VMEM((ts + 2 * HALO, c), F32), pltpu.VMEM((8, ts + 2 * HALO - 8, c), F32)],
        compiler_params=_cparams(("parallel", "parallel")),
        name="conv",
    )(v, v, v, jnp.broadcast_to(conv_w[:, None, :], (CONV_WIDTH, 8, c)), conv_b, ln_g, ln_b, g_out)


def _dft_tables(s):
    s1 = DFT_S1
    s2 = s // s1
    k1 = np.arange(s1)[None, :, None]
    p1 = np.arange(s1)[None, None, :]
    p2 = np.arange(s2)[:, None, None]
    ang = 2.0 * np.pi * ((k1 * (s2 * p1 + p2)) % s) / s
    g = np.concatenate([np.cos(ang), np.sin(ang)], axis=1)
    k2 = np.arange(s2)[:, None]
    q2 = np.arange(s2)[None, :]
    ang2 = 2.0 * np.pi * ((k2 * q2) % s2) / s2
    h = np.concatenate([np.cos(ang2), np.sin(ang2)], axis=1) / np.sqrt(s)
    return jnp.asarray(g, BF16), jnp.asarray(h, BF16)


def _seqdft_kernel(z_ref, g_ref, h_ref, o_ref, scr_ref, *, s1, s2):
    ch = DFT_CH
    for p2 in range(s2):
        r = jnp.dot(g_ref[p2], z_ref[0, p2], preferred_element_type=F32)
        a_re = r[:s1, :ch] - r[s1:, ch:]
        a_im = -(r[:s1, ch:] + r[s1:, :ch])
        scr_ref[:, p2, :] = a_re
        scr_ref[:, s2 + p2, :] = a_im
    hmat = h_ref[...]
    for k1 in range(s1):
        y = jnp.dot(hmat, scr_ref[k1].astype(BF16), preferred_element_type=F32)
        o_ref[0, pl.ds(k1, s2, stride=s1), :] = y


def _seqdft(zp, g_tab, h_tab):
    b, s2, s1, _ = zp.shape
    s = s1 * s2
    nch = D_FNET // DFT_CH
    return pl.pallas_call(
        functools.partial(_seqdft_kernel, s1=s1, s2=s2),
        grid=(b, nch),
        in_specs=[pl.BlockSpec((1, s2, s1, 2 * DFT_CH), lambda i, c: (i, 0, 0, c)),
                  pl.BlockSpec(g_tab.shape, lambda i, c: (0, 0, 0)),
                  pl.BlockSpec(h_tab.shape, lambda i, c: (0, 0))],
        out_specs=pl.BlockSpec((1, s, DFT_CH), lambda i, c: (i, 0, c)),
        out_shape=jax.ShapeDtypeStruct((b, s, D_FNET), F32),
        scratch_shapes=[pltpu.VMEM((s1, 2 * s2, DFT_CH), F32)],
        compiler_params=_cparams(("parallel", "parallel")),
        name="seqdft",
    )(zp, g_tab, h_tab)


def _mix_kernel(x_ref, cn_ref, fy_ref, mod_ref, gf_ref, wout_ref, gpost_ref, gpre_ref,
                wsg_ref, wsu_ref, wsd_ref, x1_ref, h2_ref, ysh_ref):
    fn = _rms(fy_ref[0]) * gf_ref[...]
    mixed = jnp.dot(cn_ref[0], wout_ref[:D_CONV, :], preferred_element_type=F32)
    mixed = mixed + jnp.dot(fn.astype(BF16), wout_ref[D_CONV:, :], preferred_element_type=F32)
    x1 = x_ref[0] + mod_ref[0, 2:3, :] * (_rms(mixed) * gpost_ref[...])
    x1_ref[0] = x1
    h2 = _rms(x1) * gpre_ref[...]
    h2 = h2 * (1.0 + mod_ref[0, 4:5, :]) + mod_ref[0, 3:4, :]
    h2_ref[0] = _pack_bf16_pairs(h2)
    hb = h2.astype(BF16)
    hid = _silu(jnp.dot(hb, wsg_ref[...], preferred_element_type=F32))
    hid = hid * jnp.dot(hb, wsu_ref[...], preferred_element_type=F32)
    ysh_ref[0] = _pack_bf16_pairs(jnp.dot(hid.astype(BF16), wsd_ref[...], preferred_element_type=F32))


def _mix(x, cn, fy, mod3, g_fnet, w_out_b, g_post, g_pre, wsg_b, wsu_b, wsd_b, ts):
    b, s, d = x.shape
    tok = lambda c: pl.BlockSpec((1, ts, c), lambda i, t: (i, t, 0))
    full = lambda a: pl.BlockSpec(a.shape, lambda i, t: (0,) * a.ndim)
    return pl.pallas_call(
        _mix_kernel,
        grid=(b, s // ts),
        in_specs=[tok(d), tok(D_CONV), tok(D_FNET),
                  pl.BlockSpec((1, 6, d), lambda i, t: (i, 0, 0)),
                  full(g_fnet), full(w_out_b), full(g_post), full(g_pre),
                  full(wsg_b), full(wsu_b), full(wsd_b)],
        out_specs=[tok(d), tok(d // 2), tok(d // 2)],
        out_shape=[jax.ShapeDtypeStruct((b, s, d), F32),
                   jax.ShapeDtypeStruct((b, s, d // 2), jnp.uint32),
                   jax.ShapeDtypeStruct((b, s, d // 2), jnp.uint32)],
        compiler_params=_cparams(("parallel", "parallel")),
        name="mix",
    )(x, cn, fy, mod3, g_fnet, w_out_b, g_post, g_pre, wsg_b, wsu_b, wsd_b)


def _router_kernel(h_ref, wr_ref, br_ref, u_ref, idx_ref, rank_ref, wgt_ref, cnt_ref, carry_ref, *, tr):
    e = N_EXPERTS

    @pl.when((pl.program_id(0) == 0) & (pl.program_id(1) == 0))
    def _():
        carry_ref[...] = jnp.zeros_like(carry_ref)

    logits = lax.dot_general(wr_ref[...], _unpack_bf16_pairs(h_ref[0]).astype(BF16), (((1,), (1,)), ((), ())),
                             preferred_element_type=F32)
    sc = jax.nn.sigmoid(logits)
    sb = sc + br_ref[...]
    ninf = jnp.float32(-jnp.inf)

    io_g = lax.broadcasted_iota(jnp.int32, (GROUP_SIZE, tr), 0).astype(F32)
    gs = []
    for g in range(N_EXPERT_GROUPS):
        blk = sb[g * GROUP_SIZE:(g + 1) * GROUP_SIZE]
        m1 = jnp.max(blk, axis=0, keepdims=True)
        i1 = jnp.min(jnp.where(blk == m1, io_g, float(GROUP_SIZE)), axis=0, keepdims=True)
        m2 = jnp.max(jnp.where(io_g == i1, ninf, blk), axis=0, keepdims=True)
        gs.append(m1 + m2)
    masked = []
    for g in range(N_EXPERT_GROUPS):
        beat = jnp.zeros((1, tr), F32)
        for o in range(N_EXPERT_GROUPS):
            if o == g:
                continue
            wins = (gs[o] > gs[g]) | ((gs[o] == gs[g]) & (o < g))
            beat = beat + wins.astype(F32)
        keep = beat < float(TOPK_GROUPS)
        masked.append(jnp.where(keep, sb[g * GROUP_SIZE:(g + 1) * GROUP_SIZE], ninf))
    v = jnp.concatenate(masked, axis=0)

    io_e = lax.broadcasted_iota(jnp.int32, (e, tr), 0).astype(F32)
    ids, ws = [], []
    sel = jnp.zeros((e, tr), F32)
    for _ in range(TOP_K):
        m = jnp.max(v, axis=0, keepdims=True)
        i = jnp.min(jnp.where(v == m, io_e, float(e)), axis=0, keepdims=True)
        oh = io_e == i
        ids.append(i)
        ws.append(jnp.sum(jnp.where(oh, sc, 0.0), axis=0, keepdims=True))
        v = jnp.where(oh, ninf, v)
        sel = sel + oh.astype(F32)

    wsum = ws[0]
    for k in range(1, TOP_K):
        wsum = wsum + ws[k]
    wgt_ref[...] = jnp.concatenate([w / wsum * ROUTED_SCALE for w in ws], axis=0)
    idx_ref[...] = jnp.concatenate(ids, axis=0).astype(jnp.int32)

    excl = jnp.dot(sel.astype(BF16), u_ref[...], preferred_element_type=F32)
    base = carry_ref[:, 0:1]
    rank_full = base + excl
    ranks = [jnp.sum(jnp.where(io_e == ids[k], rank_full, 0.0), axis=0, keepdims=True)
             for k in range(TOP_K)]
    rank_ref[...] = jnp.concatenate(ranks, axis=0).astype(jnp.int32)
    new = base + jnp.sum(sel, axis=1, keepdims=True)
    carry_ref[...] = jnp.broadcast_to(new, carry_ref.shape)
    cnt_ref[...] = jnp.broadcast_to(new, cnt_ref.shape).astype(jnp.int32)


def _router(h2p, wr_t_b, b_router_col, tr):
    b, s, d = h2p.shape
    t = b * s
    nt = s // tr
    u = jnp.asarray(np.triu(np.ones((tr, tr), np.float32), k=1), BF16)
    col = lambda i, j: (0, i * nt + j)
    return pl.pallas_call(
        functools.partial(_router_kernel, tr=tr),
        grid=(b, nt),
        in_specs=[pl.BlockSpec((1, tr, d), lambda i, j: (i, j, 0)),
                  pl.BlockSpec(wr_t_b.shape, lambda i, j: (0, 0)),
                  pl.BlockSpec((N_EXPERTS, 1), lambda i, j: (0, 0)),
                  pl.BlockSpec((tr, tr), lambda i, j: (0, 0))],
        out_specs=[pl.BlockSpec((TOP_K, tr), col), pl.BlockSpec((TOP_K, tr), col),
                   pl.BlockSpec((TOP_K, tr), col),
                   pl.BlockSpec((N_EXPERTS, 128), lambda i, j: (0, 0))],
        out_shape=[jax.ShapeDtypeStruct((TOP_K, t), jnp.int32),
                   jax.ShapeDtypeStruct((TOP_K, t), jnp.int32),
                   jax.ShapeDtypeStruct((TOP_K, t), F32),
                   jax.ShapeDtypeStruct((N_EXPERTS, 128), jnp.int32)],
        scratch_shapes=[pltpu.VMEM((N_EXPERTS, 128), F32)],
        compiler_params=_cparams(("arbitrary", "arbitrary")),
        name="router",
    )(h2p, wr_t_b, b_router_col, u)


def _dest_kernel(pstart_ref, idx_ref, rank_ref, dest_ref):
    idx = idx_ref[...]

    def body(g, acc):
        for j in range(8):
            e = g * 8 + j
            acc = jnp.where(idx == e, pstart_ref[e], acc)
        return acc

    dest_ref[...] = lax.fori_loop(0, N_EXPERTS // 8, body, jnp.zeros_like(idx)) + rank_ref[...]


def _dest(pstarts, idx, rank, tl):
    k, t = idx.shape
    grid_spec = pltpu.PrefetchScalarGridSpec(
        num_scalar_prefetch=1,
        grid=(t // tl,),
        in_specs=[pl.BlockSpec((k, tl), lambda i, ps: (0, i)), pl.BlockSpec((k, tl), lambda i, ps: (0, i))],
        out_specs=pl.BlockSpec((k, tl), lambda i, ps: (0, i)),
    )
    return pl.pallas_call(
        _dest_kernel,
        grid_spec=grid_spec,
        out_shape=jax.ShapeDtypeStruct((k, t), jnp.int32),
        compiler_params=_cparams(("parallel",)),
        name="dest",
    )(pstarts, idx, rank)


def _sc_mesh():
    return plsc.VectorSubcoreMesh(core_axis_name="c", subcore_axis_name="s",
                                  num_cores=SC_CORES, num_subcores=SC_SUBCORES)


def _sc_worker_base(per_worker):
    return (lax.axis_index("s") * SC_CORES + lax.axis_index("c")) * per_worker


def _dispatch(h2_flat, dest, p_rows):
    t, dh = h2_flat.shape
    r = SC_ROWS
    per_w = t // SC_WORKERS
    nchunk = per_w // r
    assert per_w % (2 * r) == 0

    @functools.partial(
        pl.kernel, mesh=_sc_mesh(),
        out_type=jax.ShapeDtypeStruct((p_rows, dh), h2_flat.dtype),
        scratch_types=[pltpu.VMEM((2, TOP_K, r), jnp.int32), pltpu.VMEM((2, r, dh), h2_flat.dtype),
                       pltpu.SemaphoreType.DMA((2,)), pltpu.SemaphoreType.DMA((2,))],
        name="sc_dispatch",
    )
    def k(rows_hbm, dest_hbm, out_hbm, idx_v, rows_v, lsem, ssem):
        base = _sc_worker_base(per_w)

        def load(ci, slot):
            t0 = base + ci * r
            for kk in range(TOP_K):
                pltpu.sync_copy(dest_hbm.at[kk, pl.ds(t0, r)], idx_v.at[slot, kk])
            pltpu.async_copy(rows_hbm.at[pl.ds(t0, r)], rows_v.at[slot], lsem.at[slot])

        def scatter(ci, slot):
            t0 = base + ci * r
            pltpu.make_async_copy(rows_hbm.at[pl.ds(t0, r)], rows_v.at[slot], lsem.at[slot]).wait()
            for kk in range(TOP_K):
                pltpu.async_copy(rows_v.at[slot], out_hbm.at[idx_v.at[slot, kk]], ssem.at[slot])

        def drain(slot):
            for kk in range(TOP_K):
                pltpu.make_async_copy(rows_v.at[slot], out_hbm.at[idx_v.at[slot, kk]], ssem.at[slot]).wait()

        load(0, 0)

        @pl.loop(0, nchunk, step=2)
        def _(c0):
            for s in range(2):
                ci = c0 + s

                @pl.when(ci + 1 < nchunk)
                def _():
                    @pl.when(ci >= 1)
                    def _():
                        drain(1 - s)
                    load(ci + 1, 1 - s)

                scatter(ci, s)

        drain(0)
        drain(1)

    return k(h2_flat, dest)


def _gather_sum(obuf, dest, wgt):
    _, dh = obuf.shape
    kk_n, t = dest.shape
    r, lanes = SC_SUM_TOKENS, SC_LANES
    per_w = t // SC_WORKERS
    nchunk = per_w // r
    assert per_w % (2 * r) == 0 and per_w % 128 == 0
    nj = dh // lanes

    @functools.partial(
        pl.kernel, mesh=_sc_mesh(),
        out_type=jax.ShapeDtypeStruct((t, 2 * dh), F32),
        scratch_types=[pltpu.VMEM((kk_n, per_w), jnp.int32), pltpu.VMEM((2, kk_n, r, dh), obuf.dtype),
                       pltpu.VMEM((kk_n, per_w), F32), pltpu.VMEM((2, r, 2 * dh), F32),
                       pltpu.SemaphoreType.DMA((2,)), pltpu.SemaphoreType.DMA((2,))],
        compiler_params=pltpu.CompilerParams(needs_layout_passes=False),
        name="sc_gather_sum",
    )
    def k(table_hbm, idx_hbm, w_hbm, out_hbm, idx_v, rows_v, w_v, out_v, gsem, wsem):
        base = _sc_worker_base(per_w)
        pltpu.sync_copy(idx_hbm.at[:, pl.ds(base, per_w)], idx_v)
        pltpu.sync_copy(w_hbm.at[:, pl.ds(base, per_w)], w_v)

        def idx_list(ci, kk):
            return idx_v.at[kk, pl.ds(pl.multiple_of(ci * r, 8), r)]

        def copies(ci, slot):
            for kk in range(kk_n):
                yield pltpu.make_async_copy(table_hbm.at[idx_list(ci, kk)], rows_v.at[slot, kk], gsem.at[slot])

        def out_copy(ci, slot):
            return pltpu.make_async_copy(out_v.at[slot], out_hbm.at[pl.ds(base + ci * r, r)], wsem.at[slot])

        def compute(ci, slot):
            @pl.loop(0, r)
            def _(i):
                tok = jnp.zeros((lanes,), jnp.int32) + (ci * r + i)
                ws = [plsc.load_gather(w_v, [jnp.full((lanes,), kk, jnp.int32), tok]) for kk in range(kk_n)]

                @plsc.parallel_loop(0, nj, unroll=4)
                def _(j):
                    col = pl.multiple_of(j * lanes, lanes)
                    lo = jnp.zeros((lanes,), F32)
                    hi = jnp.zeros((lanes,), F32)
                    for kk in range(kk_n):
                        v = rows_v[slot, kk, i, pl.ds(col, lanes)]
                        lo = lo + ws[kk] * plsc.bitcast(v << 16, F32)
                        hi = hi + ws[kk] * plsc.bitcast(v & jnp.uint32(0xFFFF0000), F32)
                    out_v[slot, i, pl.ds(col, lanes)] = lo
                    out_v[slot, i, pl.ds(dh + col, lanes)] = hi

        for c in copies(0, 0):
            c.start()

        @pl.loop(0, nchunk, step=2)
        def _(c0):
            for s in range(2):
                ci = c0 + s

                @pl.when(ci + 1 < nchunk)
                def _():
                    for c in copies(ci + 1, 1 - s):
                        c.start()

                for c in copies(ci, s):
                    c.wait()

                @pl.when(ci >= 2)
                def _():
                    out_copy(ci - 2, s).wait()

                compute(ci, s)
                out_copy(ci, s).start()

        out_copy(nchunk - 2, 0).wait()
        out_copy(nchunk - 1, 1).wait()

    return k(obuf, dest, wgt)


def _experts_kernel(be_ref, nvalid_ref, run_ref, nxt_ref, nused_ref, *refs, bm):
    ns = EXPERT_STREAMS
    xq_refs = refs[:ns]
    wg_hbm, wu_hbm, wd_hbm, o_hbm, wg_f, wu_f, wd_f, wg_s, wu_s, wd_s, xs, ob, sem, osem = refs[ns:]
    b = pl.program_id(0)
    nb = pl.num_programs(0)
    nused = nused_ref[0]
    bq = bm // ns

    def weight_copies(e, slot):
        return (pltpu.make_async_copy(wg_hbm.at[e], wg_f.at[slot], sem.at[slot]),
                pltpu.make_async_copy(wu_hbm.at[e], wu_f.at[slot], sem.at[slot]),
                pltpu.make_async_copy(wd_hbm.at[e], wd_f.at[slot], sem.at[slot]))

    def out_copies(blk, slot):
        return [pltpu.make_async_copy(ob.at[slot, pl.ds(q * bq, bq)],
                                      o_hbm.at[pl.ds(pl.multiple_of(blk * bm + q * bq, 8), bq)], osem.at[slot])
                for q in range(ns)]

    @pl.when((b >= 2) & (b - 2 < nused))
    def _():
        for c in out_copies(b - 2, b % 2):
            c.wait()

    @pl.when(b < nused)
    def _():
        e = be_ref[b]
        slot = run_ref[b] % 2

        @pl.when(b == 0)
        def _():
            for c in weight_copies(e, slot):
                c.start()

        @pl.when((b == 0) | (e != be_ref[jnp.maximum(b - 1, 0)]))
        def _():
            for c in weight_copies(e, slot):
                c.wait()

            @pl.when(nxt_ref[b] >= 0)
            def _():
                for c in weight_copies(nxt_ref[b], 1 - slot):
                    c.start()

            wg_s[...] = wg_f[slot].astype(BF16)
            wu_s[...] = wu_f[slot].astype(BF16)
            wd_s[...] = wd_f[slot].astype(BF16)

        rows = lax.broadcasted_iota(jnp.int32, xq_refs[0].shape, 0)
        for q in range(ns):
            xp = jnp.where(rows + q * bq < nvalid_ref[b], xq_refs[q][...], jnp.uint32(0))
            xs[q * bq:(q + 1) * bq, :] = _unpack_bf16_pairs(xp).astype(BF16)
        x = xs[...]
        g = jnp.dot(x, wg_s[...], preferred_element_type=F32)
        u = jnp.dot(x, wu_s[...], preferred_element_type=F32)
        hid = (_silu(g) * u).astype(BF16)
        ob[b % 2] = _pack_bf16_pairs(jnp.dot(hid, wd_s[...], preferred_element_type=F32))
        for c in out_copies(b, b % 2):
            c.start()

    @pl.when(b == nb - 1)
    def _():
        @pl.when((b >= 1) & (b - 1 < nused))
        def _():
            for c in out_copies(b - 1, 1 - b % 2):
                c.wait()

        @pl.when(b < nused)
        def _():
            for c in out_copies(b, b % 2):
                c.wait()


def _experts(xbuf, block_e, nvalid, run, nxt, nused, w_gate, w_up, w_down, bm):
    p, dh = xbuf.shape
    nb = p // bm
    d, de = w_gate.shape[1:]
    ns = EXPERT_STREAMS
    bq = bm // ns
    hbm = pl.BlockSpec(memory_space=pl.ANY)

    def quarter(q):
        return pl.BlockSpec((bq, dh), lambda b, be, nv, rn, nx, nu: (jnp.minimum(b, nu[0] - 1) * ns + q, 0))

    grid_spec = pltpu.PrefetchScalarGridSpec(
        num_scalar_prefetch=5,
        grid=(nused[0],),
        in_specs=[quarter(q) for q in range(ns)] + [hbm, hbm, hbm],
        out_specs=hbm,
        scratch_shapes=[pltpu.VMEM((2, d, de), F32), pltpu.VMEM((2, d, de), F32), pltpu.VMEM((2, de, d), F32),
                        pltpu.VMEM((d, de), BF16), pltpu.VMEM((d, de), BF16), pltpu.VMEM((de, d), BF16),
                        pltpu.VMEM((bm, 2 * dh), BF16), pltpu.VMEM((2, bm, dh), jnp.uint32),
                        pltpu.SemaphoreType.DMA((2,)), pltpu.SemaphoreType.DMA((2,))],
    )
    return pl.pallas_call(
        functools.partial(_experts_kernel, bm=bm),
        grid_spec=grid_spec,
        out_shape=jax.ShapeDtypeStruct((p, dh), jnp.uint32),
        compiler_params=_cparams(("arbitrary",)),
        name="experts",
    )(block_e, nvalid, run, nxt, nused, *([xbuf] * ns), w_gate, w_up, w_down)


def _combine_kernel(y_ref, x1_ref, ysh_ref, mod_ref, g_ref, out_ref):
    y = y_ref[...] + _unpack_bf16_pairs(ysh_ref[...])
    out_ref[...] = x1_ref[...] + mod_ref[0, 5:6, :] * (_rms(y) * g_ref[...])


def _combine(y_routed, x1_flat, ysh_flat, mod3, g_post, s, tc):
    t, d = x1_flat.shape
    per_seq = s // tc
    tok = pl.BlockSpec((tc, d), lambda i: (i, 0))
    return pl.pallas_call(
        _combine_kernel,
        grid=(t // tc,),
        in_specs=[tok, tok, pl.BlockSpec((tc, d // 2), lambda i: (i, 0)),
                  pl.BlockSpec((1, 6, d), lambda i: (i // per_seq, 0, 0)),
                  pl.BlockSpec((1, d), lambda i: (0, 0))],
        out_specs=tok,
        out_shape=jax.ShapeDtypeStruct((t, d), F32),
        compiler_params=_cparams(("parallel",)),
        name="combine",
    )(y_routed, x1_flat, ysh_flat, mod3, g_post)


def _channel_dft_table():
    c = np.arange(GROUP_DIM)
    ang = 2.0 * np.pi * ((c[:, None] * c[None, :]) % GROUP_DIM) / GROUP_DIM
    eye = np.eye(D_FNET // GROUP_DIM)
    scale = 1.0 / np.sqrt(GROUP_DIM)
    cos_m, sin_m = np.kron(eye, np.cos(ang)) * scale, np.kron(eye, np.sin(ang)) * scale
    cols = [m[:, c * DFT_CH:(c + 1) * DFT_CH] for c in range(D_FNET // DFT_CH) for m in (cos_m, sin_m)]
    return jnp.asarray(np.concatenate(cols, axis=1), BF16)


EXPERT_BLOCK_MAX = 1280
EXPERT_BLOCK_ALIGN = 64


def _expert_block_rows(n_assign):
    target = max(n_assign // N_EXPERTS * 9 // 8, 2 * EXPERT_BLOCK_ALIGN)
    k = -(-target // EXPERT_BLOCK_MAX)
    return -(-target // (k * EXPERT_BLOCK_ALIGN)) * EXPERT_BLOCK_ALIGN


def _layer(x, mod, p):
    b, s, d = x.shape
    t = b * s
    mod3 = mod.reshape(b, 6, d)
    g_tab, h_tab = _dft_tables(s)

    v, zcs = _inproj(x, mod3, p["g_mix_pre"], p["w_in_b"], _channel_dft_table())
    tile = min(TOKEN_TILE, s)
    cn = _conv(v, p["conv_w"], p["conv_b"], p["conv_ln_g"], p["conv_ln_b"], p["g_conv_out"], tile)
    fy = _seqdft(zcs, g_tab, h_tab)
    x1, h2, ysh = _mix(x, cn, fy, mod3, p["g_fnet_out"], p["w_out_b"], p["g_mix_post"], p["g_ffn_pre"],
                       p["wsg_b"], p["wsu_b"], p["wsd_b"], tile)
    idx, rank, wgt, cnt = _router(h2, p["wr_t_b"], p["b_router_col"], tile)

    n = t * TOP_K
    bm = _expert_block_rows(n)
    counts = cnt[:, 0]
    pcounts = (counts + bm - 1) // bm * bm
    pends = jnp.cumsum(pcounts)
    pstarts = pends - pcounts
    dest = _dest(pstarts.astype(jnp.int32), idx, rank, min(DEST_TILE, t))
    nb = (n + N_EXPERTS * (bm - 1) + bm - 1) // bm
    nused = (pends[-1] // bm).astype(jnp.int32)
    blk = jnp.minimum(jnp.arange(nb, dtype=jnp.int32), nused - 1) * bm
    block_e = jnp.sum((pends[None, :] <= blk[:, None]).astype(jnp.int32), axis=1)
    block_e = jnp.minimum(block_e, N_EXPERTS - 1)
    nvalid = jnp.clip(pstarts[block_e] + counts[block_e] - blk, 0, bm).astype(jnp.int32)
    first = jnp.concatenate([jnp.ones((1,), jnp.int32), (block_e[1:] != block_e[:-1]).astype(jnp.int32)])
    run = jnp.cumsum(first) - 1
    eid = jnp.arange(N_EXPERTS, dtype=jnp.int32)
    later = lax.cummin(jnp.where(pcounts > 0, eid, N_EXPERTS)[::-1])[::-1]
    nxt_e = jnp.concatenate([later[1:], jnp.full((1,), N_EXPERTS, jnp.int32)])
    nxt = jnp.where(nxt_e < N_EXPERTS, nxt_e, -1)[block_e].astype(jnp.int32)

    xbuf = _dispatch(h2.reshape(t, d // 2), dest, nb * bm)
    obuf = _experts(xbuf, block_e, nvalid, run.astype(jnp.int32), nxt, nused.reshape(1),
                    p["w_gate"], p["w_up"], p["w_down"], bm)
    y_routed = _gather_sum(obuf, dest, wgt)
    out = _combine(y_routed, x1.reshape(t, d), ysh.reshape(t, d // 2), mod3, p["g_ffn_post"], s,
                   min(COMBINE_TILE, s))
    return out.reshape(b, s, d)


def kernel(x_prompt, x_sample, c_prompt, c_sample, w_ada, b_ada, g_mix_pre, w_in, conv_w, conv_b, conv_ln_g, conv_ln_b, g_conv_out, g_fnet_out, w_out, g_mix_post, g_ffn_pre, w_router, b_router, w_gate, w_up, w_down, ws_gate, ws_up, ws_down, g_ffn_post):
    assert w_ada.shape[0] == 1, "single-layer kernel"
    bp, bs = c_prompt.shape[0], c_sample.shape[0]
    rows = -(-(bp + bs) // 8) * 8
    c_all = jnp.zeros((rows, D_MODEL), F32).at[:bp].set(c_prompt).at[bp:bp + bs].set(c_sample)
    mod = _ada(c_all, w_ada[0], b_ada)
    p = {
        "g_mix_pre": g_mix_pre, "w_in_b": w_in[0].astype(BF16),
        "conv_w": conv_w[0], "conv_b": conv_b, "conv_ln_g": conv_ln_g, "conv_ln_b": conv_ln_b,
        "g_conv_out": g_conv_out, "g_fnet_out": g_fnet_out, "w_out_b": w_out[0].astype(BF16),
        "g_mix_post": g_mix_post, "g_ffn_pre": g_ffn_pre,
        "wr_t_b": w_router[0].T.astype(BF16), "b_router_col": b_router[0][:, None],
        "w_gate": w_gate[0], "w_up": w_up[0], "w_down": w_down[0],
        "wsg_b": ws_gate[0].astype(BF16), "wsu_b": ws_up[0].astype(BF16), "wsd_b": ws_down[0].astype(BF16),
        "g_ffn_post": g_ffn_post,
    }
    y_sample = _layer(x_sample, mod[bp:bp + bs], p)
    y_prompt = _layer(x_prompt, mod[:bp], p)
    return (y_prompt, y_sample)
```
